```python
import math
import jax, jax.numpy as jnp
from jax import lax
import numpy as np

D_MODEL = 2048
BATCH = 16
SEQ = 2048
DEPTH = 1

HEAD_DIM = 64
N_SWA_HEADS = 16
N_SWA_KV = 4
N_SB_HEADS = 16
WINDOW = 128
BLOCK = 128
D_FF = 5504
EPS = 1e-6

SWA_Q = N_SWA_HEADS * HEAD_DIM
SWA_KV = N_SWA_KV * HEAD_DIM
SB_W = N_SB_HEADS * HEAD_DIM
MIX_W = SWA_Q + SB_W
IN_W = SWA_Q + 2 * SWA_KV + 3 * SB_W

kernel_name = "hybrid_swa_sink_stickbreak_macaron"


def rmsnorm(x, g):
    xf = x.astype(jnp.float32)
    y = xf * lax.rsqrt(jnp.mean(xf * xf, axis=-1, keepdims=True) + EPS)
    return (y * g.astype(jnp.float32)).astype(x.dtype)


def swiglu(h, w_gate, w_up, w_down):
    return (jax.nn.silu(h @ w_gate) * (h @ w_up)) @ w_down


def alibi_slopes(n_heads):
    i = jnp.arange(1, n_heads + 1, dtype=jnp.float32)
    return jnp.exp2(-8.0 * i / n_heads)


def swa_sink_attention(q, k, v, sinks):
    B, S, H, D = q.shape
    Hkv = k.shape[2]
    G = H // Hkv
    nb = S // BLOCK
    qb = q.reshape(B, nb, BLOCK, Hkv, G, D)
    kb = k.reshape(B, nb, BLOCK, Hkv, D)
    vb = v.reshape(B, nb, BLOCK, Hkv, D)
    pad = ((0, 0), (1, 0), (0, 0), (0, 0), (0, 0))
    kk = jnp.concatenate([jnp.pad(kb[:, :-1], pad), kb], axis=2)
    vv = jnp.concatenate([jnp.pad(vb[:, :-1], pad), vb], axis=2)
    s = jnp.einsum('bnqhgd,bnkhd->bnhgqk', qb, kk).astype(jnp.float32) * (D ** -0.5)
    q_pos = jnp.arange(BLOCK)[:, None] + BLOCK
    k_pos = jnp.arange(2 * BLOCK)[None, :]
    dist = q_pos - k_pos
    valid = (dist >= 0) & (dist < WINDOW)
    blk = jnp.arange(nb)[:, None, None]
    valid = valid[None] & ((blk > 0) | (k_pos[None] >= BLOCK))
    slopes = alibi_slopes(H).reshape(Hkv, G)
    s = s - slopes[:, :, None, None] * dist.astype(jnp.float32)
    s = jnp.where(valid[None, :, None, None], s, -jnp.inf)
    sink = jnp.broadcast_to(sinks.astype(jnp.float32).reshape(1, 1, Hkv, G, 1, 1),
                            s.shape[:-1] + (1,))
    p = jax.nn.softmax(jnp.concatenate([s, sink], axis=-1), axis=-1)[..., :-1]
    o = jnp.einsum('bnhgqk,bnkhd->bnqhgd', p.astype(v.dtype), vv)
    return o.reshape(B, S, H * D)


def stick_breaking_attention(q, k, v):
    B, S, H, D = q.shape
    nb = S // BLOCK
    qb = q.reshape(B, nb, BLOCK, H, D).transpose(1, 0, 2, 3, 4)
    k_pos = jnp.arange(S)

    def one_block(args):
        q_blk, n = args
        z = jnp.einsum('bqhd,bkhd->bhqk', q_blk, k).astype(jnp.float32) * (D ** -0.5)
        q_pos = n * BLOCK + jnp.arange(BLOCK)
        mask = k_pos[None, :] < q_pos[:, None]
        log_beta = jax.nn.log_sigmoid(z)
        log_1m = jnp.where(mask, jax.nn.log_sigmoid(-z), 0.0)
        after = lax.cumsum(log_1m, axis=3, reverse=True) - log_1m
        a = jnp.where(mask, jnp.exp(log_beta + after), 0.0)
        return jnp.einsum('bhqk,bkhd->bqhd', a.astype(v.dtype), v)

    o = lax.map(one_block, (qb, jnp.arange(nb)))
    return o.transpose(1, 0, 2, 3, 4).reshape(B, S, H * D)


def _fwd_setup_inputs(seed: int = 0) -> dict:
    key = jax.random.key(seed)
    ks = jax.random.split(key, 16)
    L = DEPTH

    def w(k, shape, fan_in):
        return jax.random.normal(k, shape, jnp.float32) * (fan_in ** -0.5)

    def gain(k, shape):
        return 1.0 + 0.02 * jax.random.normal(k, shape, jnp.float32)

    return {
        "x": jax.random.normal(ks[0], (BATCH, SEQ, D_MODEL), jnp.float32),
        "ffn1_norm": gain(ks[1], (L, D_MODEL)),
        "ffn1_w_gate": w(ks[2], (L, D_MODEL, D_FF), D_MODEL),
        "ffn1_w_up": w(ks[3], (L, D_MODEL, D_FF), D_MODEL),
        "ffn1_w_down": w(ks[4], (L, D_FF, D_MODEL), D_FF),
        "mix_norm": gain(ks[5], (L, D_MODEL)),
        "w_in": w(ks[6], (L, D_MODEL, IN_W), D_MODEL),
        "swa_sinks": jax.random.normal(ks[7], (L, N_SWA_HEADS), jnp.float32),
        "swa_out_norm": gain(ks[8], (L, SWA_Q)),
        "sb_out_norm": gain(ks[9], (L, SB_W)),
        "w_out": w(ks[10], (L, MIX_W, D_MODEL), MIX_W),
        "ffn2_norm": gain(ks[11], (L, D_MODEL)),
        "ffn2_w_gate": w(ks[12], (L, D_MODEL, D_FF), D_MODEL),
        "ffn2_w_up": w(ks[13], (L, D_MODEL, D_FF), D_MODEL),
        "ffn2_w_down": w(ks[14], (L, D_FF, D_MODEL), D_FF),
        "final_norm": gain(ks[15], (D_MODEL,)),
    }


def _fwd_reference(x, ffn1_norm, ffn1_w_gate, ffn1_w_up, ffn1_w_down, mix_norm, w_in,
              swa_sinks, swa_out_norm, sb_out_norm, w_out, ffn2_norm, ffn2_w_gate,
              ffn2_w_up, ffn2_w_down, final_norm):
    B, S, _ = x.shape
    for l in range(DEPTH):
        x = x + 0.5 * swiglu(rmsnorm(x, ffn1_norm[l]), ffn1_w_gate[l], ffn1_w_up[l], ffn1_w_down[l])

        h = rmsnorm(x, mix_norm[l])
        proj = h @ w_in[l]
        o1 = SWA_Q
        o2 = o1 + SWA_KV
        o3 = o2 + SWA_KV
        o4 = o3 + SB_W
        o5 = o4 + SB_W
        qa = proj[..., :o1].reshape(B, S, N_SWA_HEADS, HEAD_DIM)
        ka = proj[..., o1:o2].reshape(B, S, N_SWA_KV, HEAD_DIM)
        va = proj[..., o2:o3].reshape(B, S, N_SWA_KV, HEAD_DIM)
        qb = proj[..., o3:o4].reshape(B, S, N_SB_HEADS, HEAD_DIM)
        kb = proj[..., o4:o5].reshape(B, S, N_SB_HEADS, HEAD_DIM)
        vb = proj[..., o5:].reshape(B, S, N_SB_HEADS, HEAD_DIM)

        ya = swa_sink_attention(qa, ka, va, swa_sinks[l])
        yb = stick_breaking_attention(qb, kb, vb)

        y = jnp.concatenate([rmsnorm(ya, swa_out_norm[l]), rmsnorm(yb, sb_out_norm[l])], axis=-1)
        x = x + y @ w_out[l]

        x = x + 0.5 * swiglu(rmsnorm(x, ffn2_norm[l]), ffn2_w_gate[l], ffn2_w_up[l], ffn2_w_down[l])
    return rmsnorm(x, final_norm)


import jax as _jax
import jax.numpy as _jnp

TWIN_FORMAT = 'train_step'
FWD_PARAMS = ['x', 'ffn1_norm', 'ffn1_w_gate', 'ffn1_w_up', 'ffn1_w_down', 'mix_norm', 'w_in', 'swa_sinks', 'swa_out_norm', 'sb_out_norm', 'w_out', 'ffn2_norm', 'ffn2_w_gate', 'ffn2_w_up', 'ffn2_w_down', 'final_norm']
TWIN_WEIGHTS = ['ffn1_norm', 'ffn1_w_gate', 'ffn1_w_up', 'ffn1_w_down', 'mix_norm', 'w_in', 'swa_sinks', 'swa_out_norm', 'sb_out_norm', 'w_out', 'ffn2_norm', 'ffn2_w_gate', 'ffn2_w_up', 'ffn2_w_down', 'final_norm']
TWIN_DIFF_INPUT = 'x'
TWIN_INPUTS = ['x', 'ffn1_norm', 'ffn1_w_gate', 'ffn1_w_up', 'ffn1_w_down', 'mix_norm', 'w_in', 'swa_sinks', 'swa_out_norm', 'sb_out_norm', 'w_out', 'ffn2_norm', 'ffn2_w_gate', 'ffn2_w_up', 'ffn2_w_down', 'final_norm', 'loss_target', 'm_ffn1_norm', 'm_ffn1_w_gate', 'm_ffn1_w_up', 'm_ffn1_w_down', 'm_mix_norm', 'm_w_in', 'm_swa_sinks', 'm_swa_out_norm', 'm_sb_out_norm', 'm_w_out', 'm_ffn2_norm', 'm_ffn2_w_gate', 'm_ffn2_w_up', 'm_ffn2_w_down', 'm_final_norm', 'v_ffn1_norm', 'v_ffn1_w_gate', 'v_ffn1_w_up', 'v_ffn1_w_down', 'v_mix_norm', 'v_w_in', 'v_swa_sinks', 'v_swa_out_norm', 'v_sb_out_norm', 'v_w_out', 'v_ffn2_norm', 'v_ffn2_w_gate', 'v_ffn2_w_up', 'v_ffn2_w_down', 'v_final_norm']
TWIN_OUTPUTS = ['loss', 'grad_x', 'grad_ffn1_norm', 'grad_ffn1_w_gate', 'grad_ffn1_w_up', 'grad_ffn1_w_down', 'grad_mix_norm', 'grad_w_in', 'grad_swa_sinks', 'grad_swa_out_norm', 'grad_sb_out_norm', 'grad_w_out', 'grad_ffn2_norm', 'grad_ffn2_w_gate', 'grad_ffn2_w_up', 'grad_ffn2_w_down', 'grad_final_norm', 'delta_ffn1_norm', 'delta_ffn1_w_gate', 'delta_ffn1_w_up', 'delta_ffn1_w_down', 'delta_mix_norm', 'delta_w_in', 'delta_swa_sinks', 'delta_swa_out_norm', 'delta_sb_out_norm', 'delta_w_out', 'delta_ffn2_norm', 'delta_ffn2_w_gate', 'delta_ffn2_w_up', 'delta_ffn2_w_down', 'delta_final_norm', 'new_m_ffn1_norm', 'new_m_ffn1_w_gate', 'new_m_ffn1_w_up', 'new_m_ffn1_w_down', 'new_m_mix_norm', 'new_m_w_in', 'new_m_swa_sinks', 'new_m_swa_out_norm', 'new_m_sb_out_norm', 'new_m_w_out', 'new_m_ffn2_norm', 'new_m_ffn2_w_gate', 'new_m_ffn2_w_up', 'new_m_ffn2_w_down', 'new_m_final_norm', 'new_v_ffn1_norm', 'new_v_ffn1_w_gate', 'new_v_ffn1_w_up', 'new_v_ffn1_w_down', 'new_v_mix_norm', 'new_v_w_in', 'new_v_swa_sinks', 'new_v_swa_out_norm', 'new_v_sb_out_norm', 'new_v_w_out', 'new_v_ffn2_norm', 'new_v_ffn2_w_gate', 'new_v_ffn2_w_up', 'new_v_ffn2_w_down', 'new_v_final_norm']
TWIN_LEAF_KINDS = {'loss': 'loss', 'grad_x': 'grad_x', 'grad_ffn1_norm': 'grad_w', 'grad_ffn1_w_gate': 'grad_w', 'grad_ffn1_w_up': 'grad_w', 'grad_ffn1_w_down': 'grad_w', 'grad_mix_norm': 'grad_w', 'grad_w_in': 'grad_w', 'grad_swa_sinks': 'grad_w', 'grad_swa_out_norm': 'grad_w', 'grad_sb_out_norm': 'grad_w', 'grad_w_out': 'grad_w', 'grad_ffn2_norm': 'grad_w', 'grad_ffn2_w_gate': 'grad_w', 'grad_ffn2_w_up': 'grad_w', 'grad_ffn2_w_down': 'grad_w', 'grad_final_norm': 'grad_w', 'delta_ffn1_norm': 'delta_w', 'delta_ffn1_w_gate': 'delta_w', 'delta_ffn1_w_up': 'delta_w', 'delta_ffn1_w_down': 'delta_w', 'delta_mix_norm': 'delta_w', 'delta_w_in': 'delta_w', 'delta_swa_sinks': 'delta_w', 'delta_swa_out_norm': 'delta_w', 'delta_sb_out_norm': 'delta_w', 'delta_w_out': 'delta_w', 'delta_ffn2_norm': 'delta_w', 'delta_ffn2_w_gate': 'delta_w', 'delta_ffn2_w_up': 'delta_w', 'delta_ffn2_w_down': 'delta_w', 'delta_final_norm': 'delta_w', 'new_m_ffn1_norm': 'new_m', 'new_m_ffn1_w_gate': 'new_m', 'new_m_ffn1_w_up': 'new_m', 'new_m_ffn1_w_down': 'new_m', 'new_m_mix_norm': 'new_m', 'new_m_w_in': 'new_m', 'new_m_swa_sinks': 'new_m', 'new_m_swa_out_norm': 'new_m', 'new_m_sb_out_norm': 'new_m', 'new_m_w_out': 'new_m', 'new_m_ffn2_norm': 'new_m', 'new_m_ffn2_w_gate': 'new_m', 'new_m_ffn2_w_up': 'new_m', 'new_m_ffn2_w_down': 'new_m', 'new_m_final_norm': 'new_m', 'new_v_ffn1_norm': 'new_v', 'new_v_ffn1_w_gate': 'new_v', 'new_v_ffn1_w_up': 'new_v', 'new_v_ffn1_w_down': 'new_v', 'new_v_mix_norm': 'new_v', 'new_v_w_in': 'new_v', 'new_v_swa_sinks': 'new_v', 'new_v_swa_out_norm': 'new_v', 'new_v_sb_out_norm': 'new_v', 'new_v_w_out': 'new_v', 'new_v_ffn2_norm': 'new_v', 'new_v_ffn2_w_gate': 'new_v', 'new_v_ffn2_w_up': 'new_v', 'new_v_ffn2_w_down': 'new_v', 'new_v_final_norm': 'new_v'}


def _forward(args):
    return _fwd_reference(*[args[k] for k in FWD_PARAMS])


def _output_shape():
    out = _jax.eval_shape(lambda: _forward(_fwd_setup_inputs(0)))
    return out.shape, out.dtype

N_MICROBATCH = 1
ADAM_LR = 0.001
ADAM_B1 = 0.9
ADAM_B2 = 0.999
ADAM_EPS = 1e-08
ADAM_WD = 0.01
ADAM_STEP = 10
PER_EXAMPLE_BATCH_AXIS = {'x': 0, 'loss_target': 0}
SHARED_INPUTS = []
_WEIGHT_DTYPES = {'ffn1_norm': _jnp.float32, 'ffn1_w_gate': _jnp.float32, 'ffn1_w_up': _jnp.float32, 'ffn1_w_down': _jnp.float32, 'mix_norm': _jnp.float32, 'w_in': _jnp.float32, 'swa_sinks': _jnp.float32, 'swa_out_norm': _jnp.float32, 'sb_out_norm': _jnp.float32, 'w_out': _jnp.float32, 'ffn2_norm': _jnp.float32, 'ffn2_w_gate': _jnp.float32, 'ffn2_w_up': _jnp.float32, 'ffn2_w_down': _jnp.float32, 'final_norm': _jnp.float32}
MOMENT_SCALE = {'ffn1_norm': 4.414385e-02, 'ffn1_w_gate': 1.916255e-02, 'ffn1_w_up': 1.861748e-02, 'ffn1_w_down': 3.043958e-02, 'mix_norm': 9.044569e-02, 'w_in': 5.605348e-02, 'swa_sinks': 5.880264e-02, 'swa_out_norm': 6.518954e-02, 'sb_out_norm': 6.358668e-02, 'w_out': 6.273986e-02, 'ffn2_norm': 2.589526e-02, 'ffn2_w_gate': 1.126685e-02, 'ffn2_w_up': 1.091767e-02, 'ffn2_w_down': 1.787719e-02, 'final_norm': 1.599206e+01}


def _to_microbatches(a, axis):
    t = _jnp.moveaxis(a, axis, 0)
    t = t.reshape((N_MICROBATCH, t.shape[0] // N_MICROBATCH) + t.shape[1:])
    return _jnp.moveaxis(t, 1, axis + 1)


def setup_inputs(seed: int = 0) -> dict:
    inp = _fwd_setup_inputs(seed)
    key = _jax.random.fold_in(_jax.random.key(seed), 7919)
    shape, _ = _output_shape()
    out = dict(inp)
    out["loss_target"] = _jax.random.normal(_jax.random.fold_in(key, 0), shape, _jnp.float32)
    for i, name in enumerate(TWIN_WEIGHTS):
        w = inp[name].astype(_jnp.float32)
        if MOMENT_SCALE is None:
            s = _jnp.sqrt(_jnp.mean(_jnp.square(w)) + 1e-30)
        else:
            s = MOMENT_SCALE[name]
        km, kv = _jax.random.split(_jax.random.fold_in(key, i + 1))
        out[name] = w
        out["m_" + name] = s * _jax.random.normal(km, w.shape, _jnp.float32)
        out["v_" + name] = (s * s) * _jax.random.uniform(kv, w.shape, _jnp.float32, 0.5, 1.5)
    if N_MICROBATCH > 1:
        for name, axis in PER_EXAMPLE_BATCH_AXIS.items():
            out[name] = _to_microbatches(out[name], axis)
    return {'x': out['x'], 'ffn1_norm': out['ffn1_norm'], 'ffn1_w_gate': out['ffn1_w_gate'], 'ffn1_w_up': out['ffn1_w_up'], 'ffn1_w_down': out['ffn1_w_down'], 'mix_norm': out['mix_norm'], 'w_in': out['w_in'], 'swa_sinks': out['swa_sinks'], 'swa_out_norm': out['swa_out_norm'], 'sb_out_norm': out['sb_out_norm'], 'w_out': out['w_out'], 'ffn2_norm': out['ffn2_norm'], 'ffn2_w_gate': out['ffn2_w_gate'], 'ffn2_w_up': out['ffn2_w_up'], 'ffn2_w_down': out['ffn2_w_down'], 'final_norm': out['final_norm'], 'loss_target': out['loss_target'], 'm_ffn1_norm': out['m_ffn1_norm'], 'm_ffn1_w_gate': out['m_ffn1_w_gate'], 'm_ffn1_w_up': out['m_ffn1_w_up'], 'm_ffn1_w_down': out['m_ffn1_w_down'], 'm_mix_norm': out['m_mix_norm'], 'm_w_in': out['m_w_in'], 'm_swa_sinks': out['m_swa_sinks'], 'm_swa_out_norm': out['m_swa_out_norm'], 'm_sb_out_norm': out['m_sb_out_norm'], 'm_w_out': out['m_w_out'], 'm_ffn2_norm': out['m_ffn2_norm'], 'm_ffn2_w_gate': out['m_ffn2_w_gate'], 'm_ffn2_w_up': out['m_ffn2_w_up'], 'm_ffn2_w_down': out['m_ffn2_w_down'], 'm_final_norm': out['m_final_norm'], 'v_ffn1_norm': out['v_ffn1_norm'], 'v_ffn1_w_gate': out['v_ffn1_w_gate'], 'v_ffn1_w_up': out['v_ffn1_w_up'], 'v_ffn1_w_down': out['v_ffn1_w_down'], 'v_mix_norm': out['v_mix_norm'], 'v_w_in': out['v_w_in'], 'v_swa_sinks': out['v_swa_sinks'], 'v_swa_out_norm': out['v_swa_out_norm'], 'v_sb_out_norm': out['v_sb_out_norm'], 'v_w_out': out['v_w_out'], 'v_ffn2_norm': out['v_ffn2_norm'], 'v_ffn2_w_gate': out['v_ffn2_w_gate'], 'v_ffn2_w_up': out['v_ffn2_w_up'], 'v_ffn2_w_down': out['v_ffn2_w_down'], 'v_final_norm': out['v_final_norm']}


def _loss(weights, diff, rest, loss_target):
    with _jax.named_scope("forward"):
        args = {**rest, TWIN_DIFF_INPUT: diff, **{k: w.astype(_WEIGHT_DTYPES[k]) for k, w in weights.items()}}
        y = _forward(args)
    with _jax.named_scope("loss_head"):
        err = _jnp.square(y.astype(_jnp.float32) - loss_target)
        return 0.5 * _jnp.sum(_jnp.mean(err, axis=-1)) if err.ndim else 0.5 * err


def _adamw(w, g, m, v):
    m = ADAM_B1 * m + (1.0 - ADAM_B1) * g
    v = ADAM_B2 * v + (1.0 - ADAM_B2) * _jnp.square(g)
    m_hat = m / (1.0 - ADAM_B1 ** ADAM_STEP)
    v_hat = v / (1.0 - ADAM_B2 ** ADAM_STEP)
    delta = -ADAM_LR * (m_hat / (_jnp.sqrt(v_hat) + ADAM_EPS) + ADAM_WD * w)
    return delta, m, v


def reference(x, ffn1_norm, ffn1_w_gate, ffn1_w_up, ffn1_w_down, mix_norm, w_in, swa_sinks, swa_out_norm, sb_out_norm, w_out, ffn2_norm, ffn2_w_gate, ffn2_w_up, ffn2_w_down, final_norm, loss_target, m_ffn1_norm, m_ffn1_w_gate, m_ffn1_w_up, m_ffn1_w_down, m_mix_norm, m_w_in, m_swa_sinks, m_swa_out_norm, m_sb_out_norm, m_w_out, m_ffn2_norm, m_ffn2_w_gate, m_ffn2_w_up, m_ffn2_w_down, m_final_norm, v_ffn1_norm, v_ffn1_w_gate, v_ffn1_w_up, v_ffn1_w_down, v_mix_norm, v_w_in, v_swa_sinks, v_swa_out_norm, v_sb_out_norm, v_w_out, v_ffn2_norm, v_ffn2_w_gate, v_ffn2_w_up, v_ffn2_w_down, v_final_norm):
    given = dict(x=x, ffn1_norm=ffn1_norm, ffn1_w_gate=ffn1_w_gate, ffn1_w_up=ffn1_w_up, ffn1_w_down=ffn1_w_down, mix_norm=mix_norm, w_in=w_in, swa_sinks=swa_sinks, swa_out_norm=swa_out_norm, sb_out_norm=sb_out_norm, w_out=w_out, ffn2_norm=ffn2_norm, ffn2_w_gate=ffn2_w_gate, ffn2_w_up=ffn2_w_up, ffn2_w_down=ffn2_w_down, final_norm=final_norm, loss_target=loss_target, m_ffn1_norm=m_ffn1_norm, m_ffn1_w_gate=m_ffn1_w_gate, m_ffn1_w_up=m_ffn1_w_up, m_ffn1_w_down=m_ffn1_w_down, m_mix_norm=m_mix_norm, m_w_in=m_w_in, m_swa_sinks=m_swa_sinks, m_swa_out_norm=m_swa_out_norm, m_sb_out_norm=m_sb_out_norm, m_w_out=m_w_out, m_ffn2_norm=m_ffn2_norm, m_ffn2_w_gate=m_ffn2_w_gate, m_ffn2_w_up=m_ffn2_w_up, m_ffn2_w_down=m_ffn2_w_down, m_final_norm=m_final_norm, v_ffn1_norm=v_ffn1_norm, v_ffn1_w_gate=v_ffn1_w_gate, v_ffn1_w_up=v_ffn1_w_up, v_ffn1_w_down=v_ffn1_w_down, v_mix_norm=v_mix_norm, v_w_in=v_w_in, v_swa_sinks=v_swa_sinks, v_swa_out_norm=v_swa_out_norm, v_sb_out_norm=v_sb_out_norm, v_w_out=v_w_out, v_ffn2_norm=v_ffn2_norm, v_ffn2_w_gate=v_ffn2_w_gate, v_ffn2_w_up=v_ffn2_w_up, v_ffn2_w_down=v_ffn2_w_down, v_final_norm=v_final_norm)
    weights = {n: given[n] for n in TWIN_WEIGHTS}
    shared = {n: given[n] for n in SHARED_INPUTS}
    per_example = {n: given[n] for n in ['x']}
    grad_fn = _jax.value_and_grad(_loss, argnums=(0, 1))

    def one_microbatch(ex, loss_target):
        ex = dict(ex)
        diff = ex.pop(TWIN_DIFF_INPUT)
        return grad_fn(weights, diff, {**shared, **ex}, loss_target)

    if N_MICROBATCH == 1:
        loss, (grad_w, grad_x) = one_microbatch(per_example, given["loss_target"])
    else:
        def body(carry, xs):
            loss_sum, grad_sum = carry
            l_k, (gw_k, gx_k) = one_microbatch(xs[0], xs[1])
            with _jax.named_scope("update"):
                return (loss_sum + l_k, _jax.tree.map(_jnp.add, grad_sum, gw_k)), gx_k

        init = (_jnp.zeros((), _jnp.float32), _jax.tree.map(_jnp.zeros_like, weights))
        (loss, grad_w), grad_x = _jax.lax.scan(body, init, (per_example, given["loss_target"]))
    with _jax.named_scope("update"):
        delta_w, new_m, new_v = {}, {}, {}
        for n in TWIN_WEIGHTS:
            delta_w[n], new_m[n], new_v[n] = _adamw(weights[n], grad_w[n], given["m_" + n], given["v_" + n])
    return (loss, grad_x, *[grad_w[n] for n in TWIN_WEIGHTS], *[delta_w[n] for n in TWIN_WEIGHTS],
            *[new_m[n] for n in TWIN_WEIGHTS], *[new_v[n] for n in TWIN_WEIGHTS])
```

```python
import functools
import math

import jax
import jax.numpy as jnp
from jax import lax
from jax.experimental import pallas as pl
from jax.experimental.pallas import tpu as pltpu

F32 = jnp.float32
BF16 = jnp.bfloat16
MESH = pl.DeviceIdType.MESH

EPS = 1e-6
HEAD_DIM = 64
LANES = 128
N_SWA_HEADS = 16
N_SWA_KV = 4
N_SB_HEADS = 16
WINDOW = 128
SWA_BLOCK = 128
SB_BLOCK = 256
N_CHIPS = 4
ATT_SCALE = HEAD_DIM ** -0.5

ADAM_LR = 0.001
ADAM_B1 = 0.9
ADAM_B2 = 0.999
ADAM_EPS = 1e-08
ADAM_WD = 0.01
ADAM_STEP = 10

VMEM_LIMIT = 56 * 1024 * 1024
NEG_BIG = -1e30

ANY = pl.BlockSpec(memory_space=pl.ANY)
VMEM_WHOLE = pl.BlockSpec(memory_space=pltpu.VMEM)


def _row_tile(rows, cap, multiple=16):
    best = None
    for t in range(multiple, min(rows, cap) + 1, multiple):
        if rows % t == 0:
            best = t
    if best is None:
        raise ValueError(f"no row tile for {rows} rows under {cap}")
    return best


def _params(*sem):
    return pltpu.CompilerParams(dimension_semantics=sem, vmem_limit_bytes=VMEM_LIMIT)


def _dot(a, b):
    return jnp.dot(a, b, preferred_element_type=F32)


def _dot_nt(a, b):
    return lax.dot_general(a, b, (((1,), (1,)), ((), ())), preferred_element_type=F32)


def _dot_tn(a, b):
    return lax.dot_general(a, b, (((0,), (0,)), ((), ())), preferred_element_type=F32)


def _sigmoid(x):
    return 1.0 / (1.0 + jnp.exp(-x))


def _split_dot(x, u):
    hi = x.astype(BF16)
    lo = (x - hi.astype(F32)).astype(BF16)
    return _dot(hi, u) + _dot(lo, u)


def rmsnorm_cast(x, gamma, name):
    T, D = x.shape
    tr = min(512, T)

    def body(x_ref, g_ref, o_ref):
        xv = x_ref[...]
        r = lax.rsqrt(jnp.mean(xv * xv, axis=-1, keepdims=True) + EPS)
        o_ref[...] = ((xv * r) * g_ref[...]).astype(BF16)

    return pl.pallas_call(
        body, name=name, grid=(T // tr,),
        in_specs=[pl.BlockSpec((tr, D), lambda i: (i, 0)), pl.BlockSpec((1, D), lambda i: (0, 0))],
        out_specs=pl.BlockSpec((tr, D), lambda i: (i, 0)),
        out_shape=jax.ShapeDtypeStruct((T, D), BF16),
        compiler_params=_params("parallel"),
    )(x, gamma)


def _rms_bwd(xv, gamma, dy):
    d = xv.shape[-1]
    r = lax.rsqrt(jnp.mean(xv * xv, axis=-1, keepdims=True) + EPS)
    gdy = dy * gamma
    dot = jnp.sum(gdy * xv, axis=-1, keepdims=True)
    dx = r * gdy - xv * ((r * r * r) * (dot / d))
    dgamma = jnp.sum(dy * (xv * r), axis=0, keepdims=True)
    return dx, dgamma


def rmsnorm_bwd_add(x, gamma, dy, dx_up, name):
    T, D = x.shape
    tr = min(256, T)

    def body(x_ref, g_ref, dy_ref, up_ref, dx_ref, dg_ref):
        dx, dgamma = _rms_bwd(x_ref[...], g_ref[...], dy_ref[...])
        dx_ref[...] = up_ref[...] + dx

        @pl.when(pl.program_id(0) == 0)
        def _():
            dg_ref[...] = jnp.zeros_like(dg_ref)

        dg_ref[...] += dgamma

    row = pl.BlockSpec((tr, D), lambda i: (i, 0))
    vec = pl.BlockSpec((1, D), lambda i: (0, 0))
    return pl.pallas_call(
        body, name=name, grid=(T // tr,),
        in_specs=[row, vec, row, row], out_specs=[row, vec],
        out_shape=[jax.ShapeDtypeStruct((T, D), F32), jax.ShapeDtypeStruct((1, D), F32)],
        compiler_params=_params("arbitrary"),
    )(x, gamma, dy, dx_up)


def ffn_up(h, wg, wu, name):
    T, D = h.shape
    G, _, Fs = wg.shape
    tm = min(512, T)

    def body(h_ref, wg_ref, wu_ref, g_ref, u_ref, a_ref):
        hv = h_ref[...]
        g = _dot(hv, wg_ref[...])
        u = _dot(hv, wu_ref[...])
        g_ref[...] = g
        u_ref[...] = u
        a_ref[...] = ((g * _sigmoid(g)) * u).astype(BF16)

    w_spec = pl.BlockSpec((None, D, Fs), lambda s, i: (s, 0, 0), pipeline_mode=pl.Buffered(1))
    o_spec = pl.BlockSpec((None, tm, Fs), lambda s, i: (s, i, 0))
    return pl.pallas_call(
        body, name=name, grid=(G, T // tm),
        in_specs=[pl.BlockSpec((tm, D), lambda s, i: (i, 0)), w_spec, w_spec],
        out_specs=[o_spec, o_spec, o_spec],
        out_shape=[jax.ShapeDtypeStruct((G, T, Fs), F32), jax.ShapeDtypeStruct((G, T, Fs), F32),
                   jax.ShapeDtypeStruct((G, T, Fs), BF16)],
        compiler_params=_params("parallel", "parallel"),
    )(h, wg, wu)


def ffn_down(a, wd, x, name):
    G, T, Fs = a.shape
    D = wd.shape[2]
    tm = min(512, T)

    def body(a_ref, w_ref, x_ref, o_ref, acc):
        s = pl.program_id(1)

        @pl.when(s == 0)
        def _():
            acc[...] = jnp.zeros_like(acc)

        acc[...] += _dot(a_ref[...], w_ref[...])

        @pl.when(s == G - 1)
        def _():
            o_ref[...] = x_ref[...] + 0.5 * acc[...]

    return pl.pallas_call(
        body, name=name, grid=(T // tm, G),
        in_specs=[pl.BlockSpec((None, tm, Fs), lambda i, s: (s, i, 0)),
                  pl.BlockSpec((None, Fs, D), lambda i, s: (s, 0, 0)),
                  pl.BlockSpec((tm, D), lambda i, s: (i, 0))],
        out_specs=pl.BlockSpec((tm, D), lambda i, s: (i, 0)),
        out_shape=jax.ShapeDtypeStruct((T, D), F32),
        scratch_shapes=[pltpu.VMEM((tm, D), F32)],
        compiler_params=_params("parallel", "arbitrary"),
    )(a, wd, x)


def ffn_bwd_act(dx, wd, g, u, name):
    T, D = dx.shape
    G, Fs, _ = wd.shape
    tm = min(256, T)

    def body(dx_ref, w_ref, g_ref, u_ref, dg_ref, du_ref):
        da = _dot_nt((0.5 * dx_ref[...]).astype(BF16), w_ref[...])
        gv = g_ref[...]
        sig = _sigmoid(gv)
        silu = gv * sig
        du_ref[...] = (da * silu).astype(BF16)
        dg_ref[...] = ((da * u_ref[...]) * (sig * (1.0 + gv * (1.0 - sig)))).astype(BF16)

    act = pl.BlockSpec((None, tm, Fs), lambda s, i: (s, i, 0))
    return pl.pallas_call(
        body, name=name, grid=(G, T // tm),
        in_specs=[pl.BlockSpec((tm, D), lambda s, i: (i, 0)),
                  pl.BlockSpec((None, Fs, D), lambda s, i: (s, 0, 0), pipeline_mode=pl.Buffered(1)),
                  act, act],
        out_specs=[act, act],
        out_shape=[jax.ShapeDtypeStruct((G, T, Fs), BF16), jax.ShapeDtypeStruct((G, T, Fs), BF16)],
        compiler_params=_params("parallel", "parallel"),
    )(dx, wd, g, u)


def ffn_bwd_dh(dg, du, wg, wu, name):
    G, T, Fs = dg.shape
    D = wg.shape[1]
    tm = min(512, T)

    def body(dg_ref, du_ref, wg_ref, wu_ref, o_ref, acc):
        s = pl.program_id(1)

        @pl.when(s == 0)
        def _():
            acc[...] = jnp.zeros_like(acc)

        acc[...] += _dot_nt(dg_ref[...], wg_ref[...])
        acc[...] += _dot_nt(du_ref[...], wu_ref[...])

        @pl.when(s == G - 1)
        def _():
            o_ref[...] = acc[...]

    act = pl.BlockSpec((None, tm, Fs), lambda i, s: (s, i, 0))
    w_spec = pl.BlockSpec((None, D, Fs), lambda i, s: (s, 0, 0))
    return pl.pallas_call(
        body, name=name, grid=(T // tm, G),
        in_specs=[act, act, w_spec, w_spec],
        out_specs=pl.BlockSpec((tm, D), lambda i, s: (i, 0)),
        out_shape=jax.ShapeDtypeStruct((T, D), F32),
        scratch_shapes=[pltpu.VMEM((tm, D), F32)],
        compiler_params=_params("parallel", "arbitrary"),
    )(dg, du, wg, wu)


def wgrad(a, b, name, *, groups, m, n, a_block, a_map, b_block, b_map, b_scale=None, tmo=1024, tk=512):
    T = a.shape[-2]
    tk = min(tk, T)
    tmo = min(tmo, m)
    nk = T // tk

    def body(a_ref, b_ref, o_ref, acc):
        k = pl.program_id(2)

        @pl.when(k == 0)
        def _():
            acc[...] = jnp.zeros_like(acc)

        bv = b_ref[...]
        if b_scale is not None:
            bv = b_scale * bv
        acc[...] += _dot_tn(a_ref[...].astype(BF16), bv.astype(BF16))

        @pl.when(k == nk - 1)
        def _():
            o_ref[...] = acc[...].astype(BF16)

    return pl.pallas_call(
        body, name=name, grid=(groups, m // tmo, nk),
        in_specs=[pl.BlockSpec(a_block(tk, tmo), a_map), pl.BlockSpec(b_block(tk), b_map)],
        out_specs=pl.BlockSpec((None, tmo, n), lambda s, j, k: (s, j, 0)),
        out_shape=jax.ShapeDtypeStruct((groups, m, n), BF16),
        scratch_shapes=[pltpu.VMEM((tmo, n), F32)],
        compiler_params=_params("parallel", "parallel", "arbitrary"),
    )(a, b)


def in_proj(h, w_in, name):
    T, D = h.shape
    G, _, Ps = w_in.shape
    tm = min(512, T)

    def body(h_ref, w_ref, o_ref):
        o_ref[...] = _dot(h_ref[...], w_ref[...]).astype(BF16)

    return pl.pallas_call(
        body, name=name, grid=(G, T // tm),
        in_specs=[pl.BlockSpec((tm, D), lambda s, i: (i, 0)),
                  pl.BlockSpec((None, D, Ps), lambda s, i: (s, 0, 0))],
        out_specs=pl.BlockSpec((tm, Ps), lambda s, i: (i, s)),
        out_shape=jax.ShapeDtypeStruct((T, G * Ps), BF16),
        compiler_params=_params("parallel", "parallel"),
    )(h, w_in)


def in_proj_bwd(dproj, w_in, name):
    T = dproj.shape[0]
    G, D, Ps = w_in.shape
    tm = min(512, T)

    def body(d_ref, w_ref, o_ref, acc):
        s = pl.program_id(1)

        @pl.when(s == 0)
        def _():
            acc[...] = jnp.zeros_like(acc)

        acc[...] += _dot_nt(d_ref[...], w_ref[...])

        @pl.when(s == G - 1)
        def _():
            o_ref[...] = acc[...]

    return pl.pallas_call(
        body, name=name, grid=(T // tm, G),
        in_specs=[pl.BlockSpec((tm, Ps), lambda i, s: (i, s)),
                  pl.BlockSpec((None, D, Ps), lambda i, s: (s, 0, 0))],
        out_specs=pl.BlockSpec((tm, D), lambda i, s: (i, 0)),
        out_shape=jax.ShapeDtypeStruct((T, D), F32),
        scratch_shapes=[pltpu.VMEM((tm, D), F32)],
        compiler_params=_params("parallel", "arbitrary"),
    )(dproj, w_in)


def out_proj(y, w_out, x, name):
    T, M = y.shape
    D = w_out.shape[1]
    tm = min(512, T)

    def body(y_ref, w_ref, x_ref, o_ref):
        o_ref[...] = x_ref[...] + _dot(y_ref[...], w_ref[...])

    return pl.pallas_call(
        body, name=name, grid=(T // tm,),
        in_specs=[pl.BlockSpec((tm, M), lambda i: (i, 0)), pl.BlockSpec((M, D), lambda i: (0, 0)),
                  pl.BlockSpec((tm, D), lambda i: (i, 0))],
        out_specs=pl.BlockSpec((tm, D), lambda i: (i, 0)),
        out_shape=jax.ShapeDtypeStruct((T, D), F32),
        compiler_params=_params("parallel"),
    )(y, w_out, x)


def out_norm(ya, yb, ga, gb, name):
    T, W = ya.shape
    tr = min(512, T)

    def body(a_ref, b_ref, ga_ref, gb_ref, o_ref):
        for src, gam, col in ((a_ref, ga_ref, 0), (b_ref, gb_ref, W)):
            v = src[...]
            r = lax.rsqrt(jnp.mean(v * v, axis=-1, keepdims=True) + EPS)
            o_ref[:, col:col + W] = ((v * r) * gam[...]).astype(BF16)

    row = pl.BlockSpec((tr, W), lambda i: (i, 0))
    vec = pl.BlockSpec((1, W), lambda i: (0, 0))
    return pl.pallas_call(
        body, name=name, grid=(T // tr,),
        in_specs=[row, row, vec, vec],
        out_specs=pl.BlockSpec((tr, 2 * W), lambda i: (i, 0)),
        out_shape=jax.ShapeDtypeStruct((T, 2 * W), BF16),
        compiler_params=_params("parallel"),
    )(ya, yb, ga, gb)


def out_proj_bwd(dx, w_out, ya, yb, ga, gb, name):
    T, D = dx.shape
    M = w_out.shape[0]
    W = M // 2
    tm = min(256, T)

    def body(dx_ref, w_ref, a_ref, b_ref, ga_ref, gb_ref, da_ref, db_ref, dga_ref, dgb_ref):
        dy = _dot_nt(dx_ref[...].astype(BF16), w_ref[...])

        @pl.when(pl.program_id(0) == 0)
        def _():
            dga_ref[...] = jnp.zeros_like(dga_ref)
            dgb_ref[...] = jnp.zeros_like(dgb_ref)

        d_a, dg_a = _rms_bwd(a_ref[...], ga_ref[...], dy[:, :W])
        d_b, dg_b = _rms_bwd(b_ref[...], gb_ref[...], dy[:, W:])
        da_ref[...] = d_a
        db_ref[...] = d_b
        dga_ref[...] += dg_a
        dgb_ref[...] += dg_b

    row = pl.BlockSpec((tm, W), lambda i: (i, 0))
    vec = pl.BlockSpec((1, W), lambda i: (0, 0))
    return pl.pallas_call(
        body, name=name, grid=(T // tm,),
        in_specs=[pl.BlockSpec((tm, D), lambda i: (i, 0)), pl.BlockSpec((M, D), lambda i: (0, 0)),
                  row, row, vec, vec],
        out_specs=[row, row, vec, vec],
        out_shape=[jax.ShapeDtypeStruct((T, W), F32), jax.ShapeDtypeStruct((T, W), F32),
                   jax.ShapeDtypeStruct((1, W), F32), jax.ShapeDtypeStruct((1, W), F32)],
        compiler_params=_params("arbitrary"),
    )(dx, w_out, ya, yb, ga, gb)


def final_loss(x, gamma, target, name):
    T, D = x.shape
    tr = min(256, T)

    def body(x_ref, g_ref, t_ref, dx_ref, dg_ref, loss_ref):
        xv = x_ref[...]
        gam = g_ref[...]
        r = lax.rsqrt(jnp.mean(xv * xv, axis=-1, keepdims=True) + EPS)
        err = (xv * r) * gam - t_ref[...]
        part = 0.5 * jnp.sum(jnp.mean(err * err, axis=-1, keepdims=True), axis=0, keepdims=True)
        dx, dgamma = _rms_bwd(xv, gam, err / D)
        dx_ref[...] = dx

        @pl.when(pl.program_id(0) == 0)
        def _():
            dg_ref[...] = jnp.zeros_like(dg_ref)
            loss_ref[...] = jnp.zeros_like(loss_ref)

        dg_ref[...] += dgamma
        loss_ref[...] += jnp.broadcast_to(part, loss_ref.shape)

    row = pl.BlockSpec((tr, D), lambda i: (i, 0))
    vec = pl.BlockSpec((1, D), lambda i: (0, 0))
    return pl.pallas_call(
        body, name=name, grid=(T // tr,),
        in_specs=[row, vec, row],
        out_specs=[row, vec, pl.BlockSpec((1, LANES), lambda i: (0, 0))],
        out_shape=[jax.ShapeDtypeStruct((T, D), F32), jax.ShapeDtypeStruct((1, D), F32),
                   jax.ShapeDtypeStruct((1, LANES), F32)],
        compiler_params=_params("arbitrary"),
    )(x, gamma, target)


def _lane_select(src_half, dst_halves):
    r = lax.broadcasted_iota(jnp.int32, (LANES, LANES), 0)
    c = lax.broadcasted_iota(jnp.int32, (LANES, LANES), 1)
    hit = ((r >> 6) == src_half) & ((r & (HEAD_DIM - 1)) == (c & (HEAD_DIM - 1)))
    ok = functools.reduce(jnp.logical_or, [(c >> 6) == d for d in dst_halves])
    return jnp.where(hit & ok, 1.0, 0.0).astype(BF16)


def _swa_slope(kvp, local_head):
    lo = 2.0 ** (-8.0 * (local_head + 1) / N_SWA_HEADS)
    hi = 2.0 ** (-8.0 * (8 + local_head + 1) / N_SWA_HEADS)
    return jnp.where(kvp == 0, jnp.float32(lo), jnp.float32(hi))


def _swa_probs(qblk, kwin, n, start, slope, sink):
    z = _dot_nt(qblk, kwin) * ATT_SCALE
    t = n * SWA_BLOCK + lax.broadcasted_iota(jnp.int32, z.shape, 0)
    s = start + lax.broadcasted_iota(jnp.int32, z.shape, 1)
    dist = t - s
    valid = (dist >= 0) & (dist < WINDOW)
    z = jnp.where(valid, z - slope * dist.astype(F32), NEG_BIG)
    m = jnp.maximum(jnp.max(z, axis=-1, keepdims=True), sink)
    e = jnp.exp(z - m)
    den = jnp.sum(e, axis=-1, keepdims=True) + jnp.exp(sink - m)
    return e / den, m, den


def swa_fwd(proj, sinks, b_loc, seq, name):
    T = proj.shape[0]
    nblk = seq // SWA_BLOCK

    def body(sink_ref, q_ref, k_ref, v_ref, o_ref, qs, vv):
        kvp = pl.program_id(1)
        lane = lax.broadcasted_iota(jnp.int32, (SWA_BLOCK, LANES), 1)
        for kvh in range(2):
            both = _lane_select(kvh, (0, 1))
            vv[...] = _dot(v_ref[...], both).astype(BF16)
            for qp in (2 * kvh, 2 * kvh + 1):
                cols = slice(LANES * qp, LANES * (qp + 1))
                for a in range(2):
                    qs[a] = _dot(q_ref[:, cols], _lane_select(a, (kvh,))).astype(BF16)
                slopes = [_swa_slope(kvp, 2 * qp + a) for a in range(2)]
                snk = [sink_ref[0, 8 * kvp + 2 * qp + a] for a in range(2)]

                def blk(n, carry):
                    start = pl.multiple_of(jnp.maximum(n - 1, 0) * SWA_BLOCK, SWA_BLOCK)
                    rows = pl.ds(pl.multiple_of(n * SWA_BLOCK, SWA_BLOCK), SWA_BLOCK)
                    win = pl.ds(start, 2 * SWA_BLOCK)
                    outs = []
                    for a in range(2):
                        p, _, _ = _swa_probs(qs[a, rows, :], k_ref[win, :], n, start, slopes[a], snk[a])
                        outs.append(_dot(p.astype(BF16), vv[win, :]))
                    o_ref[rows, cols] = jnp.where(lane < HEAD_DIM, outs[0], outs[1])
                    return carry

                lax.fori_loop(0, nblk, blk, 0)

    return pl.pallas_call(
        body, name=name, grid=(b_loc, 2),
        in_specs=[pl.BlockSpec(memory_space=pltpu.SMEM),
                  pl.BlockSpec((seq, 4 * LANES), lambda b, p: (b, p)),
                  pl.BlockSpec((seq, LANES), lambda b, p: (b, 8 + p)),
                  pl.BlockSpec((seq, LANES), lambda b, p: (b, 10 + p))],
        out_specs=pl.BlockSpec((seq, 4 * LANES), lambda b, p: (b, p)),
        out_shape=jax.ShapeDtypeStruct((T, N_SWA_HEADS * HEAD_DIM), F32),
        scratch_shapes=[pltpu.VMEM((2, seq, LANES), BF16), pltpu.VMEM((seq, LANES), BF16)],
        compiler_params=_params("parallel", "parallel"),
    )(sinks, proj, proj, proj)


def swa_bwd(proj, sinks, dya, b_loc, seq, name):
    T = proj.shape[0]
    nblk = seq // SWA_BLOCK

    def body(sink_ref, q_ref, k_ref, v_ref, do_ref, dq_ref, dk_ref, dv_ref, ds_ref, qs, dos, kk, dk_acc, dv_acc):
        kvp = pl.program_id(0)
        b = pl.program_id(1)
        lane = lax.broadcasted_iota(jnp.int32, (SWA_BLOCK, LANES), 1)

        @pl.when(b == 0)
        def _():
            ds_ref[...] = jnp.zeros_like(ds_ref)

        dk_acc[...] = jnp.zeros_like(dk_acc)
        dv_acc[...] = jnp.zeros_like(dv_acc)
        for kvh in range(2):
            kk[...] = _dot(k_ref[...], _lane_select(kvh, (0, 1))).astype(BF16)
            for qp in (2 * kvh, 2 * kvh + 1):
                cols = slice(LANES * qp, LANES * (qp + 1))
                for a in range(2):
                    move = _lane_select(a, (kvh,))
                    qs[a] = _dot(q_ref[:, cols], move).astype(BF16)
                    dos[a] = _dot(do_ref[:, cols].astype(BF16), move).astype(BF16)
                slopes = [_swa_slope(kvp, 2 * qp + a) for a in range(2)]
                snk = [sink_ref[0, 8 * kvp + 2 * qp + a] for a in range(2)]

                def blk(n, carry):
                    start = pl.multiple_of(jnp.maximum(n - 1, 0) * SWA_BLOCK, SWA_BLOCK)
                    rows = pl.ds(pl.multiple_of(n * SWA_BLOCK, SWA_BLOCK), SWA_BLOCK)
                    win = pl.ds(start, 2 * SWA_BLOCK)
                    dqs = []
                    for a in range(2):
                        qblk = qs[a, rows, :]
                        doblk = dos[a, rows, :]
                        p, m, den = _swa_probs(qblk, k_ref[win, :], n, start, slopes[a], snk[a])
                        dp = _dot_nt(doblk, v_ref[win, :])
                        delta = jnp.sum(p * dp, axis=-1, keepdims=True)
                        p_sink = jnp.exp(snk[a] - m) / den
                        head = 2 * qp + a
                        ds_ref[head:head + 1, :] -= jnp.broadcast_to(
                            jnp.sum(p_sink * delta, axis=0, keepdims=True), (1, LANES))
                        dsc = ((p * (dp - delta)) * ATT_SCALE).astype(BF16)
                        dv_acc[win, :] += _dot_tn(p.astype(BF16), doblk)
                        dk_acc[win, :] += _dot_tn(dsc, qblk)
                        dqs.append(_dot(dsc, kk[win, :]))
                    dq_ref[rows, cols] = jnp.where(lane < HEAD_DIM, dqs[0], dqs[1]).astype(BF16)
                    return carry

                lax.fori_loop(0, nblk, blk, 0)
        dk_ref[...] = dk_acc[...].astype(BF16)
        dv_ref[...] = dv_acc[...].astype(BF16)

    return pl.pallas_call(
        body, name=name, grid=(2, b_loc),
        in_specs=[pl.BlockSpec(memory_space=pltpu.SMEM),
                  pl.BlockSpec((seq, 4 * LANES), lambda p, b: (b, p)),
                  pl.BlockSpec((seq, LANES), lambda p, b: (b, 8 + p)),
                  pl.BlockSpec((seq, LANES), lambda p, b: (b, 10 + p)),
                  pl.BlockSpec((seq, 4 * LANES), lambda p, b: (b, p))],
        out_specs=[pl.BlockSpec((seq, 4 * LANES), lambda p, b: (b, p)),
                   pl.BlockSpec((seq, LANES), lambda p, b: (b, p)),
                   pl.BlockSpec((seq, LANES), lambda p, b: (b, p)),
                   pl.BlockSpec((8, LANES), lambda p, b: (p, 0))],
        out_shape=[jax.ShapeDtypeStruct((T, N_SWA_HEADS * HEAD_DIM), BF16),
                   jax.ShapeDtypeStruct((T, N_SWA_KV * HEAD_DIM), BF16),
                   jax.ShapeDtypeStruct((T, N_SWA_KV * HEAD_DIM), BF16),
                   jax.ShapeDtypeStruct((N_SWA_HEADS, LANES), F32)],
        scratch_shapes=[pltpu.VMEM((2, seq, LANES), BF16), pltpu.VMEM((2, seq, LANES), BF16),
                        pltpu.VMEM((seq, LANES), BF16),
                        pltpu.VMEM((seq, LANES), F32), pltpu.VMEM((seq, LANES), F32)],
        compiler_params=_params("arbitrary", "arbitrary"),
    )(sinks, proj, proj, proj, dya)


def _softplus(z):
    return jnp.maximum(z, 0.0) + jnp.log(1.0 + jnp.exp(-jnp.abs(z)))


def _tri(n, rel):
    r = lax.broadcasted_iota(jnp.int32, (n, n), 0)
    c = lax.broadcasted_iota(jnp.int32, (n, n), 1)
    return jnp.where(rel(r, c), 1.0, 0.0).astype(BF16)


def sb_fwd(proj, b_loc, seq, name):
    T = proj.shape[0]
    bq = min(SB_BLOCK, seq)
    nq = seq // bq
    npair = N_SB_HEADS // 2

    def body(q_ref, k_ref, v_ref, o_ref, lt_ref):
        i = pl.program_id(2)
        lane = lax.broadcasted_iota(jnp.int32, (bq, LANES), 1)
        after = _tri(bq, lambda r, c: r > c)
        tpos = i * bq + lax.broadcasted_iota(jnp.int32, (bq, bq), 0)
        col = lax.broadcasted_iota(jnp.int32, (bq, bq), 1)
        outs, tots = [], []
        for h in range(2):
            qh = jnp.where((lane >> 6) == h, q_ref[...], jnp.zeros((bq, LANES), BF16))

            def kblock(jj, carry):
                run, acc = carry
                j = i - jj
                rows = pl.ds(pl.multiple_of(j * bq, bq), bq)
                z = _dot_nt(qh, k_ref[rows, :]) * ATT_SCALE
                mask = (j * bq + col) < tpos
                sp = _softplus(z)
                lg = jnp.where(mask, -sp, 0.0)
                c_after = _split_dot(lg, after) + run
                a = jnp.where(mask, jnp.exp((z - sp) + c_after), 0.0)
                acc = acc + _dot(a.astype(BF16), v_ref[rows, :])
                return run + jnp.sum(lg, axis=-1, keepdims=True), acc

            run, acc = lax.fori_loop(0, i + 1, kblock,
                                     (jnp.zeros((bq, 1), F32), jnp.zeros((bq, LANES), F32)))
            outs.append(acc)
            tots.append(run)
        o_ref[...] = jnp.where(lane < HEAD_DIM, outs[0], outs[1])
        lt_ref[...] = jnp.where(lane < HEAD_DIM, tots[0], tots[1])

    return pl.pallas_call(
        body, name=name, grid=(b_loc, npair, nq),
        in_specs=[pl.BlockSpec((bq, LANES), lambda b, p, i: (b * nq + i, 12 + p)),
                  pl.BlockSpec((seq, LANES), lambda b, p, i: (b, 20 + p)),
                  pl.BlockSpec((seq, LANES), lambda b, p, i: (b, 28 + p))],
        out_specs=[pl.BlockSpec((bq, LANES), lambda b, p, i: (b * nq + i, p)),
                   pl.BlockSpec((None, bq, LANES), lambda b, p, i: (b * npair + p, i, 0))],
        out_shape=[jax.ShapeDtypeStruct((T, N_SB_HEADS * HEAD_DIM), F32),
                   jax.ShapeDtypeStruct((b_loc * npair, seq, LANES), F32)],
        compiler_params=_params("parallel", "parallel", "parallel"),
    )(proj, proj, proj)


def sb_bwd(proj, dyb, ltot, b_loc, seq, name):
    T = proj.shape[0]
    bq = min(SB_BLOCK, seq)
    nq = seq // bq
    npair = N_SB_HEADS // 2

    def body(q_ref, k_ref, v_ref, do_ref, lt_ref, dq_ref, dk_ref, dv_ref, dq_acc, dk_acc, dv_acc):
        lane = lax.broadcasted_iota(jnp.int32, (bq, LANES), 1)
        upto = _tri(bq, lambda r, c: r <= c)
        before = _tri(bq, lambda r, c: r < c)
        row = lax.broadcasted_iota(jnp.int32, (bq, bq), 0)
        col = lax.broadcasted_iota(jnp.int32, (bq, bq), 1)
        dq_acc[...] = jnp.zeros_like(dq_acc)
        dk_acc[...] = jnp.zeros_like(dk_acc)
        dv_acc[...] = jnp.zeros_like(dv_acc)
        for h in range(2):
            mine = (lane >> 6) == h

            def qblock(i, carry):
                qrows = pl.ds(pl.multiple_of(i * bq, bq), bq)
                qh = jnp.where(mine, q_ref[qrows, :], jnp.zeros((bq, LANES), BF16))
                doh = jnp.where(mine, do_ref[qrows, :], 0.0).astype(BF16)
                total = jnp.sum(jnp.where(lane == h * HEAD_DIM, lt_ref[qrows, :], 0.0), axis=-1, keepdims=True)
                tpos = i * bq + row

                def kblock(j, inner):
                    l_pre, g_pre, dq = inner
                    krows = pl.ds(pl.multiple_of(j * bq, bq), bq)
                    kj = k_ref[krows, :]
                    z = _dot_nt(qh, kj) * ATT_SCALE
                    mask = (j * bq + col) < tpos
                    sp = _softplus(z)
                    lg = jnp.where(mask, -sp, 0.0)
                    log_beta = z - sp
                    c_after = total - (_split_dot(lg, upto) + l_pre)
                    a = jnp.where(mask, jnp.exp(log_beta + c_after), 0.0)
                    g = a * _dot_nt(doh, v_ref[krows, :])
                    p = _split_dot(g, before) + g_pre
                    beta = jnp.exp(log_beta)
                    dz = (jnp.where(mask, g * (1.0 - beta) - beta * p, 0.0) * ATT_SCALE).astype(BF16)
                    dv_acc[krows, :] += _dot_tn(a.astype(BF16), doh)
                    dk_acc[krows, :] += _dot_tn(dz, qh)
                    dq = dq + _dot(dz, jnp.where(mine, kj, jnp.zeros_like(kj)))
                    return (l_pre + jnp.sum(lg, axis=-1, keepdims=True),
                            g_pre + jnp.sum(g, axis=-1, keepdims=True), dq)

                zero = jnp.zeros((bq, 1), F32)
                _, _, dq = lax.fori_loop(0, i + 1, kblock, (zero, zero, jnp.zeros((bq, LANES), F32)))
                dq_acc[qrows, :] += dq
                return carry

            lax.fori_loop(0, nq, qblock, 0)
        dq_ref[...] = dq_acc[...].astype(BF16)
        dk_ref[...] = dk_acc[...].astype(BF16)
        dv_ref[...] = dv_acc[...].astype(BF16)

    pair = pl.BlockSpec((seq, LANES), lambda b, p: (b, p))
    shape = jax.ShapeDtypeStruct((T, N_SB_HEADS * HEAD_DIM), BF16)
    return pl.pallas_call(
        body, name=name, grid=(b_loc, npair),
        in_specs=[pl.BlockSpec((seq, LANES), lambda b, p: (b, 12 + p)),
                  pl.BlockSpec((seq, LANES), lambda b, p: (b, 20 + p)),
                  pl.BlockSpec((seq, LANES), lambda b, p: (b, 28 + p)),
                  pair,
                  pl.BlockSpec((None, seq, LANES), lambda b, p: (b * npair + p, 0, 0))],
        out_specs=[pair, pair, pair],
        out_shape=[shape, shape, shape],
        scratch_shapes=[pltpu.VMEM((seq, LANES), F32)] * 3,
        compiler_params=_params("parallel", "parallel"),
    )(proj, proj, proj, dyb, ltot)


def _place():
    return lax.axis_index("x"), lax.axis_index("y"), lax.axis_index("c")


def _other_chips(x, y):
    return [(1 - x, y), (x, 1 - y), (1 - x, 1 - y)]


def chip_exchange(arrays, name, *, scatter):
    n = len(arrays)
    out_shapes = [jax.ShapeDtypeStruct((N_CHIPS,) + a.shape[-2:], a.dtype) for a in arrays]

    def body(*refs):
        ins, outs = refs[:n], refs[n:2 * n]
        send_sems, recv_sems, local_sems = refs[2 * n:]
        x, y, c = _place()
        me = 2 * x + y
        started = []
        for k in range(n):
            own = pltpu.make_async_copy(ins[k].at[me] if scatter else ins[k], outs[k].at[me], local_sems.at[k])
            own.start()
            started.append(own)
            for j, (px, py) in enumerate(_other_chips(x, y)):
                src = ins[k].at[2 * px + py] if scatter else ins[k]
                cp = pltpu.make_async_remote_copy(
                    src_ref=src, dst_ref=outs[k].at[me],
                    send_sem=send_sems.at[3 * k + j], recv_sem=recv_sems.at[3 * k + j],
                    device_id=(px, py, c), device_id_type=MESH)
                cp.start()
                started.append(cp)
        for cp in started:
            cp.wait()

    return pl.pallas_call(
        body, name=name,
        in_specs=[ANY] * n, out_specs=[ANY] * n, out_shape=out_shapes,
        scratch_shapes=[pltpu.SemaphoreType.DMA((3 * n,)), pltpu.SemaphoreType.DMA((3 * n,)),
                        pltpu.SemaphoreType.DMA((n,))],
    )(*arrays)


def sibling_exchange(arrays, name):
    n = len(arrays)

    def body(*refs):
        ins, outs = refs[:n], refs[n:2 * n]
        send_sems, recv_sems = refs[2 * n:]
        x, y, c = _place()
        started = []
        for k in range(n):
            cp = pltpu.make_async_remote_copy(
                src_ref=ins[k], dst_ref=outs[k], send_sem=send_sems.at[k], recv_sem=recv_sems.at[k],
                device_id=(x, y, 1 - c), device_id_type=MESH)
            cp.start()
            started.append(cp)
        for cp in started:
            cp.wait()

    return pl.pallas_call(
        body, name=name,
        in_specs=[ANY] * n, out_specs=[ANY] * n,
        out_shape=[jax.ShapeDtypeStruct(a.shape, a.dtype) for a in arrays],
        scratch_shapes=[pltpu.SemaphoreType.DMA((n,)), pltpu.SemaphoreType.DMA((n,))],
    )(*arrays)


def allreduce_small(v, name):
    R = v.shape[0]
    ndev = 8

    def body(v_ref, o_ref, land, send_sems, recv_sems):
        x, y, c = _place()
        me = 4 * x + 2 * y + c
        land[me] = v_ref[...]
        started = []
        for f in range(1, ndev):
            fx, fy, fc = (f >> 2) & 1, (f >> 1) & 1, f & 1
            peer = ((1 - x) if fx else x, (1 - y) if fy else y, (1 - c) if fc else c)
            cp = pltpu.make_async_remote_copy(
                src_ref=v_ref, dst_ref=land.at[me], send_sem=send_sems.at[f - 1], recv_sem=recv_sems.at[f - 1],
                device_id=peer, device_id_type=MESH)
            cp.start()
            started.append(cp)
        for cp in started:
            cp.wait()
        total = land[0]
        for d in range(1, ndev):
            total = total + land[d]
        o_ref[...] = total

    return pl.pallas_call(
        body, name=name,
        in_specs=[VMEM_WHOLE], out_specs=VMEM_WHOLE,
        out_shape=jax.ShapeDtypeStruct(v.shape, F32),
        scratch_shapes=[pltpu.VMEM((ndev, R, LANES), F32),
                        pltpu.SemaphoreType.DMA((ndev - 1,)), pltpu.SemaphoreType.DMA((ndev - 1,))],
    )(v)


def sum_chips(parts, name):
    _, r, c = parts.shape
    tr = _row_tile(r, (1 << 20) // c)

    def body(p_ref, o_ref):
        total = p_ref[0].astype(F32)
        for s in range(1, N_CHIPS):
            total = total + p_ref[s].astype(F32)
        o_ref[...] = total

    return pl.pallas_call(
        body, name=name, grid=(r // tr,),
        in_specs=[pl.BlockSpec((N_CHIPS, tr, c), lambda i: (0, i, 0))],
        out_specs=pl.BlockSpec((tr, c), lambda i: (i, 0)),
        out_shape=jax.ShapeDtypeStruct((r, c), F32),
        compiler_params=_params("parallel"),
    )(parts)


def _adamw_math(w, g, m, v):
    m = ADAM_B1 * m + (1.0 - ADAM_B1) * g
    v = ADAM_B2 * v + (1.0 - ADAM_B2) * (g * g)
    m_hat = m / (1.0 - ADAM_B1 ** ADAM_STEP)
    v_hat = v / (1.0 - ADAM_B2 ** ADAM_STEP)
    delta = -ADAM_LR * (m_hat / (jnp.sqrt(v_hat) + ADAM_EPS) + ADAM_WD * w)
    return delta, m, v


def adamw(w, m, v, g_parts, name):
    r, c = w.shape
    tr = _row_tile(r, (1 << 19) // c, multiple=8)
    npart = len(g_parts)

    def body(*refs):
        w_ref, m_ref, v_ref = refs[:3]
        parts = refs[3:3 + npart]
        g_ref, d_ref, nm_ref, nv_ref = refs[3 + npart:]
        g = parts[0][...]
        for p in parts[1:]:
            g = g + p[...]
        delta, nm, nv = _adamw_math(w_ref[...], g, m_ref[...], v_ref[...])
        g_ref[...] = g
        d_ref[...] = delta
        nm_ref[...] = nm
        nv_ref[...] = nv

    blk = pl.BlockSpec((tr, c), lambda i: (i, 0))
    shape = jax.ShapeDtypeStruct((r, c), F32)
    return pl.pallas_call(
        body, name=name, grid=(r // tr,),
        in_specs=[blk] * (3 + npart), out_specs=[blk] * 4, out_shape=[shape] * 4,
        compiler_params=_params("parallel"),
    )(w, m, v, *g_parts)


MATRICES = ("ffn1_w_gate", "ffn1_w_up", "ffn1_w_down", "w_in", "w_out", "ffn2_w_gate", "ffn2_w_up", "ffn2_w_down")
VECTORS = ("ffn1_norm", "mix_norm", "swa_sinks", "swa_out_norm", "sb_out_norm", "ffn2_norm", "final_norm")
WEIGHTS = ("ffn1_norm", "ffn1_w_gate", "ffn1_w_up", "ffn1_w_down", "mix_norm", "w_in", "swa_sinks", "swa_out_norm",
           "sb_out_norm", "w_out", "ffn2_norm", "ffn2_w_gate", "ffn2_w_up", "ffn2_w_down", "final_norm")


def _pack_rows(vec):
    flat = vec.reshape(-1).astype(F32)
    rows = -(-flat.shape[0] // LANES)
    rows8 = -(-rows // 8) * 8
    flat = jnp.pad(flat, (0, rows8 * LANES - flat.shape[0]))
    return flat.reshape(rows8, LANES)


def _ffn_backward(tag, dx, x_in, gamma, h, g, u, a, wg, wu, wd):
    T, D = dx.shape
    G, _, Fs = wg.shape
    dg, du = ffn_bwd_act(dx, wd, g, u, tag + "_bwd_act")
    d_wd = wgrad(a, dx, tag + "_dwd", groups=G, m=Fs, n=D,
                 a_block=lambda tk, tmo: (None, tk, tmo), a_map=lambda s, j, k: (s, k, 0),
                 b_block=lambda tk: (tk, D), b_map=lambda s, j, k: (k, 0), b_scale=0.5, tmo=Fs)
    grads = []
    for nm, d_act in (("_dwg", dg), ("_dwu", du)):
        grads.append(wgrad(h, d_act, tag + nm, groups=G, m=D, n=Fs,
                           a_block=lambda tk, tmo: (tk, tmo), a_map=lambda s, j, k: (k, j),
                           b_block=lambda tk: (None, tk, Fs), b_map=lambda s, j, k: (s, k, 0)))
    dh = ffn_bwd_dh(dg, du, wg, wu, tag + "_bwd_dh")
    dx_in, dgamma = rmsnorm_bwd_add(x_in, gamma, dh, dx, tag + "_norm_bwd")
    return dx_in, dgamma, grads[0], grads[1], d_wd


def kernel(x, ffn1_norm, ffn1_w_gate, ffn1_w_up, ffn1_w_down, mix_norm, w_in, swa_sinks, swa_out_norm, sb_out_norm, w_out, ffn2_norm, ffn2_w_gate, ffn2_w_up, ffn2_w_down, final_norm, loss_target, m_ffn1_norm, m_ffn1_w_gate, m_ffn1_w_up, m_ffn1_w_down, m_mix_norm, m_w_in, m_swa_sinks, m_swa_out_norm, m_sb_out_norm, m_w_out, m_ffn2_norm, m_ffn2_w_gate, m_ffn2_w_up, m_ffn2_w_down, m_final_norm, v_ffn1_norm, v_ffn1_w_gate, v_ffn1_w_up, v_ffn1_w_down, v_mix_norm, v_w_in, v_swa_sinks, v_swa_out_norm, v_sb_out_norm, v_w_out, v_ffn2_norm, v_ffn2_w_gate, v_ffn2_w_up, v_ffn2_w_down, v_final_norm):
    given = dict(locals())
    b_loc, seq, D = x.shape
    T = b_loc * seq
    x0 = x.reshape(T, D)
    target = loss_target.reshape(T, D)
    final_g = final_norm.reshape(1, D)

    shards = [given[nm][0].astype(BF16) for nm in MATRICES]
    gathered = dict(zip(MATRICES, chip_exchange(shards, "gather_weights", scatter=False)))
    w_out_all = gathered["w_out"].reshape(-1, D)

    h1 = rmsnorm_cast(x0, ffn1_norm, "ffn1_norm_fwd")
    g1, u1, a1 = ffn_up(h1, gathered["ffn1_w_gate"], gathered["ffn1_w_up"], "ffn1_up")
    x1 = ffn_down(a1, gathered["ffn1_w_down"], x0, "ffn1_down")
    h2 = rmsnorm_cast(x1, mix_norm, "mix_norm_fwd")
    proj = in_proj(h2, gathered["w_in"], "in_proj")
    ya = swa_fwd(proj, swa_sinks, b_loc, seq, "swa_fwd")
    yb, ltot = sb_fwd(proj, b_loc, seq, "sb_fwd")
    y = out_norm(ya, yb, swa_out_norm, sb_out_norm, "out_norm_fwd")
    x2 = out_proj(y, w_out_all, x1, "out_proj")
    h3 = rmsnorm_cast(x2, ffn2_norm, "ffn2_norm_fwd")
    g2, u2, a2 = ffn_up(h3, gathered["ffn2_w_gate"], gathered["ffn2_w_up"], "ffn2_up")
    x3 = ffn_down(a2, gathered["ffn2_w_down"], x2, "ffn2_down")
    dx3, d_final, loss_part = final_loss(x3, final_g, target, "final_loss")

    dx2, d_ffn2_norm, d_g2, d_u2, d_d2 = _ffn_backward(
        "ffn2", dx3, x2, ffn2_norm, h3, g2, u2, a2,
        gathered["ffn2_w_gate"], gathered["ffn2_w_up"], gathered["ffn2_w_down"])
    M = w_out_all.shape[0]
    Os = M // N_CHIPS
    d_w_out = wgrad(y, dx2, "dw_out", groups=N_CHIPS, m=Os, n=D,
                    a_block=lambda tk, tmo: (tk, tmo), a_map=lambda s, j, k: (k, s),
                    b_block=lambda tk: (tk, D), b_map=lambda s, j, k: (k, 0), tmo=Os)
    dya, dyb, d_swa_norm, d_sb_norm = out_proj_bwd(dx2, w_out_all, ya, yb, swa_out_norm, sb_out_norm, "out_proj_bwd")
    dqa, dka, dva, d_sinks = swa_bwd(proj, swa_sinks, dya, b_loc, seq, "swa_bwd")
    dqb, dkb, dvb = sb_bwd(proj, dyb, ltot, b_loc, seq, "sb_bwd")
    dproj = jnp.concatenate([dqa, dka, dva, dqb, dkb, dvb], axis=1)
    Ps = gathered["w_in"].shape[2]
    d_w_in = wgrad(h2, dproj, "dw_in", groups=N_CHIPS, m=D, n=Ps,
                   a_block=lambda tk, tmo: (tk, tmo), a_map=lambda s, j, k: (k, j),
                   b_block=lambda tk: (tk, Ps), b_map=lambda s, j, k: (k, s))
    dh2 = in_proj_bwd(dproj, gathered["w_in"], "in_proj_bwd")
    dx1, d_mix_norm = rmsnorm_bwd_add(x1, mix_norm, dh2, dx2, "mix_norm_bwd")
    dx0, d_ffn1_norm, d_g1, d_u1, d_d1 = _ffn_backward(
        "ffn1", dx1, x0, ffn1_norm, h1, g1, u1, a1,
        gathered["ffn1_w_gate"], gathered["ffn1_w_up"], gathered["ffn1_w_down"])

    full = dict(zip(MATRICES, (d_g1, d_u1, d_d1, d_w_in, d_w_out, d_g2, d_u2, d_d2)))
    landed = chip_exchange([full[nm] for nm in MATRICES], "scatter_grads", scatter=True)
    mine = [sum_chips(p, "sum_" + nm) for nm, p in zip(MATRICES, landed)]
    theirs = sibling_exchange(mine, "swap_sums")

    vec_grads = dict(ffn1_norm=d_ffn1_norm, mix_norm=d_mix_norm, swa_sinks=d_sinks[:, 0], swa_out_norm=d_swa_norm,
                     sb_out_norm=d_sb_norm, ffn2_norm=d_ffn2_norm, final_norm=d_final)
    packed = [_pack_rows(vec_grads[nm]) for nm in VECTORS] + [_pack_rows(loss_part[0, :1])]
    offsets = [0]
    for p in packed:
        offsets.append(offsets[-1] + p.shape[0])
    reduced = allreduce_small(jnp.concatenate(packed, axis=0), "reduce_vectors")
    loss = reduced[offsets[len(VECTORS)], 0]

    out = {}
    for nm, own, sib in zip(MATRICES, mine, theirs):
        shape = given[nm].shape
        res = adamw(given[nm][0], given["m_" + nm][0], given["v_" + nm][0], [own, sib], "adamw_" + nm)
        out[nm] = [r.reshape(shape) for r in res]
    w_rows = jnp.concatenate([_pack_rows(given[nm]) for nm in VECTORS], axis=0)
    m_rows = jnp.concatenate([_pack_rows(given["m_" + nm]) for nm in VECTORS], axis=0)
    v_rows = jnp.concatenate([_pack_rows(given["v_" + nm]) for nm in VECTORS], axis=0)
    g_rows = reduced[:offsets[len(VECTORS)]]
    small = adamw(w_rows, m_rows, v_rows, [g_rows], "adamw_vectors")
    for i, nm in enumerate(VECTORS):
        shape = given[nm].shape
        size = math.prod(shape)
        out[nm] = [r[offsets[i]:offsets[i + 1]].reshape(-1)[:size].reshape(shape) for r in small]

    grad_x = dx0.reshape(b_loc, seq, D)
    return (loss, grad_x, *[out[nm][0] for nm in WEIGHTS], *[out[nm][1] for nm in WEIGHTS],
            *[out[nm][2] for nm in WEIGHTS], *[out[nm][3] for nm in WEIGHTS])
```

```python
import functools
import math

import jax
import jax.numpy as jnp
from jax import lax
from jax.experimental import pallas as pl
from jax.experimental.pallas import tpu as pltpu

F32 = jnp.float32
BF16 = jnp.bfloat16
MESH = pl.DeviceIdType.MESH

EPS = 1e-6
HEAD_DIM = 64
LANES = 128
N_SWA_HEADS = 16
N_SWA_KV = 4
N_SB_HEADS = 16
WINDOW = 128
SWA_BLOCK = 128
SB_BLOCK = 256
N_CHIPS = 4
ATT_SCALE = HEAD_DIM ** -0.5

ADAM_LR = 0.001
ADAM_B1 = 0.9
ADAM_B2 = 0.999
ADAM_EPS = 1e-08
ADAM_WD = 0.01
ADAM_STEP = 10

VMEM_LIMIT = 56 * 1024 * 1024
NEG_BIG = -1e30

ANY = pl.BlockSpec(memory_space=pl.ANY)
VMEM_WHOLE = pl.BlockSpec(memory_space=pltpu.VMEM)


def _row_tile(rows, cap, multiple=16):
    best = None
    for t in range(multiple, min(rows, cap) + 1, multiple):
        if rows % t == 0:
            best = t
    if best is None:
        raise ValueError(f"no row tile for {rows} rows under {cap}")
    return best


def _params(*sem):
    return pltpu.CompilerParams(dimension_semantics=sem, vmem_limit_bytes=VMEM_LIMIT)


def _dot(a, b):
    return jnp.dot(a, b, preferred_element_type=F32)


def _dot_nt(a, b):
    return lax.dot_general(a, b, (((1,), (1,)), ((), ())), preferred_element_type=F32)


def _dot_tn(a, b):
    return lax.dot_general(a, b, (((0,), (0,)), ((), ())), preferred_element_type=F32)


def _sigmoid(x):
    return 1.0 / (1.0 + jnp.exp(-x))


def _split_dot(x, u):
    hi = x.astype(BF16)
    lo = (x - hi.astype(F32)).astype(BF16)
    return _dot(hi, u) + _dot(lo, u)


def _tiles(r, c, max_elems, multiple=16):
    col_opts = [c] + [t for t in (2048, 1024, 512, 256, 128) if t < c and c % t == 0]
    best = None
    for tc in col_opts:
        for tr in [r] + list(range(multiple, r, multiple)):
            if r % tr == 0 and tr * tc <= max_elems and (best is None or tr * tc > best[0] * best[1]):
                best = (tr, tc)
    if best is None:
        raise ValueError(f"no tile for [{r}, {c}] under {max_elems} elements")
    return best


class Rider:
    def __init__(self, arrays, out_shapes, sem_shapes, start, finish):
        self.arrays, self.out_shapes, self.sem_shapes = list(arrays), list(out_shapes), list(sem_shapes)
        self.start, self.finish = start, finish


def _place():
    return lax.axis_index("x"), lax.axis_index("y"), lax.axis_index("c")


def _other_chips(x, y):
    return [(1 - x, y), (x, 1 - y), (1 - x, 1 - y)]


def _remote(src, dst, send_sem, recv_sem, device):
    return pltpu.make_async_remote_copy(src_ref=src, dst_ref=dst, send_sem=send_sem, recv_sem=recv_sem,
                                        device_id=device, device_id_type=MESH)


def gather_rider(shards):
    n = len(shards)

    def copies(ins, outs, sems):
        ici_s, ici_r, d2d_s, d2d_r, own_sem = sems
        x, y, c = _place()
        me = 2 * x + y
        own, ici, fwd = [], [], []
        for k in range(n):
            own.append(pltpu.make_async_copy(ins[k], outs[k].at[me], own_sem.at[k]))
            for j, (px, py) in enumerate(_other_chips(x, y)):
                i = 3 * k + j
                ici.append(_remote(ins[k].at[c], outs[k].at[me, c], ici_s.at[i], ici_r.at[i], (px, py, c)))
                got = outs[k].at[2 * px + py, c]
                fwd.append(_remote(got, got, d2d_s.at[i], d2d_r.at[i], (x, y, 1 - c)))
        return own, ici, fwd

    def start(ins, outs, sems):
        own, ici, _ = copies(ins, outs, sems)
        for cp in own + ici:
            cp.start()

    def finish(ins, outs, sems):
        own, ici, fwd = copies(ins, outs, sems)
        for a, f in zip(ici, fwd):
            a.wait_recv()
            f.start()
        for a in ici:
            a.wait_send()
        for f in fwd:
            f.wait()
        for o in own:
            o.wait()

    dma = pltpu.SemaphoreType.DMA
    return Rider(shards, [jax.ShapeDtypeStruct((N_CHIPS,) + s.shape, s.dtype) for s in shards],
                 [dma((3 * n,)), dma((3 * n,)), dma((3 * n,)), dma((3 * n,)), dma((n,))], start, finish)


def half_swap_rider(grads):
    n = len(grads)

    def copies(ins, outs, sems):
        x, y, c = _place()
        return [_remote(ins[k].at[1 - c], outs[k], sems[0].at[k], sems[1].at[k], (x, y, 1 - c)) for k in range(n)]

    def start(ins, outs, sems):
        for cp in copies(ins, outs, sems):
            cp.start()

    def finish(ins, outs, sems):
        for cp in copies(ins, outs, sems):
            cp.wait()

    dma = pltpu.SemaphoreType.DMA
    return Rider(grads, [jax.ShapeDtypeStruct(g.shape[1:], g.dtype) for g in grads], [dma((n,)), dma((n,))],
                 start, finish)


def scatter_rider(halves):
    n = len(halves)

    def copies(ins, outs, sems):
        send, recv, own_sem = sems
        x, y, c = _place()
        me = 2 * x + y
        cps = []
        for k in range(n):
            cps.append(pltpu.make_async_copy(ins[k].at[me], outs[k].at[me], own_sem.at[k]))
            for j, (px, py) in enumerate(_other_chips(x, y)):
                i = 3 * k + j
                cps.append(_remote(ins[k].at[2 * px + py], outs[k].at[me], send.at[i], recv.at[i], (px, py, c)))
        return cps

    def start(ins, outs, sems):
        for cp in copies(ins, outs, sems):
            cp.start()

    def finish(ins, outs, sems):
        for cp in copies(ins, outs, sems):
            cp.wait()

    dma = pltpu.SemaphoreType.DMA
    return Rider(halves, [jax.ShapeDtypeStruct(h.shape, h.dtype) for h in halves],
                 [dma((3 * n,)), dma((3 * n,)), dma((n,))], start, finish)


def final_swap_rider(sums):
    n = len(sums)

    def copies(ins, outs, sems):
        send, recv, own_sem = sems
        x, y, c = _place()
        cps = []
        for k in range(n):
            cps.append(pltpu.make_async_copy(ins[k], outs[k].at[c], own_sem.at[k]))
            cps.append(_remote(ins[k], outs[k].at[c], send.at[k], recv.at[k], (x, y, 1 - c)))
        return cps

    def start(ins, outs, sems):
        for cp in copies(ins, outs, sems):
            cp.start()

    def finish(ins, outs, sems):
        for cp in copies(ins, outs, sems):
            cp.wait()

    dma = pltpu.SemaphoreType.DMA
    return Rider(sums, [jax.ShapeDtypeStruct((2,) + s.shape, s.dtype) for s in sums],
                 [dma((n,)), dma((n,)), dma((n,))], start, finish)


def _call(body, *, name, grid, in_specs, out_specs, out_shape, scratch_shapes=(), sem, args, riders=()):
    in_specs, out_specs, out_shape = list(in_specs), list(out_specs), list(out_shape)
    scratch_shapes = list(scratch_shapes)
    if not riders:
        outs = pl.pallas_call(body, name=name, grid=grid, in_specs=in_specs, out_specs=out_specs, out_shape=out_shape,
                              scratch_shapes=scratch_shapes, compiler_params=_params(*sem))(*args)
        return list(outs), []
    n_in, n_out, n_scr = len(in_specs), len(out_specs), len(scratch_shapes)
    r_in = [len(r.arrays) for r in riders]
    r_out = [len(r.out_shapes) for r in riders]
    r_sem = [len(r.sem_shapes) for r in riders]

    def wrapped(*refs):
        pos = 0

        def take(count):
            nonlocal pos
            got = refs[pos:pos + count]
            pos += count
            return got

        ins = take(n_in)
        rins = [take(k) for k in r_in]
        outs = take(n_out)
        routs = [take(k) for k in r_out]
        scr = take(n_scr)
        rsems = [take(k) for k in r_sem]
        first = functools.reduce(jnp.logical_and, [pl.program_id(a) == 0 for a in range(len(grid))])
        last = functools.reduce(jnp.logical_and, [pl.program_id(a) == grid[a] - 1 for a in range(len(grid))])

        @pl.when(first)
        def _():
            for r, ri, ro, rs in zip(riders, rins, routs, rsems):
                r.start(ri, ro, rs)

        body(*ins, *outs, *scr)

        @pl.when(last)
        def _():
            for r, ri, ro, rs in zip(riders, rins, routs, rsems):
                r.finish(ri, ro, rs)

    any_spec = pl.BlockSpec(memory_space=pl.ANY)
    res = pl.pallas_call(
        wrapped, name=name, grid=grid,
        in_specs=in_specs + [any_spec] * sum(r_in),
        out_specs=out_specs + [any_spec] * sum(r_out),
        out_shape=out_shape + [s for r in riders for s in r.out_shapes],
        scratch_shapes=scratch_shapes + [s for r in riders for s in r.sem_shapes],
        compiler_params=_params(*(["arbitrary"] * len(grid))),
    )(*args, *[a for r in riders for a in r.arrays])
    res = list(res)
    outs, rest = res[:n_out], res[n_out:]
    routs = []
    for k in r_out:
        routs.append(rest[:k])
        rest = rest[k:]
    return outs, routs


def exchange_now(riders, name):
    n_in = [len(r.arrays) for r in riders]
    n_out = [len(r.out_shapes) for r in riders]
    n_sem = [len(r.sem_shapes) for r in riders]

    def body(*refs):
        pos = 0
        groups = []
        for counts in (n_in, n_out, n_sem):
            part = []
            for k in counts:
                part.append(refs[pos:pos + k])
                pos += k
            groups.append(part)
        for r, ri, ro, rs in zip(riders, *groups):
            r.start(ri, ro, rs)
        for r, ri, ro, rs in zip(riders, *groups):
            r.finish(ri, ro, rs)

    any_spec = pl.BlockSpec(memory_space=pl.ANY)
    res = pl.pallas_call(
        body, name=name,
        in_specs=[any_spec] * sum(n_in), out_specs=[any_spec] * sum(n_out),
        out_shape=[s for r in riders for s in r.out_shapes],
        scratch_shapes=[s for r in riders for s in r.sem_shapes],
    )(*[a for r in riders for a in r.arrays])
    res = list(res)
    routs = []
    for k in n_out:
        routs.append(res[:k])
        res = res[k:]
    return routs


def rmsnorm_cast(x, gamma, name):
    T, D = x.shape
    tr = min(512, T)

    def body(x_ref, g_ref, o_ref):
        xv = x_ref[...]
        r = lax.rsqrt(jnp.mean(xv * xv, axis=-1, keepdims=True) + EPS)
        o_ref[...] = ((xv * r) * g_ref[...]).astype(BF16)

    return pl.pallas_call(
        body, name=name, grid=(T // tr,),
        in_specs=[pl.BlockSpec((tr, D), lambda i: (i, 0)), pl.BlockSpec((1, D), lambda i: (0, 0))],
        out_specs=pl.BlockSpec((tr, D), lambda i: (i, 0)),
        out_shape=jax.ShapeDtypeStruct((T, D), BF16),
        compiler_params=_params("parallel"),
    )(x, gamma)


def _rms_bwd(xv, gamma, dy):
    d = xv.shape[-1]
    r = lax.rsqrt(jnp.mean(xv * xv, axis=-1, keepdims=True) + EPS)
    gdy = dy * gamma
    dot = jnp.sum(gdy * xv, axis=-1, keepdims=True)
    dx = r * gdy - xv * ((r * r * r) * (dot / d))
    dgamma = jnp.sum(dy * (xv * r), axis=0, keepdims=True)
    return dx, dgamma


def rmsnorm_bwd_add(x, gamma, dy, dx_up, name, riders=()):
    T, D = x.shape
    tr = min(256, T)

    def body(x_ref, g_ref, dy_ref, up_ref, dx_ref, dg_ref):
        dx, dgamma = _rms_bwd(x_ref[...], g_ref[...], dy_ref[...])
        dx_ref[...] = up_ref[...] + dx

        @pl.when(pl.program_id(0) == 0)
        def _():
            dg_ref[...] = jnp.zeros_like(dg_ref)

        dg_ref[...] += dgamma

    row = pl.BlockSpec((tr, D), lambda i: (i, 0))
    vec = pl.BlockSpec((1, D), lambda i: (0, 0))
    return _call(
        body, name=name, grid=(T // tr,),
        in_specs=[row, vec, row, row], out_specs=[row, vec],
        out_shape=[jax.ShapeDtypeStruct((T, D), F32), jax.ShapeDtypeStruct((1, D), F32)],
        sem=("arbitrary",), args=(x, gamma, dy, dx_up), riders=riders)


def ffn_up(h, wg, wu, name, riders=()):
    T, D = h.shape
    G, _, Fs = wg.shape
    tm = min(512, T)

    def body(h_ref, wg_ref, wu_ref, g_ref, u_ref, a_ref):
        hv = h_ref[...]
        g = _dot(hv, wg_ref[...])
        u = _dot(hv, wu_ref[...])
        g_ref[...] = g
        u_ref[...] = u
        a_ref[...] = ((g * _sigmoid(g)) * u).astype(BF16)

    w_spec = pl.BlockSpec((None, D, Fs), lambda s, i: (s, 0, 0), pipeline_mode=pl.Buffered(1))
    o_spec = pl.BlockSpec((None, tm, Fs), lambda s, i: (s, i, 0))
    return _call(
        body, name=name, grid=(G, T // tm),
        in_specs=[pl.BlockSpec((tm, D), lambda s, i: (i, 0)), w_spec, w_spec],
        out_specs=[o_spec, o_spec, o_spec],
        out_shape=[jax.ShapeDtypeStruct((G, T, Fs), F32), jax.ShapeDtypeStruct((G, T, Fs), F32),
                   jax.ShapeDtypeStruct((G, T, Fs), BF16)],
        sem=("parallel", "parallel"), args=(h, wg, wu), riders=riders)


def ffn_down(a, wd, x, name, riders=()):
    G, T, Fs = a.shape
    D = wd.shape[2]
    tm = min(512, T)

    def body(a_ref, w_ref, x_ref, o_ref, acc):
        s = pl.program_id(1)

        @pl.when(s == 0)
        def _():
            acc[...] = jnp.zeros_like(acc)

        acc[...] += _dot(a_ref[...], w_ref[...])

        @pl.when(s == G - 1)
        def _():
            o_ref[...] = x_ref[...] + 0.5 * acc[...]

    return _call(
        body, name=name, grid=(T // tm, G),
        in_specs=[pl.BlockSpec((None, tm, Fs), lambda i, s: (s, i, 0)),
                  pl.BlockSpec((None, Fs, D), lambda i, s: (s, 0, 0)),
                  pl.BlockSpec((tm, D), lambda i, s: (i, 0))],
        out_specs=[pl.BlockSpec((tm, D), lambda i, s: (i, 0))],
        out_shape=[jax.ShapeDtypeStruct((T, D), F32)],
        scratch_shapes=[pltpu.VMEM((tm, D), F32)],
        sem=("parallel", "arbitrary"), args=(a, wd, x), riders=riders)


def ffn_bwd_act(dx, wd, g, u, name, riders=()):
    T, D = dx.shape
    G, Fs, _ = wd.shape
    tm = min(256, T)

    def body(dx_ref, w_ref, g_ref, u_ref, dg_ref, du_ref):
        da = _dot_nt((0.5 * dx_ref[...]).astype(BF16), w_ref[...])
        gv = g_ref[...]
        sig = _sigmoid(gv)
        silu = gv * sig
        du_ref[...] = (da * silu).astype(BF16)
        dg_ref[...] = ((da * u_ref[...]) * (sig * (1.0 + gv * (1.0 - sig)))).astype(BF16)

    act = pl.BlockSpec((None, tm, Fs), lambda s, i: (s, i, 0))
    return _call(
        body, name=name, grid=(G, T // tm),
        in_specs=[pl.BlockSpec((tm, D), lambda s, i: (i, 0)),
                  pl.BlockSpec((None, Fs, D), lambda s, i: (s, 0, 0), pipeline_mode=pl.Buffered(1)),
                  act, act],
        out_specs=[act, act],
        out_shape=[jax.ShapeDtypeStruct((G, T, Fs), BF16), jax.ShapeDtypeStruct((G, T, Fs), BF16)],
        sem=("parallel", "parallel"), args=(dx, wd, g, u), riders=riders)


def ffn_bwd_dh(dg, du, wg, wu, name, riders=()):
    G, T, Fs = dg.shape
    D = wg.shape[1]
    tm = min(512, T)

    def body(dg_ref, du_ref, wg_ref, wu_ref, o_ref, acc):
        s = pl.program_id(1)

        @pl.when(s == 0)
        def _():
            acc[...] = jnp.zeros_like(acc)

        acc[...] += _dot_nt(dg_ref[...], wg_ref[...])
        acc[...] += _dot_nt(du_ref[...], wu_ref[...])

        @pl.when(s == G - 1)
        def _():
            o_ref[...] = acc[...]

    act = pl.BlockSpec((None, tm, Fs), lambda i, s: (s, i, 0))
    w_spec = pl.BlockSpec((None, D, Fs), lambda i, s: (s, 0, 0))
    return _call(
        body, name=name, grid=(T // tm, G),
        in_specs=[act, act, w_spec, w_spec],
        out_specs=[pl.BlockSpec((tm, D), lambda i, s: (i, 0))],
        out_shape=[jax.ShapeDtypeStruct((T, D), F32)],
        scratch_shapes=[pltpu.VMEM((tm, D), F32)],
        sem=("parallel", "arbitrary"), args=(dg, du, wg, wu), riders=riders)


def wgrad(a, b, name, *, groups, m, n, a_block, a_map, b_block, b_map, b_scale=None, tk=512, riders=()):
    T = a.shape[-2]
    tk = min(tk, T)
    nk = T // tk

    def body(a_ref, b_ref, o_ref, acc):
        k = pl.program_id(2)

        @pl.when(k == 0)
        def _():
            acc[...] = jnp.zeros_like(acc)

        bv = b_ref[...]
        if b_scale is not None:
            bv = b_scale * bv
        acc[...] += _dot_tn(a_ref[...].astype(BF16), bv.astype(BF16))

        @pl.when(k == nk - 1)
        def _():
            o_ref[...] = acc[...].astype(BF16)

    return _call(
        body, name=name, grid=(groups, 2, nk),
        in_specs=[pl.BlockSpec(a_block(tk), a_map), pl.BlockSpec(b_block(tk), b_map)],
        out_specs=[pl.BlockSpec((None, None, m, n), lambda s, j, k: (j, s, 0, 0))],
        out_shape=[jax.ShapeDtypeStruct((2, groups, m, n), BF16)],
        scratch_shapes=[pltpu.VMEM((m, n), F32)],
        sem=("parallel", "parallel", "arbitrary"), args=(a, b), riders=riders)


def in_proj(h, w_in, name):
    T, D = h.shape
    G, _, Ps = w_in.shape
    tm = min(512, T)

    def body(h_ref, w_ref, o_ref):
        o_ref[...] = _dot(h_ref[...], w_ref[...]).astype(BF16)

    return pl.pallas_call(
        body, name=name, grid=(G, T // tm),
        in_specs=[pl.BlockSpec((tm, D), lambda s, i: (i, 0)),
                  pl.BlockSpec((None, D, Ps), lambda s, i: (s, 0, 0))],
        out_specs=pl.BlockSpec((tm, Ps), lambda s, i: (i, s)),
        out_shape=jax.ShapeDtypeStruct((T, G * Ps), BF16),
        compiler_params=_params("parallel", "parallel"),
    )(h, w_in)


def in_proj_bwd(dproj, w_in, name, riders=()):
    T = dproj.shape[0]
    G, D, Ps = w_in.shape
    tm = min(512, T)

    def body(d_ref, w_ref, o_ref, acc):
        s = pl.program_id(1)

        @pl.when(s == 0)
        def _():
            acc[...] = jnp.zeros_like(acc)

        acc[...] += _dot_nt(d_ref[...], w_ref[...])

        @pl.when(s == G - 1)
        def _():
            o_ref[...] = acc[...]

    return _call(
        body, name=name, grid=(T // tm, G),
        in_specs=[pl.BlockSpec((tm, Ps), lambda i, s: (i, s)),
                  pl.BlockSpec((None, D, Ps), lambda i, s: (s, 0, 0))],
        out_specs=[pl.BlockSpec((tm, D), lambda i, s: (i, 0))],
        out_shape=[jax.ShapeDtypeStruct((T, D), F32)],
        scratch_shapes=[pltpu.VMEM((tm, D), F32)],
        sem=("parallel", "arbitrary"), args=(dproj, w_in), riders=riders)


def out_proj(y, w_out, x, name):
    T, M = y.shape
    D = w_out.shape[1]
    tm = min(512, T)

    def body(y_ref, w_ref, x_ref, o_ref):
        o_ref[...] = x_ref[...] + _dot(y_ref[...], w_ref[...])

    return pl.pallas_call(
        body, name=name, grid=(T // tm,),
        in_specs=[pl.BlockSpec((tm, M), lambda i: (i, 0)), pl.BlockSpec((M, D), lambda i: (0, 0)),
                  pl.BlockSpec((tm, D), lambda i: (i, 0))],
        out_specs=pl.BlockSpec((tm, D), lambda i: (i, 0)),
        out_shape=jax.ShapeDtypeStruct((T, D), F32),
        compiler_params=_params("parallel"),
    )(y, w_out, x)


def out_norm(ya, yb, ga, gb, name):
    T, W = ya.shape
    tr = min(512, T)

    def body(a_ref, b_ref, ga_ref, gb_ref, o_ref):
        for src, gam, col in ((a_ref, ga_ref, 0), (b_ref, gb_ref, W)):
            v = src[...]
            r = lax.rsqrt(jnp.mean(v * v, axis=-1, keepdims=True) + EPS)
            o_ref[:, col:col + W] = ((v * r) * gam[...]).astype(BF16)

    row = pl.BlockSpec((tr, W), lambda i: (i, 0))
    vec = pl.BlockSpec((1, W), lambda i: (0, 0))
    return pl.pallas_call(
        body, name=name, grid=(T // tr,),
        in_specs=[row, row, vec, vec],
        out_specs=pl.BlockSpec((tr, 2 * W), lambda i: (i, 0)),
        out_shape=jax.ShapeDtypeStruct((T, 2 * W), BF16),
        compiler_params=_params("parallel"),
    )(ya, yb, ga, gb)


def out_proj_bwd(dx, w_out, ya, yb, ga, gb, name):
    T, D = dx.shape
    M = w_out.shape[0]
    W = M // 2
    tm = min(256, T)

    def body(dx_ref, w_ref, a_ref, b_ref, ga_ref, gb_ref, da_ref, db_ref, dga_ref, dgb_ref):
        dy = _dot_nt(dx_ref[...].astype(BF16), w_ref[...])

        @pl.when(pl.program_id(0) == 0)
        def _():
            dga_ref[...] = jnp.zeros_like(dga_ref)
            dgb_ref[...] = jnp.zeros_like(dgb_ref)

        d_a, dg_a = _rms_bwd(a_ref[...], ga_ref[...], dy[:, :W])
        d_b, dg_b = _rms_bwd(b_ref[...], gb_ref[...], dy[:, W:])
        da_ref[...] = d_a
        db_ref[...] = d_b
        dga_ref[...] += dg_a
        dgb_ref[...] += dg_b

    row = pl.BlockSpec((tm, W), lambda i: (i, 0))
    vec = pl.BlockSpec((1, W), lambda i: (0, 0))
    return pl.pallas_call(
        body, name=name, grid=(T // tm,),
        in_specs=[pl.BlockSpec((tm, D), lambda i: (i, 0)), pl.BlockSpec((M, D), lambda i: (0, 0)),
                  row, row, vec, vec],
        out_specs=[row, row, vec, vec],
        out_shape=[jax.ShapeDtypeStruct((T, W), F32), jax.ShapeDtypeStruct((T, W), F32),
                   jax.ShapeDtypeStruct((1, W), F32), jax.ShapeDtypeStruct((1, W), F32)],
        compiler_params=_params("arbitrary"),
    )(dx, w_out, ya, yb, ga, gb)


def final_loss(x, gamma, target, name):
    T, D = x.shape
    tr = min(256, T)

    def body(x_ref, g_ref, t_ref, dx_ref, dg_ref, loss_ref):
        xv = x_ref[...]
        gam = g_ref[...]
        r = lax.rsqrt(jnp.mean(xv * xv, axis=-1, keepdims=True) + EPS)
        err = (xv * r) * gam - t_ref[...]
        part = 0.5 * jnp.sum(jnp.mean(err * err, axis=-1, keepdims=True), axis=0, keepdims=True)
        dx, dgamma = _rms_bwd(xv, gam, err / D)
        dx_ref[...] = dx

        @pl.when(pl.program_id(0) == 0)
        def _():
            dg_ref[...] = jnp.zeros_like(dg_ref)
            loss_ref[...] = jnp.zeros_like(loss_ref)

        dg_ref[...] += dgamma
        loss_ref[...] += jnp.broadcast_to(part, loss_ref.shape)

    row = pl.BlockSpec((tr, D), lambda i: (i, 0))
    vec = pl.BlockSpec((1, D), lambda i: (0, 0))
    return pl.pallas_call(
        body, name=name, grid=(T // tr,),
        in_specs=[row, vec, row],
        out_specs=[row, vec, pl.BlockSpec((1, LANES), lambda i: (0, 0))],
        out_shape=[jax.ShapeDtypeStruct((T, D), F32), jax.ShapeDtypeStruct((1, D), F32),
                   jax.ShapeDtypeStruct((1, LANES), F32)],
        compiler_params=_params("arbitrary"),
    )(x, gamma, target)


def _lane_select(src_half, dst_halves):
    r = lax.broadcasted_iota(jnp.int32, (LANES, LANES), 0)
    c = lax.broadcasted_iota(jnp.int32, (LANES, LANES), 1)
    hit = ((r >> 6) == src_half) & ((r & (HEAD_DIM - 1)) == (c & (HEAD_DIM - 1)))
    ok = functools.reduce(jnp.logical_or, [(c >> 6) == d for d in dst_halves])
    return jnp.where(hit & ok, 1.0, 0.0).astype(BF16)


def _swa_slope(kvp, local_head):
    lo = 2.0 ** (-8.0 * (local_head + 1) / N_SWA_HEADS)
    hi = 2.0 ** (-8.0 * (8 + local_head + 1) / N_SWA_HEADS)
    return jnp.where(kvp == 0, jnp.float32(lo), jnp.float32(hi))


def _swa_probs(qblk, kwin, n, start, slope, sink):
    z = _dot_nt(qblk, kwin) * ATT_SCALE
    t = n * SWA_BLOCK + lax.broadcasted_iota(jnp.int32, z.shape, 0)
    s = start + lax.broadcasted_iota(jnp.int32, z.shape, 1)
    dist = t - s
    valid = (dist >= 0) & (dist < WINDOW)
    z = jnp.where(valid, z - slope * dist.astype(F32), NEG_BIG)
    m = jnp.maximum(jnp.max(z, axis=-1, keepdims=True), sink)
    e = jnp.exp(z - m)
    den = jnp.sum(e, axis=-1, keepdims=True) + jnp.exp(sink - m)
    return e / den, m, den


def swa_fwd(proj, sinks, b_loc, seq, name):
    T = proj.shape[0]
    nblk = seq // SWA_BLOCK

    def body(sink_ref, q_ref, k_ref, v_ref, o_ref, qs, vv):
        kvp = pl.program_id(1)
        lane = lax.broadcasted_iota(jnp.int32, (SWA_BLOCK, LANES), 1)
        for kvh in range(2):
            both = _lane_select(kvh, (0, 1))
            vv[...] = _dot(v_ref[...], both).astype(BF16)
            for qp in (2 * kvh, 2 * kvh + 1):
                cols = slice(LANES * qp, LANES * (qp + 1))
                for a in range(2):
                    qs[a] = _dot(q_ref[:, cols], _lane_select(a, (kvh,))).astype(BF16)
                slopes = [_swa_slope(kvp, 2 * qp + a) for a in range(2)]
                snk = [sink_ref[0, 8 * kvp + 2 * qp + a] for a in range(2)]

                def blk(n, carry):
                    start = pl.multiple_of(jnp.maximum(n - 1, 0) * SWA_BLOCK, SWA_BLOCK)
                    rows = pl.ds(pl.multiple_of(n * SWA_BLOCK, SWA_BLOCK), SWA_BLOCK)
                    win = pl.ds(start, 2 * SWA_BLOCK)
                    outs = []
                    for a in range(2):
                        p, _, _ = _swa_probs(qs[a, rows, :], k_ref[win, :], n, start, slopes[a], snk[a])
                        outs.append(_dot(p.astype(BF16), vv[win, :]))
                    o_ref[rows, cols] = jnp.where(lane < HEAD_DIM, outs[0], outs[1])
                    return carry

                lax.fori_loop(0, nblk, blk, 0)

    return pl.pallas_call(
        body, name=name, grid=(b_loc, 2),
        in_specs=[pl.BlockSpec(memory_space=pltpu.SMEM),
                  pl.BlockSpec((seq, 4 * LANES), lambda b, p: (b, p)),
                  pl.BlockSpec((seq, LANES), lambda b, p: (b, 8 + p)),
                  pl.BlockSpec((seq, LANES), lambda b, p: (b, 10 + p))],
        out_specs=pl.BlockSpec((seq, 4 * LANES), lambda b, p: (b, p)),
        out_shape=jax.ShapeDtypeStruct((T, N_SWA_HEADS * HEAD_DIM), F32),
        scratch_shapes=[pltpu.VMEM((2, seq, LANES), BF16), pltpu.VMEM((seq, LANES), BF16)],
        compiler_params=_params("parallel", "parallel"),
    )(sinks, proj, proj, proj)


def swa_bwd(proj, sinks, dya, b_loc, seq, name, riders=()):
    T = proj.shape[0]
    nblk = seq // SWA_BLOCK

    def body(sink_ref, q_ref, k_ref, v_ref, do_ref, dq_ref, dk_ref, dv_ref, ds_ref, qs, dos, kk, dk_acc, dv_acc):
        kvp = pl.program_id(0)
        b = pl.program_id(1)
        lane = lax.broadcasted_iota(jnp.int32, (SWA_BLOCK, LANES), 1)

        @pl.when(b == 0)
        def _():
            ds_ref[...] = jnp.zeros_like(ds_ref)

        dk_acc[...] = jnp.zeros_like(dk_acc)
        dv_acc[...] = jnp.zeros_like(dv_acc)
        for kvh in range(2):
            kk[...] = _dot(k_ref[...], _lane_select(kvh, (0, 1))).astype(BF16)
            for qp in (2 * kvh, 2 * kvh + 1):
                cols = slice(LANES * qp, LANES * (qp + 1))
                for a in range(2):
                    move = _lane_select(a, (kvh,))
                    qs[a] = _dot(q_ref[:, cols], move).astype(BF16)
                    dos[a] = _dot(do_ref[:, cols].astype(BF16), move).astype(BF16)
                slopes = [_swa_slope(kvp, 2 * qp + a) for a in range(2)]
                snk = [sink_ref[0, 8 * kvp + 2 * qp + a] for a in range(2)]

                def blk(n, carry):
                    start = pl.multiple_of(jnp.maximum(n - 1, 0) * SWA_BLOCK, SWA_BLOCK)
                    rows = pl.ds(pl.multiple_of(n * SWA_BLOCK, SWA_BLOCK), SWA_BLOCK)
                    win = pl.ds(start, 2 * SWA_BLOCK)
                    dqs = []
                    for a in range(2):
                        qblk = qs[a, rows, :]
                        doblk = dos[a, rows, :]
                        p, m, den = _swa_probs(qblk, k_ref[win, :], n, start, slopes[a], snk[a])
                        dp = _dot_nt(doblk, v_ref[win, :])
                        delta = jnp.sum(p * dp, axis=-1, keepdims=True)
                        p_sink = jnp.exp(snk[a] - m) / den
                        head = 2 * qp + a
                        ds_ref[head:head + 1, :] -= jnp.broadcast_to(
                            jnp.sum(p_sink * delta, axis=0, keepdims=True), (1, LANES))
                        dsc = ((p * (dp - delta)) * ATT_SCALE).astype(BF16)
                        dv_acc[win, :] += _dot_tn(p.astype(BF16), doblk)
                        dk_acc[win, :] += _dot_tn(dsc, qblk)
                        dqs.append(_dot(dsc, kk[win, :]))
                    dq_ref[rows, cols] = jnp.where(lane < HEAD_DIM, dqs[0], dqs[1]).astype(BF16)
                    return carry

                lax.fori_loop(0, nblk, blk, 0)
        dk_ref[...] = dk_acc[...].astype(BF16)
        dv_ref[...] = dv_acc[...].astype(BF16)

    return _call(
        body, name=name, grid=(2, b_loc),
        in_specs=[pl.BlockSpec(memory_space=pltpu.SMEM),
                  pl.BlockSpec((seq, 4 * LANES), lambda p, b: (b, p)),
                  pl.BlockSpec((seq, LANES), lambda p, b: (b, 8 + p)),
                  pl.BlockSpec((seq, LANES), lambda p, b: (b, 10 + p)),
                  pl.BlockSpec((seq, 4 * LANES), lambda p, b: (b, p))],
        out_specs=[pl.BlockSpec((seq, 4 * LANES), lambda p, b: (b, p)),
                   pl.BlockSpec((seq, LANES), lambda p, b: (b, p)),
                   pl.BlockSpec((seq, LANES), lambda p, b: (b, p)),
                   pl.BlockSpec((8, LANES), lambda p, b: (p, 0))],
        out_shape=[jax.ShapeDtypeStruct((T, N_SWA_HEADS * HEAD_DIM), BF16),
                   jax.ShapeDtypeStruct((T, N_SWA_KV * HEAD_DIM), BF16),
                   jax.ShapeDtypeStruct((T, N_SWA_KV * HEAD_DIM), BF16),
                   jax.ShapeDtypeStruct((N_SWA_HEADS, LANES), F32)],
        scratch_shapes=[pltpu.VMEM((2, seq, LANES), BF16), pltpu.VMEM((2, seq, LANES), BF16),
                        pltpu.VMEM((seq, LANES), BF16),
                        pltpu.VMEM((seq, LANES), F32), pltpu.VMEM((seq, LANES), F32)],
        sem=("arbitrary", "arbitrary"), args=(sinks, proj, proj, proj, dya), riders=riders)


def _softplus(z):
    return jnp.maximum(z, 0.0) + jnp.log(1.0 + jnp.exp(-jnp.abs(z)))


def _tri(n, rel):
    r = lax.broadcasted_iota(jnp.int32, (n, n), 0)
    c = lax.broadcasted_iota(jnp.int32, (n, n), 1)
    return jnp.where(rel(r, c), 1.0, 0.0).astype(BF16)


def sb_fwd(proj, b_loc, seq, name, riders=()):
    T = proj.shape[0]
    bq = min(SB_BLOCK, seq)
    nq = seq // bq
    npair = N_SB_HEADS // 2

    def body(q_ref, k_ref, v_ref, o_ref, lt_ref):
        i = pl.program_id(2)
        lane = lax.broadcasted_iota(jnp.int32, (bq, LANES), 1)
        after = _tri(bq, lambda r, c: r > c)
        tpos = i * bq + lax.broadcasted_iota(jnp.int32, (bq, bq), 0)
        col = lax.broadcasted_iota(jnp.int32, (bq, bq), 1)
        outs, tots = [], []
        for h in range(2):
            qh = jnp.where((lane >> 6) == h, q_ref[...], jnp.zeros((bq, LANES), BF16))

            def kblock(jj, carry):
                run, acc = carry
                j = i - jj
                rows = pl.ds(pl.multiple_of(j * bq, bq), bq)
                z = _dot_nt(qh, k_ref[rows, :]) * ATT_SCALE
                mask = (j * bq + col) < tpos
                sp = _softplus(z)
                lg = jnp.where(mask, -sp, 0.0)
                c_after = _split_dot(lg, after) + run
                a = jnp.where(mask, jnp.exp((z - sp) + c_after), 0.0)
                acc = acc + _dot(a.astype(BF16), v_ref[rows, :])
                return run + jnp.sum(lg, axis=-1, keepdims=True), acc

            run, acc = lax.fori_loop(0, i + 1, kblock,
                                     (jnp.zeros((bq, 1), F32), jnp.zeros((bq, LANES), F32)))
            outs.append(acc)
            tots.append(run)
        o_ref[...] = jnp.where(lane < HEAD_DIM, outs[0], outs[1])
        lt_ref[...] = jnp.where(lane < HEAD_DIM, tots[0], tots[1])

    return _call(
        body, name=name, grid=(b_loc, npair, nq),
        in_specs=[pl.BlockSpec((bq, LANES), lambda b, p, i: (b * nq + i, 12 + p)),
                  pl.BlockSpec((seq, LANES), lambda b, p, i: (b, 20 + p)),
                  pl.BlockSpec((seq, LANES), lambda b, p, i: (b, 28 + p))],
        out_specs=[pl.BlockSpec((bq, LANES), lambda b, p, i: (b * nq + i, p)),
                   pl.BlockSpec((None, bq, LANES), lambda b, p, i: (b * npair + p, i, 0))],
        out_shape=[jax.ShapeDtypeStruct((T, N_SB_HEADS * HEAD_DIM), F32),
                   jax.ShapeDtypeStruct((b_loc * npair, seq, LANES), F32)],
        sem=("parallel", "parallel", "parallel"), args=(proj, proj, proj), riders=riders)


def sb_bwd(proj, dyb, ltot, b_loc, seq, name, riders=()):
    T = proj.shape[0]
    bq = min(SB_BLOCK, seq)
    nq = seq // bq
    npair = N_SB_HEADS // 2

    def body(q_ref, k_ref, v_ref, do_ref, lt_ref, dq_ref, dk_ref, dv_ref, dq_acc, dk_acc, dv_acc):
        lane = lax.broadcasted_iota(jnp.int32, (bq, LANES), 1)
        upto = _tri(bq, lambda r, c: r <= c)
        before = _tri(bq, lambda r, c: r < c)
        row = lax.broadcasted_iota(jnp.int32, (bq, bq), 0)
        col = lax.broadcasted_iota(jnp.int32, (bq, bq), 1)
        dq_acc[...] = jnp.zeros_like(dq_acc)
        dk_acc[...] = jnp.zeros_like(dk_acc)
        dv_acc[...] = jnp.zeros_like(dv_acc)
        for h in range(2):
            mine = (lane >> 6) == h

            def qblock(i, carry):
                qrows = pl.ds(pl.multiple_of(i * bq, bq), bq)
                qh = jnp.where(mine, q_ref[qrows, :], jnp.zeros((bq, LANES), BF16))
                doh = jnp.where(mine, do_ref[qrows, :], 0.0).astype(BF16)
                total = jnp.sum(jnp.where(lane == h * HEAD_DIM, lt_ref[qrows, :], 0.0), axis=-1, keepdims=True)
                tpos = i * bq + row

                def kblock(j, inner):
                    l_pre, g_pre, dq = inner
                    krows = pl.ds(pl.multiple_of(j * bq, bq), bq)
                    kj = k_ref[krows, :]
                    z = _dot_nt(qh, kj) * ATT_SCALE
                    mask = (j * bq + col) < tpos
                    sp = _softplus(z)
                    lg = jnp.where(mask, -sp, 0.0)
                    log_beta = z - sp
                    c_after = total - (_split_dot(lg, upto) + l_pre)
                    a = jnp.where(mask, jnp.exp(log_beta + c_after), 0.0)
                    g = a * _dot_nt(doh, v_ref[krows, :])
                    p = _split_dot(g, before) + g_pre
                    beta = jnp.exp(log_beta)
                    dz = (jnp.where(mask, g * (1.0 - beta) - beta * p, 0.0) * ATT_SCALE).astype(BF16)
                    dv_acc[krows, :] += _dot_tn(a.astype(BF16), doh)
                    dk_acc[krows, :] += _dot_tn(dz, qh)
                    dq = dq + _dot(dz, jnp.where(mine, kj, jnp.zeros_like(kj)))
                    return (l_pre + jnp.sum(lg, axis=-1, keepdims=True),
                            g_pre + jnp.sum(g, axis=-1, keepdims=True), dq)

                zero = jnp.zeros((bq, 1), F32)
                _, _, dq = lax.fori_loop(0, i + 1, kblock, (zero, zero, jnp.zeros((bq, LANES), F32)))
                dq_acc[qrows, :] += dq
                return carry

            lax.fori_loop(0, nq, qblock, 0)
        dq_ref[...] = dq_acc[...].astype(BF16)
        dk_ref[...] = dk_acc[...].astype(BF16)
        dv_ref[...] = dv_acc[...].astype(BF16)

    pair = pl.BlockSpec((seq, LANES), lambda b, p: (b, p))
    shape = jax.ShapeDtypeStruct((T, N_SB_HEADS * HEAD_DIM), BF16)
    return _call(
        body, name=name, grid=(b_loc, npair),
        in_specs=[pl.BlockSpec((seq, LANES), lambda b, p: (b, 12 + p)),
                  pl.BlockSpec((seq, LANES), lambda b, p: (b, 20 + p)),
                  pl.BlockSpec((seq, LANES), lambda b, p: (b, 28 + p)),
                  pair,
                  pl.BlockSpec((None, seq, LANES), lambda b, p: (b * npair + p, 0, 0))],
        out_specs=[pair, pair, pair],
        out_shape=[shape, shape, shape],
        scratch_shapes=[pltpu.VMEM((seq, LANES), F32)] * 3,
        sem=("parallel", "parallel"), args=(proj, proj, proj, dyb, ltot), riders=riders)


def allreduce_small(v, name):
    R = v.shape[0]
    ndev = 8

    def body(v_ref, o_ref, land, send_sems, recv_sems):
        x, y, c = _place()
        me = 4 * x + 2 * y + c
        land[me] = v_ref[...]
        started = []
        for f in range(1, ndev):
            fx, fy, fc = (f >> 2) & 1, (f >> 1) & 1, f & 1
            peer = ((1 - x) if fx else x, (1 - y) if fy else y, (1 - c) if fc else c)
            cp = pltpu.make_async_remote_copy(
                src_ref=v_ref, dst_ref=land.at[me], send_sem=send_sems.at[f - 1], recv_sem=recv_sems.at[f - 1],
                device_id=peer, device_id_type=MESH)
            cp.start()
            started.append(cp)
        for cp in started:
            cp.wait()
        total = land[0]
        for d in range(1, ndev):
            total = total + land[d]
        o_ref[...] = total

    return pl.pallas_call(
        body, name=name,
        in_specs=[VMEM_WHOLE], out_specs=VMEM_WHOLE,
        out_shape=jax.ShapeDtypeStruct(v.shape, F32),
        scratch_shapes=[pltpu.VMEM((ndev, R, LANES), F32),
                        pltpu.SemaphoreType.DMA((ndev - 1,)), pltpu.SemaphoreType.DMA((ndev - 1,))],
    )(v)


def sum_cores(grads, theirs, name):
    _, _, r, c = grads.shape
    tr, tc = _tiles(r, c, 1 << 18)
    core = lax.axis_index("c").astype(jnp.int32).reshape(1)

    def body(core_ref, g_ref, t_ref, o_ref):
        o_ref[...] = (g_ref[...].astype(F32) + t_ref[...].astype(F32)).astype(BF16)

    blk = pl.BlockSpec((N_CHIPS, tr, tc), lambda i, j, core_ref: (0, i, j))
    return pl.pallas_call(
        body, name=name,
        grid_spec=pltpu.PrefetchScalarGridSpec(
            num_scalar_prefetch=1, grid=(r // tr, c // tc),
            in_specs=[pl.BlockSpec((None, N_CHIPS, tr, tc), lambda i, j, core_ref: (core_ref[0], 0, i, j)), blk],
            out_specs=blk),
        out_shape=jax.ShapeDtypeStruct((N_CHIPS, r, c), BF16),
        compiler_params=_params("parallel", "parallel"),
    )(core, grads, theirs)


def sum_chips(parts, name):
    _, r, c = parts.shape
    tr, tc = _tiles(r, c, 1 << 18)

    def body(p_ref, o_ref):
        total = p_ref[0].astype(F32)
        for s in range(1, N_CHIPS):
            total = total + p_ref[s].astype(F32)
        o_ref[...] = total

    return pl.pallas_call(
        body, name=name, grid=(r // tr, c // tc),
        in_specs=[pl.BlockSpec((N_CHIPS, tr, tc), lambda i, j: (0, i, j))],
        out_specs=pl.BlockSpec((tr, tc), lambda i, j: (i, j)),
        out_shape=jax.ShapeDtypeStruct((r, c), F32),
        compiler_params=_params("parallel", "parallel"),
    )(parts)


def _adamw_math(w, g, m, v):
    m = ADAM_B1 * m + (1.0 - ADAM_B1) * g
    v = ADAM_B2 * v + (1.0 - ADAM_B2) * (g * g)
    m_hat = m / (1.0 - ADAM_B1 ** ADAM_STEP)
    v_hat = v / (1.0 - ADAM_B2 ** ADAM_STEP)
    delta = -ADAM_LR * (m_hat / (jnp.sqrt(v_hat) + ADAM_EPS) + ADAM_WD * w)
    return delta, m, v


def adamw(w, m, v, g, name, col_halves=False):
    r, c = w.shape
    half = c // 2 if col_halves else c
    tr, tc = _tiles(r, half, 1 << 18, multiple=8)
    per_half = half // tc

    def body(w_ref, m_ref, v_ref, gin_ref, g_ref, d_ref, nm_ref, nv_ref):
        g = gin_ref[...]
        delta, nm, nv = _adamw_math(w_ref[...], g, m_ref[...], v_ref[...])
        g_ref[...] = g
        d_ref[...] = delta
        nm_ref[...] = nm
        nv_ref[...] = nv

    blk = pl.BlockSpec((tr, tc), lambda i, j: (i, j))
    if col_halves:
        g_spec = pl.BlockSpec((None, tr, tc), lambda i, j: (j // per_half, i, j % per_half))
    else:
        g_spec = blk
    shape = jax.ShapeDtypeStruct((r, c), F32)
    return pl.pallas_call(
        body, name=name, grid=(r // tr, c // tc),
        in_specs=[blk, blk, blk, g_spec], out_specs=[blk] * 4, out_shape=[shape] * 4,
        compiler_params=_params("parallel", "parallel"),
    )(w, m, v, g)


MATRICES = ("ffn1_w_gate", "ffn1_w_up", "ffn1_w_down", "w_in", "w_out", "ffn2_w_gate", "ffn2_w_up", "ffn2_w_down")
VECTORS = ("ffn1_norm", "mix_norm", "swa_sinks", "swa_out_norm", "sb_out_norm", "ffn2_norm", "final_norm")
WEIGHTS = ("ffn1_norm", "ffn1_w_gate", "ffn1_w_up", "ffn1_w_down", "mix_norm", "w_in", "swa_sinks", "swa_out_norm",
           "sb_out_norm", "w_out", "ffn2_norm", "ffn2_w_gate", "ffn2_w_up", "ffn2_w_down", "final_norm")


def _pack_rows(vec):
    flat = vec.reshape(-1).astype(F32)
    rows = -(-flat.shape[0] // LANES)
    rows8 = -(-rows // 8) * 8
    flat = jnp.pad(flat, (0, rows8 * LANES - flat.shape[0]))
    return flat.reshape(rows8, LANES)


class GradPipe:
    def __init__(self, name, grads):
        self.name, self.step, self.value = name, 0, grads

    def rider(self):
        make = (half_swap_rider, scatter_rider, final_swap_rider)[self.step]
        return make([self.value])

    def landed(self, outs):
        if self.step == 0:
            self.value = sum_cores(self.value, outs[0], "sum_cores_" + self.name)
        elif self.step == 1:
            self.value = sum_chips(outs[0], "sum_chips_" + self.name)
        else:
            self.value = outs[0]
        self.step += 1


def _hosted(fn, *args, advance=(), **kw):
    outs, routs = fn(*args, riders=[p.rider() for p in advance], **kw)
    for p, ro in zip(advance, routs):
        p.landed(ro)
    return outs


def kernel(x, ffn1_norm, ffn1_w_gate, ffn1_w_up, ffn1_w_down, mix_norm, w_in, swa_sinks, swa_out_norm, sb_out_norm, w_out, ffn2_norm, ffn2_w_gate, ffn2_w_up, ffn2_w_down, final_norm, loss_target, m_ffn1_norm, m_ffn1_w_gate, m_ffn1_w_up, m_ffn1_w_down, m_mix_norm, m_w_in, m_swa_sinks, m_swa_out_norm, m_sb_out_norm, m_w_out, m_ffn2_norm, m_ffn2_w_gate, m_ffn2_w_up, m_ffn2_w_down, m_final_norm, v_ffn1_norm, v_ffn1_w_gate, v_ffn1_w_up, v_ffn1_w_down, v_mix_norm, v_w_in, v_swa_sinks, v_swa_out_norm, v_sb_out_norm, v_w_out, v_ffn2_norm, v_ffn2_w_gate, v_ffn2_w_up, v_ffn2_w_down, v_final_norm):
    given = dict(locals())
    b_loc, seq, D = x.shape
    T = b_loc * seq
    x0 = x.reshape(T, D)
    target = loss_target.reshape(T, D)
    final_g = final_norm.reshape(1, D)

    gathered = {}

    def gather(names):
        shards = []
        for nm in names:
            w = given[nm][0].astype(BF16)
            shards.append(w.reshape(2, w.shape[0] // 2, w.shape[1]))
        return gather_rider(shards)

    def keep(names, outs):
        for nm, o in zip(names, outs):
            gathered[nm] = o.reshape(N_CHIPS, 2 * o.shape[2], o.shape[3])

    first = ("ffn1_w_gate", "ffn1_w_up")
    second = ("ffn1_w_down", "w_in")
    third = ("w_out",)
    fourth = ("ffn2_w_gate", "ffn2_w_up", "ffn2_w_down")
    keep(first, exchange_now([gather(first)], "gather_ffn1_up")[0])

    h1 = rmsnorm_cast(x0, ffn1_norm, "ffn1_norm_fwd")
    (g1, u1, a1), landed = ffn_up(h1, gathered["ffn1_w_gate"], gathered["ffn1_w_up"], "ffn1_up",
                                  riders=[gather(second)])
    keep(second, landed[0])
    (x1,), landed = ffn_down(a1, gathered["ffn1_w_down"], x0, "ffn1_down", riders=[gather(third)])
    keep(third, landed[0])
    w_out_all = gathered["w_out"].reshape(-1, D)
    h2 = rmsnorm_cast(x1, mix_norm, "mix_norm_fwd")
    proj = in_proj(h2, gathered["w_in"], "in_proj")
    ya = swa_fwd(proj, swa_sinks, b_loc, seq, "swa_fwd")
    (yb, ltot), landed = sb_fwd(proj, b_loc, seq, "sb_fwd", riders=[gather(fourth)])
    keep(fourth, landed[0])
    y = out_norm(ya, yb, swa_out_norm, sb_out_norm, "out_norm_fwd")
    x2 = out_proj(y, w_out_all, x1, "out_proj")
    h3 = rmsnorm_cast(x2, ffn2_norm, "ffn2_norm_fwd")
    (g2, u2, a2), _ = ffn_up(h3, gathered["ffn2_w_gate"], gathered["ffn2_w_up"], "ffn2_up")
    (x3,), _ = ffn_down(a2, gathered["ffn2_w_down"], x2, "ffn2_down")
    dx3, d_final, loss_part = final_loss(x3, final_g, target, "final_loss")

    Fs = gathered["ffn1_w_gate"].shape[2]
    Ps = gathered["w_in"].shape[2]
    Os = w_out_all.shape[0] // N_CHIPS
    pipes = {}

    def down_grad(tag, act, dx, advance):
        (grads,) = _hosted(wgrad, act, dx, tag + "_dwd", groups=N_CHIPS, m=Fs, n=D // 2,
                           a_block=lambda tk: (None, tk, Fs), a_map=lambda s, j, k: (s, k, 0),
                           b_block=lambda tk: (tk, D // 2), b_map=lambda s, j, k: (k, j), b_scale=0.5,
                           advance=advance)
        return grads

    def up_grad(name, h, d_act, advance):
        (grads,) = _hosted(wgrad, h, d_act, name, groups=N_CHIPS, m=D // 2, n=Fs,
                           a_block=lambda tk: (tk, D // 2), a_map=lambda s, j, k: (k, j),
                           b_block=lambda tk: (None, tk, Fs), b_map=lambda s, j, k: (s, k, 0), advance=advance)
        return grads

    (dg2, du2), _ = ffn_bwd_act(dx3, gathered["ffn2_w_down"], g2, u2, "ffn2_bwd_act")
    pipes["ffn2_w_down"] = p_d2 = GradPipe("ffn2_w_down", down_grad("ffn2", a2, dx3, []))
    pipes["ffn2_w_gate"] = p_g2 = GradPipe("ffn2_w_gate", up_grad("ffn2_dwg", h3, dg2, [p_d2]))
    pipes["ffn2_w_up"] = p_u2 = GradPipe("ffn2_w_up", up_grad("ffn2_dwu", h3, du2, [p_d2, p_g2]))
    (dh3,) = _hosted(ffn_bwd_dh, dg2, du2, gathered["ffn2_w_gate"], gathered["ffn2_w_up"], "ffn2_bwd_dh",
                     advance=[p_d2, p_g2, p_u2])
    (dx2, d_ffn2_norm), _ = rmsnorm_bwd_add(x2, ffn2_norm, dh3, dx3, "ffn2_norm_bwd")

    (d_w_out,) = _hosted(wgrad, y, dx2, "dw_out", groups=N_CHIPS, m=Os // 2, n=D,
                         a_block=lambda tk: (tk, Os // 2), a_map=lambda s, j, k: (k, 2 * s + j),
                         b_block=lambda tk: (tk, D), b_map=lambda s, j, k: (k, 0), advance=[p_g2])
    pipes["w_out"] = p_out = GradPipe("w_out", d_w_out)
    dya, dyb, d_swa_norm, d_sb_norm = out_proj_bwd(dx2, w_out_all, ya, yb, swa_out_norm, sb_out_norm, "out_proj_bwd")
    dqa, dka, dva, d_sinks = _hosted(swa_bwd, proj, swa_sinks, dya, b_loc, seq, "swa_bwd", advance=[p_u2, p_out])
    dqb, dkb, dvb = _hosted(sb_bwd, proj, dyb, ltot, b_loc, seq, "sb_bwd", advance=[p_u2, p_out])
    dproj = jnp.concatenate([dqa, dka, dva, dqb, dkb, dvb], axis=1)
    (d_w_in,) = _hosted(wgrad, h2, dproj, "dw_in", groups=N_CHIPS, m=D // 2, n=Ps,
                        a_block=lambda tk: (tk, D // 2), a_map=lambda s, j, k: (k, j),
                        b_block=lambda tk: (tk, Ps), b_map=lambda s, j, k: (k, s), advance=[p_out])
    pipes["w_in"] = p_in = GradPipe("w_in", d_w_in)
    (dh2,) = _hosted(in_proj_bwd, dproj, gathered["w_in"], "in_proj_bwd", advance=[p_in])
    (dx1, d_mix_norm), _ = rmsnorm_bwd_add(x1, mix_norm, dh2, dx2, "mix_norm_bwd")

    dg1, du1 = _hosted(ffn_bwd_act, dx1, gathered["ffn1_w_down"], g1, u1, "ffn1_bwd_act", advance=[p_in])
    pipes["ffn1_w_down"] = p_d1 = GradPipe("ffn1_w_down", down_grad("ffn1", a1, dx1, [p_in]))
    pipes["ffn1_w_gate"] = p_g1 = GradPipe("ffn1_w_gate", up_grad("ffn1_dwg", h1, dg1, [p_d1]))
    pipes["ffn1_w_up"] = p_u1 = GradPipe("ffn1_w_up", up_grad("ffn1_dwu", h1, du1, [p_d1, p_g1]))
    (dh1,) = _hosted(ffn_bwd_dh, dg1, du1, gathered["ffn1_w_gate"], gathered["ffn1_w_up"], "ffn1_bwd_dh",
                     advance=[p_d1, p_g1, p_u1])
    dx0, d_ffn1_norm = _hosted(rmsnorm_bwd_add, x0, ffn1_norm, dh1, dx1, "ffn1_norm_bwd", advance=[p_g1, p_u1])
    (last,) = exchange_now([p_u1.rider()], "swap_ffn1_w_up")
    p_u1.landed(last)

    vec_grads = dict(ffn1_norm=d_ffn1_norm, mix_norm=d_mix_norm, swa_sinks=d_sinks[:, 0], swa_out_norm=d_swa_norm,
                     sb_out_norm=d_sb_norm, ffn2_norm=d_ffn2_norm, final_norm=d_final)
    packed = [_pack_rows(vec_grads[nm]) for nm in VECTORS] + [_pack_rows(loss_part[0, :1])]
    offsets = [0]
    for p in packed:
        offsets.append(offsets[-1] + p.shape[0])
    reduced = allreduce_small(jnp.concatenate(packed, axis=0), "reduce_vectors")
    loss = reduced[offsets[len(VECTORS)], 0]

    out = {}
    for nm in MATRICES:
        shape = given[nm].shape
        g = pipes[nm].value
        col_halves = nm.endswith("w_down")
        if not col_halves:
            g = g.reshape(shape[1:])
        res = adamw(given[nm][0], given["m_" + nm][0], given["v_" + nm][0], g, "adamw_" + nm, col_halves=col_halves)
        out[nm] = [r.reshape(shape) for r in res]
    w_rows = jnp.concatenate([_pack_rows(given[nm]) for nm in VECTORS], axis=0)
    m_rows = jnp.concatenate([_pack_rows(given["m_" + nm]) for nm in VECTORS], axis=0)
    v_rows = jnp.concatenate([_pack_rows(given["v_" + nm]) for nm in VECTORS], axis=0)
    g_rows = reduced[:offsets[len(VECTORS)]]
    small = adamw(w_rows, m_rows, v_rows, g_rows, "adamw_vectors")
    for i, nm in enumerate(VECTORS):
        shape = given[nm].shape
        size = math.prod(shape)
        out[nm] = [r[offsets[i]:offsets[i + 1]].reshape(-1)[:size].reshape(shape) for r in small]

    grad_x = dx0.reshape(b_loc, seq, D)
    return (loss, grad_x, *[out[nm][0] for nm in WEIGHTS], *[out[nm][1] for nm in WEIGHTS],
            *[out[nm][2] for nm in WEIGHTS], *[out[nm][3] for nm in WEIGHTS])
```

```python
import functools
import math

import jax
import jax.numpy as jnp
from jax import lax
from jax.experimental import pallas as pl
from jax.experimental.pallas import tpu as pltpu

F32 = jnp.float32
BF16 = jnp.bfloat16
MESH = pl.DeviceIdType.MESH

EPS = 1e-6
HEAD_DIM = 64
LANES = 128
N_SWA_HEADS = 16
N_SWA_KV = 4
N_SB_HEADS = 16
WINDOW = 128
SWA_BLOCK = 128
SB_BLOCK = 256
N_CHIPS = 4
ATT_SCALE = HEAD_DIM ** -0.5

ADAM_LR = 0.001
ADAM_B1 = 0.9
ADAM_B2 = 0.999
ADAM_EPS = 1e-08
ADAM_WD = 0.01
ADAM_STEP = 10

VMEM_LIMIT = 56 * 1024 * 1024
NEG_BIG = -1e30

ANY = pl.BlockSpec(memory_space=pl.ANY)
VMEM_WHOLE = pl.BlockSpec(memory_space=pltpu.VMEM)


def _row_tile(rows, cap, multiple=16):
    best = None
    for t in range(multiple, min(rows, cap) + 1, multiple):
        if rows % t == 0:
            best = t
    if best is None:
        raise ValueError(f"no row tile for {rows} rows under {cap}")
    return best


def _params(*sem):
    return pltpu.CompilerParams(dimension_semantics=sem, vmem_limit_bytes=VMEM_LIMIT)


def _dot(a, b):
    return jnp.dot(a, b, preferred_element_type=F32)


def _dot_nt(a, b):
    return lax.dot_general(a, b, (((1,), (1,)), ((), ())), preferred_element_type=F32)


def _dot_tn(a, b):
    return lax.dot_general(a, b, (((0,), (0,)), ((), ())), preferred_element_type=F32)


def _sigmoid(x):
    return 1.0 / (1.0 + jnp.exp(-x))


def _split_dot(x, u):
    hi = x.astype(BF16)
    lo = (x - hi.astype(F32)).astype(BF16)
    return _dot(hi, u) + _dot(lo, u)


def _tiles(r, c, max_elems, multiple=16):
    col_opts = [c] + [t for t in (2048, 1024, 512, 256, 128) if t < c and c % t == 0]
    best = None
    for tc in col_opts:
        for tr in [r] + list(range(multiple, r, multiple)):
            if r % tr == 0 and tr * tc <= max_elems and (best is None or tr * tc > best[0] * best[1]):
                best = (tr, tc)
    if best is None:
        raise ValueError(f"no tile for [{r}, {c}] under {max_elems} elements")
    return best


class Rider:
    def __init__(self, arrays, out_shapes, sem_shapes, start, finish):
        self.arrays, self.out_shapes, self.sem_shapes = list(arrays), list(out_shapes), list(sem_shapes)
        self.start, self.finish = start, finish


def _place():
    return lax.axis_index("x"), lax.axis_index("y"), lax.axis_index("c")


def _other_chips(x, y):
    return [(1 - x, y), (x, 1 - y), (1 - x, 1 - y)]


def _split(ref):
    rows = ref.shape[-2]
    n = next(k for k in (4, 2, 1) if rows % (16 * k) == 0)
    step = rows // n
    lead = [()] if len(ref.shape) == 2 else [(s,) for s in range(ref.shape[0])]
    return [ref.at[(*idx, pl.ds(i * step, step))] for idx in lead for i in range(n)]


class _Copy:
    def __init__(self, src, dst, make):
        self.src, self.dst, self.make = src, dst, make

    def start(self):
        for s, d in zip(_split(self.src), _split(self.dst)):
            self.make(s, d).start()

    def wait(self):
        self.make(self.src, self.dst).wait()

    def wait_send(self):
        self.make(self.src, self.dst).wait_send()

    def wait_recv(self):
        self.make(self.src, self.dst).wait_recv()


def _remote(src, dst, send_sem, recv_sem, device):
    return _Copy(src, dst, lambda s, d: pltpu.make_async_remote_copy(
        src_ref=s, dst_ref=d, send_sem=send_sem, recv_sem=recv_sem, device_id=device, device_id_type=MESH))


def _local(src, dst, sem):
    return _Copy(src, dst, lambda s, d: pltpu.make_async_copy(s, d, sem))


def gather_rider(shards):
    n = len(shards)

    def copies(ins, outs, sems):
        ici_s, ici_r, d2d_s, d2d_r, own_sem = sems
        x, y, c = _place()
        me = 2 * x + y
        own, ici, fwd = [], [], []
        for k in range(n):
            own.append(_local(ins[k], outs[k].at[me], own_sem.at[k]))
            for j, (px, py) in enumerate(_other_chips(x, y)):
                i = 3 * k + j
                ici.append(_remote(ins[k].at[c], outs[k].at[me, c], ici_s.at[i], ici_r.at[i], (px, py, c)))
                got = outs[k].at[2 * px + py, c]
                fwd.append(_remote(got, got, d2d_s.at[i], d2d_r.at[i], (x, y, 1 - c)))
        return own, ici, fwd

    def start(ins, outs, sems):
        own, ici, _ = copies(ins, outs, sems)
        for cp in own + ici:
            cp.start()

    def finish(ins, outs, sems):
        own, ici, fwd = copies(ins, outs, sems)
        for a, f in zip(ici, fwd):
            a.wait_recv()
            f.start()
        for a in ici:
            a.wait_send()
        for f in fwd:
            f.wait()
        for o in own:
            o.wait()

    dma = pltpu.SemaphoreType.DMA
    return Rider(shards, [jax.ShapeDtypeStruct((N_CHIPS,) + s.shape, s.dtype) for s in shards],
                 [dma((3 * n,)), dma((3 * n,)), dma((3 * n,)), dma((3 * n,)), dma((n,))], start, finish)


def half_swap_rider(grads):
    n = len(grads)

    def copies(ins, outs, sems):
        x, y, c = _place()
        return [_remote(ins[k].at[1 - c], outs[k], sems[0].at[k], sems[1].at[k], (x, y, 1 - c)) for k in range(n)]

    def start(ins, outs, sems):
        for cp in copies(ins, outs, sems):
            cp.start()

    def finish(ins, outs, sems):
        for cp in copies(ins, outs, sems):
            cp.wait()

    dma = pltpu.SemaphoreType.DMA
    return Rider(grads, [jax.ShapeDtypeStruct(g.shape[1:], g.dtype) for g in grads], [dma((n,)), dma((n,))],
                 start, finish)


def scatter_rider(halves):
    n = len(halves)

    def copies(ins, outs, sems):
        send, recv, own_sem = sems
        x, y, c = _place()
        me = 2 * x + y
        cps = []
        for k in range(n):
            cps.append(_local(ins[k].at[me], outs[k].at[me], own_sem.at[k]))
            for j, (px, py) in enumerate(_other_chips(x, y)):
                i = 3 * k + j
                cps.append(_remote(ins[k].at[2 * px + py], outs[k].at[me], send.at[i], recv.at[i], (px, py, c)))
        return cps

    def start(ins, outs, sems):
        for cp in copies(ins, outs, sems):
            cp.start()

    def finish(ins, outs, sems):
        for cp in copies(ins, outs, sems):
            cp.wait()

    dma = pltpu.SemaphoreType.DMA
    return Rider(halves, [jax.ShapeDtypeStruct(h.shape, h.dtype) for h in halves],
                 [dma((3 * n,)), dma((3 * n,)), dma((n,))], start, finish)


def final_swap_rider(sums):
    n = len(sums)

    def copies(ins, outs, sems):
        send, recv, own_sem = sems
        x, y, c = _place()
        cps = []
        for k in range(n):
            cps.append(_local(ins[k], outs[k].at[c], own_sem.at[k]))
            cps.append(_remote(ins[k], outs[k].at[c], send.at[k], recv.at[k], (x, y, 1 - c)))
        return cps

    def start(ins, outs, sems):
        for cp in copies(ins, outs, sems):
            cp.start()

    def finish(ins, outs, sems):
        for cp in copies(ins, outs, sems):
            cp.wait()

    dma = pltpu.SemaphoreType.DMA
    return Rider(sums, [jax.ShapeDtypeStruct((2,) + s.shape, s.dtype) for s in sums],
                 [dma((n,)), dma((n,)), dma((n,))], start, finish)


def _call(body, *, name, grid, in_specs, out_specs, out_shape, scratch_shapes=(), sem, args, riders=(), aliases=None):
    in_specs, out_specs, out_shape = list(in_specs), list(out_specs), list(out_shape)
    scratch_shapes = list(scratch_shapes)
    aliases = aliases or {}
    if not riders:
        outs = pl.pallas_call(body, name=name, grid=grid, in_specs=in_specs, out_specs=out_specs, out_shape=out_shape,
                              scratch_shapes=scratch_shapes, input_output_aliases=aliases,
                              compiler_params=_params(*sem))(*args)
        return list(outs), []
    n_in, n_out, n_scr = len(in_specs), len(out_specs), len(scratch_shapes)
    r_in = [len(r.arrays) for r in riders]
    r_out = [len(r.out_shapes) for r in riders]
    r_sem = [len(r.sem_shapes) for r in riders]

    def wrapped(*refs):
        pos = 0

        def take(count):
            nonlocal pos
            got = refs[pos:pos + count]
            pos += count
            return got

        ins = take(n_in)
        rins = [take(k) for k in r_in]
        outs = take(n_out)
        routs = [take(k) for k in r_out]
        scr = take(n_scr)
        rsems = [take(k) for k in r_sem]
        first = functools.reduce(jnp.logical_and, [pl.program_id(a) == 0 for a in range(len(grid))])
        last = functools.reduce(jnp.logical_and, [pl.program_id(a) == grid[a] - 1 for a in range(len(grid))])

        @pl.when(first)
        def _():
            for r, ri, ro, rs in zip(riders, rins, routs, rsems):
                r.start(ri, ro, rs)

        body(*ins, *outs, *scr)

        @pl.when(last)
        def _():
            for r, ri, ro, rs in zip(riders, rins, routs, rsems):
                r.finish(ri, ro, rs)

    any_spec = pl.BlockSpec(memory_space=pl.ANY)
    res = pl.pallas_call(
        wrapped, name=name, grid=grid,
        in_specs=in_specs + [any_spec] * sum(r_in),
        out_specs=out_specs + [any_spec] * sum(r_out),
        out_shape=out_shape + [s for r in riders for s in r.out_shapes],
        scratch_shapes=scratch_shapes + [s for r in riders for s in r.sem_shapes],
        input_output_aliases=aliases,
        compiler_params=_params(*(["arbitrary"] * len(grid))),
    )(*args, *[a for r in riders for a in r.arrays])
    res = list(res)
    outs, rest = res[:n_out], res[n_out:]
    routs = []
    for k in r_out:
        routs.append(rest[:k])
        rest = rest[k:]
    return outs, routs


def exchange_now(riders, name):
    n_in = [len(r.arrays) for r in riders]
    n_out = [len(r.out_shapes) for r in riders]
    n_sem = [len(r.sem_shapes) for r in riders]

    def body(*refs):
        pos = 0
        groups = []
        for counts in (n_in, n_out, n_sem):
            part = []
            for k in counts:
                part.append(refs[pos:pos + k])
                pos += k
            groups.append(part)
        for r, ri, ro, rs in zip(riders, *groups):
            r.start(ri, ro, rs)
        for r, ri, ro, rs in zip(riders, *groups):
            r.finish(ri, ro, rs)

    any_spec = pl.BlockSpec(memory_space=pl.ANY)
    res = pl.pallas_call(
        body, name=name,
        in_specs=[any_spec] * sum(n_in), out_specs=[any_spec] * sum(n_out),
        out_shape=[s for r in riders for s in r.out_shapes],
        scratch_shapes=[s for r in riders for s in r.sem_shapes],
    )(*[a for r in riders for a in r.arrays])
    res = list(res)
    routs = []
    for k in n_out:
        routs.append(res[:k])
        res = res[k:]
    return routs


def rmsnorm_cast(x, gamma, name):
    T, D = x.shape
    tr = min(512, T)

    def body(x_ref, g_ref, o_ref):
        xv = x_ref[...]
        r = lax.rsqrt(jnp.mean(xv * xv, axis=-1, keepdims=True) + EPS)
        o_ref[...] = ((xv * r) * g_ref[...]).astype(BF16)

    return pl.pallas_call(
        body, name=name, grid=(T // tr,),
        in_specs=[pl.BlockSpec((tr, D), lambda i: (i, 0)), pl.BlockSpec((1, D), lambda i: (0, 0))],
        out_specs=pl.BlockSpec((tr, D), lambda i: (i, 0)),
        out_shape=jax.ShapeDtypeStruct((T, D), BF16),
        compiler_params=_params("parallel"),
    )(x, gamma)


def _rms_bwd(xv, gamma, dy):
    d = xv.shape[-1]
    r = lax.rsqrt(jnp.mean(xv * xv, axis=-1, keepdims=True) + EPS)
    gdy = dy * gamma
    dot = jnp.sum(gdy * xv, axis=-1, keepdims=True)
    dx = r * gdy - xv * ((r * r * r) * (dot / d))
    dgamma = jnp.sum(dy * (xv * r), axis=0, keepdims=True)
    return dx, dgamma


def rmsnorm_bwd_add(x, gamma, dy, dx_up, name, riders=()):
    T, D = x.shape
    tr = min(256, T)

    def body(x_ref, g_ref, dy_ref, up_ref, dx_ref, dg_ref):
        dx, dgamma = _rms_bwd(x_ref[...], g_ref[...], dy_ref[...])
        dx_ref[...] = up_ref[...] + dx

        @pl.when(pl.program_id(0) == 0)
        def _():
            dg_ref[...] = jnp.zeros_like(dg_ref)

        dg_ref[...] += dgamma

    row = pl.BlockSpec((tr, D), lambda i: (i, 0))
    vec = pl.BlockSpec((1, D), lambda i: (0, 0))
    return _call(
        body, name=name, grid=(T // tr,),
        in_specs=[row, vec, row, row], out_specs=[row, vec],
        out_shape=[jax.ShapeDtypeStruct((T, D), F32), jax.ShapeDtypeStruct((1, D), F32)],
        sem=("arbitrary",), args=(x, gamma, dy, dx_up), riders=riders)


def ffn_up(h, wg, wu, name, riders=()):
    T, D = h.shape
    G, _, Fs = wg.shape
    tm = min(512, T)

    def body(h_ref, wg_ref, wu_ref, g_ref, u_ref, a_ref):
        hv = h_ref[...]
        g = _dot(hv, wg_ref[...])
        u = _dot(hv, wu_ref[...])
        g_ref[...] = g
        u_ref[...] = u
        a_ref[...] = ((g * _sigmoid(g)) * u).astype(BF16)

    w_spec = pl.BlockSpec((None, D, Fs), lambda s, i: (s, 0, 0), pipeline_mode=pl.Buffered(1))
    o_spec = pl.BlockSpec((None, tm, Fs), lambda s, i: (s, i, 0))
    return _call(
        body, name=name, grid=(G, T // tm),
        in_specs=[pl.BlockSpec((tm, D), lambda s, i: (i, 0)), w_spec, w_spec],
        out_specs=[o_spec, o_spec, o_spec],
        out_shape=[jax.ShapeDtypeStruct((G, T, Fs), F32), jax.ShapeDtypeStruct((G, T, Fs), F32),
                   jax.ShapeDtypeStruct((G, T, Fs), BF16)],
        sem=("parallel", "parallel"), args=(h, wg, wu), riders=riders)


def ffn_gate(h, wg, name, riders=()):
    T, D = h.shape
    G, _, Fs = wg.shape
    tm = min(512, T)

    def body(h_ref, wg_ref, g_ref):
        g_ref[...] = _dot(h_ref[...], wg_ref[...])

    return _call(
        body, name=name, grid=(G, T // tm),
        in_specs=[pl.BlockSpec((tm, D), lambda s, i: (i, 0)),
                  pl.BlockSpec((None, D, Fs), lambda s, i: (s, 0, 0), pipeline_mode=pl.Buffered(1))],
        out_specs=[pl.BlockSpec((None, tm, Fs), lambda s, i: (s, i, 0))],
        out_shape=[jax.ShapeDtypeStruct((G, T, Fs), F32)],
        sem=("parallel", "parallel"), args=(h, wg), riders=riders)


def ffn_up_act(h, wu, g, name, riders=()):
    T, D = h.shape
    G, _, Fs = wu.shape
    tm = min(512, T)

    def body(h_ref, wu_ref, g_ref, u_ref, a_ref):
        u = _dot(h_ref[...], wu_ref[...])
        g = g_ref[...]
        u_ref[...] = u
        a_ref[...] = ((g * _sigmoid(g)) * u).astype(BF16)

    act = pl.BlockSpec((None, tm, Fs), lambda s, i: (s, i, 0))
    return _call(
        body, name=name, grid=(G, T // tm),
        in_specs=[pl.BlockSpec((tm, D), lambda s, i: (i, 0)),
                  pl.BlockSpec((None, D, Fs), lambda s, i: (s, 0, 0), pipeline_mode=pl.Buffered(1)), act],
        out_specs=[act, act],
        out_shape=[jax.ShapeDtypeStruct((G, T, Fs), F32), jax.ShapeDtypeStruct((G, T, Fs), BF16)],
        sem=("parallel", "parallel"), args=(h, wu, g), riders=riders)


def ffn_down(a, wd, x, name, riders=()):
    G, T, Fs = a.shape
    D = wd.shape[2]
    tm = min(512, T)

    def body(a_ref, w_ref, x_ref, o_ref, acc):
        s = pl.program_id(1)

        @pl.when(s == 0)
        def _():
            acc[...] = jnp.zeros_like(acc)

        acc[...] += _dot(a_ref[...], w_ref[...])

        @pl.when(s == G - 1)
        def _():
            o_ref[...] = x_ref[...] + 0.5 * acc[...]

    return _call(
        body, name=name, grid=(T // tm, G),
        in_specs=[pl.BlockSpec((None, tm, Fs), lambda i, s: (s, i, 0)),
                  pl.BlockSpec((None, Fs, D), lambda i, s: (s, 0, 0)),
                  pl.BlockSpec((tm, D), lambda i, s: (i, 0))],
        out_specs=[pl.BlockSpec((tm, D), lambda i, s: (i, 0))],
        out_shape=[jax.ShapeDtypeStruct((T, D), F32)],
        scratch_shapes=[pltpu.VMEM((tm, D), F32)],
        sem=("parallel", "arbitrary"), args=(a, wd, x), riders=riders)


def ffn_bwd_act(dx, wd, g, u, name, riders=()):
    T, D = dx.shape
    G, Fs, _ = wd.shape
    tm = min(256, T)

    def body(dx_ref, w_ref, g_ref, u_ref, dg_ref, du_ref):
        da = _dot_nt((0.5 * dx_ref[...]).astype(BF16), w_ref[...])
        gv = g_ref[...]
        sig = _sigmoid(gv)
        silu = gv * sig
        du_ref[...] = (da * silu).astype(BF16)
        dg_ref[...] = ((da * u_ref[...]) * (sig * (1.0 + gv * (1.0 - sig)))).astype(BF16)

    act = pl.BlockSpec((None, tm, Fs), lambda s, i: (s, i, 0))
    return _call(
        body, name=name, grid=(G, T // tm),
        in_specs=[pl.BlockSpec((tm, D), lambda s, i: (i, 0)),
                  pl.BlockSpec((None, Fs, D), lambda s, i: (s, 0, 0), pipeline_mode=pl.Buffered(1)),
                  act, act],
        out_specs=[act, act],
        out_shape=[jax.ShapeDtypeStruct((G, T, Fs), BF16), jax.ShapeDtypeStruct((G, T, Fs), BF16)],
        sem=("parallel", "parallel"), args=(dx, wd, g, u), riders=riders)


def ffn_bwd_dh(dg, du, wg, wu, name, riders=(), part=(0, 1), prev=None):
    G, T, Fs = dg.shape
    D = wg.shape[1]
    tm = min(512, T // part[1])
    nblk = T // tm // part[1]
    off = part[0] * nblk

    def body(dg_ref, du_ref, wg_ref, wu_ref, *rest):
        o_ref, acc = rest[-2:]
        s = pl.program_id(1)

        @pl.when(s == 0)
        def _():
            acc[...] = jnp.zeros_like(acc)

        acc[...] += _dot_nt(dg_ref[...], wg_ref[...])
        acc[...] += _dot_nt(du_ref[...], wu_ref[...])

        @pl.when(s == G - 1)
        def _():
            o_ref[...] = acc[...]

    act = pl.BlockSpec((None, tm, Fs), lambda i, s: (s, i + off, 0))
    w_spec = pl.BlockSpec((None, D, Fs), lambda i, s: (s, 0, 0))
    extra = [] if prev is None else [prev]
    return _call(
        body, name=name, grid=(nblk, G),
        in_specs=[act, act, w_spec, w_spec] + [pl.BlockSpec(memory_space=pl.ANY)] * len(extra),
        out_specs=[pl.BlockSpec((tm, D), lambda i, s: (i + off, 0))],
        out_shape=[jax.ShapeDtypeStruct((T, D), F32)],
        scratch_shapes=[pltpu.VMEM((tm, D), F32)],
        sem=("parallel", "arbitrary"), args=(dg, du, wg, wu, *extra), riders=riders,
        aliases={4: 0} if extra else None)


def wgrad(a, b, name, *, groups, m, n, a_block, a_map, b_block, b_map, b_scale=None, tk=512, riders=()):
    T = a.shape[-2]
    tk = min(tk, T)
    nk = T // tk

    def body(a_ref, b_ref, o_ref, acc):
        k = pl.program_id(2)

        @pl.when(k == 0)
        def _():
            acc[...] = jnp.zeros_like(acc)

        bv = b_ref[...]
        if b_scale is not None:
            bv = b_scale * bv
        acc[...] += _dot_tn(a_ref[...].astype(BF16), bv.astype(BF16))

        @pl.when(k == nk - 1)
        def _():
            o_ref[...] = acc[...].astype(BF16)

    return _call(
        body, name=name, grid=(groups, 2, nk),
        in_specs=[pl.BlockSpec(a_block(tk), a_map), pl.BlockSpec(b_block(tk), b_map)],
        out_specs=[pl.BlockSpec((None, None, m, n), lambda s, j, k: (j, s, 0, 0))],
        out_shape=[jax.ShapeDtypeStruct((2, groups, m, n), BF16)],
        scratch_shapes=[pltpu.VMEM((m, n), F32)],
        sem=("parallel", "parallel", "arbitrary"), args=(a, b), riders=riders)


def in_proj(h, w_in, name):
    T, D = h.shape
    G, _, Ps = w_in.shape
    tm = min(512, T)

    def body(h_ref, w_ref, o_ref):
        o_ref[...] = _dot(h_ref[...], w_ref[...]).astype(BF16)

    return pl.pallas_call(
        body, name=name, grid=(G, T // tm),
        in_specs=[pl.BlockSpec((tm, D), lambda s, i: (i, 0)),
                  pl.BlockSpec((None, D, Ps), lambda s, i: (s, 0, 0))],
        out_specs=pl.BlockSpec((tm, Ps), lambda s, i: (i, s)),
        out_shape=jax.ShapeDtypeStruct((T, G * Ps), BF16),
        compiler_params=_params("parallel", "parallel"),
    )(h, w_in)


def in_proj_bwd(dproj, w_in, name, riders=()):
    T = dproj.shape[0]
    G, D, Ps = w_in.shape
    tm = min(512, T)

    def body(d_ref, w_ref, o_ref, acc):
        s = pl.program_id(1)

        @pl.when(s == 0)
        def _():
            acc[...] = jnp.zeros_like(acc)

        acc[...] += _dot_nt(d_ref[...], w_ref[...])

        @pl.when(s == G - 1)
        def _():
            o_ref[...] = acc[...]

    return _call(
        body, name=name, grid=(T // tm, G),
        in_specs=[pl.BlockSpec((tm, Ps), lambda i, s: (i, s)),
                  pl.BlockSpec((None, D, Ps), lambda i, s: (s, 0, 0))],
        out_specs=[pl.BlockSpec((tm, D), lambda i, s: (i, 0))],
        out_shape=[jax.ShapeDtypeStruct((T, D), F32)],
        scratch_shapes=[pltpu.VMEM((tm, D), F32)],
        sem=("parallel", "arbitrary"), args=(dproj, w_in), riders=riders)


def out_proj(y, w_out, x, name):
    T, M = y.shape
    D = w_out.shape[1]
    tm = min(512, T)

    def body(y_ref, w_ref, x_ref, o_ref):
        o_ref[...] = x_ref[...] + _dot(y_ref[...], w_ref[...])

    return pl.pallas_call(
        body, name=name, grid=(T // tm,),
        in_specs=[pl.BlockSpec((tm, M), lambda i: (i, 0)), pl.BlockSpec((M, D), lambda i: (0, 0)),
                  pl.BlockSpec((tm, D), lambda i: (i, 0))],
        out_specs=pl.BlockSpec((tm, D), lambda i: (i, 0)),
        out_shape=jax.ShapeDtypeStruct((T, D), F32),
        compiler_params=_params("parallel"),
    )(y, w_out, x)


def out_norm(ya, yb, ga, gb, name):
    T, W = ya.shape
    tr = min(512, T)

    def body(a_ref, b_ref, ga_ref, gb_ref, o_ref):
        for src, gam, col in ((a_ref, ga_ref, 0), (b_ref, gb_ref, W)):
            v = src[...]
            r = lax.rsqrt(jnp.mean(v * v, axis=-1, keepdims=True) + EPS)
            o_ref[:, col:col + W] = ((v * r) * gam[...]).astype(BF16)

    row = pl.BlockSpec((tr, W), lambda i: (i, 0))
    vec = pl.BlockSpec((1, W), lambda i: (0, 0))
    return pl.pallas_call(
        body, name=name, grid=(T // tr,),
        in_specs=[row, row, vec, vec],
        out_specs=pl.BlockSpec((tr, 2 * W), lambda i: (i, 0)),
        out_shape=jax.ShapeDtypeStruct((T, 2 * W), BF16),
        compiler_params=_params("parallel"),
    )(ya, yb, ga, gb)


def out_proj_bwd(dx, w_out, ya, yb, ga, gb, name):
    T, D = dx.shape
    M = w_out.shape[0]
    W = M // 2
    tm = min(256, T)

    def body(dx_ref, w_ref, a_ref, b_ref, ga_ref, gb_ref, da_ref, db_ref, dga_ref, dgb_ref):
        dy = _dot_nt(dx_ref[...].astype(BF16), w_ref[...])

        @pl.when(pl.program_id(0) == 0)
        def _():
            dga_ref[...] = jnp.zeros_like(dga_ref)
            dgb_ref[...] = jnp.zeros_like(dgb_ref)

        d_a, dg_a = _rms_bwd(a_ref[...], ga_ref[...], dy[:, :W])
        d_b, dg_b = _rms_bwd(b_ref[...], gb_ref[...], dy[:, W:])
        da_ref[...] = d_a
        db_ref[...] = d_b
        dga_ref[...] += dg_a
        dgb_ref[...] += dg_b

    row = pl.BlockSpec((tm, W), lambda i: (i, 0))
    vec = pl.BlockSpec((1, W), lambda i: (0, 0))
    return pl.pallas_call(
        body, name=name, grid=(T // tm,),
        in_specs=[pl.BlockSpec((tm, D), lambda i: (i, 0)), pl.BlockSpec((M, D), lambda i: (0, 0)),
                  row, row, vec, vec],
        out_specs=[row, row, vec, vec],
        out_shape=[jax.ShapeDtypeStruct((T, W), F32), jax.ShapeDtypeStruct((T, W), F32),
                   jax.ShapeDtypeStruct((1, W), F32), jax.ShapeDtypeStruct((1, W), F32)],
        compiler_params=_params("arbitrary"),
    )(dx, w_out, ya, yb, ga, gb)


def final_loss(x, gamma, target, name):
    T, D = x.shape
    tr = min(256, T)

    def body(x_ref, g_ref, t_ref, dx_ref, dg_ref, loss_ref):
        xv = x_ref[...]
        gam = g_ref[...]
        r = lax.rsqrt(jnp.mean(xv * xv, axis=-1, keepdims=True) + EPS)
        err = (xv * r) * gam - t_ref[...]
        part = 0.5 * jnp.sum(jnp.mean(err * err, axis=-1, keepdims=True), axis=0, keepdims=True)
        dx, dgamma = _rms_bwd(xv, gam, err / D)
        dx_ref[...] = dx

        @pl.when(pl.program_id(0) == 0)
        def _():
            dg_ref[...] = jnp.zeros_like(dg_ref)
            loss_ref[...] = jnp.zeros_like(loss_ref)

        dg_ref[...] += dgamma
        loss_ref[...] += jnp.broadcast_to(part, loss_ref.shape)

    row = pl.BlockSpec((tr, D), lambda i: (i, 0))
    vec = pl.BlockSpec((1, D), lambda i: (0, 0))
    return pl.pallas_call(
        body, name=name, grid=(T // tr,),
        in_specs=[row, vec, row],
        out_specs=[row, vec, pl.BlockSpec((1, LANES), lambda i: (0, 0))],
        out_shape=[jax.ShapeDtypeStruct((T, D), F32), jax.ShapeDtypeStruct((1, D), F32),
                   jax.ShapeDtypeStruct((1, LANES), F32)],
        compiler_params=_params("arbitrary"),
    )(x, gamma, target)


def _lane_select(src_half, dst_halves):
    r = lax.broadcasted_iota(jnp.int32, (LANES, LANES), 0)
    c = lax.broadcasted_iota(jnp.int32, (LANES, LANES), 1)
    hit = ((r >> 6) == src_half) & ((r & (HEAD_DIM - 1)) == (c & (HEAD_DIM - 1)))
    ok = functools.reduce(jnp.logical_or, [(c >> 6) == d for d in dst_halves])
    return jnp.where(hit & ok, 1.0, 0.0).astype(BF16)


def _swa_slope(kvp, local_head):
    lo = 2.0 ** (-8.0 * (local_head + 1) / N_SWA_HEADS)
    hi = 2.0 ** (-8.0 * (8 + local_head + 1) / N_SWA_HEADS)
    return jnp.where(kvp == 0, jnp.float32(lo), jnp.float32(hi))


def _swa_probs(qblk, kwin, n, start, slope, sink):
    z = _dot_nt(qblk, kwin) * ATT_SCALE
    t = n * SWA_BLOCK + lax.broadcasted_iota(jnp.int32, z.shape, 0)
    s = start + lax.broadcasted_iota(jnp.int32, z.shape, 1)
    dist = t - s
    valid = (dist >= 0) & (dist < WINDOW)
    z = jnp.where(valid, z - slope * dist.astype(F32), NEG_BIG)
    m = jnp.maximum(jnp.max(z, axis=-1, keepdims=True), sink)
    e = jnp.exp(z - m)
    den = jnp.sum(e, axis=-1, keepdims=True) + jnp.exp(sink - m)
    return e / den, m, den


def swa_fwd(proj, sinks, b_loc, seq, name):
    T = proj.shape[0]
    nblk = seq // SWA_BLOCK

    def body(sink_ref, q_ref, k_ref, v_ref, o_ref, qs, vv):
        kvp = pl.program_id(1)
        lane = lax.broadcasted_iota(jnp.int32, (SWA_BLOCK, LANES), 1)
        for kvh in range(2):
            both = _lane_select(kvh, (0, 1))
            vv[...] = _dot(v_ref[...], both).astype(BF16)
            for qp in (2 * kvh, 2 * kvh + 1):
                cols = slice(LANES * qp, LANES * (qp + 1))
                for a in range(2):
                    qs[a] = _dot(q_ref[:, cols], _lane_select(a, (kvh,))).astype(BF16)
                slopes = [_swa_slope(kvp, 2 * qp + a) for a in range(2)]
                snk = [sink_ref[0, 8 * kvp + 2 * qp + a] for a in range(2)]

                def blk(n, carry):
                    start = pl.multiple_of(jnp.maximum(n - 1, 0) * SWA_BLOCK, SWA_BLOCK)
                    rows = pl.ds(pl.multiple_of(n * SWA_BLOCK, SWA_BLOCK), SWA_BLOCK)
                    win = pl.ds(start, 2 * SWA_BLOCK)
                    outs = []
                    for a in range(2):
                        p, _, _ = _swa_probs(qs[a, rows, :], k_ref[win, :], n, start, slopes[a], snk[a])
                        outs.append(_dot(p.astype(BF16), vv[win, :]))
                    o_ref[rows, cols] = jnp.where(lane < HEAD_DIM, outs[0], outs[1])
                    return carry

                lax.fori_loop(0, nblk, blk, 0)

    return pl.pallas_call(
        body, name=name, grid=(b_loc, 2),
        in_specs=[pl.BlockSpec(memory_space=pltpu.SMEM),
                  pl.BlockSpec((seq, 4 * LANES), lambda b, p: (b, p)),
                  pl.BlockSpec((seq, LANES), lambda b, p: (b, 8 + p)),
                  pl.BlockSpec((seq, LANES), lambda b, p: (b, 10 + p))],
        out_specs=pl.BlockSpec((seq, 4 * LANES), lambda b, p: (b, p)),
        out_shape=jax.ShapeDtypeStruct((T, N_SWA_HEADS * HEAD_DIM), F32),
        scratch_shapes=[pltpu.VMEM((2, seq, LANES), BF16), pltpu.VMEM((seq, LANES), BF16)],
        compiler_params=_params("parallel", "parallel"),
    )(sinks, proj, proj, proj)


def swa_bwd(proj, sinks, dya, b_loc, seq, name, riders=()):
    T = proj.shape[0]
    nblk = seq // SWA_BLOCK

    def body(sink_ref, q_ref, k_ref, v_ref, do_ref, dq_ref, dk_ref, dv_ref, ds_ref, qs, dos, kk, dk_acc, dv_acc):
        kvp = pl.program_id(0)
        b = pl.program_id(1)
        lane = lax.broadcasted_iota(jnp.int32, (SWA_BLOCK, LANES), 1)

        @pl.when(b == 0)
        def _():
            ds_ref[...] = jnp.zeros_like(ds_ref)

        dk_acc[...] = jnp.zeros_like(dk_acc)
        dv_acc[...] = jnp.zeros_like(dv_acc)
        for kvh in range(2):
            kk[...] = _dot(k_ref[...], _lane_select(kvh, (0, 1))).astype(BF16)
            for qp in (2 * kvh, 2 * kvh + 1):
                cols = slice(LANES * qp, LANES * (qp + 1))
                for a in range(2):
                    move = _lane_select(a, (kvh,))
                    qs[a] = _dot(q_ref[:, cols], move).astype(BF16)
                    dos[a] = _dot(do_ref[:, cols].astype(BF16), move).astype(BF16)
                slopes = [_swa_slope(kvp, 2 * qp + a) for a in range(2)]
                snk = [sink_ref[0, 8 * kvp + 2 * qp + a] for a in range(2)]

                def blk(n, carry):
                    start = pl.multiple_of(jnp.maximum(n - 1, 0) * SWA_BLOCK, SWA_BLOCK)
                    rows = pl.ds(pl.multiple_of(n * SWA_BLOCK, SWA_BLOCK), SWA_BLOCK)
                    win = pl.ds(start, 2 * SWA_BLOCK)
                    dqs = []
                    for a in range(2):
                        qblk = qs[a, rows, :]
                        doblk = dos[a, rows, :]
                        p, m, den = _swa_probs(qblk, k_ref[win, :], n, start, slopes[a], snk[a])
                        dp = _dot_nt(doblk, v_ref[win, :])
                        delta = jnp.sum(p * dp, axis=-1, keepdims=True)
                        p_sink = jnp.exp(snk[a] - m) / den
                        head = 2 * qp + a
                        ds_ref[head:head + 1, :] -= jnp.broadcast_to(
                            jnp.sum(p_sink * delta, axis=0, keepdims=True), (1, LANES))
                        dsc = ((p * (dp - delta)) * ATT_SCALE).astype(BF16)
                        dv_acc[win, :] += _dot_tn(p.astype(BF16), doblk)
                        dk_acc[win, :] += _dot_tn(dsc, qblk)
                        dqs.append(_dot(dsc, kk[win, :]))
                    dq_ref[rows, cols] = jnp.where(lane < HEAD_DIM, dqs[0], dqs[1]).astype(BF16)
                    return carry

                lax.fori_loop(0, nblk, blk, 0)
        dk_ref[...] = dk_acc[...].astype(BF16)
        dv_ref[...] = dv_acc[...].astype(BF16)

    return _call(
        body, name=name, grid=(2, b_loc),
        in_specs=[pl.BlockSpec(memory_space=pltpu.SMEM),
                  pl.BlockSpec((seq, 4 * LANES), lambda p, b: (b, p)),
                  pl.BlockSpec((seq, LANES), lambda p, b: (b, 8 + p)),
                  pl.BlockSpec((seq, LANES), lambda p, b: (b, 10 + p)),
                  pl.BlockSpec((seq, 4 * LANES), lambda p, b: (b, p))],
        out_specs=[pl.BlockSpec((seq, 4 * LANES), lambda p, b: (b, p)),
                   pl.BlockSpec((seq, LANES), lambda p, b: (b, p)),
                   pl.BlockSpec((seq, LANES), lambda p, b: (b, p)),
                   pl.BlockSpec((8, LANES), lambda p, b: (p, 0))],
        out_shape=[jax.ShapeDtypeStruct((T, N_SWA_HEADS * HEAD_DIM), BF16),
                   jax.ShapeDtypeStruct((T, N_SWA_KV * HEAD_DIM), BF16),
                   jax.ShapeDtypeStruct((T, N_SWA_KV * HEAD_DIM), BF16),
                   jax.ShapeDtypeStruct((N_SWA_HEADS, LANES), F32)],
        scratch_shapes=[pltpu.VMEM((2, seq, LANES), BF16), pltpu.VMEM((2, seq, LANES), BF16),
                        pltpu.VMEM((seq, LANES), BF16),
                        pltpu.VMEM((seq, LANES), F32), pltpu.VMEM((seq, LANES), F32)],
        sem=("arbitrary", "arbitrary"), args=(sinks, proj, proj, proj, dya), riders=riders)


def _softplus(z):
    return jnp.maximum(z, 0.0) + jnp.log(1.0 + jnp.exp(-jnp.abs(z)))


def _tri(n, rel):
    r = lax.broadcasted_iota(jnp.int32, (n, n), 0)
    c = lax.broadcasted_iota(jnp.int32, (n, n), 1)
    return jnp.where(rel(r, c), 1.0, 0.0).astype(BF16)


def sb_fwd(proj, b_loc, seq, name, riders=()):
    T = proj.shape[0]
    bq = min(SB_BLOCK, seq)
    nq = seq // bq
    npair = N_SB_HEADS // 2

    def body(q_ref, k_ref, v_ref, o_ref, lt_ref):
        i = pl.program_id(2)
        lane = lax.broadcasted_iota(jnp.int32, (bq, LANES), 1)
        after = _tri(bq, lambda r, c: r > c)
        tpos = i * bq + lax.broadcasted_iota(jnp.int32, (bq, bq), 0)
        col = lax.broadcasted_iota(jnp.int32, (bq, bq), 1)
        qv = q_ref[...] * jnp.asarray(ATT_SCALE, BF16)
        qs = [jnp.where((lane >> 6) == h, qv, jnp.zeros((bq, LANES), BF16)) for h in range(2)]

        def tile(j, state, mask):
            rows = pl.ds(pl.multiple_of(j * bq, bq), bq)
            kj, vj = k_ref[rows, :], v_ref[rows, :]
            new = []
            for h in range(2):
                run, acc = state[2 * h], state[2 * h + 1]
                z = _dot_nt(qs[h], kj)
                sp = _softplus(z)
                lg = -sp if mask is None else jnp.where(mask, -sp, 0.0)
                a = jnp.exp((z - sp) + (_split_dot(lg, after) + run))
                if mask is not None:
                    a = jnp.where(mask, a, 0.0)
                new += [run + jnp.sum(lg, axis=-1, keepdims=True), acc + _dot(a.astype(BF16), vj)]
            return tuple(new)

        zero = (jnp.zeros((bq, 1), F32), jnp.zeros((bq, LANES), F32))
        state = tile(i, zero + zero, (i * bq + col) < tpos)
        state = lax.fori_loop(0, i, lambda jj, st: tile(i - 1 - jj, st, None), state)
        o_ref[...] = jnp.where(lane < HEAD_DIM, state[1], state[3])
        lt_ref[...] = jnp.where(lane < HEAD_DIM, state[0], state[2])

    return _call(
        body, name=name, grid=(b_loc, npair, nq),
        in_specs=[pl.BlockSpec((bq, LANES), lambda b, p, i: (b * nq + i, 12 + p)),
                  pl.BlockSpec((seq, LANES), lambda b, p, i: (b, 20 + p)),
                  pl.BlockSpec((seq, LANES), lambda b, p, i: (b, 28 + p))],
        out_specs=[pl.BlockSpec((bq, LANES), lambda b, p, i: (b * nq + i, p)),
                   pl.BlockSpec((None, bq, LANES), lambda b, p, i: (b * npair + p, i, 0))],
        out_shape=[jax.ShapeDtypeStruct((T, N_SB_HEADS * HEAD_DIM), F32),
                   jax.ShapeDtypeStruct((b_loc * npair, seq, LANES), F32)],
        sem=("parallel", "parallel", "parallel"), args=(proj, proj, proj), riders=riders)


def sb_bwd(proj, dyb, ltot, b_loc, seq, name, riders=()):
    T = proj.shape[0]
    bq = min(SB_BLOCK, seq)
    nq = seq // bq
    npair = N_SB_HEADS // 2

    def body(q_ref, k_ref, v_ref, do_ref, lt_ref, dq_ref, dk_ref, dv_ref, dk_acc, dv_acc):
        lane = lax.broadcasted_iota(jnp.int32, (bq, LANES), 1)
        upto = _tri(bq, lambda r, c: r <= c)
        before = _tri(bq, lambda r, c: r < c)
        row = lax.broadcasted_iota(jnp.int32, (bq, bq), 0)
        col = lax.broadcasted_iota(jnp.int32, (bq, bq), 1)
        dk_acc[...] = jnp.zeros_like(dk_acc)
        dv_acc[...] = jnp.zeros_like(dv_acc)
        scale = jnp.asarray(ATT_SCALE, BF16)
        heads = [(lane >> 6) == h for h in range(2)]
        blank = jnp.zeros((bq, LANES), BF16)

        def qblock(i, carry):
            qrows = pl.ds(pl.multiple_of(i * bq, bq), bq)
            qv = q_ref[qrows, :] * scale
            dov = do_ref[qrows, :].astype(BF16)
            ltv = lt_ref[qrows, :]
            qs = [jnp.where(m, qv, blank) for m in heads]
            dos = [jnp.where(m, dov, blank) for m in heads]
            totals = [jnp.sum(jnp.where(lane == h * HEAD_DIM, ltv, 0.0), axis=-1, keepdims=True) for h in range(2)]

            def tile(j, state, mask):
                krows = pl.ds(pl.multiple_of(j * bq, bq), bq)
                kj, vj = k_ref[krows, :], v_ref[krows, :]
                kv = kj * scale
                dq = state[4]
                dk = dv = None
                new = []
                for h in range(2):
                    l_pre, g_pre = state[2 * h], state[2 * h + 1]
                    z = _dot_nt(qs[h], kj)
                    sp = _softplus(z)
                    lg = -sp if mask is None else jnp.where(mask, -sp, 0.0)
                    log_beta = z - sp
                    a = jnp.exp(log_beta + (totals[h] - (_split_dot(lg, upto) + l_pre)))
                    if mask is not None:
                        a = jnp.where(mask, a, 0.0)
                    g = a * _dot_nt(dos[h], vj)
                    beta = jnp.exp(log_beta)
                    dz = g * (1.0 - beta) - beta * (_split_dot(g, before) + g_pre)
                    if mask is not None:
                        dz = jnp.where(mask, dz, 0.0)
                    dzb = dz.astype(BF16)
                    dv_h = _dot_tn(a.astype(BF16), dos[h])
                    dk_h = _dot_tn(dzb, qs[h])
                    dv = dv_h if dv is None else dv + dv_h
                    dk = dk_h if dk is None else dk + dk_h
                    dq = dq + _dot(dzb, jnp.where(heads[h], kv, blank))
                    new += [l_pre + jnp.sum(lg, axis=-1, keepdims=True), g_pre + jnp.sum(g, axis=-1, keepdims=True)]
                dv_acc[krows, :] += dv
                dk_acc[krows, :] += dk
                return (*new, dq)

            zero = jnp.zeros((bq, 1), F32)
            state = lax.fori_loop(0, i, lambda j, st: tile(j, st, None),
                                  (zero, zero, zero, zero, jnp.zeros((bq, LANES), F32)))
            state = tile(i, state, col < row)
            dq_ref[qrows, :] = state[4].astype(BF16)
            return carry

        lax.fori_loop(0, nq, qblock, 0)
        dk_ref[...] = dk_acc[...].astype(BF16)
        dv_ref[...] = dv_acc[...].astype(BF16)

    pair = pl.BlockSpec((seq, LANES), lambda b, p: (b, p))
    shape = jax.ShapeDtypeStruct((T, N_SB_HEADS * HEAD_DIM), BF16)
    return _call(
        body, name=name, grid=(b_loc, npair),
        in_specs=[pl.BlockSpec((seq, LANES), lambda b, p: (b, 12 + p)),
                  pl.BlockSpec((seq, LANES), lambda b, p: (b, 20 + p)),
                  pl.BlockSpec((seq, LANES), lambda b, p: (b, 28 + p)),
                  pair,
                  pl.BlockSpec((None, seq, LANES), lambda b, p: (b * npair + p, 0, 0))],
        out_specs=[pair, pair, pair],
        out_shape=[shape, shape, shape],
        scratch_shapes=[pltpu.VMEM((seq, LANES), F32)] * 2,
        sem=("parallel", "parallel"), args=(proj, proj, proj, dyb, ltot), riders=riders)


def allreduce_small(v, name):
    R = v.shape[0]
    ndev = 8

    def body(v_ref, o_ref, land, send_sems, recv_sems):
        x, y, c = _place()
        me = 4 * x + 2 * y + c
        land[me] = v_ref[...]
        started = []
        for f in range(1, ndev):
            fx, fy, fc = (f >> 2) & 1, (f >> 1) & 1, f & 1
            peer = ((1 - x) if fx else x, (1 - y) if fy else y, (1 - c) if fc else c)
            cp = pltpu.make_async_remote_copy(
                src_ref=v_ref, dst_ref=land.at[me], send_sem=send_sems.at[f - 1], recv_sem=recv_sems.at[f - 1],
                device_id=peer, device_id_type=MESH)
            cp.start()
            started.append(cp)
        for cp in started:
            cp.wait()
        total = land[0]
        for d in range(1, ndev):
            total = total + land[d]
        o_ref[...] = total

    return pl.pallas_call(
        body, name=name,
        in_specs=[VMEM_WHOLE], out_specs=VMEM_WHOLE,
        out_shape=jax.ShapeDtypeStruct(v.shape, F32),
        scratch_shapes=[pltpu.VMEM((ndev, R, LANES), F32),
                        pltpu.SemaphoreType.DMA((ndev - 1,)), pltpu.SemaphoreType.DMA((ndev - 1,))],
    )(v)


def sum_cores(grads, theirs, name):
    _, _, r, c = grads.shape
    tr, tc = _tiles(r, c, 1 << 18)
    core = lax.axis_index("c").astype(jnp.int32).reshape(1)

    def body(core_ref, g_ref, t_ref, o_ref):
        o_ref[...] = (g_ref[...].astype(F32) + t_ref[...].astype(F32)).astype(BF16)

    blk = pl.BlockSpec((N_CHIPS, tr, tc), lambda i, j, core_ref: (0, i, j))
    return pl.pallas_call(
        body, name=name,
        grid_spec=pltpu.PrefetchScalarGridSpec(
            num_scalar_prefetch=1, grid=(r // tr, c // tc),
            in_specs=[pl.BlockSpec((None, N_CHIPS, tr, tc), lambda i, j, core_ref: (core_ref[0], 0, i, j)), blk],
            out_specs=blk),
        out_shape=jax.ShapeDtypeStruct((N_CHIPS, r, c), BF16),
        compiler_params=_params("parallel", "parallel"),
    )(core, grads, theirs)


def sum_chips(parts, name):
    _, r, c = parts.shape
    tr, tc = _tiles(r, c, 1 << 18)

    def body(p_ref, o_ref):
        total = p_ref[0].astype(F32)
        for s in range(1, N_CHIPS):
            total = total + p_ref[s].astype(F32)
        o_ref[...] = total

    return pl.pallas_call(
        body, name=name, grid=(r // tr, c // tc),
        in_specs=[pl.BlockSpec((N_CHIPS, tr, tc), lambda i, j: (0, i, j))],
        out_specs=pl.BlockSpec((tr, tc), lambda i, j: (i, j)),
        out_shape=jax.ShapeDtypeStruct((r, c), F32),
        compiler_params=_params("parallel", "parallel"),
    )(parts)


def _adamw_math(w, g, m, v):
    m = ADAM_B1 * m + (1.0 - ADAM_B1) * g
    v = ADAM_B2 * v + (1.0 - ADAM_B2) * (g * g)
    m_hat = m / (1.0 - ADAM_B1 ** ADAM_STEP)
    v_hat = v / (1.0 - ADAM_B2 ** ADAM_STEP)
    delta = -ADAM_LR * (m_hat / (jnp.sqrt(v_hat) + ADAM_EPS) + ADAM_WD * w)
    return delta, m, v


def adamw(w, m, v, g, name, col_halves=False):
    r, c = w.shape
    half = c // 2 if col_halves else c
    tr, tc = _tiles(r, half, 1 << 18, multiple=8)
    per_half = half // tc

    def body(w_ref, m_ref, v_ref, gin_ref, g_ref, d_ref, nm_ref, nv_ref):
        g = gin_ref[...]
        delta, nm, nv = _adamw_math(w_ref[...], g, m_ref[...], v_ref[...])
        g_ref[...] = g
        d_ref[...] = delta
        nm_ref[...] = nm
        nv_ref[...] = nv

    blk = pl.BlockSpec((tr, tc), lambda i, j: (i, j))
    if col_halves:
        g_spec = pl.BlockSpec((None, tr, tc), lambda i, j: (j // per_half, i, j % per_half))
    else:
        g_spec = blk
    shape = jax.ShapeDtypeStruct((r, c), F32)
    return pl.pallas_call(
        body, name=name, grid=(r // tr, c // tc),
        in_specs=[blk, blk, blk, g_spec], out_specs=[blk] * 4, out_shape=[shape] * 4,
        compiler_params=_params("parallel", "parallel"),
    )(w, m, v, g)


MATRICES = ("ffn1_w_gate", "ffn1_w_up", "ffn1_w_down", "w_in", "w_out", "ffn2_w_gate", "ffn2_w_up", "ffn2_w_down")
VECTORS = ("ffn1_norm", "mix_norm", "swa_sinks", "swa_out_norm", "sb_out_norm", "ffn2_norm", "final_norm")
WEIGHTS = ("ffn1_norm", "ffn1_w_gate", "ffn1_w_up", "ffn1_w_down", "mix_norm", "w_in", "swa_sinks", "swa_out_norm",
           "sb_out_norm", "w_out", "ffn2_norm", "ffn2_w_gate", "ffn2_w_up", "ffn2_w_down", "final_norm")


def _pack_rows(vec):
    flat = vec.reshape(-1).astype(F32)
    rows = -(-flat.shape[0] // LANES)
    rows8 = -(-rows // 8) * 8
    flat = jnp.pad(flat, (0, rows8 * LANES - flat.shape[0]))
    return flat.reshape(rows8, LANES)


class GradPipe:
    def __init__(self, name, grads):
        self.name, self.step, self.value = name, 0, grads

    def rider(self):
        make = (half_swap_rider, scatter_rider, final_swap_rider)[self.step]
        return make([self.value])

    def landed(self, outs):
        if self.step == 0:
            self.value = sum_cores(self.value, outs[0], "sum_cores_" + self.name)
        elif self.step == 1:
            self.value = sum_chips(outs[0], "sum_chips_" + self.name)
        else:
            self.value = outs[0]
        self.step += 1


def _hosted(fn, *args, advance=(), **kw):
    outs, routs = fn(*args, riders=[p.rider() for p in advance], **kw)
    for p, ro in zip(advance, routs):
        p.landed(ro)
    return outs


def kernel(x, ffn1_norm, ffn1_w_gate, ffn1_w_up, ffn1_w_down, mix_norm, w_in, swa_sinks, swa_out_norm, sb_out_norm, w_out, ffn2_norm, ffn2_w_gate, ffn2_w_up, ffn2_w_down, final_norm, loss_target, m_ffn1_norm, m_ffn1_w_gate, m_ffn1_w_up, m_ffn1_w_down, m_mix_norm, m_w_in, m_swa_sinks, m_swa_out_norm, m_sb_out_norm, m_w_out, m_ffn2_norm, m_ffn2_w_gate, m_ffn2_w_up, m_ffn2_w_down, m_final_norm, v_ffn1_norm, v_ffn1_w_gate, v_ffn1_w_up, v_ffn1_w_down, v_mix_norm, v_w_in, v_swa_sinks, v_swa_out_norm, v_sb_out_norm, v_w_out, v_ffn2_norm, v_ffn2_w_gate, v_ffn2_w_up, v_ffn2_w_down, v_final_norm):
    given = dict(locals())
    b_loc, seq, D = x.shape
    T = b_loc * seq
    x0 = x.reshape(T, D)
    target = loss_target.reshape(T, D)
    final_g = final_norm.reshape(1, D)

    gathered = {}

    def gather(names):
        shards = []
        for nm in names:
            w = given[nm][0].astype(BF16)
            shards.append(w.reshape(2, w.shape[0] // 2, w.shape[1]))
        return gather_rider(shards)

    def keep(names, outs):
        for nm, o in zip(names, outs):
            gathered[nm] = o.reshape(N_CHIPS, 2 * o.shape[2], o.shape[3])

    first = ("ffn1_w_gate",)
    then = ("ffn1_w_up",)
    second = ("ffn1_w_down", "w_in")
    third = ("w_out",)
    fourth = ("ffn2_w_gate", "ffn2_w_up", "ffn2_w_down")
    keep(first, exchange_now([gather(first)], "gather_ffn1_gate")[0])

    h1 = rmsnorm_cast(x0, ffn1_norm, "ffn1_norm_fwd")
    (g1,), landed = ffn_gate(h1, gathered["ffn1_w_gate"], "ffn1_gate", riders=[gather(then)])
    keep(then, landed[0])
    (u1, a1), landed = ffn_up_act(h1, gathered["ffn1_w_up"], g1, "ffn1_up_act", riders=[gather(second)])
    keep(second, landed[0])
    (x1,), landed = ffn_down(a1, gathered["ffn1_w_down"], x0, "ffn1_down", riders=[gather(third)])
    keep(third, landed[0])
    w_out_all = gathered["w_out"].reshape(-1, D)
    h2 = rmsnorm_cast(x1, mix_norm, "mix_norm_fwd")
    proj = in_proj(h2, gathered["w_in"], "in_proj")
    ya = swa_fwd(proj, swa_sinks, b_loc, seq, "swa_fwd")
    (yb, ltot), landed = sb_fwd(proj, b_loc, seq, "sb_fwd", riders=[gather(fourth)])
    keep(fourth, landed[0])
    y = out_norm(ya, yb, swa_out_norm, sb_out_norm, "out_norm_fwd")
    x2 = out_proj(y, w_out_all, x1, "out_proj")
    h3 = rmsnorm_cast(x2, ffn2_norm, "ffn2_norm_fwd")
    (g2, u2, a2), _ = ffn_up(h3, gathered["ffn2_w_gate"], gathered["ffn2_w_up"], "ffn2_up")
    (x3,), _ = ffn_down(a2, gathered["ffn2_w_down"], x2, "ffn2_down")
    dx3, d_final, loss_part = final_loss(x3, final_g, target, "final_loss")

    Fs = gathered["ffn1_w_gate"].shape[2]
    Ps = gathered["w_in"].shape[2]
    Os = w_out_all.shape[0] // N_CHIPS
    pipes = {}

    def down_grad(tag, act, dx, advance):
        (grads,) = _hosted(wgrad, act, dx, tag + "_dwd", groups=N_CHIPS, m=Fs, n=D // 2,
                           a_block=lambda tk: (None, tk, Fs), a_map=lambda s, j, k: (s, k, 0),
                           b_block=lambda tk: (tk, D // 2), b_map=lambda s, j, k: (k, j), b_scale=0.5,
                           advance=advance)
        return grads

    def up_grad(name, h, d_act, advance):
        (grads,) = _hosted(wgrad, h, d_act, name, groups=N_CHIPS, m=D // 2, n=Fs,
                           a_block=lambda tk: (tk, D // 2), a_map=lambda s, j, k: (k, j),
                           b_block=lambda tk: (None, tk, Fs), b_map=lambda s, j, k: (s, k, 0), advance=advance)
        return grads

    (dg2, du2), _ = ffn_bwd_act(dx3, gathered["ffn2_w_down"], g2, u2, "ffn2_bwd_act")
    pipes["ffn2_w_down"] = p_d2 = GradPipe("ffn2_w_down", down_grad("ffn2", a2, dx3, []))
    pipes["ffn2_w_gate"] = p_g2 = GradPipe("ffn2_w_gate", up_grad("ffn2_dwg", h3, dg2, [p_d2]))
    pipes["ffn2_w_up"] = p_u2 = GradPipe("ffn2_w_up", up_grad("ffn2_dwu", h3, du2, [p_d2, p_g2]))
    (dh3,) = _hosted(ffn_bwd_dh, dg2, du2, gathered["ffn2_w_gate"], gathered["ffn2_w_up"], "ffn2_bwd_dh",
                     advance=[p_d2, p_g2, p_u2])
    (dx2, d_ffn2_norm), _ = rmsnorm_bwd_add(x2, ffn2_norm, dh3, dx3, "ffn2_norm_bwd")

    (d_w_out,) = _hosted(wgrad, y, dx2, "dw_out", groups=N_CHIPS, m=Os // 2, n=D,
                         a_block=lambda tk: (tk, Os // 2), a_map=lambda s, j, k: (k, 2 * s + j),
                         b_block=lambda tk: (tk, D), b_map=lambda s, j, k: (k, 0), advance=[p_g2])
    pipes["w_out"] = p_out = GradPipe("w_out", d_w_out)
    dya, dyb, d_swa_norm, d_sb_norm = out_proj_bwd(dx2, w_out_all, ya, yb, swa_out_norm, sb_out_norm, "out_proj_bwd")
    dqa, dka, dva, d_sinks = _hosted(swa_bwd, proj, swa_sinks, dya, b_loc, seq, "swa_bwd", advance=[p_u2, p_out])
    dqb, dkb, dvb = _hosted(sb_bwd, proj, dyb, ltot, b_loc, seq, "sb_bwd", advance=[p_u2, p_out])
    dproj = jnp.concatenate([dqa, dka, dva, dqb, dkb, dvb], axis=1)
    (d_w_in,) = _hosted(wgrad, h2, dproj, "dw_in", groups=N_CHIPS, m=D // 2, n=Ps,
                        a_block=lambda tk: (tk, D // 2), a_map=lambda s, j, k: (k, j),
                        b_block=lambda tk: (tk, Ps), b_map=lambda s, j, k: (k, s), advance=[p_out])
    pipes["w_in"] = p_in = GradPipe("w_in", d_w_in)
    (dh2,) = _hosted(in_proj_bwd, dproj, gathered["w_in"], "in_proj_bwd", advance=[p_in])
    (dx1, d_mix_norm), _ = rmsnorm_bwd_add(x1, mix_norm, dh2, dx2, "mix_norm_bwd")

    dg1, du1 = _hosted(ffn_bwd_act, dx1, gathered["ffn1_w_down"], g1, u1, "ffn1_bwd_act", advance=[p_in])
    pipes["ffn1_w_down"] = p_d1 = GradPipe("ffn1_w_down", down_grad("ffn1", a1, dx1, [p_in]))
    pipes["ffn1_w_gate"] = p_g1 = GradPipe("ffn1_w_gate", up_grad("ffn1_dwg", h1, dg1, [p_d1]))
    pipes["ffn1_w_up"] = p_u1 = GradPipe("ffn1_w_up", up_grad("ffn1_dwu", h1, du1, [p_d1, p_g1]))
    (dh1,) = _hosted(ffn_bwd_dh, dg1, du1, gathered["ffn1_w_gate"], gathered["ffn1_w_up"], "ffn1_bwd_dh_a",
                     part=(0, 2), advance=[p_d1, p_g1, p_u1])
    (dh1,) = _hosted(ffn_bwd_dh, dg1, du1, gathered["ffn1_w_gate"], gathered["ffn1_w_up"], "ffn1_bwd_dh_b",
                     part=(1, 2), prev=dh1, advance=[p_g1, p_u1])
    dx0, d_ffn1_norm = _hosted(rmsnorm_bwd_add, x0, ffn1_norm, dh1, dx1, "ffn1_norm_bwd", advance=[p_u1])

    vec_grads = dict(ffn1_norm=d_ffn1_norm, mix_norm=d_mix_norm, swa_sinks=d_sinks[:, 0], swa_out_norm=d_swa_norm,
                     sb_out_norm=d_sb_norm, ffn2_norm=d_ffn2_norm, final_norm=d_final)
    packed = [_pack_rows(vec_grads[nm]) for nm in VECTORS] + [_pack_rows(loss_part[0, :1])]
    offsets = [0]
    for p in packed:
        offsets.append(offsets[-1] + p.shape[0])
    reduced = allreduce_small(jnp.concatenate(packed, axis=0), "reduce_vectors")
    loss = reduced[offsets[len(VECTORS)], 0]

    out = {}
    for nm in MATRICES:
        shape = given[nm].shape
        g = pipes[nm].value
        col_halves = nm.endswith("w_down")
        if not col_halves:
            g = g.reshape(shape[1:])
        res = adamw(given[nm][0], given["m_" + nm][0], given["v_" + nm][0], g, "adamw_" + nm, col_halves=col_halves)
        out[nm] = [r.reshape(shape) for r in res]
    w_rows = jnp.concatenate([_pack_rows(given[nm]) for nm in VECTORS], axis=0)
    m_rows = jnp.concatenate([_pack_rows(given["m_" + nm]) for nm in VECTORS], axis=0)
    v_rows = jnp.concatenate([_pack_rows(given["v_" + nm]) for nm in VECTORS], axis=0)
    g_rows = reduced[:offsets[len(VECTORS)]]
    small = adamw(w_rows, m_rows, v_rows, g_rows, "adamw_vectors")
    for i, nm in enumerate(VECTORS):
        shape = given[nm].shape
        size = math.prod(shape)
        out[nm] = [r[offsets[i]:offsets[i + 1]].reshape(-1)[:size].reshape(shape) for r in small]

    grad_x = dx0.reshape(b_loc, seq, D)
    return (loss, grad_x, *[out[nm][0] for nm in WEIGHTS], *[out[nm][1] for nm in WEIGHTS],
            *[out[nm][2] for nm in WEIGHTS], *[out[nm][3] for nm in WEIGHTS])
```

```python
import functools
import math

import jax
import jax.numpy as jnp
from jax import lax
from jax.experimental import pallas as pl
from jax.experimental.pallas import tpu as pltpu

F32 = jnp.float32
BF16 = jnp.bfloat16
MESH = pl.DeviceIdType.MESH

EPS = 1e-6
HEAD_DIM = 64
LANES = 128
N_SWA_HEADS = 16
N_SWA_KV = 4
N_SB_HEADS = 16
WINDOW = 128
SWA_BLOCK = 128
SB_BLOCK = 256
N_CHIPS = 4
ATT_SCALE = HEAD_DIM ** -0.5

ADAM_LR = 0.001
ADAM_B1 = 0.9
ADAM_B2 = 0.999
ADAM_EPS = 1e-08
ADAM_WD = 0.01
ADAM_STEP = 10

VMEM_LIMIT = 56 * 1024 * 1024
NEG_BIG = -1e30

ANY = pl.BlockSpec(memory_space=pl.ANY)
VMEM_WHOLE = pl.BlockSpec(memory_space=pltpu.VMEM)


def _row_tile(rows, cap, multiple=16):
    best = None
    for t in range(multiple, min(rows, cap) + 1, multiple):
        if rows % t == 0:
            best = t
    if best is None:
        raise ValueError(f"no row tile for {rows} rows under {cap}")
    return best


def _params(*sem):
    return pltpu.CompilerParams(dimension_semantics=sem, vmem_limit_bytes=VMEM_LIMIT)


def _dot(a, b):
    return jnp.dot(a, b, preferred_element_type=F32)


def _dot_nt(a, b):
    return lax.dot_general(a, b, (((1,), (1,)), ((), ())), preferred_element_type=F32)


def _dot_tn(a, b):
    return lax.dot_general(a, b, (((0,), (0,)), ((), ())), preferred_element_type=F32)


def _sigmoid(x):
    return 1.0 / (1.0 + jnp.exp(-x))


def _split_dot(x, u):
    hi = x.astype(BF16)
    lo = (x - hi.astype(F32)).astype(BF16)
    return _dot(hi, u) + _dot(lo, u)


def _tiles(r, c, max_elems, multiple=16):
    col_opts = [c] + [t for t in (2048, 1024, 512, 256, 128) if t < c and c % t == 0]
    best = None
    for tc in col_opts:
        for tr in [r] + list(range(multiple, r, multiple)):
            if r % tr == 0 and tr * tc <= max_elems and (best is None or tr * tc > best[0] * best[1]):
                best = (tr, tc)
    if best is None:
        raise ValueError(f"no tile for [{r}, {c}] under {max_elems} elements")
    return best


class Rider:
    def __init__(self, arrays, out_shapes, sem_shapes, start, finish):
        self.arrays, self.out_shapes, self.sem_shapes = list(arrays), list(out_shapes), list(sem_shapes)
        self.start, self.finish = start, finish


def _place():
    return lax.axis_index("x"), lax.axis_index("y"), lax.axis_index("c")


def _other_chips(x, y):
    return [(1 - x, y), (x, 1 - y), (1 - x, 1 - y)]


def _split(ref):
    rows = ref.shape[-2]
    n = next(k for k in (4, 2, 1) if rows % (16 * k) == 0)
    step = rows // n
    lead = [()] if len(ref.shape) == 2 else [(s,) for s in range(ref.shape[0])]
    return [ref.at[(*idx, pl.ds(i * step, step))] for idx in lead for i in range(n)]


class _Copy:
    def __init__(self, src, dst, make):
        self.src, self.dst, self.make = src, dst, make

    def start(self):
        for s, d in zip(_split(self.src), _split(self.dst)):
            self.make(s, d).start()

    def wait(self):
        self.make(self.src, self.dst).wait()

    def wait_send(self):
        self.make(self.src, self.dst).wait_send()

    def wait_recv(self):
        self.make(self.src, self.dst).wait_recv()


def _remote(src, dst, send_sem, recv_sem, device):
    return _Copy(src, dst, lambda s, d: pltpu.make_async_remote_copy(
        src_ref=s, dst_ref=d, send_sem=send_sem, recv_sem=recv_sem, device_id=device, device_id_type=MESH))


def _local(src, dst, sem):
    return _Copy(src, dst, lambda s, d: pltpu.make_async_copy(s, d, sem))


def gather_rider(shards):
    n = len(shards)

    def copies(ins, outs, sems):
        ici_s, ici_r, d2d_s, d2d_r, own_sem = sems
        x, y, c = _place()
        me = 2 * x + y
        own, ici, fwd = [], [], []
        for k in range(n):
            own.append(_local(ins[k], outs[k].at[me], own_sem.at[k]))
            for j, (px, py) in enumerate(_other_chips(x, y)):
                i = 3 * k + j
                ici.append(_remote(ins[k].at[c], outs[k].at[me, c], ici_s.at[i], ici_r.at[i], (px, py, c)))
                got = outs[k].at[2 * px + py, c]
                fwd.append(_remote(got, got, d2d_s.at[i], d2d_r.at[i], (x, y, 1 - c)))
        return own, ici, fwd

    def start(ins, outs, sems):
        own, ici, _ = copies(ins, outs, sems)
        for cp in own + ici:
            cp.start()

    def finish(ins, outs, sems):
        own, ici, fwd = copies(ins, outs, sems)
        for a, f in zip(ici, fwd):
            a.wait_recv()
            f.start()
        for a in ici:
            a.wait_send()
        for f in fwd:
            f.wait()
        for o in own:
            o.wait()

    dma = pltpu.SemaphoreType.DMA
    return Rider(shards, [jax.ShapeDtypeStruct((N_CHIPS,) + s.shape, s.dtype) for s in shards],
                 [dma((3 * n,)), dma((3 * n,)), dma((3 * n,)), dma((3 * n,)), dma((n,))], start, finish)


def half_swap_rider(grads):
    n = len(grads)

    def copies(ins, outs, sems):
        x, y, c = _place()
        return [_remote(ins[k].at[1 - c], outs[k], sems[0].at[k], sems[1].at[k], (x, y, 1 - c)) for k in range(n)]

    def start(ins, outs, sems):
        for cp in copies(ins, outs, sems):
            cp.start()

    def finish(ins, outs, sems):
        for cp in copies(ins, outs, sems):
            cp.wait()

    dma = pltpu.SemaphoreType.DMA
    return Rider(grads, [jax.ShapeDtypeStruct(g.shape[1:], g.dtype) for g in grads], [dma((n,)), dma((n,))],
                 start, finish)


def scatter_rider(halves):
    n = len(halves)

    def copies(ins, outs, sems):
        send, recv, own_sem = sems
        x, y, c = _place()
        me = 2 * x + y
        cps = []
        for k in range(n):
            cps.append(_local(ins[k].at[me], outs[k].at[me], own_sem.at[k]))
            for j, (px, py) in enumerate(_other_chips(x, y)):
                i = 3 * k + j
                cps.append(_remote(ins[k].at[2 * px + py], outs[k].at[me], send.at[i], recv.at[i], (px, py, c)))
        return cps

    def start(ins, outs, sems):
        for cp in copies(ins, outs, sems):
            cp.start()

    def finish(ins, outs, sems):
        for cp in copies(ins, outs, sems):
            cp.wait()

    dma = pltpu.SemaphoreType.DMA
    return Rider(halves, [jax.ShapeDtypeStruct(h.shape, h.dtype) for h in halves],
                 [dma((3 * n,)), dma((3 * n,)), dma((n,))], start, finish)


def final_swap_rider(sums):
    n = len(sums)

    def copies(ins, outs, sems):
        send, recv, own_sem = sems
        x, y, c = _place()
        cps = []
        for k in range(n):
            cps.append(_local(ins[k], outs[k].at[c], own_sem.at[k]))
            cps.append(_remote(ins[k], outs[k].at[c], send.at[k], recv.at[k], (x, y, 1 - c)))
        return cps

    def start(ins, outs, sems):
        for cp in copies(ins, outs, sems):
            cp.start()

    def finish(ins, outs, sems):
        for cp in copies(ins, outs, sems):
            cp.wait()

    dma = pltpu.SemaphoreType.DMA
    return Rider(sums, [jax.ShapeDtypeStruct((2,) + s.shape, s.dtype) for s in sums],
                 [dma((n,)), dma((n,)), dma((n,))], start, finish)


def _call(body, *, name, grid, in_specs, out_specs, out_shape, scratch_shapes=(), sem, args, riders=(), aliases=None):
    in_specs, out_specs, out_shape = list(in_specs), list(out_specs), list(out_shape)
    scratch_shapes = list(scratch_shapes)
    aliases = aliases or {}
    if not riders:
        outs = pl.pallas_call(body, name=name, grid=grid, in_specs=in_specs, out_specs=out_specs, out_shape=out_shape,
                              scratch_shapes=scratch_shapes, input_output_aliases=aliases,
                              compiler_params=_params(*sem))(*args)
        return list(outs), []
    n_in, n_out, n_scr = len(in_specs), len(out_specs), len(scratch_shapes)
    r_in = [len(r.arrays) for r in riders]
    r_out = [len(r.out_shapes) for r in riders]
    r_sem = [len(r.sem_shapes) for r in riders]

    def wrapped(*refs):
        pos = 0

        def take(count):
            nonlocal pos
            got = refs[pos:pos + count]
            pos += count
            return got

        ins = take(n_in)
        rins = [take(k) for k in r_in]
        outs = take(n_out)
        routs = [take(k) for k in r_out]
        scr = take(n_scr)
        rsems = [take(k) for k in r_sem]
        first = functools.reduce(jnp.logical_and, [pl.program_id(a) == 0 for a in range(len(grid))])
        last = functools.reduce(jnp.logical_and, [pl.program_id(a) == grid[a] - 1 for a in range(len(grid))])

        @pl.when(first)
        def _():
            for r, ri, ro, rs in zip(riders, rins, routs, rsems):
                r.start(ri, ro, rs)

        body(*ins, *outs, *scr)

        @pl.when(last)
        def _():
            for r, ri, ro, rs in zip(riders, rins, routs, rsems):
                r.finish(ri, ro, rs)

    any_spec = pl.BlockSpec(memory_space=pl.ANY)
    res = pl.pallas_call(
        wrapped, name=name, grid=grid,
        in_specs=in_specs + [any_spec] * sum(r_in),
        out_specs=out_specs + [any_spec] * sum(r_out),
        out_shape=out_shape + [s for r in riders for s in r.out_shapes],
        scratch_shapes=scratch_shapes + [s for r in riders for s in r.sem_shapes],
        input_output_aliases=aliases,
        compiler_params=_params(*(["arbitrary"] * len(grid))),
    )(*args, *[a for r in riders for a in r.arrays])
    res = list(res)
    outs, rest = res[:n_out], res[n_out:]
    routs = []
    for k in r_out:
        routs.append(rest[:k])
        rest = rest[k:]
    return outs, routs


def exchange_now(riders, name):
    n_in = [len(r.arrays) for r in riders]
    n_out = [len(r.out_shapes) for r in riders]
    n_sem = [len(r.sem_shapes) for r in riders]

    def body(*refs):
        pos = 0
        groups = []
        for counts in (n_in, n_out, n_sem):
            part = []
            for k in counts:
                part.append(refs[pos:pos + k])
                pos += k
            groups.append(part)
        for r, ri, ro, rs in zip(riders, *groups):
            r.start(ri, ro, rs)
        for r, ri, ro, rs in zip(riders, *groups):
            r.finish(ri, ro, rs)

    any_spec = pl.BlockSpec(memory_space=pl.ANY)
    res = pl.pallas_call(
        body, name=name,
        in_specs=[any_spec] * sum(n_in), out_specs=[any_spec] * sum(n_out),
        out_shape=[s for r in riders for s in r.out_shapes],
        scratch_shapes=[s for r in riders for s in r.sem_shapes],
    )(*[a for r in riders for a in r.arrays])
    res = list(res)
    routs = []
    for k in n_out:
        routs.append(res[:k])
        res = res[k:]
    return routs


def rmsnorm_cast(x, gamma, name):
    T, D = x.shape
    tr = min(512, T)

    def body(x_ref, g_ref, o_ref):
        xv = x_ref[...]
        r = lax.rsqrt(jnp.mean(xv * xv, axis=-1, keepdims=True) + EPS)
        o_ref[...] = ((xv * r) * g_ref[...]).astype(BF16)

    return pl.pallas_call(
        body, name=name, grid=(T // tr,),
        in_specs=[pl.BlockSpec((tr, D), lambda i: (i, 0)), pl.BlockSpec((1, D), lambda i: (0, 0))],
        out_specs=pl.BlockSpec((tr, D), lambda i: (i, 0)),
        out_shape=jax.ShapeDtypeStruct((T, D), BF16),
        compiler_params=_params("parallel"),
    )(x, gamma)


def _rms_bwd(xv, gamma, dy):
    d = xv.shape[-1]
    r = lax.rsqrt(jnp.mean(xv * xv, axis=-1, keepdims=True) + EPS)
    gdy = dy * gamma
    dot = jnp.sum(gdy * xv, axis=-1, keepdims=True)
    dx = r * gdy - xv * ((r * r * r) * (dot / d))
    dgamma = jnp.sum(dy * (xv * r), axis=0, keepdims=True)
    return dx, dgamma


def rmsnorm_bwd_add(x, gamma, dy, dx_up, name, riders=()):
    T, D = x.shape
    tr = min(256, T)

    def body(x_ref, g_ref, dy_ref, up_ref, dx_ref, dg_ref):
        dx, dgamma = _rms_bwd(x_ref[...], g_ref[...], dy_ref[...])
        dx_ref[...] = up_ref[...] + dx

        @pl.when(pl.program_id(0) == 0)
        def _():
            dg_ref[...] = jnp.zeros_like(dg_ref)

        dg_ref[...] += dgamma

    row = pl.BlockSpec((tr, D), lambda i: (i, 0))
    vec = pl.BlockSpec((1, D), lambda i: (0, 0))
    return _call(
        body, name=name, grid=(T // tr,),
        in_specs=[row, vec, row, row], out_specs=[row, vec],
        out_shape=[jax.ShapeDtypeStruct((T, D), F32), jax.ShapeDtypeStruct((1, D), F32)],
        sem=("arbitrary",), args=(x, gamma, dy, dx_up), riders=riders)


def ffn_up(h, wg, wu, name, riders=()):
    T, D = h.shape
    G, _, Fs = wg.shape
    tm = min(512, T)

    def body(h_ref, wg_ref, wu_ref, g_ref, u_ref, a_ref):
        hv = h_ref[...]
        g = _dot(hv, wg_ref[...])
        u = _dot(hv, wu_ref[...])
        g_ref[...] = g
        u_ref[...] = u
        a_ref[...] = ((g * _sigmoid(g)) * u).astype(BF16)

    w_spec = pl.BlockSpec((None, D, Fs), lambda s, i: (s, 0, 0), pipeline_mode=pl.Buffered(1))
    o_spec = pl.BlockSpec((None, tm, Fs), lambda s, i: (s, i, 0))
    return _call(
        body, name=name, grid=(G, T // tm),
        in_specs=[pl.BlockSpec((tm, D), lambda s, i: (i, 0)), w_spec, w_spec],
        out_specs=[o_spec, o_spec, o_spec],
        out_shape=[jax.ShapeDtypeStruct((G, T, Fs), F32), jax.ShapeDtypeStruct((G, T, Fs), F32),
                   jax.ShapeDtypeStruct((G, T, Fs), BF16)],
        sem=("parallel", "parallel"), args=(h, wg, wu), riders=riders)


def ffn_gate(h, wg, name, riders=()):
    T, D = h.shape
    G, _, Fs = wg.shape
    tm = min(512, T)

    def body(h_ref, wg_ref, g_ref):
        g_ref[...] = _dot(h_ref[...], wg_ref[...])

    return _call(
        body, name=name, grid=(G, T // tm),
        in_specs=[pl.BlockSpec((tm, D), lambda s, i: (i, 0)),
                  pl.BlockSpec((None, D, Fs), lambda s, i: (s, 0, 0), pipeline_mode=pl.Buffered(1))],
        out_specs=[pl.BlockSpec((None, tm, Fs), lambda s, i: (s, i, 0))],
        out_shape=[jax.ShapeDtypeStruct((G, T, Fs), F32)],
        sem=("parallel", "parallel"), args=(h, wg), riders=riders)


def ffn_up_act(h, wu, g, name, riders=()):
    T, D = h.shape
    G, _, Fs = wu.shape
    tm = min(512, T)

    def body(h_ref, wu_ref, g_ref, u_ref, a_ref):
        u = _dot(h_ref[...], wu_ref[...])
        g = g_ref[...]
        u_ref[...] = u
        a_ref[...] = ((g * _sigmoid(g)) * u).astype(BF16)

    act = pl.BlockSpec((None, tm, Fs), lambda s, i: (s, i, 0))
    return _call(
        body, name=name, grid=(G, T // tm),
        in_specs=[pl.BlockSpec((tm, D), lambda s, i: (i, 0)),
                  pl.BlockSpec((None, D, Fs), lambda s, i: (s, 0, 0), pipeline_mode=pl.Buffered(1)), act],
        out_specs=[act, act],
        out_shape=[jax.ShapeDtypeStruct((G, T, Fs), F32), jax.ShapeDtypeStruct((G, T, Fs), BF16)],
        sem=("parallel", "parallel"), args=(h, wu, g), riders=riders)


def ffn_down(a, wd, x, name, riders=()):
    G, T, Fs = a.shape
    D = wd.shape[2]
    tm = min(512, T)

    def body(a_ref, w_ref, x_ref, o_ref, acc):
        s = pl.program_id(1)

        @pl.when(s == 0)
        def _():
            acc[...] = jnp.zeros_like(acc)

        acc[...] += _dot(a_ref[...], w_ref[...])

        @pl.when(s == G - 1)
        def _():
            o_ref[...] = x_ref[...] + 0.5 * acc[...]

    return _call(
        body, name=name, grid=(T // tm, G),
        in_specs=[pl.BlockSpec((None, tm, Fs), lambda i, s: (s, i, 0)),
                  pl.BlockSpec((None, Fs, D), lambda i, s: (s, 0, 0)),
                  pl.BlockSpec((tm, D), lambda i, s: (i, 0))],
        out_specs=[pl.BlockSpec((tm, D), lambda i, s: (i, 0))],
        out_shape=[jax.ShapeDtypeStruct((T, D), F32)],
        scratch_shapes=[pltpu.VMEM((tm, D), F32)],
        sem=("parallel", "arbitrary"), args=(a, wd, x), riders=riders)


def ffn_bwd_act(dx, wd, g, u, name, riders=()):
    T, D = dx.shape
    G, Fs, _ = wd.shape
    tm = min(256, T)

    def body(dx_ref, w_ref, g_ref, u_ref, dg_ref, du_ref):
        da = _dot_nt((0.5 * dx_ref[...]).astype(BF16), w_ref[...])
        gv = g_ref[...]
        sig = _sigmoid(gv)
        silu = gv * sig
        du_ref[...] = (da * silu).astype(BF16)
        dg_ref[...] = ((da * u_ref[...]) * (sig * (1.0 + gv * (1.0 - sig)))).astype(BF16)

    act = pl.BlockSpec((None, tm, Fs), lambda s, i: (s, i, 0))
    return _call(
        body, name=name, grid=(G, T // tm),
        in_specs=[pl.BlockSpec((tm, D), lambda s, i: (i, 0)),
                  pl.BlockSpec((None, Fs, D), lambda s, i: (s, 0, 0), pipeline_mode=pl.Buffered(1)),
                  act, act],
        out_specs=[act, act],
        out_shape=[jax.ShapeDtypeStruct((G, T, Fs), BF16), jax.ShapeDtypeStruct((G, T, Fs), BF16)],
        sem=("parallel", "parallel"), args=(dx, wd, g, u), riders=riders)


def ffn_bwd_dh(dg, du, wg, wu, name, riders=(), part=(0, 1), prev=None):
    G, T, Fs = dg.shape
    D = wg.shape[1]
    tm = min(512, T // part[1])
    nblk = T // tm // part[1]
    off = part[0] * nblk

    def body(dg_ref, du_ref, wg_ref, wu_ref, *rest):
        o_ref, acc = rest[-2:]
        s = pl.program_id(1)

        @pl.when(s == 0)
        def _():
            acc[...] = jnp.zeros_like(acc)

        acc[...] += _dot_nt(dg_ref[...], wg_ref[...])
        acc[...] += _dot_nt(du_ref[...], wu_ref[...])

        @pl.when(s == G - 1)
        def _():
            o_ref[...] = acc[...]

    act = pl.BlockSpec((None, tm, Fs), lambda i, s: (s, i + off, 0))
    w_spec = pl.BlockSpec((None, D, Fs), lambda i, s: (s, 0, 0))
    extra = [] if prev is None else [prev]
    return _call(
        body, name=name, grid=(nblk, G),
        in_specs=[act, act, w_spec, w_spec] + [pl.BlockSpec(memory_space=pl.ANY)] * len(extra),
        out_specs=[pl.BlockSpec((tm, D), lambda i, s: (i + off, 0))],
        out_shape=[jax.ShapeDtypeStruct((T, D), F32)],
        scratch_shapes=[pltpu.VMEM((tm, D), F32)],
        sem=("parallel", "arbitrary"), args=(dg, du, wg, wu, *extra), riders=riders,
        aliases={4: 0} if extra else None)


def wgrad(a, b, name, *, groups, m, n, a_block, a_map, b_block, b_map, b_scale=None, tk=512, riders=()):
    T = a.shape[-2]
    tk = min(tk, T)
    nk = T // tk

    def body(a_ref, b_ref, o_ref, acc):
        k = pl.program_id(2)

        @pl.when(k == 0)
        def _():
            acc[...] = jnp.zeros_like(acc)

        bv = b_ref[...]
        if b_scale is not None:
            bv = b_scale * bv
        acc[...] += _dot_tn(a_ref[...].astype(BF16), bv.astype(BF16))

        @pl.when(k == nk - 1)
        def _():
            o_ref[...] = acc[...].astype(BF16)

    return _call(
        body, name=name, grid=(groups, 2, nk),
        in_specs=[pl.BlockSpec(a_block(tk), a_map), pl.BlockSpec(b_block(tk), b_map)],
        out_specs=[pl.BlockSpec((None, None, m, n), lambda s, j, k: (j, s, 0, 0))],
        out_shape=[jax.ShapeDtypeStruct((2, groups, m, n), BF16)],
        scratch_shapes=[pltpu.VMEM((m, n), F32)],
        sem=("parallel", "parallel", "arbitrary"), args=(a, b), riders=riders)


def in_proj(h, w_in, name, riders=()):
    T, D = h.shape
    G, _, Ps = w_in.shape
    tm = min(512, T)

    def body(h_ref, w_ref, o_ref):
        o_ref[...] = _dot(h_ref[...], w_ref[...]).astype(BF16)

    return _call(
        body, name=name, grid=(G, T // tm),
        in_specs=[pl.BlockSpec((tm, D), lambda s, i: (i, 0)),
                  pl.BlockSpec((None, D, Ps), lambda s, i: (s, 0, 0))],
        out_specs=[pl.BlockSpec((tm, Ps), lambda s, i: (i, s))],
        out_shape=[jax.ShapeDtypeStruct((T, G * Ps), BF16)],
        sem=("parallel", "parallel"), args=(h, w_in), riders=riders)


def in_proj_bwd(dproj, w_in, name, riders=()):
    T = dproj.shape[0]
    G, D, Ps = w_in.shape
    tm = min(512, T)

    def body(d_ref, w_ref, o_ref, acc):
        s = pl.program_id(1)

        @pl.when(s == 0)
        def _():
            acc[...] = jnp.zeros_like(acc)

        acc[...] += _dot_nt(d_ref[...], w_ref[...])

        @pl.when(s == G - 1)
        def _():
            o_ref[...] = acc[...]

    return _call(
        body, name=name, grid=(T // tm, G),
        in_specs=[pl.BlockSpec((tm, Ps), lambda i, s: (i, s)),
                  pl.BlockSpec((None, D, Ps), lambda i, s: (s, 0, 0))],
        out_specs=[pl.BlockSpec((tm, D), lambda i, s: (i, 0))],
        out_shape=[jax.ShapeDtypeStruct((T, D), F32)],
        scratch_shapes=[pltpu.VMEM((tm, D), F32)],
        sem=("parallel", "arbitrary"), args=(dproj, w_in), riders=riders)


def out_proj(y, w_out, x, name):
    T, M = y.shape
    D = w_out.shape[1]
    tm = min(512, T)

    def body(y_ref, w_ref, x_ref, o_ref):
        o_ref[...] = x_ref[...] + _dot(y_ref[...], w_ref[...])

    return pl.pallas_call(
        body, name=name, grid=(T // tm,),
        in_specs=[pl.BlockSpec((tm, M), lambda i: (i, 0)), pl.BlockSpec((M, D), lambda i: (0, 0)),
                  pl.BlockSpec((tm, D), lambda i: (i, 0))],
        out_specs=pl.BlockSpec((tm, D), lambda i: (i, 0)),
        out_shape=jax.ShapeDtypeStruct((T, D), F32),
        compiler_params=_params("parallel"),
    )(y, w_out, x)


def out_norm(ya, yb, ga, gb, name):
    T, W = ya.shape
    tr = min(512, T)

    def body(a_ref, b_ref, ga_ref, gb_ref, o_ref):
        for src, gam, col in ((a_ref, ga_ref, 0), (b_ref, gb_ref, W)):
            v = src[...]
            r = lax.rsqrt(jnp.mean(v * v, axis=-1, keepdims=True) + EPS)
            o_ref[:, col:col + W] = ((v * r) * gam[...]).astype(BF16)

    row = pl.BlockSpec((tr, W), lambda i: (i, 0))
    vec = pl.BlockSpec((1, W), lambda i: (0, 0))
    return pl.pallas_call(
        body, name=name, grid=(T // tr,),
        in_specs=[row, row, vec, vec],
        out_specs=pl.BlockSpec((tr, 2 * W), lambda i: (i, 0)),
        out_shape=jax.ShapeDtypeStruct((T, 2 * W), BF16),
        compiler_params=_params("parallel"),
    )(ya, yb, ga, gb)


def out_proj_bwd(dx, w_out, ya, yb, ga, gb, name):
    T, D = dx.shape
    M = w_out.shape[0]
    W = M // 2
    tm = min(256, T)

    def body(dx_ref, w_ref, a_ref, b_ref, ga_ref, gb_ref, da_ref, db_ref, dga_ref, dgb_ref):
        dy = _dot_nt(dx_ref[...].astype(BF16), w_ref[...])

        @pl.when(pl.program_id(0) == 0)
        def _():
            dga_ref[...] = jnp.zeros_like(dga_ref)
            dgb_ref[...] = jnp.zeros_like(dgb_ref)

        d_a, dg_a = _rms_bwd(a_ref[...], ga_ref[...], dy[:, :W])
        d_b, dg_b = _rms_bwd(b_ref[...], gb_ref[...], dy[:, W:])
        da_ref[...] = d_a
        db_ref[...] = d_b
        dga_ref[...] += dg_a
        dgb_ref[...] += dg_b

    row = pl.BlockSpec((tm, W), lambda i: (i, 0))
    vec = pl.BlockSpec((1, W), lambda i: (0, 0))
    return pl.pallas_call(
        body, name=name, grid=(T // tm,),
        in_specs=[pl.BlockSpec((tm, D), lambda i: (i, 0)), pl.BlockSpec((M, D), lambda i: (0, 0)),
                  row, row, vec, vec],
        out_specs=[row, row, vec, vec],
        out_shape=[jax.ShapeDtypeStruct((T, W), F32), jax.ShapeDtypeStruct((T, W), F32),
                   jax.ShapeDtypeStruct((1, W), F32), jax.ShapeDtypeStruct((1, W), F32)],
        compiler_params=_params("arbitrary"),
    )(dx, w_out, ya, yb, ga, gb)


def final_loss(x, gamma, target, name):
    T, D = x.shape
    tr = min(256, T)

    def body(x_ref, g_ref, t_ref, dx_ref, dg_ref, loss_ref):
        xv = x_ref[...]
        gam = g_ref[...]
        r = lax.rsqrt(jnp.mean(xv * xv, axis=-1, keepdims=True) + EPS)
        err = (xv * r) * gam - t_ref[...]
        part = 0.5 * jnp.sum(jnp.mean(err * err, axis=-1, keepdims=True), axis=0, keepdims=True)
        dx, dgamma = _rms_bwd(xv, gam, err / D)
        dx_ref[...] = dx

        @pl.when(pl.program_id(0) == 0)
        def _():
            dg_ref[...] = jnp.zeros_like(dg_ref)
            loss_ref[...] = jnp.zeros_like(loss_ref)

        dg_ref[...] += dgamma
        loss_ref[...] += jnp.broadcast_to(part, loss_ref.shape)

    row = pl.BlockSpec((tr, D), lambda i: (i, 0))
    vec = pl.BlockSpec((1, D), lambda i: (0, 0))
    return pl.pallas_call(
        body, name=name, grid=(T // tr,),
        in_specs=[row, vec, row],
        out_specs=[row, vec, pl.BlockSpec((1, LANES), lambda i: (0, 0))],
        out_shape=[jax.ShapeDtypeStruct((T, D), F32), jax.ShapeDtypeStruct((1, D), F32),
                   jax.ShapeDtypeStruct((1, LANES), F32)],
        compiler_params=_params("arbitrary"),
    )(x, gamma, target)


def _lane_select(src_half, dst_halves):
    r = lax.broadcasted_iota(jnp.int32, (LANES, LANES), 0)
    c = lax.broadcasted_iota(jnp.int32, (LANES, LANES), 1)
    hit = ((r >> 6) == src_half) & ((r & (HEAD_DIM - 1)) == (c & (HEAD_DIM - 1)))
    ok = functools.reduce(jnp.logical_or, [(c >> 6) == d for d in dst_halves])
    return jnp.where(hit & ok, 1.0, 0.0).astype(BF16)


def _swa_slope(kvp, local_head):
    lo = 2.0 ** (-8.0 * (local_head + 1) / N_SWA_HEADS)
    hi = 2.0 ** (-8.0 * (8 + local_head + 1) / N_SWA_HEADS)
    return jnp.where(kvp == 0, jnp.float32(lo), jnp.float32(hi))


def _swa_probs(qblk, kwin, n, start, slope, sink):
    z = _dot_nt(qblk, kwin) * ATT_SCALE
    t = n * SWA_BLOCK + lax.broadcasted_iota(jnp.int32, z.shape, 0)
    s = start + lax.broadcasted_iota(jnp.int32, z.shape, 1)
    dist = t - s
    valid = (dist >= 0) & (dist < WINDOW)
    z = jnp.where(valid, z - slope * dist.astype(F32), NEG_BIG)
    m = jnp.maximum(jnp.max(z, axis=-1, keepdims=True), sink)
    e = jnp.exp(z - m)
    den = jnp.sum(e, axis=-1, keepdims=True) + jnp.exp(sink - m)
    return e / den, m, den


def swa_fwd(proj, sinks, b_loc, seq, name, riders=()):
    T = proj.shape[0]
    nblk = seq // SWA_BLOCK

    def body(sink_ref, q_ref, k_ref, v_ref, o_ref, qs, vv):
        kvp = pl.program_id(1)
        lane = lax.broadcasted_iota(jnp.int32, (SWA_BLOCK, LANES), 1)
        for kvh in range(2):
            both = _lane_select(kvh, (0, 1))
            vv[...] = _dot(v_ref[...], both).astype(BF16)
            for qp in (2 * kvh, 2 * kvh + 1):
                cols = slice(LANES * qp, LANES * (qp + 1))
                for a in range(2):
                    qs[a] = _dot(q_ref[:, cols], _lane_select(a, (kvh,))).astype(BF16)
                slopes = [_swa_slope(kvp, 2 * qp + a) for a in range(2)]
                snk = [sink_ref[0, 8 * kvp + 2 * qp + a] for a in range(2)]

                def blk(n, carry):
                    start = pl.multiple_of(jnp.maximum(n - 1, 0) * SWA_BLOCK, SWA_BLOCK)
                    rows = pl.ds(pl.multiple_of(n * SWA_BLOCK, SWA_BLOCK), SWA_BLOCK)
                    win = pl.ds(start, 2 * SWA_BLOCK)
                    outs = []
                    for a in range(2):
                        p, _, _ = _swa_probs(qs[a, rows, :], k_ref[win, :], n, start, slopes[a], snk[a])
                        outs.append(_dot(p.astype(BF16), vv[win, :]))
                    o_ref[rows, cols] = jnp.where(lane < HEAD_DIM, outs[0], outs[1])
                    return carry

                lax.fori_loop(0, nblk, blk, 0)

    return _call(
        body, name=name, grid=(b_loc, 2),
        in_specs=[pl.BlockSpec(memory_space=pltpu.SMEM),
                  pl.BlockSpec((seq, 4 * LANES), lambda b, p: (b, p)),
                  pl.BlockSpec((seq, LANES), lambda b, p: (b, 8 + p)),
                  pl.BlockSpec((seq, LANES), lambda b, p: (b, 10 + p))],
        out_specs=[pl.BlockSpec((seq, 4 * LANES), lambda b, p: (b, p))],
        out_shape=[jax.ShapeDtypeStruct((T, N_SWA_HEADS * HEAD_DIM), F32)],
        scratch_shapes=[pltpu.VMEM((2, seq, LANES), BF16), pltpu.VMEM((seq, LANES), BF16)],
        sem=("parallel", "parallel"), args=(sinks, proj, proj, proj), riders=riders)


def swa_bwd(proj, sinks, dya, b_loc, seq, name, riders=()):
    T = proj.shape[0]
    nblk = seq // SWA_BLOCK

    def body(sink_ref, q_ref, k_ref, v_ref, do_ref, dq_ref, dk_ref, dv_ref, ds_ref, qs, dos, kk, dk_acc, dv_acc):
        kvp = pl.program_id(0)
        b = pl.program_id(1)
        lane = lax.broadcasted_iota(jnp.int32, (SWA_BLOCK, LANES), 1)

        @pl.when(b == 0)
        def _():
            ds_ref[...] = jnp.zeros_like(ds_ref)

        dk_acc[...] = jnp.zeros_like(dk_acc)
        dv_acc[...] = jnp.zeros_like(dv_acc)
        for kvh in range(2):
            kk[...] = _dot(k_ref[...], _lane_select(kvh, (0, 1))).astype(BF16)
            for qp in (2 * kvh, 2 * kvh + 1):
                cols = slice(LANES * qp, LANES * (qp + 1))
                for a in range(2):
                    move = _lane_select(a, (kvh,))
                    qs[a] = _dot(q_ref[:, cols], move).astype(BF16)
                    dos[a] = _dot(do_ref[:, cols].astype(BF16), move).astype(BF16)
                slopes = [_swa_slope(kvp, 2 * qp + a) for a in range(2)]
                snk = [sink_ref[0, 8 * kvp + 2 * qp + a] for a in range(2)]

                def blk(n, carry):
                    start = pl.multiple_of(jnp.maximum(n - 1, 0) * SWA_BLOCK, SWA_BLOCK)
                    rows = pl.ds(pl.multiple_of(n * SWA_BLOCK, SWA_BLOCK), SWA_BLOCK)
                    win = pl.ds(start, 2 * SWA_BLOCK)
                    dqs = []
                    for a in range(2):
                        qblk = qs[a, rows, :]
                        doblk = dos[a, rows, :]
                        p, m, den = _swa_probs(qblk, k_ref[win, :], n, start, slopes[a], snk[a])
                        dp = _dot_nt(doblk, v_ref[win, :])
                        delta = jnp.sum(p * dp, axis=-1, keepdims=True)
                        p_sink = jnp.exp(snk[a] - m) / den
                        head = 2 * qp + a
                        ds_ref[head:head + 1, :] -= jnp.broadcast_to(
                            jnp.sum(p_sink * delta, axis=0, keepdims=True), (1, LANES))
                        dsc = ((p * (dp - delta)) * ATT_SCALE).astype(BF16)
                        dv_acc[win, :] += _dot_tn(p.astype(BF16), doblk)
                        dk_acc[win, :] += _dot_tn(dsc, qblk)
                        dqs.append(_dot(dsc, kk[win, :]))
                    dq_ref[rows, cols] = jnp.where(lane < HEAD_DIM, dqs[0], dqs[1]).astype(BF16)
                    return carry

                lax.fori_loop(0, nblk, blk, 0)
        dk_ref[...] = dk_acc[...].astype(BF16)
        dv_ref[...] = dv_acc[...].astype(BF16)

    return _call(
        body, name=name, grid=(2, b_loc),
        in_specs=[pl.BlockSpec(memory_space=pltpu.SMEM),
                  pl.BlockSpec((seq, 4 * LANES), lambda p, b: (b, p)),
                  pl.BlockSpec((seq, LANES), lambda p, b: (b, 8 + p)),
                  pl.BlockSpec((seq, LANES), lambda p, b: (b, 10 + p)),
                  pl.BlockSpec((seq, 4 * LANES), lambda p, b: (b, p))],
        out_specs=[pl.BlockSpec((seq, 4 * LANES), lambda p, b: (b, p)),
                   pl.BlockSpec((seq, LANES), lambda p, b: (b, p)),
                   pl.BlockSpec((seq, LANES), lambda p, b: (b, p)),
                   pl.BlockSpec((8, LANES), lambda p, b: (p, 0))],
        out_shape=[jax.ShapeDtypeStruct((T, N_SWA_HEADS * HEAD_DIM), BF16),
                   jax.ShapeDtypeStruct((T, N_SWA_KV * HEAD_DIM), BF16),
                   jax.ShapeDtypeStruct((T, N_SWA_KV * HEAD_DIM), BF16),
                   jax.ShapeDtypeStruct((N_SWA_HEADS, LANES), F32)],
        scratch_shapes=[pltpu.VMEM((2, seq, LANES), BF16), pltpu.VMEM((2, seq, LANES), BF16),
                        pltpu.VMEM((seq, LANES), BF16),
                        pltpu.VMEM((seq, LANES), F32), pltpu.VMEM((seq, LANES), F32)],
        sem=("arbitrary", "arbitrary"), args=(sinks, proj, proj, proj, dya), riders=riders)


def _softplus(z):
    return jnp.maximum(z, 0.0) + jnp.log(1.0 + jnp.exp(-jnp.abs(z)))


def _tri(n, rel):
    r = lax.broadcasted_iota(jnp.int32, (n, n), 0)
    c = lax.broadcasted_iota(jnp.int32, (n, n), 1)
    return jnp.where(rel(r, c), 1.0, 0.0).astype(BF16)


def sb_fwd(proj, b_loc, seq, name, riders=()):
    T = proj.shape[0]
    bq = min(SB_BLOCK, seq)
    nq = seq // bq
    npair = N_SB_HEADS // 2

    def body(q_ref, k_ref, v_ref, o_ref, lt_ref):
        i = pl.program_id(2)
        lane = lax.broadcasted_iota(jnp.int32, (bq, LANES), 1)
        after = _tri(bq, lambda r, c: r > c)
        tpos = i * bq + lax.broadcasted_iota(jnp.int32, (bq, bq), 0)
        col = lax.broadcasted_iota(jnp.int32, (bq, bq), 1)
        qv = q_ref[...] * jnp.asarray(ATT_SCALE, BF16)
        qs = [jnp.where((lane >> 6) == h, qv, jnp.zeros((bq, LANES), BF16)) for h in range(2)]

        def tile(j, state, mask):
            rows = pl.ds(pl.multiple_of(j * bq, bq), bq)
            kj, vj = k_ref[rows, :], v_ref[rows, :]
            new = []
            for h in range(2):
                run, acc = state[2 * h], state[2 * h + 1]
                z = _dot_nt(qs[h], kj)
                sp = _softplus(z)
                lg = -sp if mask is None else jnp.where(mask, -sp, 0.0)
                a = jnp.exp((z - sp) + (_split_dot(lg, after) + run))
                if mask is not None:
                    a = jnp.where(mask, a, 0.0)
                new += [run + jnp.sum(lg, axis=-1, keepdims=True), acc + _dot(a.astype(BF16), vj)]
            return tuple(new)

        zero = (jnp.zeros((bq, 1), F32), jnp.zeros((bq, LANES), F32))
        state = tile(i, zero + zero, (i * bq + col) < tpos)
        state = lax.fori_loop(0, i, lambda jj, st: tile(i - 1 - jj, st, None), state)
        o_ref[...] = jnp.where(lane < HEAD_DIM, state[1], state[3])
        lt_ref[...] = jnp.where(lane < HEAD_DIM, state[0], state[2])

    return _call(
        body, name=name, grid=(b_loc, npair, nq),
        in_specs=[pl.BlockSpec((bq, LANES), lambda b, p, i: (b * nq + i, 12 + p)),
                  pl.BlockSpec((seq, LANES), lambda b, p, i: (b, 20 + p)),
                  pl.BlockSpec((seq, LANES), lambda b, p, i: (b, 28 + p))],
        out_specs=[pl.BlockSpec((bq, LANES), lambda b, p, i: (b * nq + i, p)),
                   pl.BlockSpec((None, bq, LANES), lambda b, p, i: (b * npair + p, i, 0))],
        out_shape=[jax.ShapeDtypeStruct((T, N_SB_HEADS * HEAD_DIM), F32),
                   jax.ShapeDtypeStruct((b_loc * npair, seq, LANES), F32)],
        sem=("parallel", "parallel", "parallel"), args=(proj, proj, proj), riders=riders)


def sb_bwd(proj, dyb, ltot, b_loc, seq, name, riders=()):
    T = proj.shape[0]
    bq = min(SB_BLOCK, seq)
    nq = seq // bq
    npair = N_SB_HEADS // 2

    def body(q_ref, k_ref, v_ref, do_ref, lt_ref, dq_ref, dk_ref, dv_ref, dk_acc, dv_acc):
        lane = lax.broadcasted_iota(jnp.int32, (bq, LANES), 1)
        upto = _tri(bq, lambda r, c: r <= c)
        before = _tri(bq, lambda r, c: r < c)
        row = lax.broadcasted_iota(jnp.int32, (bq, bq), 0)
        col = lax.broadcasted_iota(jnp.int32, (bq, bq), 1)
        dk_acc[...] = jnp.zeros_like(dk_acc)
        dv_acc[...] = jnp.zeros_like(dv_acc)
        scale = jnp.asarray(ATT_SCALE, BF16)
        heads = [(lane >> 6) == h for h in range(2)]
        blank = jnp.zeros((bq, LANES), BF16)

        def qblock(i, carry):
            qrows = pl.ds(pl.multiple_of(i * bq, bq), bq)
            qv = q_ref[qrows, :] * scale
            dov = do_ref[qrows, :].astype(BF16)
            ltv = lt_ref[qrows, :]
            qs = [jnp.where(m, qv, blank) for m in heads]
            dos = [jnp.where(m, dov, blank) for m in heads]
            totals = [jnp.sum(jnp.where(lane == h * HEAD_DIM, ltv, 0.0), axis=-1, keepdims=True) for h in range(2)]

            def tile(j, state, mask):
                krows = pl.ds(pl.multiple_of(j * bq, bq), bq)
                kj, vj = k_ref[krows, :], v_ref[krows, :]
                kv = kj * scale
                dq = state[4]
                dk = dv = None
                new = []
                for h in range(2):
                    l_pre, g_pre = state[2 * h], state[2 * h + 1]
                    z = _dot_nt(qs[h], kj)
                    sp = _softplus(z)
                    lg = -sp if mask is None else jnp.where(mask, -sp, 0.0)
                    log_beta = z - sp
                    a = jnp.exp(log_beta + (totals[h] - (_split_dot(lg, upto) + l_pre)))
                    if mask is not None:
                        a = jnp.where(mask, a, 0.0)
                    g = a * _dot_nt(dos[h], vj)
                    beta = jnp.exp(log_beta)
                    dz = g * (1.0 - beta) - beta * (_split_dot(g, before) + g_pre)
                    if mask is not None:
                        dz = jnp.where(mask, dz, 0.0)
                    dzb = dz.astype(BF16)
                    dv_h = _dot_tn(a.astype(BF16), dos[h])
                    dk_h = _dot_tn(dzb, qs[h])
                    dv = dv_h if dv is None else dv + dv_h
                    dk = dk_h if dk is None else dk + dk_h
                    dq = dq + _dot(dzb, jnp.where(heads[h], kv, blank))
                    new += [l_pre + jnp.sum(lg, axis=-1, keepdims=True), g_pre + jnp.sum(g, axis=-1, keepdims=True)]
                dv_acc[krows, :] += dv
                dk_acc[krows, :] += dk
                return (*new, dq)

            zero = jnp.zeros((bq, 1), F32)
            state = lax.fori_loop(0, i, lambda j, st: tile(j, st, None),
                                  (zero, zero, zero, zero, jnp.zeros((bq, LANES), F32)))
            state = tile(i, state, col < row)
            dq_ref[qrows, :] = state[4].astype(BF16)
            return carry

        lax.fori_loop(0, nq, qblock, 0)
        dk_ref[...] = dk_acc[...].astype(BF16)
        dv_ref[...] = dv_acc[...].astype(BF16)

    pair = pl.BlockSpec((seq, LANES), lambda b, p: (b, p))
    shape = jax.ShapeDtypeStruct((T, N_SB_HEADS * HEAD_DIM), BF16)
    return _call(
        body, name=name, grid=(b_loc, npair),
        in_specs=[pl.BlockSpec((seq, LANES), lambda b, p: (b, 12 + p)),
                  pl.BlockSpec((seq, LANES), lambda b, p: (b, 20 + p)),
                  pl.BlockSpec((seq, LANES), lambda b, p: (b, 28 + p)),
                  pair,
                  pl.BlockSpec((None, seq, LANES), lambda b, p: (b * npair + p, 0, 0))],
        out_specs=[pair, pair, pair],
        out_shape=[shape, shape, shape],
        scratch_shapes=[pltpu.VMEM((seq, LANES), F32)] * 2,
        sem=("parallel", "parallel"), args=(proj, proj, proj, dyb, ltot), riders=riders)


def allreduce_small(v, name):
    R = v.shape[0]
    ndev = 8

    def body(v_ref, o_ref, land, send_sems, recv_sems):
        x, y, c = _place()
        me = 4 * x + 2 * y + c
        land[me] = v_ref[...]
        started = []
        for f in range(1, ndev):
            fx, fy, fc = (f >> 2) & 1, (f >> 1) & 1, f & 1
            peer = ((1 - x) if fx else x, (1 - y) if fy else y, (1 - c) if fc else c)
            cp = pltpu.make_async_remote_copy(
                src_ref=v_ref, dst_ref=land.at[me], send_sem=send_sems.at[f - 1], recv_sem=recv_sems.at[f - 1],
                device_id=peer, device_id_type=MESH)
            cp.start()
            started.append(cp)
        for cp in started:
            cp.wait()
        total = land[0]
        for d in range(1, ndev):
            total = total + land[d]
        o_ref[...] = total

    return pl.pallas_call(
        body, name=name,
        in_specs=[VMEM_WHOLE], out_specs=VMEM_WHOLE,
        out_shape=jax.ShapeDtypeStruct(v.shape, F32),
        scratch_shapes=[pltpu.VMEM((ndev, R, LANES), F32),
                        pltpu.SemaphoreType.DMA((ndev - 1,)), pltpu.SemaphoreType.DMA((ndev - 1,))],
    )(v)


def sum_cores(grads, theirs, name):
    _, _, r, c = grads.shape
    tr, tc = _tiles(r, c, 1 << 18)
    core = lax.axis_index("c").astype(jnp.int32).reshape(1)

    def body(core_ref, g_ref, t_ref, o_ref):
        o_ref[...] = (g_ref[...].astype(F32) + t_ref[...].astype(F32)).astype(BF16)

    blk = pl.BlockSpec((N_CHIPS, tr, tc), lambda i, j, core_ref: (0, i, j))
    return pl.pallas_call(
        body, name=name,
        grid_spec=pltpu.PrefetchScalarGridSpec(
            num_scalar_prefetch=1, grid=(r // tr, c // tc),
            in_specs=[pl.BlockSpec((None, N_CHIPS, tr, tc), lambda i, j, core_ref: (core_ref[0], 0, i, j)), blk],
            out_specs=blk),
        out_shape=jax.ShapeDtypeStruct((N_CHIPS, r, c), BF16),
        compiler_params=_params("parallel", "parallel"),
    )(core, grads, theirs)


def sum_chips(parts, name):
    _, r, c = parts.shape
    tr, tc = _tiles(r, c, 1 << 18)

    def body(p_ref, o_ref):
        total = p_ref[0].astype(F32)
        for s in range(1, N_CHIPS):
            total = total + p_ref[s].astype(F32)
        o_ref[...] = total

    return pl.pallas_call(
        body, name=name, grid=(r // tr, c // tc),
        in_specs=[pl.BlockSpec((N_CHIPS, tr, tc), lambda i, j: (0, i, j))],
        out_specs=pl.BlockSpec((tr, tc), lambda i, j: (i, j)),
        out_shape=jax.ShapeDtypeStruct((r, c), F32),
        compiler_params=_params("parallel", "parallel"),
    )(parts)


def _adamw_math(w, g, m, v):
    m = ADAM_B1 * m + (1.0 - ADAM_B1) * g
    v = ADAM_B2 * v + (1.0 - ADAM_B2) * (g * g)
    m_hat = m / (1.0 - ADAM_B1 ** ADAM_STEP)
    v_hat = v / (1.0 - ADAM_B2 ** ADAM_STEP)
    delta = -ADAM_LR * (m_hat / (jnp.sqrt(v_hat) + ADAM_EPS) + ADAM_WD * w)
    return delta, m, v


def adamw(w, m, v, g, name, col_halves=False):
    r, c = w.shape
    half = c // 2 if col_halves else c
    tr, tc = _tiles(r, half, 1 << 18, multiple=8)
    per_half = half // tc
    g_cols = g.shape[-1]
    padded = not col_halves and g_cols != c
    if padded and tc != c:
        raise ValueError("a padded gradient is read in full-width row blocks")

    def body(w_ref, m_ref, v_ref, gin_ref, g_ref, d_ref, nm_ref, nv_ref):
        g = gin_ref[:, :c] if padded else gin_ref[...]
        delta, nm, nv = _adamw_math(w_ref[...], g, m_ref[...], v_ref[...])
        g_ref[...] = g
        d_ref[...] = delta
        nm_ref[...] = nm
        nv_ref[...] = nv

    blk = pl.BlockSpec((tr, tc), lambda i, j: (i, j))
    if col_halves:
        g_spec = pl.BlockSpec((None, tr, tc), lambda i, j: (j // per_half, i, j % per_half))
    elif padded:
        g_spec = pl.BlockSpec((tr, g_cols), lambda i, j: (i, 0))
    else:
        g_spec = blk
    shape = jax.ShapeDtypeStruct((r, c), F32)
    return pl.pallas_call(
        body, name=name, grid=(r // tr, c // tc),
        in_specs=[blk, blk, blk, g_spec], out_specs=[blk] * 4, out_shape=[shape] * 4,
        compiler_params=_params("parallel", "parallel"),
    )(w, m, v, g)


MATRICES = ("ffn1_w_gate", "ffn1_w_up", "ffn1_w_down", "w_in", "w_out", "ffn2_w_gate", "ffn2_w_up", "ffn2_w_down")
VECTORS = ("ffn1_norm", "mix_norm", "swa_sinks", "swa_out_norm", "sb_out_norm", "ffn2_norm", "final_norm")
WEIGHTS = ("ffn1_norm", "ffn1_w_gate", "ffn1_w_up", "ffn1_w_down", "mix_norm", "w_in", "swa_sinks", "swa_out_norm",
           "sb_out_norm", "w_out", "ffn2_norm", "ffn2_w_gate", "ffn2_w_up", "ffn2_w_down", "final_norm")


def _pack_rows(vec):
    flat = vec.reshape(-1).astype(F32)
    rows = -(-flat.shape[0] // LANES)
    rows8 = -(-rows // 8) * 8
    flat = jnp.pad(flat, (0, rows8 * LANES - flat.shape[0]))
    return flat.reshape(rows8, LANES)


class GradPipe:
    def __init__(self, name, grads):
        self.name, self.step, self.value = name, 0, grads

    def rider(self):
        make = (half_swap_rider, scatter_rider, final_swap_rider)[self.step]
        return make([self.value])

    def landed(self, outs):
        if self.step == 0:
            self.value = sum_cores(self.value, outs[0], "sum_cores_" + self.name)
        elif self.step == 1:
            self.value = sum_chips(outs[0], "sum_chips_" + self.name)
        else:
            self.value = outs[0]
        self.step += 1


def _hosted(fn, *args, advance=(), **kw):
    outs, routs = fn(*args, riders=[p.rider() for p in advance], **kw)
    for p, ro in zip(advance, routs):
        p.landed(ro)
    return outs


def kernel(x, ffn1_norm, ffn1_w_gate, ffn1_w_up, ffn1_w_down, mix_norm, w_in, swa_sinks, swa_out_norm, sb_out_norm, w_out, ffn2_norm, ffn2_w_gate, ffn2_w_up, ffn2_w_down, final_norm, loss_target, m_ffn1_norm, m_ffn1_w_gate, m_ffn1_w_up, m_ffn1_w_down, m_mix_norm, m_w_in, m_swa_sinks, m_swa_out_norm, m_sb_out_norm, m_w_out, m_ffn2_norm, m_ffn2_w_gate, m_ffn2_w_up, m_ffn2_w_down, m_final_norm, v_ffn1_norm, v_ffn1_w_gate, v_ffn1_w_up, v_ffn1_w_down, v_mix_norm, v_w_in, v_swa_sinks, v_swa_out_norm, v_sb_out_norm, v_w_out, v_ffn2_norm, v_ffn2_w_gate, v_ffn2_w_up, v_ffn2_w_down, v_final_norm):
    given = dict(locals())
    b_loc, seq, D = x.shape
    T = b_loc * seq
    x0 = x.reshape(T, D)
    target = loss_target.reshape(T, D)
    final_g = final_norm.reshape(1, D)

    gathered = {}

    def gather(*names):
        shards = []
        for nm in names:
            w = given[nm][0].astype(BF16)
            hidden_axis = 0 if nm.endswith("w_down") else 1 if nm.startswith("ffn") else None
            if hidden_axis is not None:
                pad = -w.shape[hidden_axis] % LANES
                w = jnp.pad(w, [(0, pad) if a == hidden_axis else (0, 0) for a in range(2)])
            shards.append(w.reshape(2, w.shape[0] // 2, w.shape[1]))
        return names, gather_rider(shards)

    def hosting(fn, *args, fetch):
        names, rider = fetch
        outs, landed = fn(*args, riders=[rider])
        for nm, o in zip(names, landed[0]):
            gathered[nm] = o.reshape(N_CHIPS, 2 * o.shape[2], o.shape[3])
        return outs

    hosting(lambda riders: ([], exchange_now(riders, "gather_ffn1_gate")), fetch=gather("ffn1_w_gate"))

    h1 = rmsnorm_cast(x0, ffn1_norm, "ffn1_norm_fwd")
    (g1,) = hosting(ffn_gate, h1, gathered["ffn1_w_gate"], "ffn1_gate", fetch=gather("ffn1_w_up"))
    u1, a1 = hosting(ffn_up_act, h1, gathered["ffn1_w_up"], g1, "ffn1_up_act", fetch=gather("ffn1_w_down"))
    (x1,) = hosting(ffn_down, a1, gathered["ffn1_w_down"], x0, "ffn1_down", fetch=gather("w_in"))
    h2 = rmsnorm_cast(x1, mix_norm, "mix_norm_fwd")
    (proj,) = hosting(in_proj, h2, gathered["w_in"], "in_proj", fetch=gather("w_out"))
    w_out_all = gathered["w_out"].reshape(-1, D)
    (ya,) = hosting(swa_fwd, proj, swa_sinks, b_loc, seq, "swa_fwd", fetch=gather("ffn2_w_gate"))
    yb, ltot = hosting(sb_fwd, proj, b_loc, seq, "sb_fwd", fetch=gather("ffn2_w_up", "ffn2_w_down"))
    y = out_norm(ya, yb, swa_out_norm, sb_out_norm, "out_norm_fwd")
    x2 = out_proj(y, w_out_all, x1, "out_proj")
    h3 = rmsnorm_cast(x2, ffn2_norm, "ffn2_norm_fwd")
    (g2, u2, a2), _ = ffn_up(h3, gathered["ffn2_w_gate"], gathered["ffn2_w_up"], "ffn2_up")
    (x3,), _ = ffn_down(a2, gathered["ffn2_w_down"], x2, "ffn2_down")
    dx3, d_final, loss_part = final_loss(x3, final_g, target, "final_loss")

    Fs = gathered["ffn1_w_gate"].shape[2]
    Ps = gathered["w_in"].shape[2]
    Os = w_out_all.shape[0] // N_CHIPS
    pipes = {}

    def down_grad(tag, act, dx, advance):
        (grads,) = _hosted(wgrad, act, dx, tag + "_dwd", groups=N_CHIPS, m=Fs, n=D // 2,
                           a_block=lambda tk: (None, tk, Fs), a_map=lambda s, j, k: (s, k, 0),
                           b_block=lambda tk: (tk, D // 2), b_map=lambda s, j, k: (k, j), b_scale=0.5,
                           advance=advance)
        return grads

    def up_grad(name, h, d_act, advance):
        (grads,) = _hosted(wgrad, h, d_act, name, groups=N_CHIPS, m=D // 2, n=Fs,
                           a_block=lambda tk: (tk, D // 2), a_map=lambda s, j, k: (k, j),
                           b_block=lambda tk: (None, tk, Fs), b_map=lambda s, j, k: (s, k, 0), advance=advance)
        return grads

    (dg2, du2), _ = ffn_bwd_act(dx3, gathered["ffn2_w_down"], g2, u2, "ffn2_bwd_act")
    pipes["ffn2_w_down"] = p_d2 = GradPipe("ffn2_w_down", down_grad("ffn2", a2, dx3, []))
    pipes["ffn2_w_gate"] = p_g2 = GradPipe("ffn2_w_gate", up_grad("ffn2_dwg", h3, dg2, [p_d2]))
    pipes["ffn2_w_up"] = p_u2 = GradPipe("ffn2_w_up", up_grad("ffn2_dwu", h3, du2, [p_d2, p_g2]))
    (dh3,) = _hosted(ffn_bwd_dh, dg2, du2, gathered["ffn2_w_gate"], gathered["ffn2_w_up"], "ffn2_bwd_dh",
                     advance=[p_d2, p_g2, p_u2])
    (dx2, d_ffn2_norm), _ = rmsnorm_bwd_add(x2, ffn2_norm, dh3, dx3, "ffn2_norm_bwd")

    (d_w_out,) = _hosted(wgrad, y, dx2, "dw_out", groups=N_CHIPS, m=Os // 2, n=D,
                         a_block=lambda tk: (tk, Os // 2), a_map=lambda s, j, k: (k, 2 * s + j),
                         b_block=lambda tk: (tk, D), b_map=lambda s, j, k: (k, 0), advance=[p_g2])
    pipes["w_out"] = p_out = GradPipe("w_out", d_w_out)
    dya, dyb, d_swa_norm, d_sb_norm = out_proj_bwd(dx2, w_out_all, ya, yb, swa_out_norm, sb_out_norm, "out_proj_bwd")
    dqa, dka, dva, d_sinks = _hosted(swa_bwd, proj, swa_sinks, dya, b_loc, seq, "swa_bwd", advance=[p_u2, p_out])
    dqb, dkb, dvb = _hosted(sb_bwd, proj, dyb, ltot, b_loc, seq, "sb_bwd", advance=[p_u2, p_out])
    dproj = jnp.concatenate([dqa, dka, dva, dqb, dkb, dvb], axis=1)
    (d_w_in,) = _hosted(wgrad, h2, dproj, "dw_in", groups=N_CHIPS, m=D // 2, n=Ps,
                        a_block=lambda tk: (tk, D // 2), a_map=lambda s, j, k: (k, j),
                        b_block=lambda tk: (tk, Ps), b_map=lambda s, j, k: (k, s), advance=[p_out])
    pipes["w_in"] = p_in = GradPipe("w_in", d_w_in)
    (dh2,) = _hosted(in_proj_bwd, dproj, gathered["w_in"], "in_proj_bwd", advance=[p_in])
    (dx1, d_mix_norm), _ = rmsnorm_bwd_add(x1, mix_norm, dh2, dx2, "mix_norm_bwd")

    pipes["ffn1_w_down"] = p_d1 = GradPipe("ffn1_w_down", down_grad("ffn1", a1, dx1, [p_in]))
    dg1, du1 = _hosted(ffn_bwd_act, dx1, gathered["ffn1_w_down"], g1, u1, "ffn1_bwd_act", advance=[p_in, p_d1])
    pipes["ffn1_w_gate"] = p_g1 = GradPipe("ffn1_w_gate", up_grad("ffn1_dwg", h1, dg1, [p_d1]))
    pipes["ffn1_w_up"] = p_u1 = GradPipe("ffn1_w_up", up_grad("ffn1_dwu", h1, du1, [p_d1, p_g1]))
    (dh1,) = _hosted(ffn_bwd_dh, dg1, du1, gathered["ffn1_w_gate"], gathered["ffn1_w_up"], "ffn1_bwd_dh_a",
                     part=(0, 2), advance=[p_g1, p_u1])
    (dh1,) = _hosted(ffn_bwd_dh, dg1, du1, gathered["ffn1_w_gate"], gathered["ffn1_w_up"], "ffn1_bwd_dh_b",
                     part=(1, 2), prev=dh1, advance=[p_g1, p_u1])
    (dx0, d_ffn1_norm), _ = rmsnorm_bwd_add(x0, ffn1_norm, dh1, dx1, "ffn1_norm_bwd")
    (last,) = exchange_now([p_u1.rider()], "swap_ffn1_w_up")
    p_u1.landed(last)

    vec_grads = dict(ffn1_norm=d_ffn1_norm, mix_norm=d_mix_norm, swa_sinks=d_sinks[:, 0], swa_out_norm=d_swa_norm,
                     sb_out_norm=d_sb_norm, ffn2_norm=d_ffn2_norm, final_norm=d_final)
    packed = [_pack_rows(vec_grads[nm]) for nm in VECTORS] + [_pack_rows(loss_part[0, :1])]
    offsets = [0]
    for p in packed:
        offsets.append(offsets[-1] + p.shape[0])
    reduced = allreduce_small(jnp.concatenate(packed, axis=0), "reduce_vectors")
    loss = reduced[offsets[len(VECTORS)], 0]

    out = {}
    for nm in MATRICES:
        shape = given[nm].shape
        g = pipes[nm].value
        col_halves = nm.endswith("w_down")
        if not col_halves:
            g = g.reshape(-1, g.shape[-1])
        res = adamw(given[nm][0], given["m_" + nm][0], given["v_" + nm][0], g, "adamw_" + nm, col_halves=col_halves)
        out[nm] = [r.reshape(shape) for r in res]
    w_rows = jnp.concatenate([_pack_rows(given[nm]) for nm in VECTORS], axis=0)
    m_rows = jnp.concatenate([_pack_rows(given["m_" + nm]) for nm in VECTORS], axis=0)
    v_rows = jnp.concatenate([_pack_rows(given["v_" + nm]) for nm in VECTORS], axis=0)
    g_rows = reduced[:offsets[len(VECTORS)]]
    small = adamw(w_rows, m_rows, v_rows, g_rows, "adamw_vectors")
    for i, nm in enumerate(VECTORS):
        shape = given[nm].shape
        size = math.prod(shape)
        out[nm] = [r[offsets[i]:offsets[i + 1]].reshape(-1)[:size].reshape(shape) for r in small]

    grad_x = dx0.reshape(b_loc, seq, D)
    return (loss, grad_x, *[out[nm][0] for nm in WEIGHTS], *[out[nm][1] for nm in WEIGHTS],
            *[out[nm][2] for nm in WEIGHTS], *[out[nm][3] for nm in WEIGHTS])
```

```python
import functools
import math

import jax
import jax.numpy as jnp
from jax import lax
from jax.experimental import pallas as pl
from jax.experimental.pallas import tpu as pltpu

F32 = jnp.float32
BF16 = jnp.bfloat16
MESH = pl.DeviceIdType.MESH

EPS = 1e-6
HEAD_DIM = 64
LANES = 128
N_SWA_HEADS = 16
N_SWA_KV = 4
N_SB_HEADS = 16
WINDOW = 128
SWA_BLOCK = 128
SB_BLOCK = 256
N_CHIPS = 4
ATT_SCALE = HEAD_DIM ** -0.5

ADAM_LR = 0.001
ADAM_B1 = 0.9
ADAM_B2 = 0.999
ADAM_EPS = 1e-08
ADAM_WD = 0.01
ADAM_STEP = 10

VMEM_LIMIT = 56 * 1024 * 1024
NEG_BIG = -1e30

ANY = pl.BlockSpec(memory_space=pl.ANY)
VMEM_WHOLE = pl.BlockSpec(memory_space=pltpu.VMEM)


def _row_tile(rows, cap, multiple=16):
    best = None
    for t in range(multiple, min(rows, cap) + 1, multiple):
        if rows % t == 0:
            best = t
    if best is None:
        raise ValueError(f"no row tile for {rows} rows under {cap}")
    return best


def _params(*sem):
    return pltpu.CompilerParams(dimension_semantics=sem, vmem_limit_bytes=VMEM_LIMIT)


def _dot(a, b):
    return jnp.dot(a, b, preferred_element_type=F32)


def _dot_nt(a, b):
    return lax.dot_general(a, b, (((1,), (1,)), ((), ())), preferred_element_type=F32)


def _dot_tn(a, b):
    return lax.dot_general(a, b, (((0,), (0,)), ((), ())), preferred_element_type=F32)


def _sigmoid(x):
    return 1.0 / (1.0 + jnp.exp(-x))


def _split_dot(x, u):
    hi = x.astype(BF16)
    lo = (x - hi.astype(F32)).astype(BF16)
    return _dot(hi, u) + _dot(lo, u)


def _tiles(r, c, max_elems, multiple=16):
    col_opts = [c] + [t for t in (2048, 1024, 512, 256, 128) if t < c and c % t == 0]
    best = None
    for tc in col_opts:
        for tr in [r] + list(range(multiple, r, multiple)):
            if r % tr == 0 and tr * tc <= max_elems and (best is None or tr * tc > best[0] * best[1]):
                best = (tr, tc)
    if best is None:
        raise ValueError(f"no tile for [{r}, {c}] under {max_elems} elements")
    return best


class Rider:
    def __init__(self, arrays, out_shapes, sem_shapes, start, finish):
        self.arrays, self.out_shapes, self.sem_shapes = list(arrays), list(out_shapes), list(sem_shapes)
        self.start, self.finish = start, finish


def _place():
    return lax.axis_index("x"), lax.axis_index("y"), lax.axis_index("c")


def _other_chips(x, y):
    return [(1 - x, y), (x, 1 - y), (1 - x, 1 - y)]


def _split(ref):
    rows = ref.shape[-2]
    n = next(k for k in (4, 2, 1) if rows % (16 * k) == 0)
    step = rows // n
    lead = [()] if len(ref.shape) == 2 else [(s,) for s in range(ref.shape[0])]
    return [ref.at[(*idx, pl.ds(i * step, step))] for idx in lead for i in range(n)]


class _Copy:
    def __init__(self, src, dst, make):
        self.src, self.dst, self.make = src, dst, make

    def start(self):
        for s, d in zip(_split(self.src), _split(self.dst)):
            self.make(s, d).start()

    def wait(self):
        self.make(self.src, self.dst).wait()

    def wait_send(self):
        self.make(self.src, self.dst).wait_send()

    def wait_recv(self):
        self.make(self.src, self.dst).wait_recv()


def _remote(src, dst, send_sem, recv_sem, device):
    return _Copy(src, dst, lambda s, d: pltpu.make_async_remote_copy(
        src_ref=s, dst_ref=d, send_sem=send_sem, recv_sem=recv_sem, device_id=device, device_id_type=MESH))


def gather_rider(shards):
    n = len(shards)

    def copies(ins, outs, sems):
        ici_s, ici_r, d2d_s, d2d_r, own_s, own_r = sems
        x, y, c = _place()
        me = 2 * x + y
        own, ici, fwd = [], [], []
        for k in range(n):
            own.append(_remote(ins[k], outs[k].at[me], own_s.at[k], own_r.at[k], (x, y, 1 - c)))
            for j, (px, py) in enumerate(_other_chips(x, y)):
                i = 3 * k + j
                ici.append(_remote(ins[k].at[c], outs[k].at[me, c], ici_s.at[i], ici_r.at[i], (px, py, c)))
                got = outs[k].at[2 * px + py, c]
                fwd.append(_remote(got, got, d2d_s.at[i], d2d_r.at[i], (x, y, 1 - c)))
        return own, ici, fwd

    def start(ins, outs, sems):
        own, ici, _ = copies(ins, outs, sems)
        for cp in own + ici:
            cp.start()

    def finish(ins, outs, sems):
        own, ici, fwd = copies(ins, outs, sems)
        for a, f in zip(ici, fwd):
            a.wait_recv()
            f.start()
        for a in ici:
            a.wait_send()
        for f in fwd:
            f.wait()
        for o in own:
            o.wait()

    dma = pltpu.SemaphoreType.DMA
    return Rider(shards, [jax.ShapeDtypeStruct((N_CHIPS,) + s.shape, s.dtype) for s in shards],
                 [dma((3 * n,)), dma((3 * n,)), dma((3 * n,)), dma((3 * n,)), dma((n,)), dma((n,))], start, finish)


def half_swap_rider(grads):
    n = len(grads)

    def copies(ins, outs, sems):
        x, y, c = _place()
        return [_remote(ins[k].at[1 - c], outs[k], sems[0].at[k], sems[1].at[k], (x, y, 1 - c)) for k in range(n)]

    def start(ins, outs, sems):
        for cp in copies(ins, outs, sems):
            cp.start()

    def finish(ins, outs, sems):
        for cp in copies(ins, outs, sems):
            cp.wait()

    dma = pltpu.SemaphoreType.DMA
    return Rider(grads, [jax.ShapeDtypeStruct(g.shape[1:], g.dtype) for g in grads], [dma((n,)), dma((n,))],
                 start, finish)


def scatter_rider(halves):
    n = len(halves)

    def copies(ins, outs, sems):
        send, recv = sems
        x, y, c = _place()
        cps = []
        for k in range(n):
            for j, (px, py) in enumerate(_other_chips(x, y)):
                i = 3 * k + j
                cps.append(_remote(ins[k].at[2 * px + py], outs[k].at[j], send.at[i], recv.at[i], (px, py, c)))
        return cps

    def start(ins, outs, sems):
        for cp in copies(ins, outs, sems):
            cp.start()

    def finish(ins, outs, sems):
        for cp in copies(ins, outs, sems):
            cp.wait()

    dma = pltpu.SemaphoreType.DMA
    return Rider(halves, [jax.ShapeDtypeStruct((3,) + h.shape[1:], h.dtype) for h in halves],
                 [dma((3 * n,)), dma((3 * n,))], start, finish)


def final_swap_rider(sums):
    n = len(sums)

    def copies(ins, outs, sems):
        send, recv = sems
        x, y, c = _place()
        return [_remote(ins[k], outs[k], send.at[k], recv.at[k], (x, y, 1 - c)) for k in range(n)]

    def start(ins, outs, sems):
        for cp in copies(ins, outs, sems):
            cp.start()

    def finish(ins, outs, sems):
        for cp in copies(ins, outs, sems):
            cp.wait()

    dma = pltpu.SemaphoreType.DMA
    return Rider(sums, [jax.ShapeDtypeStruct(s.shape, s.dtype) for s in sums], [dma((n,)), dma((n,))],
                 start, finish)


def _call(body, *, name, grid, in_specs, out_specs, out_shape, scratch_shapes=(), sem, args, riders=(), aliases=None):
    in_specs, out_specs, out_shape = list(in_specs), list(out_specs), list(out_shape)
    scratch_shapes = list(scratch_shapes)
    aliases = aliases or {}
    if not riders:
        outs = pl.pallas_call(body, name=name, grid=grid, in_specs=in_specs, out_specs=out_specs, out_shape=out_shape,
                              scratch_shapes=scratch_shapes, input_output_aliases=aliases,
                              compiler_params=_params(*sem))(*args)
        return list(outs), []
    n_in, n_out, n_scr = len(in_specs), len(out_specs), len(scratch_shapes)
    r_in = [len(r.arrays) for r in riders]
    r_out = [len(r.out_shapes) for r in riders]
    r_sem = [len(r.sem_shapes) for r in riders]

    def wrapped(*refs):
        pos = 0

        def take(count):
            nonlocal pos
            got = refs[pos:pos + count]
            pos += count
            return got

        ins = take(n_in)
        rins = [take(k) for k in r_in]
        outs = take(n_out)
        routs = [take(k) for k in r_out]
        scr = take(n_scr)
        rsems = [take(k) for k in r_sem]
        first = functools.reduce(jnp.logical_and, [pl.program_id(a) == 0 for a in range(len(grid))])
        last = functools.reduce(jnp.logical_and, [pl.program_id(a) == grid[a] - 1 for a in range(len(grid))])

        @pl.when(first)
        def _():
            for r, ri, ro, rs in zip(riders, rins, routs, rsems):
                r.start(ri, ro, rs)

        body(*ins, *outs, *scr)

        @pl.when(last)
        def _():
            for r, ri, ro, rs in zip(riders, rins, routs, rsems):
                r.finish(ri, ro, rs)

    any_spec = pl.BlockSpec(memory_space=pl.ANY)
    res = pl.pallas_call(
        wrapped, name=name, grid=grid,
        in_specs=in_specs + [any_spec] * sum(r_in),
        out_specs=out_specs + [any_spec] * sum(r_out),
        out_shape=out_shape + [s for r in riders for s in r.out_shapes],
        scratch_shapes=scratch_shapes + [s for r in riders for s in r.sem_shapes],
        input_output_aliases=aliases,
        compiler_params=_params(*(["arbitrary"] * len(grid))),
    )(*args, *[a for r in riders for a in r.arrays])
    res = list(res)
    outs, rest = res[:n_out], res[n_out:]
    routs = []
    for k in r_out:
        routs.append(rest[:k])
        rest = rest[k:]
    return outs, routs


def exchange_now(riders, name):
    n_in = [len(r.arrays) for r in riders]
    n_out = [len(r.out_shapes) for r in riders]
    n_sem = [len(r.sem_shapes) for r in riders]

    def body(*refs):
        pos = 0
        groups = []
        for counts in (n_in, n_out, n_sem):
            part = []
            for k in counts:
                part.append(refs[pos:pos + k])
                pos += k
            groups.append(part)
        for r, ri, ro, rs in zip(riders, *groups):
            r.start(ri, ro, rs)
        for r, ri, ro, rs in zip(riders, *groups):
            r.finish(ri, ro, rs)

    any_spec = pl.BlockSpec(memory_space=pl.ANY)
    res = pl.pallas_call(
        body, name=name,
        in_specs=[any_spec] * sum(n_in), out_specs=[any_spec] * sum(n_out),
        out_shape=[s for r in riders for s in r.out_shapes],
        scratch_shapes=[s for r in riders for s in r.sem_shapes],
    )(*[a for r in riders for a in r.arrays])
    res = list(res)
    routs = []
    for k in n_out:
        routs.append(res[:k])
        res = res[k:]
    return routs


def rmsnorm_cast(x, gamma, name):
    T, D = x.shape
    tr = min(512, T)

    def body(x_ref, g_ref, o_ref):
        xv = x_ref[...]
        r = lax.rsqrt(jnp.mean(xv * xv, axis=-1, keepdims=True) + EPS)
        o_ref[...] = ((xv * r) * g_ref[...]).astype(BF16)

    return pl.pallas_call(
        body, name=name, grid=(T // tr,),
        in_specs=[pl.BlockSpec((tr, D), lambda i: (i, 0)), pl.BlockSpec((1, D), lambda i: (0, 0))],
        out_specs=pl.BlockSpec((tr, D), lambda i: (i, 0)),
        out_shape=jax.ShapeDtypeStruct((T, D), BF16),
        compiler_params=_params("parallel"),
    )(x, gamma)


def _rms_bwd(xv, gamma, dy):
    d = xv.shape[-1]
    r = lax.rsqrt(jnp.mean(xv * xv, axis=-1, keepdims=True) + EPS)
    gdy = dy * gamma
    dot = jnp.sum(gdy * xv, axis=-1, keepdims=True)
    dx = r * gdy - xv * ((r * r * r) * (dot / d))
    dgamma = jnp.sum(dy * (xv * r), axis=0, keepdims=True)
    return dx, dgamma


def rmsnorm_bwd_add(x, gamma, dy, dx_up, name, riders=()):
    T, D = x.shape
    tr = min(256, T)

    def body(x_ref, g_ref, dy_ref, up_ref, dx_ref, dg_ref):
        dx, dgamma = _rms_bwd(x_ref[...], g_ref[...], dy_ref[...])
        dx_ref[...] = up_ref[...] + dx

        @pl.when(pl.program_id(0) == 0)
        def _():
            dg_ref[...] = jnp.zeros_like(dg_ref)

        dg_ref[...] += dgamma

    row = pl.BlockSpec((tr, D), lambda i: (i, 0))
    vec = pl.BlockSpec((1, D), lambda i: (0, 0))
    return _call(
        body, name=name, grid=(T // tr,),
        in_specs=[row, vec, row, row], out_specs=[row, vec],
        out_shape=[jax.ShapeDtypeStruct((T, D), F32), jax.ShapeDtypeStruct((1, D), F32)],
        sem=("arbitrary",), args=(x, gamma, dy, dx_up), riders=riders)


def ffn_up(h, wg, wu, name, riders=()):
    T, D = h.shape
    G, _, Fs = wg.shape
    tm = min(512, T)

    def body(h_ref, wg_ref, wu_ref, g_ref, u_ref, a_ref):
        hv = h_ref[...]
        g = _dot(hv, wg_ref[...])
        u = _dot(hv, wu_ref[...])
        g_ref[...] = g
        u_ref[...] = u
        a_ref[...] = ((g * _sigmoid(g)) * u).astype(BF16)

    w_spec = pl.BlockSpec((None, D, Fs), lambda s, i: (s, 0, 0), pipeline_mode=pl.Buffered(1))
    o_spec = pl.BlockSpec((None, tm, Fs), lambda s, i: (s, i, 0))
    return _call(
        body, name=name, grid=(G, T // tm),
        in_specs=[pl.BlockSpec((tm, D), lambda s, i: (i, 0)), w_spec, w_spec],
        out_specs=[o_spec, o_spec, o_spec],
        out_shape=[jax.ShapeDtypeStruct((G, T, Fs), F32), jax.ShapeDtypeStruct((G, T, Fs), F32),
                   jax.ShapeDtypeStruct((G, T, Fs), BF16)],
        sem=("parallel", "parallel"), args=(h, wg, wu), riders=riders)


def ffn_gate(h, wg, name, riders=()):
    T, D = h.shape
    G, _, Fs = wg.shape
    tm = min(512, T)

    def body(h_ref, wg_ref, g_ref):
        g_ref[...] = _dot(h_ref[...], wg_ref[...])

    return _call(
        body, name=name, grid=(G, T // tm),
        in_specs=[pl.BlockSpec((tm, D), lambda s, i: (i, 0)),
                  pl.BlockSpec((None, D, Fs), lambda s, i: (s, 0, 0), pipeline_mode=pl.Buffered(1))],
        out_specs=[pl.BlockSpec((None, tm, Fs), lambda s, i: (s, i, 0))],
        out_shape=[jax.ShapeDtypeStruct((G, T, Fs), F32)],
        sem=("parallel", "parallel"), args=(h, wg), riders=riders)


def ffn_up_act(h, wu, g, name, riders=()):
    T, D = h.shape
    G, _, Fs = wu.shape
    tm = min(512, T)

    def body(h_ref, wu_ref, g_ref, u_ref, a_ref):
        u = _dot(h_ref[...], wu_ref[...])
        g = g_ref[...]
        u_ref[...] = u
        a_ref[...] = ((g * _sigmoid(g)) * u).astype(BF16)

    act = pl.BlockSpec((None, tm, Fs), lambda s, i: (s, i, 0))
    return _call(
        body, name=name, grid=(G, T // tm),
        in_specs=[pl.BlockSpec((tm, D), lambda s, i: (i, 0)),
                  pl.BlockSpec((None, D, Fs), lambda s, i: (s, 0, 0), pipeline_mode=pl.Buffered(1)), act],
        out_specs=[act, act],
        out_shape=[jax.ShapeDtypeStruct((G, T, Fs), F32), jax.ShapeDtypeStruct((G, T, Fs), BF16)],
        sem=("parallel", "parallel"), args=(h, wu, g), riders=riders)


def ffn_down(a, wd, x, name, riders=()):
    G, T, Fs = a.shape
    D = wd.shape[2]
    tm = min(512, T)

    def body(a_ref, w_ref, x_ref, o_ref, acc):
        s = pl.program_id(1)

        @pl.when(s == 0)
        def _():
            acc[...] = jnp.zeros_like(acc)

        acc[...] += _dot(a_ref[...], w_ref[...])

        @pl.when(s == G - 1)
        def _():
            o_ref[...] = x_ref[...] + 0.5 * acc[...]

    return _call(
        body, name=name, grid=(T // tm, G),
        in_specs=[pl.BlockSpec((None, tm, Fs), lambda i, s: (s, i, 0)),
                  pl.BlockSpec((None, Fs, D), lambda i, s: (s, 0, 0)),
                  pl.BlockSpec((tm, D), lambda i, s: (i, 0))],
        out_specs=[pl.BlockSpec((tm, D), lambda i, s: (i, 0))],
        out_shape=[jax.ShapeDtypeStruct((T, D), F32)],
        scratch_shapes=[pltpu.VMEM((tm, D), F32)],
        sem=("parallel", "arbitrary"), args=(a, wd, x), riders=riders)


def ffn_bwd_act(dx, wd, g, u, name, riders=()):
    T, D = dx.shape
    G, Fs, _ = wd.shape
    tm = min(256, T)

    def body(dx_ref, w_ref, g_ref, u_ref, dg_ref, du_ref):
        da = _dot_nt((0.5 * dx_ref[...]).astype(BF16), w_ref[...])
        gv = g_ref[...]
        sig = _sigmoid(gv)
        silu = gv * sig
        du_ref[...] = (da * silu).astype(BF16)
        dg_ref[...] = ((da * u_ref[...]) * (sig * (1.0 + gv * (1.0 - sig)))).astype(BF16)

    act = pl.BlockSpec((None, tm, Fs), lambda s, i: (s, i, 0))
    return _call(
        body, name=name, grid=(G, T // tm),
        in_specs=[pl.BlockSpec((tm, D), lambda s, i: (i, 0)),
                  pl.BlockSpec((None, Fs, D), lambda s, i: (s, 0, 0), pipeline_mode=pl.Buffered(1)),
                  act, act],
        out_specs=[act, act],
        out_shape=[jax.ShapeDtypeStruct((G, T, Fs), BF16), jax.ShapeDtypeStruct((G, T, Fs), BF16)],
        sem=("parallel", "parallel"), args=(dx, wd, g, u), riders=riders)


def ffn_bwd_dh(dg, du, wg, wu, name, riders=(), part=(0, 1), prev=None):
    G, T, Fs = dg.shape
    D = wg.shape[1]
    tm = min(512, T // part[1])
    nblk = T // tm // part[1]
    off = part[0] * nblk

    def body(dg_ref, du_ref, wg_ref, wu_ref, *rest):
        o_ref, acc = rest[-2:]
        s = pl.program_id(1)

        @pl.when(s == 0)
        def _():
            acc[...] = jnp.zeros_like(acc)

        acc[...] += _dot_nt(dg_ref[...], wg_ref[...])
        acc[...] += _dot_nt(du_ref[...], wu_ref[...])

        @pl.when(s == G - 1)
        def _():
            o_ref[...] = acc[...]

    act = pl.BlockSpec((None, tm, Fs), lambda i, s: (s, i + off, 0))
    w_spec = pl.BlockSpec((None, D, Fs), lambda i, s: (s, 0, 0))
    extra = [] if prev is None else [prev]
    return _call(
        body, name=name, grid=(nblk, G),
        in_specs=[act, act, w_spec, w_spec] + [pl.BlockSpec(memory_space=pl.ANY)] * len(extra),
        out_specs=[pl.BlockSpec((tm, D), lambda i, s: (i + off, 0))],
        out_shape=[jax.ShapeDtypeStruct((T, D), F32)],
        scratch_shapes=[pltpu.VMEM((tm, D), F32)],
        sem=("parallel", "arbitrary"), args=(dg, du, wg, wu, *extra), riders=riders,
        aliases={4: 0} if extra else None)


def wgrad(a, b, name, *, groups, m, n, a_block, a_map, b_block, b_map, b_scale=None, tk=512, riders=()):
    T = a.shape[-2]
    tk = min(tk, T)
    nk = T // tk

    def body(a_ref, b_ref, o_ref, acc):
        k = pl.program_id(2)

        @pl.when(k == 0)
        def _():
            acc[...] = jnp.zeros_like(acc)

        bv = b_ref[...]
        if b_scale is not None:
            bv = b_scale * bv
        acc[...] += _dot_tn(a_ref[...].astype(BF16), bv.astype(BF16))

        @pl.when(k == nk - 1)
        def _():
            o_ref[...] = acc[...].astype(BF16)

    return _call(
        body, name=name, grid=(groups, 2, nk),
        in_specs=[pl.BlockSpec(a_block(tk), a_map), pl.BlockSpec(b_block(tk), b_map)],
        out_specs=[pl.BlockSpec((None, None, m, n), lambda s, j, k: (j, s, 0, 0))],
        out_shape=[jax.ShapeDtypeStruct((2, groups, m, n), BF16)],
        scratch_shapes=[pltpu.VMEM((m, n), F32)],
        sem=("parallel", "parallel", "arbitrary"), args=(a, b), riders=riders)


def in_proj(h, w_in, name, riders=()):
    T, D = h.shape
    G, _, Ps = w_in.shape
    tm = min(512, T)

    def body(h_ref, w_ref, o_ref):
        o_ref[...] = _dot(h_ref[...], w_ref[...]).astype(BF16)

    return _call(
        body, name=name, grid=(G, T // tm),
        in_specs=[pl.BlockSpec((tm, D), lambda s, i: (i, 0)),
                  pl.BlockSpec((None, D, Ps), lambda s, i: (s, 0, 0))],
        out_specs=[pl.BlockSpec((tm, Ps), lambda s, i: (i, s))],
        out_shape=[jax.ShapeDtypeStruct((T, G * Ps), BF16)],
        sem=("parallel", "parallel"), args=(h, w_in), riders=riders)


def in_proj_bwd(dproj, w_in, name, riders=()):
    T = dproj.shape[0]
    G, D, Ps = w_in.shape
    tm = min(512, T)

    def body(d_ref, w_ref, o_ref, acc):
        s = pl.program_id(1)

        @pl.when(s == 0)
        def _():
            acc[...] = jnp.zeros_like(acc)

        acc[...] += _dot_nt(d_ref[...], w_ref[...])

        @pl.when(s == G - 1)
        def _():
            o_ref[...] = acc[...]

    return _call(
        body, name=name, grid=(T // tm, G),
        in_specs=[pl.BlockSpec((tm, Ps), lambda i, s: (i, s)),
                  pl.BlockSpec((None, D, Ps), lambda i, s: (s, 0, 0))],
        out_specs=[pl.BlockSpec((tm, D), lambda i, s: (i, 0))],
        out_shape=[jax.ShapeDtypeStruct((T, D), F32)],
        scratch_shapes=[pltpu.VMEM((tm, D), F32)],
        sem=("parallel", "arbitrary"), args=(dproj, w_in), riders=riders)


def out_proj(y, w_out, x, name):
    T, M = y.shape
    D = w_out.shape[1]
    tm = min(512, T)

    def body(y_ref, w_ref, x_ref, o_ref):
        o_ref[...] = x_ref[...] + _dot(y_ref[...], w_ref[...])

    return pl.pallas_call(
        body, name=name, grid=(T // tm,),
        in_specs=[pl.BlockSpec((tm, M), lambda i: (i, 0)), pl.BlockSpec((M, D), lambda i: (0, 0)),
                  pl.BlockSpec((tm, D), lambda i: (i, 0))],
        out_specs=pl.BlockSpec((tm, D), lambda i: (i, 0)),
        out_shape=jax.ShapeDtypeStruct((T, D), F32),
        compiler_params=_params("parallel"),
    )(y, w_out, x)


def out_norm(ya, yb, ga, gb, name):
    T, W = ya.shape
    tr = min(512, T)

    def body(a_ref, b_ref, ga_ref, gb_ref, o_ref):
        for src, gam, col in ((a_ref, ga_ref, 0), (b_ref, gb_ref, W)):
            v = src[...]
            r = lax.rsqrt(jnp.mean(v * v, axis=-1, keepdims=True) + EPS)
            o_ref[:, col:col + W] = ((v * r) * gam[...]).astype(BF16)

    row = pl.BlockSpec((tr, W), lambda i: (i, 0))
    vec = pl.BlockSpec((1, W), lambda i: (0, 0))
    return pl.pallas_call(
        body, name=name, grid=(T // tr,),
        in_specs=[row, row, vec, vec],
        out_specs=pl.BlockSpec((tr, 2 * W), lambda i: (i, 0)),
        out_shape=jax.ShapeDtypeStruct((T, 2 * W), BF16),
        compiler_params=_params("parallel"),
    )(ya, yb, ga, gb)


def out_proj_bwd(dx, w_out, ya, yb, ga, gb, name):
    T, D = dx.shape
    M = w_out.shape[0]
    W = M // 2
    tm = min(256, T)

    def body(dx_ref, w_ref, a_ref, b_ref, ga_ref, gb_ref, da_ref, db_ref, dga_ref, dgb_ref):
        dy = _dot_nt(dx_ref[...].astype(BF16), w_ref[...])

        @pl.when(pl.program_id(0) == 0)
        def _():
            dga_ref[...] = jnp.zeros_like(dga_ref)
            dgb_ref[...] = jnp.zeros_like(dgb_ref)

        d_a, dg_a = _rms_bwd(a_ref[...], ga_ref[...], dy[:, :W])
        d_b, dg_b = _rms_bwd(b_ref[...], gb_ref[...], dy[:, W:])
        da_ref[...] = d_a
        db_ref[...] = d_b
        dga_ref[...] += dg_a
        dgb_ref[...] += dg_b

    row = pl.BlockSpec((tm, W), lambda i: (i, 0))
    vec = pl.BlockSpec((1, W), lambda i: (0, 0))
    return pl.pallas_call(
        body, name=name, grid=(T // tm,),
        in_specs=[pl.BlockSpec((tm, D), lambda i: (i, 0)), pl.BlockSpec((M, D), lambda i: (0, 0)),
                  row, row, vec, vec],
        out_specs=[row, row, vec, vec],
        out_shape=[jax.ShapeDtypeStruct((T, W), F32), jax.ShapeDtypeStruct((T, W), F32),
                   jax.ShapeDtypeStruct((1, W), F32), jax.ShapeDtypeStruct((1, W), F32)],
        compiler_params=_params("arbitrary"),
    )(dx, w_out, ya, yb, ga, gb)


def final_loss(x, gamma, target, name):
    T, D = x.shape
    tr = min(256, T)

    def body(x_ref, g_ref, t_ref, dx_ref, dg_ref, loss_ref):
        xv = x_ref[...]
        gam = g_ref[...]
        r = lax.rsqrt(jnp.mean(xv * xv, axis=-1, keepdims=True) + EPS)
        err = (xv * r) * gam - t_ref[...]
        part = 0.5 * jnp.sum(jnp.mean(err * err, axis=-1, keepdims=True), axis=0, keepdims=True)
        dx, dgamma = _rms_bwd(xv, gam, err / D)
        dx_ref[...] = dx

        @pl.when(pl.program_id(0) == 0)
        def _():
            dg_ref[...] = jnp.zeros_like(dg_ref)
            loss_ref[...] = jnp.zeros_like(loss_ref)

        dg_ref[...] += dgamma
        loss_ref[...] += jnp.broadcast_to(part, loss_ref.shape)

    row = pl.BlockSpec((tr, D), lambda i: (i, 0))
    vec = pl.BlockSpec((1, D), lambda i: (0, 0))
    return pl.pallas_call(
        body, name=name, grid=(T // tr,),
        in_specs=[row, vec, row],
        out_specs=[row, vec, pl.BlockSpec((1, LANES), lambda i: (0, 0))],
        out_shape=[jax.ShapeDtypeStruct((T, D), F32), jax.ShapeDtypeStruct((1, D), F32),
                   jax.ShapeDtypeStruct((1, LANES), F32)],
        compiler_params=_params("arbitrary"),
    )(x, gamma, target)


def _lane_select(src_half, dst_halves):
    r = lax.broadcasted_iota(jnp.int32, (LANES, LANES), 0)
    c = lax.broadcasted_iota(jnp.int32, (LANES, LANES), 1)
    hit = ((r >> 6) == src_half) & ((r & (HEAD_DIM - 1)) == (c & (HEAD_DIM - 1)))
    ok = functools.reduce(jnp.logical_or, [(c >> 6) == d for d in dst_halves])
    return jnp.where(hit & ok, 1.0, 0.0).astype(BF16)


def _swa_slope(kvp, local_head):
    lo = 2.0 ** (-8.0 * (local_head + 1) / N_SWA_HEADS)
    hi = 2.0 ** (-8.0 * (8 + local_head + 1) / N_SWA_HEADS)
    return jnp.where(kvp == 0, jnp.float32(lo), jnp.float32(hi))


def _swa_probs(qblk, kwin, n, start, slope, sink):
    z = _dot_nt(qblk, kwin) * ATT_SCALE
    t = n * SWA_BLOCK + lax.broadcasted_iota(jnp.int32, z.shape, 0)
    s = start + lax.broadcasted_iota(jnp.int32, z.shape, 1)
    dist = t - s
    valid = (dist >= 0) & (dist < WINDOW)
    z = jnp.where(valid, z - slope * dist.astype(F32), NEG_BIG)
    m = jnp.maximum(jnp.max(z, axis=-1, keepdims=True), sink)
    e = jnp.exp(z - m)
    den = jnp.sum(e, axis=-1, keepdims=True) + jnp.exp(sink - m)
    return e / den, m, den


def swa_fwd(proj, sinks, b_loc, seq, name, riders=()):
    T = proj.shape[0]
    nblk = seq // SWA_BLOCK

    def body(sink_ref, q_ref, k_ref, v_ref, o_ref, qs, vv):
        kvp = pl.program_id(1)
        lane = lax.broadcasted_iota(jnp.int32, (SWA_BLOCK, LANES), 1)
        for kvh in range(2):
            both = _lane_select(kvh, (0, 1))
            vv[...] = _dot(v_ref[...], both).astype(BF16)
            for qp in (2 * kvh, 2 * kvh + 1):
                cols = slice(LANES * qp, LANES * (qp + 1))
                for a in range(2):
                    qs[a] = _dot(q_ref[:, cols], _lane_select(a, (kvh,))).astype(BF16)
                slopes = [_swa_slope(kvp, 2 * qp + a) for a in range(2)]
                snk = [sink_ref[0, 8 * kvp + 2 * qp + a] for a in range(2)]

                def blk(n, carry):
                    start = pl.multiple_of(jnp.maximum(n - 1, 0) * SWA_BLOCK, SWA_BLOCK)
                    rows = pl.ds(pl.multiple_of(n * SWA_BLOCK, SWA_BLOCK), SWA_BLOCK)
                    win = pl.ds(start, 2 * SWA_BLOCK)
                    outs = []
                    for a in range(2):
                        p, _, _ = _swa_probs(qs[a, rows, :], k_ref[win, :], n, start, slopes[a], snk[a])
                        outs.append(_dot(p.astype(BF16), vv[win, :]))
                    o_ref[rows, cols] = jnp.where(lane < HEAD_DIM, outs[0], outs[1])
                    return carry

                lax.fori_loop(0, nblk, blk, 0)

    return _call(
        body, name=name, grid=(b_loc, 2),
        in_specs=[pl.BlockSpec(memory_space=pltpu.SMEM),
                  pl.BlockSpec((seq, 4 * LANES), lambda b, p: (b, p)),
                  pl.BlockSpec((seq, LANES), lambda b, p: (b, 8 + p)),
                  pl.BlockSpec((seq, LANES), lambda b, p: (b, 10 + p))],
        out_specs=[pl.BlockSpec((seq, 4 * LANES), lambda b, p: (b, p))],
        out_shape=[jax.ShapeDtypeStruct((T, N_SWA_HEADS * HEAD_DIM), F32)],
        scratch_shapes=[pltpu.VMEM((2, seq, LANES), BF16), pltpu.VMEM((seq, LANES), BF16)],
        sem=("parallel", "parallel"), args=(sinks, proj, proj, proj), riders=riders)


def swa_bwd(proj, sinks, dya, b_loc, seq, name, riders=()):
    T = proj.shape[0]
    nblk = seq // SWA_BLOCK

    def body(sink_ref, q_ref, k_ref, v_ref, do_ref, dq_ref, dk_ref, dv_ref, ds_ref, qs, dos, kk, dk_acc, dv_acc):
        kvp = pl.program_id(0)
        b = pl.program_id(1)
        lane = lax.broadcasted_iota(jnp.int32, (SWA_BLOCK, LANES), 1)

        @pl.when(b == 0)
        def _():
            ds_ref[...] = jnp.zeros_like(ds_ref)

        dk_acc[...] = jnp.zeros_like(dk_acc)
        dv_acc[...] = jnp.zeros_like(dv_acc)
        for kvh in range(2):
            kk[...] = _dot(k_ref[...], _lane_select(kvh, (0, 1))).astype(BF16)
            for qp in (2 * kvh, 2 * kvh + 1):
                cols = slice(LANES * qp, LANES * (qp + 1))
                for a in range(2):
                    move = _lane_select(a, (kvh,))
                    qs[a] = _dot(q_ref[:, cols], move).astype(BF16)
                    dos[a] = _dot(do_ref[:, cols].astype(BF16), move).astype(BF16)
                slopes = [_swa_slope(kvp, 2 * qp + a) for a in range(2)]
                snk = [sink_ref[0, 8 * kvp + 2 * qp + a] for a in range(2)]

                def blk(n, carry):
                    start = pl.multiple_of(jnp.maximum(n - 1, 0) * SWA_BLOCK, SWA_BLOCK)
                    rows = pl.ds(pl.multiple_of(n * SWA_BLOCK, SWA_BLOCK), SWA_BLOCK)
                    win = pl.ds(start, 2 * SWA_BLOCK)
                    dqs = []
                    for a in range(2):
                        qblk = qs[a, rows, :]
                        doblk = dos[a, rows, :]
                        p, m, den = _swa_probs(qblk, k_ref[win, :], n, start, slopes[a], snk[a])
                        dp = _dot_nt(doblk, v_ref[win, :])
                        delta = jnp.sum(p * dp, axis=-1, keepdims=True)
                        p_sink = jnp.exp(snk[a] - m) / den
                        head = 2 * qp + a
                        ds_ref[head:head + 1, :] -= jnp.broadcast_to(
                            jnp.sum(p_sink * delta, axis=0, keepdims=True), (1, LANES))
                        dsc = ((p * (dp - delta)) * ATT_SCALE).astype(BF16)
                        dv_acc[win, :] += _dot_tn(p.astype(BF16), doblk)
                        dk_acc[win, :] += _dot_tn(dsc, qblk)
                        dqs.append(_dot(dsc, kk[win, :]))
                    dq_ref[rows, cols] = jnp.where(lane < HEAD_DIM, dqs[0], dqs[1]).astype(BF16)
                    return carry

                lax.fori_loop(0, nblk, blk, 0)
        dk_ref[...] = dk_acc[...].astype(BF16)
        dv_ref[...] = dv_acc[...].astype(BF16)

    return _call(
        body, name=name, grid=(2, b_loc),
        in_specs=[pl.BlockSpec(memory_space=pltpu.SMEM),
                  pl.BlockSpec((seq, 4 * LANES), lambda p, b: (b, p)),
                  pl.BlockSpec((seq, LANES), lambda p, b: (b, 8 + p)),
                  pl.BlockSpec((seq, LANES), lambda p, b: (b, 10 + p)),
                  pl.BlockSpec((seq, 4 * LANES), lambda p, b: (b, p))],
        out_specs=[pl.BlockSpec((seq, 4 * LANES), lambda p, b: (b, p)),
                   pl.BlockSpec((seq, LANES), lambda p, b: (b, p)),
                   pl.BlockSpec((seq, LANES), lambda p, b: (b, p)),
                   pl.BlockSpec((8, LANES), lambda p, b: (p, 0))],
        out_shape=[jax.ShapeDtypeStruct((T, N_SWA_HEADS * HEAD_DIM), BF16),
                   jax.ShapeDtypeStruct((T, N_SWA_KV * HEAD_DIM), BF16),
                   jax.ShapeDtypeStruct((T, N_SWA_KV * HEAD_DIM), BF16),
                   jax.ShapeDtypeStruct((N_SWA_HEADS, LANES), F32)],
        scratch_shapes=[pltpu.VMEM((2, seq, LANES), BF16), pltpu.VMEM((2, seq, LANES), BF16),
                        pltpu.VMEM((seq, LANES), BF16),
                        pltpu.VMEM((seq, LANES), F32), pltpu.VMEM((seq, LANES), F32)],
        sem=("arbitrary", "arbitrary"), args=(sinks, proj, proj, proj, dya), riders=riders)


def _softplus(z):
    return jnp.maximum(z, 0.0) + jnp.log(1.0 + jnp.exp(-jnp.abs(z)))


def _tri(n, rel):
    r = lax.broadcasted_iota(jnp.int32, (n, n), 0)
    c = lax.broadcasted_iota(jnp.int32, (n, n), 1)
    return jnp.where(rel(r, c), 1.0, 0.0).astype(BF16)


def sb_fwd(proj, b_loc, seq, name, riders=()):
    T = proj.shape[0]
    bq = min(SB_BLOCK, seq)
    nq = seq // bq
    npair = N_SB_HEADS // 2

    def body(q_ref, k_ref, v_ref, o_ref, lt_ref):
        i = pl.program_id(2)
        lane = lax.broadcasted_iota(jnp.int32, (bq, LANES), 1)
        after = _tri(bq, lambda r, c: r > c)
        tpos = i * bq + lax.broadcasted_iota(jnp.int32, (bq, bq), 0)
        col = lax.broadcasted_iota(jnp.int32, (bq, bq), 1)
        qv = q_ref[...] * jnp.asarray(ATT_SCALE, BF16)
        qs = [jnp.where((lane >> 6) == h, qv, jnp.zeros((bq, LANES), BF16)) for h in range(2)]

        def tile(j, state, mask):
            rows = pl.ds(pl.multiple_of(j * bq, bq), bq)
            kj, vj = k_ref[rows, :], v_ref[rows, :]
            new = []
            for h in range(2):
                run, acc = state[2 * h], state[2 * h + 1]
                z = _dot_nt(qs[h], kj)
                sp = _softplus(z)
                lg = -sp if mask is None else jnp.where(mask, -sp, 0.0)
                a = jnp.exp((z - sp) + (_split_dot(lg, after) + run))
                if mask is not None:
                    a = jnp.where(mask, a, 0.0)
                new += [run + jnp.sum(lg, axis=-1, keepdims=True), acc + _dot(a.astype(BF16), vj)]
            return tuple(new)

        zero = (jnp.zeros((bq, 1), F32), jnp.zeros((bq, LANES), F32))
        state = tile(i, zero + zero, (i * bq + col) < tpos)
        state = lax.fori_loop(0, i, lambda jj, st: tile(i - 1 - jj, st, None), state)
        o_ref[...] = jnp.where(lane < HEAD_DIM, state[1], state[3])
        lt_ref[...] = jnp.where(lane < HEAD_DIM, state[0], state[2])

    return _call(
        body, name=name, grid=(b_loc, npair, nq),
        in_specs=[pl.BlockSpec((bq, LANES), lambda b, p, i: (b * nq + i, 12 + p)),
                  pl.BlockSpec((seq, LANES), lambda b, p, i: (b, 20 + p)),
                  pl.BlockSpec((seq, LANES), lambda b, p, i: (b, 28 + p))],
        out_specs=[pl.BlockSpec((bq, LANES), lambda b, p, i: (b * nq + i, p)),
                   pl.BlockSpec((None, bq, LANES), lambda b, p, i: (b * npair + p, i, 0))],
        out_shape=[jax.ShapeDtypeStruct((T, N_SB_HEADS * HEAD_DIM), F32),
                   jax.ShapeDtypeStruct((b_loc * npair, seq, LANES), F32)],
        sem=("parallel", "parallel", "parallel"), args=(proj, proj, proj), riders=riders)


def sb_bwd(proj, dyb, ltot, b_loc, seq, name, riders=()):
    T = proj.shape[0]
    bq = min(SB_BLOCK, seq)
    nq = seq // bq
    npair = N_SB_HEADS // 2

    def body(q_ref, k_ref, v_ref, do_ref, lt_ref, dq_ref, dk_ref, dv_ref, dk_acc, dv_acc):
        lane = lax.broadcasted_iota(jnp.int32, (bq, LANES), 1)
        upto = _tri(bq, lambda r, c: r <= c)
        before = _tri(bq, lambda r, c: r < c)
        row = lax.broadcasted_iota(jnp.int32, (bq, bq), 0)
        col = lax.broadcasted_iota(jnp.int32, (bq, bq), 1)
        dk_acc[...] = jnp.zeros_like(dk_acc)
        dv_acc[...] = jnp.zeros_like(dv_acc)
        scale = jnp.asarray(ATT_SCALE, BF16)
        heads = [(lane >> 6) == h for h in range(2)]
        blank = jnp.zeros((bq, LANES), BF16)

        def qblock(i, carry):
            qrows = pl.ds(pl.multiple_of(i * bq, bq), bq)
            qv = q_ref[qrows, :] * scale
            dov = do_ref[qrows, :].astype(BF16)
            ltv = lt_ref[qrows, :]
            qs = [jnp.where(m, qv, blank) for m in heads]
            dos = [jnp.where(m, dov, blank) for m in heads]
            totals = [jnp.sum(jnp.where(lane == h * HEAD_DIM, ltv, 0.0), axis=-1, keepdims=True) for h in range(2)]

            def tile(j, state, mask):
                krows = pl.ds(pl.multiple_of(j * bq, bq), bq)
                kj, vj = k_ref[krows, :], v_ref[krows, :]
                kv = kj * scale
                dq = state[4]
                dk = dv = None
                new = []
                for h in range(2):
                    l_pre, g_pre = state[2 * h], state[2 * h + 1]
                    z = _dot_nt(qs[h], kj)
                    sp = _softplus(z)
                    lg = -sp if mask is None else jnp.where(mask, -sp, 0.0)
                    log_beta = z - sp
                    a = jnp.exp(log_beta + (totals[h] - (_split_dot(lg, upto) + l_pre)))
                    if mask is not None:
                        a = jnp.where(mask, a, 0.0)
                    g = a * _dot_nt(dos[h], vj)
                    beta = jnp.exp(log_beta)
                    dz = g * (1.0 - beta) - beta * (_split_dot(g, before) + g_pre)
                    if mask is not None:
                        dz = jnp.where(mask, dz, 0.0)
                    dzb = dz.astype(BF16)
                    dv_h = _dot_tn(a.astype(BF16), dos[h])
                    dk_h = _dot_tn(dzb, qs[h])
                    dv = dv_h if dv is None else dv + dv_h
                    dk = dk_h if dk is None else dk + dk_h
                    dq = dq + _dot(dzb, jnp.where(heads[h], kv, blank))
                    new += [l_pre + jnp.sum(lg, axis=-1, keepdims=True), g_pre + jnp.sum(g, axis=-1, keepdims=True)]
                dv_acc[krows, :] += dv
                dk_acc[krows, :] += dk
                return (*new, dq)

            zero = jnp.zeros((bq, 1), F32)
            state = lax.fori_loop(0, i, lambda j, st: tile(j, st, None),
                                  (zero, zero, zero, zero, jnp.zeros((bq, LANES), F32)))
            state = tile(i, state, col < row)
            dq_ref[qrows, :] = state[4].astype(BF16)
            return carry

        lax.fori_loop(0, nq, qblock, 0)
        dk_ref[...] = dk_acc[...].astype(BF16)
        dv_ref[...] = dv_acc[...].astype(BF16)

    pair = pl.BlockSpec((seq, LANES), lambda b, p: (b, p))
    shape = jax.ShapeDtypeStruct((T, N_SB_HEADS * HEAD_DIM), BF16)
    return _call(
        body, name=name, grid=(b_loc, npair),
        in_specs=[pl.BlockSpec((seq, LANES), lambda b, p: (b, 12 + p)),
                  pl.BlockSpec((seq, LANES), lambda b, p: (b, 20 + p)),
                  pl.BlockSpec((seq, LANES), lambda b, p: (b, 28 + p)),
                  pair,
                  pl.BlockSpec((None, seq, LANES), lambda b, p: (b * npair + p, 0, 0))],
        out_specs=[pair, pair, pair],
        out_shape=[shape, shape, shape],
        scratch_shapes=[pltpu.VMEM((seq, LANES), F32)] * 2,
        sem=("parallel", "parallel"), args=(proj, proj, proj, dyb, ltot), riders=riders)


def allreduce_small(v, name):
    R = v.shape[0]
    ndev = 8

    def body(v_ref, o_ref, land, send_sems, recv_sems):
        x, y, c = _place()
        me = 4 * x + 2 * y + c
        land[me] = v_ref[...]
        started = []
        for f in range(1, ndev):
            fx, fy, fc = (f >> 2) & 1, (f >> 1) & 1, f & 1
            peer = ((1 - x) if fx else x, (1 - y) if fy else y, (1 - c) if fc else c)
            cp = pltpu.make_async_remote_copy(
                src_ref=v_ref, dst_ref=land.at[me], send_sem=send_sems.at[f - 1], recv_sem=recv_sems.at[f - 1],
                device_id=peer, device_id_type=MESH)
            cp.start()
            started.append(cp)
        for cp in started:
            cp.wait()
        total = land[0]
        for d in range(1, ndev):
            total = total + land[d]
        o_ref[...] = total

    return pl.pallas_call(
        body, name=name,
        in_specs=[VMEM_WHOLE], out_specs=VMEM_WHOLE,
        out_shape=jax.ShapeDtypeStruct(v.shape, F32),
        scratch_shapes=[pltpu.VMEM((ndev, R, LANES), F32),
                        pltpu.SemaphoreType.DMA((ndev - 1,)), pltpu.SemaphoreType.DMA((ndev - 1,))],
    )(v)


def sum_cores(grads, theirs, name):
    _, _, r, c = grads.shape
    tr, tc = _tiles(r, c, 1 << 18)
    core = lax.axis_index("c").astype(jnp.int32).reshape(1)

    def body(core_ref, g_ref, t_ref, o_ref):
        o_ref[...] = (g_ref[...].astype(F32) + t_ref[...].astype(F32)).astype(BF16)

    blk = pl.BlockSpec((N_CHIPS, tr, tc), lambda i, j, core_ref: (0, i, j))
    return pl.pallas_call(
        body, name=name,
        grid_spec=pltpu.PrefetchScalarGridSpec(
            num_scalar_prefetch=1, grid=(r // tr, c // tc),
            in_specs=[pl.BlockSpec((None, N_CHIPS, tr, tc), lambda i, j, core_ref: (core_ref[0], 0, i, j)), blk],
            out_specs=blk),
        out_shape=jax.ShapeDtypeStruct((N_CHIPS, r, c), BF16),
        compiler_params=_params("parallel", "parallel"),
    )(core, grads, theirs)


def sum_chips(halves, landed, name):
    _, r, c = landed.shape
    tr, tc = _tiles(r, c, 1 << 18)
    chip = (2 * lax.axis_index("x") + lax.axis_index("y")).astype(jnp.int32).reshape(1)

    def body(chip_ref, own_ref, p_ref, o_ref):
        total = own_ref[...].astype(F32)
        for j in range(N_CHIPS - 1):
            total = total + p_ref[j].astype(F32)
        o_ref[...] = total

    return pl.pallas_call(
        body, name=name,
        grid_spec=pltpu.PrefetchScalarGridSpec(
            num_scalar_prefetch=1, grid=(r // tr, c // tc),
            in_specs=[pl.BlockSpec((None, tr, tc), lambda i, j, chip_ref: (chip_ref[0], i, j)),
                      pl.BlockSpec((N_CHIPS - 1, tr, tc), lambda i, j, chip_ref: (0, i, j))],
            out_specs=pl.BlockSpec((tr, tc), lambda i, j, chip_ref: (i, j))),
        out_shape=jax.ShapeDtypeStruct((r, c), F32),
        compiler_params=_params("parallel", "parallel"),
    )(chip, halves, landed)


def _adamw_math(w, g, m, v):
    m = ADAM_B1 * m + (1.0 - ADAM_B1) * g
    v = ADAM_B2 * v + (1.0 - ADAM_B2) * (g * g)
    m_hat = m / (1.0 - ADAM_B1 ** ADAM_STEP)
    v_hat = v / (1.0 - ADAM_B2 ** ADAM_STEP)
    delta = -ADAM_LR * (m_hat / (jnp.sqrt(v_hat) + ADAM_EPS) + ADAM_WD * w)
    return delta, m, v


def adamw(w, m, v, g, name):
    r, c = w.shape
    tr, tc = _tiles(r, c, 1 << 18, multiple=8)

    def body(w_ref, m_ref, v_ref, gin_ref, g_ref, d_ref, nm_ref, nv_ref):
        g = gin_ref[...]
        delta, nm, nv = _adamw_math(w_ref[...], g, m_ref[...], v_ref[...])
        g_ref[...] = g
        d_ref[...] = delta
        nm_ref[...] = nm
        nv_ref[...] = nv

    blk = pl.BlockSpec((tr, tc), lambda i, j: (i, j))
    shape = jax.ShapeDtypeStruct((r, c), F32)
    return pl.pallas_call(
        body, name=name, grid=(r // tr, c // tc),
        in_specs=[blk] * 4, out_specs=[blk] * 4, out_shape=[shape] * 4,
        compiler_params=_params("parallel", "parallel"),
    )(w, m, v, g)


def adamw_halves(w, m, v, mine, theirs, name, col_halves=False):
    r, c = w.shape
    hr, hc = (r, c // 2) if col_halves else (r // 2, c)
    tr, tc = _tiles(hr, hc, 1 << 18, multiple=8)
    g_cols = mine.shape[-1]
    padded = g_cols != hc
    if padded and tc != hc:
        raise ValueError("a gradient with padded columns is read in full-width row blocks")
    per_half = hc // tc if col_halves else hr // tr
    core = lax.axis_index("c").astype(jnp.int32).reshape(1)

    def body(core_ref, w_ref, m_ref, v_ref, mine_ref, theirs_ref, g_ref, d_ref, nm_ref, nv_ref):
        half = pl.program_id(1 if col_halves else 0) // per_half
        cols = slice(0, hc) if padded else slice(None)
        g = jnp.where(half == core_ref[0], mine_ref[:, cols], theirs_ref[:, cols])
        delta, nm, nv = _adamw_math(w_ref[...], g, m_ref[...], v_ref[...])
        g_ref[...] = g
        d_ref[...] = delta
        nm_ref[...] = nm
        nv_ref[...] = nv

    blk = pl.BlockSpec((tr, tc), lambda i, j, core_ref: (i, j))
    if col_halves:
        g_spec = pl.BlockSpec((tr, tc), lambda i, j, core_ref: (i, j % per_half))
    else:
        g_spec = pl.BlockSpec((tr, g_cols if padded else tc), lambda i, j, core_ref: (i % per_half, j))
    shape = jax.ShapeDtypeStruct((r, c), F32)
    return pl.pallas_call(
        body, name=name,
        grid_spec=pltpu.PrefetchScalarGridSpec(
            num_scalar_prefetch=1, grid=(r // tr, c // tc),
            in_specs=[blk, blk, blk, g_spec, g_spec], out_specs=[blk] * 4),
        out_shape=[shape] * 4,
        compiler_params=_params("parallel", "parallel"),
    )(core, w, m, v, mine, theirs)


MATRICES = ("ffn1_w_gate", "ffn1_w_up", "ffn1_w_down", "w_in", "w_out", "ffn2_w_gate", "ffn2_w_up", "ffn2_w_down")
VECTORS = ("ffn1_norm", "mix_norm", "swa_sinks", "swa_out_norm", "sb_out_norm", "ffn2_norm", "final_norm")
WEIGHTS = ("ffn1_norm", "ffn1_w_gate", "ffn1_w_up", "ffn1_w_down", "mix_norm", "w_in", "swa_sinks", "swa_out_norm",
           "sb_out_norm", "w_out", "ffn2_norm", "ffn2_w_gate", "ffn2_w_up", "ffn2_w_down", "final_norm")


def _pack_rows(vec):
    flat = vec.reshape(-1).astype(F32)
    rows = -(-flat.shape[0] // LANES)
    rows8 = -(-rows // 8) * 8
    flat = jnp.pad(flat, (0, rows8 * LANES - flat.shape[0]))
    return flat.reshape(rows8, LANES)


class GradPipe:
    def __init__(self, name, grads):
        self.name, self.step, self.value = name, 0, grads

    def rider(self):
        make = (half_swap_rider, scatter_rider, final_swap_rider)[self.step]
        return make([self.value])

    def landed(self, outs):
        if self.step == 0:
            self.value = sum_cores(self.value, outs[0], "sum_cores_" + self.name)
        elif self.step == 1:
            self.value = sum_chips(self.value, outs[0], "sum_chips_" + self.name)
        else:
            self.value = (self.value, outs[0])
        self.step += 1


def _hosted(fn, *args, advance=(), **kw):
    outs, routs = fn(*args, riders=[p.rider() for p in advance], **kw)
    for p, ro in zip(advance, routs):
        p.landed(ro)
    return outs


def kernel(x, ffn1_norm, ffn1_w_gate, ffn1_w_up, ffn1_w_down, mix_norm, w_in, swa_sinks, swa_out_norm, sb_out_norm, w_out, ffn2_norm, ffn2_w_gate, ffn2_w_up, ffn2_w_down, final_norm, loss_target, m_ffn1_norm, m_ffn1_w_gate, m_ffn1_w_up, m_ffn1_w_down, m_mix_norm, m_w_in, m_swa_sinks, m_swa_out_norm, m_sb_out_norm, m_w_out, m_ffn2_norm, m_ffn2_w_gate, m_ffn2_w_up, m_ffn2_w_down, m_final_norm, v_ffn1_norm, v_ffn1_w_gate, v_ffn1_w_up, v_ffn1_w_down, v_mix_norm, v_w_in, v_swa_sinks, v_swa_out_norm, v_sb_out_norm, v_w_out, v_ffn2_norm, v_ffn2_w_gate, v_ffn2_w_up, v_ffn2_w_down, v_final_norm):
    given = dict(locals())
    b_loc, seq, D = x.shape
    T = b_loc * seq
    x0 = x.reshape(T, D)
    target = loss_target.reshape(T, D)
    final_g = final_norm.reshape(1, D)

    gathered = {}

    def gather(*names):
        shards = []
        for nm in names:
            w = given[nm][0].astype(BF16)
            hidden_axis = 0 if nm.endswith("w_down") else 1 if nm.startswith("ffn") else None
            if hidden_axis is not None:
                pad = -w.shape[hidden_axis] % LANES
                w = jnp.pad(w, [(0, pad) if a == hidden_axis else (0, 0) for a in range(2)])
            shards.append(w.reshape(2, w.shape[0] // 2, w.shape[1]))
        return names, gather_rider(shards)

    def hosting(fn, *args, fetch):
        names, rider = fetch
        outs, landed = fn(*args, riders=[rider])
        for nm, o in zip(names, landed[0]):
            gathered[nm] = o.reshape(N_CHIPS, 2 * o.shape[2], o.shape[3])
        return outs

    hosting(lambda riders: ([], exchange_now(riders, "gather_ffn1_gate")), fetch=gather("ffn1_w_gate"))

    h1 = rmsnorm_cast(x0, ffn1_norm, "ffn1_norm_fwd")
    (g1,) = hosting(ffn_gate, h1, gathered["ffn1_w_gate"], "ffn1_gate", fetch=gather("ffn1_w_up"))
    u1, a1 = hosting(ffn_up_act, h1, gathered["ffn1_w_up"], g1, "ffn1_up_act", fetch=gather("ffn1_w_down"))
    (x1,) = hosting(ffn_down, a1, gathered["ffn1_w_down"], x0, "ffn1_down", fetch=gather("w_in"))
    h2 = rmsnorm_cast(x1, mix_norm, "mix_norm_fwd")
    (proj,) = hosting(in_proj, h2, gathered["w_in"], "in_proj", fetch=gather("w_out"))
    w_out_all = gathered["w_out"].reshape(-1, D)
    (ya,) = hosting(swa_fwd, proj, swa_sinks, b_loc, seq, "swa_fwd", fetch=gather("ffn2_w_gate"))
    yb, ltot = hosting(sb_fwd, proj, b_loc, seq, "sb_fwd", fetch=gather("ffn2_w_up", "ffn2_w_down"))
    y = out_norm(ya, yb, swa_out_norm, sb_out_norm, "out_norm_fwd")
    x2 = out_proj(y, w_out_all, x1, "out_proj")
    h3 = rmsnorm_cast(x2, ffn2_norm, "ffn2_norm_fwd")
    (g2, u2, a2), _ = ffn_up(h3, gathered["ffn2_w_gate"], gathered["ffn2_w_up"], "ffn2_up")
    (x3,), _ = ffn_down(a2, gathered["ffn2_w_down"], x2, "ffn2_down")
    dx3, d_final, loss_part = final_loss(x3, final_g, target, "final_loss")

    Fs = gathered["ffn1_w_gate"].shape[2]
    Ps = gathered["w_in"].shape[2]
    Os = w_out_all.shape[0] // N_CHIPS
    pipes = {}

    def down_grad(tag, act, dx, advance):
        (grads,) = _hosted(wgrad, act, dx, tag + "_dwd", groups=N_CHIPS, m=Fs, n=D // 2,
                           a_block=lambda tk: (None, tk, Fs), a_map=lambda s, j, k: (s, k, 0),
                           b_block=lambda tk: (tk, D // 2), b_map=lambda s, j, k: (k, j), b_scale=0.5,
                           advance=advance)
        return grads

    def up_grad(name, h, d_act, advance):
        (grads,) = _hosted(wgrad, h, d_act, name, groups=N_CHIPS, m=D // 2, n=Fs,
                           a_block=lambda tk: (tk, D // 2), a_map=lambda s, j, k: (k, j),
                           b_block=lambda tk: (None, tk, Fs), b_map=lambda s, j, k: (s, k, 0), advance=advance)
        return grads

    (dg2, du2), _ = ffn_bwd_act(dx3, gathered["ffn2_w_down"], g2, u2, "ffn2_bwd_act")
    pipes["ffn2_w_down"] = p_d2 = GradPipe("ffn2_w_down", down_grad("ffn2", a2, dx3, []))
    pipes["ffn2_w_gate"] = p_g2 = GradPipe("ffn2_w_gate", up_grad("ffn2_dwg", h3, dg2, [p_d2]))
    pipes["ffn2_w_up"] = p_u2 = GradPipe("ffn2_w_up", up_grad("ffn2_dwu", h3, du2, [p_d2, p_g2]))
    (dh3,) = _hosted(ffn_bwd_dh, dg2, du2, gathered["ffn2_w_gate"], gathered["ffn2_w_up"], "ffn2_bwd_dh",
                     advance=[p_d2, p_g2, p_u2])
    (dx2, d_ffn2_norm), _ = rmsnorm_bwd_add(x2, ffn2_norm, dh3, dx3, "ffn2_norm_bwd")

    (d_w_out,) = _hosted(wgrad, y, dx2, "dw_out", groups=N_CHIPS, m=Os // 2, n=D,
                         a_block=lambda tk: (tk, Os // 2), a_map=lambda s, j, k: (k, 2 * s + j),
                         b_block=lambda tk: (tk, D), b_map=lambda s, j, k: (k, 0), advance=[p_g2])
    pipes["w_out"] = p_out = GradPipe("w_out", d_w_out)
    dya, dyb, d_swa_norm, d_sb_norm = out_proj_bwd(dx2, w_out_all, ya, yb, swa_out_norm, sb_out_norm, "out_proj_bwd")
    dqa, dka, dva, d_sinks = _hosted(swa_bwd, proj, swa_sinks, dya, b_loc, seq, "swa_bwd", advance=[p_u2, p_out])
    dqb, dkb, dvb = _hosted(sb_bwd, proj, dyb, ltot, b_loc, seq, "sb_bwd", advance=[p_u2, p_out])
    dproj = jnp.concatenate([dqa, dka, dva, dqb, dkb, dvb], axis=1)
    (d_w_in,) = _hosted(wgrad, h2, dproj, "dw_in", groups=N_CHIPS, m=D // 2, n=Ps,
                        a_block=lambda tk: (tk, D // 2), a_map=lambda s, j, k: (k, j),
                        b_block=lambda tk: (tk, Ps), b_map=lambda s, j, k: (k, s), advance=[p_out])
    pipes["w_in"] = p_in = GradPipe("w_in", d_w_in)
    (dh2,) = _hosted(in_proj_bwd, dproj, gathered["w_in"], "in_proj_bwd", advance=[p_in])
    (dx1, d_mix_norm), _ = rmsnorm_bwd_add(x1, mix_norm, dh2, dx2, "mix_norm_bwd")

    pipes["ffn1_w_down"] = p_d1 = GradPipe("ffn1_w_down", down_grad("ffn1", a1, dx1, [p_in]))
    dg1, du1 = _hosted(ffn_bwd_act, dx1, gathered["ffn1_w_down"], g1, u1, "ffn1_bwd_act", advance=[p_in, p_d1])
    pipes["ffn1_w_gate"] = p_g1 = GradPipe("ffn1_w_gate", up_grad("ffn1_dwg", h1, dg1, [p_d1]))
    pipes["ffn1_w_up"] = p_u1 = GradPipe("ffn1_w_up", up_grad("ffn1_dwu", h1, du1, [p_d1, p_g1]))
    (dh1,) = _hosted(ffn_bwd_dh, dg1, du1, gathered["ffn1_w_gate"], gathered["ffn1_w_up"], "ffn1_bwd_dh_a",
                     part=(0, 2), advance=[p_g1, p_u1])
    (dh1,) = _hosted(ffn_bwd_dh, dg1, du1, gathered["ffn1_w_gate"], gathered["ffn1_w_up"], "ffn1_bwd_dh_b",
                     part=(1, 2), prev=dh1, advance=[p_g1, p_u1])
    (dx0, d_ffn1_norm), _ = rmsnorm_bwd_add(x0, ffn1_norm, dh1, dx1, "ffn1_norm_bwd")
    (last,) = exchange_now([p_u1.rider()], "swap_ffn1_w_up")
    p_u1.landed(last)

    vec_grads = dict(ffn1_norm=d_ffn1_norm, mix_norm=d_mix_norm, swa_sinks=d_sinks[:, 0], swa_out_norm=d_swa_norm,
                     sb_out_norm=d_sb_norm, ffn2_norm=d_ffn2_norm, final_norm=d_final)
    packed = [_pack_rows(vec_grads[nm]) for nm in VECTORS] + [_pack_rows(loss_part[0, :1])]
    offsets = [0]
    for p in packed:
        offsets.append(offsets[-1] + p.shape[0])
    reduced = allreduce_small(jnp.concatenate(packed, axis=0), "reduce_vectors")
    loss = reduced[offsets[len(VECTORS)], 0]

    out = {}
    for nm in MATRICES:
        shape = given[nm].shape
        mine, theirs = pipes[nm].value
        res = adamw_halves(given[nm][0], given["m_" + nm][0], given["v_" + nm][0], mine, theirs, "adamw_" + nm,
                           col_halves=nm.endswith("w_down"))
        out[nm] = [r.reshape(shape) for r in res]
    w_rows = jnp.concatenate([_pack_rows(given[nm]) for nm in VECTORS], axis=0)
    m_rows = jnp.concatenate([_pack_rows(given["m_" + nm]) for nm in VECTORS], axis=0)
    v_rows = jnp.concatenate([_pack_rows(given["v_" + nm]) for nm in VECTORS], axis=0)
    g_rows = reduced[:offsets[len(VECTORS)]]
    small = adamw(w_rows, m_rows, v_rows, g_rows, "adamw_vectors")
    for i, nm in enumerate(VECTORS):
        shape = given[nm].shape
        size = math.prod(shape)
        out[nm] = [r[offsets[i]:offsets[i + 1]].reshape(-1)[:size].reshape(shape) for r in small]

    grad_x = dx0.reshape(b_loc, seq, D)
    return (loss, grad_x, *[out[nm][0] for nm in WEIGHTS], *[out[nm][1] for nm in WEIGHTS],
            *[out[nm][2] for nm in WEIGHTS], *[out[nm][3] for nm in WEIGHTS])
```

```python
import functools
import math

import jax
import jax.numpy as jnp
from jax import lax
from jax.experimental import pallas as pl
from jax.experimental.pallas import tpu as pltpu

F32 = jnp.float32
BF16 = jnp.bfloat16
MESH = pl.DeviceIdType.MESH

EPS = 1e-6
HEAD_DIM = 64
LANES = 128
N_SWA_HEADS = 16
N_SWA_KV = 4
N_SB_HEADS = 16
WINDOW = 128
SWA_BLOCK = 128
SB_BLOCK = 256
N_CHIPS = 4
ATT_SCALE = HEAD_DIM ** -0.5

ADAM_LR = 0.001
ADAM_B1 = 0.9
ADAM_B2 = 0.999
ADAM_EPS = 1e-08
ADAM_WD = 0.01
ADAM_STEP = 10

VMEM_LIMIT = 56 * 1024 * 1024
NEG_BIG = -1e30

ANY = pl.BlockSpec(memory_space=pl.ANY)
VMEM_WHOLE = pl.BlockSpec(memory_space=pltpu.VMEM)


def _row_tile(rows, cap, multiple=16):
    best = None
    for t in range(multiple, min(rows, cap) + 1, multiple):
        if rows % t == 0:
            best = t
    if best is None:
        raise ValueError(f"no row tile for {rows} rows under {cap}")
    return best


def _params(*sem):
    return pltpu.CompilerParams(dimension_semantics=sem, vmem_limit_bytes=VMEM_LIMIT)


def _dot(a, b):
    return jnp.dot(a, b, preferred_element_type=F32)


def _dot_nt(a, b):
    return lax.dot_general(a, b, (((1,), (1,)), ((), ())), preferred_element_type=F32)


def _dot_tn(a, b):
    return lax.dot_general(a, b, (((0,), (0,)), ((), ())), preferred_element_type=F32)


def _sigmoid(x):
    return 1.0 / (1.0 + jnp.exp(-x))


def _split_dot(x, u):
    hi = x.astype(BF16)
    lo = (x - hi.astype(F32)).astype(BF16)
    return _dot(hi, u) + _dot(lo, u)


def _tiles(r, c, max_elems, multiple=16):
    col_opts = [c] + [t for t in (2048, 1024, 512, 256, 128) if t < c and c % t == 0]
    best = None
    for tc in col_opts:
        for tr in [r] + list(range(multiple, r, multiple)):
            if r % tr == 0 and tr * tc <= max_elems and (best is None or tr * tc > best[0] * best[1]):
                best = (tr, tc)
    if best is None:
        raise ValueError(f"no tile for [{r}, {c}] under {max_elems} elements")
    return best


class Rider:
    def __init__(self, arrays, out_shapes, sem_shapes, start, finish):
        self.arrays, self.out_shapes, self.sem_shapes = list(arrays), list(out_shapes), list(sem_shapes)
        self.start, self.finish = start, finish


def _place():
    return lax.axis_index("x"), lax.axis_index("y"), lax.axis_index("c")


def _other_chips(x, y):
    return [(1 - x, y), (x, 1 - y), (1 - x, 1 - y)]


def _split(ref):
    rows = ref.shape[-2]
    n = next(k for k in (4, 2, 1) if rows % (16 * k) == 0)
    step = rows // n
    lead = [()] if len(ref.shape) == 2 else [(s,) for s in range(ref.shape[0])]
    return [ref.at[(*idx, pl.ds(i * step, step))] for idx in lead for i in range(n)]


class _Copy:
    def __init__(self, src, dst, make):
        self.src, self.dst, self.make = src, dst, make

    def start(self):
        for s, d in zip(_split(self.src), _split(self.dst)):
            self.make(s, d).start()

    def wait(self):
        self.make(self.src, self.dst).wait()

    def wait_send(self):
        self.make(self.src, self.dst).wait_send()

    def wait_recv(self):
        self.make(self.src, self.dst).wait_recv()


def _remote(src, dst, send_sem, recv_sem, device):
    return _Copy(src, dst, lambda s, d: pltpu.make_async_remote_copy(
        src_ref=s, dst_ref=d, send_sem=send_sem, recv_sem=recv_sem, device_id=device, device_id_type=MESH))


def gather_rider(shards):
    n = len(shards)

    def copies(ins, outs, sems):
        ici_s, ici_r, d2d_s, d2d_r, own_s, own_r = sems
        x, y, c = _place()
        me = 2 * x + y
        own, ici, fwd = [], [], []
        for k in range(n):
            own.append(_remote(ins[k], outs[k].at[me], own_s.at[k], own_r.at[k], (x, y, 1 - c)))
            for j, (px, py) in enumerate(_other_chips(x, y)):
                i = 3 * k + j
                ici.append(_remote(ins[k].at[c], outs[k].at[me, c], ici_s.at[i], ici_r.at[i], (px, py, c)))
                got = outs[k].at[2 * px + py, c]
                fwd.append(_remote(got, got, d2d_s.at[i], d2d_r.at[i], (x, y, 1 - c)))
        return own, ici, fwd

    def start(ins, outs, sems):
        own, ici, _ = copies(ins, outs, sems)
        for cp in own + ici:
            cp.start()

    def finish(ins, outs, sems):
        own, ici, fwd = copies(ins, outs, sems)
        for a, f in zip(ici, fwd):
            a.wait_recv()
            f.start()
        for a in ici:
            a.wait_send()
        for f in fwd:
            f.wait()
        for o in own:
            o.wait()

    dma = pltpu.SemaphoreType.DMA
    return Rider(shards, [jax.ShapeDtypeStruct((N_CHIPS,) + s.shape, s.dtype) for s in shards],
                 [dma((3 * n,)), dma((3 * n,)), dma((3 * n,)), dma((3 * n,)), dma((n,)), dma((n,))], start, finish)


def half_swap_rider(grads):
    n = len(grads)

    def copies(ins, outs, sems):
        x, y, c = _place()
        return [_remote(ins[k].at[1 - c], outs[k], sems[0].at[k], sems[1].at[k], (x, y, 1 - c)) for k in range(n)]

    def start(ins, outs, sems):
        for cp in copies(ins, outs, sems):
            cp.start()

    def finish(ins, outs, sems):
        for cp in copies(ins, outs, sems):
            cp.wait()

    dma = pltpu.SemaphoreType.DMA
    return Rider(grads, [jax.ShapeDtypeStruct(g.shape[1:], g.dtype) for g in grads], [dma((n,)), dma((n,))],
                 start, finish)


def scatter_rider(halves):
    n = len(halves)

    def copies(ins, outs, sems):
        send, recv = sems
        x, y, c = _place()
        cps = []
        for k in range(n):
            for j, (px, py) in enumerate(_other_chips(x, y)):
                i = 3 * k + j
                cps.append(_remote(ins[k].at[2 * px + py], outs[k].at[j], send.at[i], recv.at[i], (px, py, c)))
        return cps

    def start(ins, outs, sems):
        for cp in copies(ins, outs, sems):
            cp.start()

    def finish(ins, outs, sems):
        for cp in copies(ins, outs, sems):
            cp.wait()

    dma = pltpu.SemaphoreType.DMA
    return Rider(halves, [jax.ShapeDtypeStruct((3,) + h.shape[1:], h.dtype) for h in halves],
                 [dma((3 * n,)), dma((3 * n,))], start, finish)


def final_swap_rider(sums):
    n = len(sums)

    def copies(ins, outs, sems):
        send, recv = sems
        x, y, c = _place()
        return [_remote(ins[k], outs[k], send.at[k], recv.at[k], (x, y, 1 - c)) for k in range(n)]

    def start(ins, outs, sems):
        for cp in copies(ins, outs, sems):
            cp.start()

    def finish(ins, outs, sems):
        for cp in copies(ins, outs, sems):
            cp.wait()

    dma = pltpu.SemaphoreType.DMA
    return Rider(sums, [jax.ShapeDtypeStruct(s.shape, s.dtype) for s in sums], [dma((n,)), dma((n,))],
                 start, finish)


def _call(body, *, name, grid, in_specs, out_specs, out_shape, scratch_shapes=(), sem, args, riders=(), aliases=None):
    in_specs, out_specs, out_shape = list(in_specs), list(out_specs), list(out_shape)
    scratch_shapes = list(scratch_shapes)
    aliases = aliases or {}
    if not riders:
        outs = pl.pallas_call(body, name=name, grid=grid, in_specs=in_specs, out_specs=out_specs, out_shape=out_shape,
                              scratch_shapes=scratch_shapes, input_output_aliases=aliases,
                              compiler_params=_params(*sem))(*args)
        return list(outs), []
    n_in, n_out, n_scr = len(in_specs), len(out_specs), len(scratch_shapes)
    r_in = [len(r.arrays) for r in riders]
    r_out = [len(r.out_shapes) for r in riders]
    r_sem = [len(r.sem_shapes) for r in riders]

    def wrapped(*refs):
        pos = 0

        def take(count):
            nonlocal pos
            got = refs[pos:pos + count]
            pos += count
            return got

        ins = take(n_in)
        rins = [take(k) for k in r_in]
        outs = take(n_out)
        routs = [take(k) for k in r_out]
        scr = take(n_scr)
        rsems = [take(k) for k in r_sem]
        first = functools.reduce(jnp.logical_and, [pl.program_id(a) == 0 for a in range(len(grid))])
        last = functools.reduce(jnp.logical_and, [pl.program_id(a) == grid[a] - 1 for a in range(len(grid))])

        @pl.when(first)
        def _():
            for r, ri, ro, rs in zip(riders, rins, routs, rsems):
                r.start(ri, ro, rs)

        body(*ins, *outs, *scr)

        @pl.when(last)
        def _():
            for r, ri, ro, rs in zip(riders, rins, routs, rsems):
                r.finish(ri, ro, rs)

    any_spec = pl.BlockSpec(memory_space=pl.ANY)
    res = pl.pallas_call(
        wrapped, name=name, grid=grid,
        in_specs=in_specs + [any_spec] * sum(r_in),
        out_specs=out_specs + [any_spec] * sum(r_out),
        out_shape=out_shape + [s for r in riders for s in r.out_shapes],
        scratch_shapes=scratch_shapes + [s for r in riders for s in r.sem_shapes],
        input_output_aliases=aliases,
        compiler_params=_params(*(["arbitrary"] * len(grid))),
    )(*args, *[a for r in riders for a in r.arrays])
    res = list(res)
    outs, rest = res[:n_out], res[n_out:]
    routs = []
    for k in r_out:
        routs.append(rest[:k])
        rest = rest[k:]
    return outs, routs


def exchange_now(riders, name):
    n_in = [len(r.arrays) for r in riders]
    n_out = [len(r.out_shapes) for r in riders]
    n_sem = [len(r.sem_shapes) for r in riders]

    def body(*refs):
        pos = 0
        groups = []
        for counts in (n_in, n_out, n_sem):
            part = []
            for k in counts:
                part.append(refs[pos:pos + k])
                pos += k
            groups.append(part)
        for r, ri, ro, rs in zip(riders, *groups):
            r.start(ri, ro, rs)
        for r, ri, ro, rs in zip(riders, *groups):
            r.finish(ri, ro, rs)

    any_spec = pl.BlockSpec(memory_space=pl.ANY)
    res = pl.pallas_call(
        body, name=name,
        in_specs=[any_spec] * sum(n_in), out_specs=[any_spec] * sum(n_out),
        out_shape=[s for r in riders for s in r.out_shapes],
        scratch_shapes=[s for r in riders for s in r.sem_shapes],
    )(*[a for r in riders for a in r.arrays])
    res = list(res)
    routs = []
    for k in n_out:
        routs.append(res[:k])
        res = res[k:]
    return routs


def rmsnorm_cast(x, gamma, name):
    T, D = x.shape
    tr = min(512, T)

    def body(x_ref, g_ref, o_ref):
        xv = x_ref[...]
        r = lax.rsqrt(jnp.mean(xv * xv, axis=-1, keepdims=True) + EPS)
        o_ref[...] = ((xv * r) * g_ref[...]).astype(BF16)

    return pl.pallas_call(
        body, name=name, grid=(T // tr,),
        in_specs=[pl.BlockSpec((tr, D), lambda i: (i, 0)), pl.BlockSpec((1, D), lambda i: (0, 0))],
        out_specs=pl.BlockSpec((tr, D), lambda i: (i, 0)),
        out_shape=jax.ShapeDtypeStruct((T, D), BF16),
        compiler_params=_params("parallel"),
    )(x, gamma)


def _rms_bwd(xv, gamma, dy):
    d = xv.shape[-1]
    r = lax.rsqrt(jnp.mean(xv * xv, axis=-1, keepdims=True) + EPS)
    gdy = dy * gamma
    dot = jnp.sum(gdy * xv, axis=-1, keepdims=True)
    dx = r * gdy - xv * ((r * r * r) * (dot / d))
    dgamma = jnp.sum(dy * (xv * r), axis=0, keepdims=True)
    return dx, dgamma


def rmsnorm_bwd_add(x, gamma, dy, dx_up, name, riders=()):
    T, D = x.shape
    tr = min(256, T)

    def body(x_ref, g_ref, dy_ref, up_ref, dx_ref, dg_ref):
        dx, dgamma = _rms_bwd(x_ref[...], g_ref[...], dy_ref[...])
        dx_ref[...] = up_ref[...] + dx

        @pl.when(pl.program_id(0) == 0)
        def _():
            dg_ref[...] = jnp.zeros_like(dg_ref)

        dg_ref[...] += dgamma

    row = pl.BlockSpec((tr, D), lambda i: (i, 0))
    vec = pl.BlockSpec((1, D), lambda i: (0, 0))
    return _call(
        body, name=name, grid=(T // tr,),
        in_specs=[row, vec, row, row], out_specs=[row, vec],
        out_shape=[jax.ShapeDtypeStruct((T, D), F32), jax.ShapeDtypeStruct((1, D), F32)],
        sem=("arbitrary",), args=(x, gamma, dy, dx_up), riders=riders)


def ffn_up(h, wg, wu, name, riders=()):
    T, D = h.shape
    G, _, Fs = wg.shape
    tm = min(512, T)

    def body(h_ref, wg_ref, wu_ref, g_ref, u_ref, a_ref):
        hv = h_ref[...]
        g = _dot(hv, wg_ref[...])
        u = _dot(hv, wu_ref[...])
        g_ref[...] = g
        u_ref[...] = u
        a_ref[...] = ((g * _sigmoid(g)) * u).astype(BF16)

    w_spec = pl.BlockSpec((None, D, Fs), lambda s, i: (s, 0, 0), pipeline_mode=pl.Buffered(1))
    o_spec = pl.BlockSpec((None, tm, Fs), lambda s, i: (s, i, 0))
    return _call(
        body, name=name, grid=(G, T // tm),
        in_specs=[pl.BlockSpec((tm, D), lambda s, i: (i, 0)), w_spec, w_spec],
        out_specs=[o_spec, o_spec, o_spec],
        out_shape=[jax.ShapeDtypeStruct((G, T, Fs), F32), jax.ShapeDtypeStruct((G, T, Fs), F32),
                   jax.ShapeDtypeStruct((G, T, Fs), BF16)],
        sem=("parallel", "parallel"), args=(h, wg, wu), riders=riders)


def ffn_gate(h, wg, name, riders=()):
    T, D = h.shape
    G, _, Fs = wg.shape
    tm = min(512, T)

    def body(h_ref, wg_ref, g_ref):
        g_ref[...] = _dot(h_ref[...], wg_ref[...])

    return _call(
        body, name=name, grid=(G, T // tm),
        in_specs=[pl.BlockSpec((tm, D), lambda s, i: (i, 0)),
                  pl.BlockSpec((None, D, Fs), lambda s, i: (s, 0, 0), pipeline_mode=pl.Buffered(1))],
        out_specs=[pl.BlockSpec((None, tm, Fs), lambda s, i: (s, i, 0))],
        out_shape=[jax.ShapeDtypeStruct((G, T, Fs), F32)],
        sem=("parallel", "parallel"), args=(h, wg), riders=riders)


def ffn_up_act(h, wu, g, name, riders=()):
    T, D = h.shape
    G, _, Fs = wu.shape
    tm = min(512, T)

    def body(h_ref, wu_ref, g_ref, u_ref, a_ref):
        u = _dot(h_ref[...], wu_ref[...])
        g = g_ref[...]
        u_ref[...] = u
        a_ref[...] = ((g * _sigmoid(g)) * u).astype(BF16)

    act = pl.BlockSpec((None, tm, Fs), lambda s, i: (s, i, 0))
    return _call(
        body, name=name, grid=(G, T // tm),
        in_specs=[pl.BlockSpec((tm, D), lambda s, i: (i, 0)),
                  pl.BlockSpec((None, D, Fs), lambda s, i: (s, 0, 0), pipeline_mode=pl.Buffered(1)), act],
        out_specs=[act, act],
        out_shape=[jax.ShapeDtypeStruct((G, T, Fs), F32), jax.ShapeDtypeStruct((G, T, Fs), BF16)],
        sem=("parallel", "parallel"), args=(h, wu, g), riders=riders)


def ffn_down(a, wd, x, name, riders=()):
    G, T, Fs = a.shape
    D = wd.shape[2]
    tm = min(512, T)

    def body(a_ref, w_ref, x_ref, o_ref, acc):
        s = pl.program_id(1)

        @pl.when(s == 0)
        def _():
            acc[...] = jnp.zeros_like(acc)

        acc[...] += _dot(a_ref[...], w_ref[...])

        @pl.when(s == G - 1)
        def _():
            o_ref[...] = x_ref[...] + 0.5 * acc[...]

    return _call(
        body, name=name, grid=(T // tm, G),
        in_specs=[pl.BlockSpec((None, tm, Fs), lambda i, s: (s, i, 0)),
                  pl.BlockSpec((None, Fs, D), lambda i, s: (s, 0, 0)),
                  pl.BlockSpec((tm, D), lambda i, s: (i, 0))],
        out_specs=[pl.BlockSpec((tm, D), lambda i, s: (i, 0))],
        out_shape=[jax.ShapeDtypeStruct((T, D), F32)],
        scratch_shapes=[pltpu.VMEM((tm, D), F32)],
        sem=("parallel", "arbitrary"), args=(a, wd, x), riders=riders)


def ffn_bwd_act(dx, wd, g, u, name, riders=()):
    T, D = dx.shape
    G, Fs, _ = wd.shape
    tm = min(256, T)

    def body(dx_ref, w_ref, g_ref, u_ref, dg_ref, du_ref):
        da = _dot_nt((0.5 * dx_ref[...]).astype(BF16), w_ref[...])
        gv = g_ref[...]
        sig = _sigmoid(gv)
        silu = gv * sig
        du_ref[...] = (da * silu).astype(BF16)
        dg_ref[...] = ((da * u_ref[...]) * (sig * (1.0 + gv * (1.0 - sig)))).astype(BF16)

    act = pl.BlockSpec((None, tm, Fs), lambda s, i: (s, i, 0))
    return _call(
        body, name=name, grid=(G, T // tm),
        in_specs=[pl.BlockSpec((tm, D), lambda s, i: (i, 0)),
                  pl.BlockSpec((None, Fs, D), lambda s, i: (s, 0, 0), pipeline_mode=pl.Buffered(1)),
                  act, act],
        out_specs=[act, act],
        out_shape=[jax.ShapeDtypeStruct((G, T, Fs), BF16), jax.ShapeDtypeStruct((G, T, Fs), BF16)],
        sem=("parallel", "parallel"), args=(dx, wd, g, u), riders=riders)


def ffn_bwd_dh(dg, du, wg, wu, name, riders=(), part=(0, 1), prev=None):
    G, T, Fs = dg.shape
    D = wg.shape[1]
    tm = min(512, T // part[1])
    nblk = T // tm // part[1]
    off = part[0] * nblk

    def body(dg_ref, du_ref, wg_ref, wu_ref, *rest):
        o_ref, acc = rest[-2:]
        s = pl.program_id(1)

        @pl.when(s == 0)
        def _():
            acc[...] = jnp.zeros_like(acc)

        acc[...] += _dot_nt(dg_ref[...], wg_ref[...])
        acc[...] += _dot_nt(du_ref[...], wu_ref[...])

        @pl.when(s == G - 1)
        def _():
            o_ref[...] = acc[...]

    act = pl.BlockSpec((None, tm, Fs), lambda i, s: (s, i + off, 0))
    w_spec = pl.BlockSpec((None, D, Fs), lambda i, s: (s, 0, 0))
    extra = [] if prev is None else [prev]
    return _call(
        body, name=name, grid=(nblk, G),
        in_specs=[act, act, w_spec, w_spec] + [pl.BlockSpec(memory_space=pl.ANY)] * len(extra),
        out_specs=[pl.BlockSpec((tm, D), lambda i, s: (i + off, 0))],
        out_shape=[jax.ShapeDtypeStruct((T, D), F32)],
        scratch_shapes=[pltpu.VMEM((tm, D), F32)],
        sem=("parallel", "arbitrary"), args=(dg, du, wg, wu, *extra), riders=riders,
        aliases={4: 0} if extra else None)


def wgrad(a, b, name, *, groups, m, n, a_block, a_map, b_block, b_map, b_scale=None, tk=512, riders=()):
    T = a.shape[-2]
    tk = min(tk, T)
    nk = T // tk

    def body(a_ref, b_ref, o_ref, acc):
        k = pl.program_id(2)

        @pl.when(k == 0)
        def _():
            acc[...] = jnp.zeros_like(acc)

        bv = b_ref[...]
        if b_scale is not None:
            bv = b_scale * bv
        acc[...] += _dot_tn(a_ref[...].astype(BF16), bv.astype(BF16))

        @pl.when(k == nk - 1)
        def _():
            o_ref[...] = acc[...].astype(BF16)

    return _call(
        body, name=name, grid=(groups, 2, nk),
        in_specs=[pl.BlockSpec(a_block(tk), a_map), pl.BlockSpec(b_block(tk), b_map)],
        out_specs=[pl.BlockSpec((None, None, m, n), lambda s, j, k: (j, s, 0, 0))],
        out_shape=[jax.ShapeDtypeStruct((2, groups, m, n), BF16)],
        scratch_shapes=[pltpu.VMEM((m, n), F32)],
        sem=("parallel", "parallel", "arbitrary"), args=(a, b), riders=riders)


def in_proj(h, w_in, name, riders=()):
    T, D = h.shape
    G, _, Ps = w_in.shape
    tm = min(512, T)

    def body(h_ref, w_ref, o_ref):
        o_ref[...] = _dot(h_ref[...], w_ref[...]).astype(BF16)

    return _call(
        body, name=name, grid=(G, T // tm),
        in_specs=[pl.BlockSpec((tm, D), lambda s, i: (i, 0)),
                  pl.BlockSpec((None, D, Ps), lambda s, i: (s, 0, 0))],
        out_specs=[pl.BlockSpec((tm, Ps), lambda s, i: (i, s))],
        out_shape=[jax.ShapeDtypeStruct((T, G * Ps), BF16)],
        sem=("parallel", "parallel"), args=(h, w_in), riders=riders)


def in_proj_bwd(dproj, w_in, name, riders=()):
    T = dproj.shape[0]
    G, D, Ps = w_in.shape
    tm = min(512, T)

    def body(d_ref, w_ref, o_ref, acc):
        s = pl.program_id(1)

        @pl.when(s == 0)
        def _():
            acc[...] = jnp.zeros_like(acc)

        acc[...] += _dot_nt(d_ref[...], w_ref[...])

        @pl.when(s == G - 1)
        def _():
            o_ref[...] = acc[...]

    return _call(
        body, name=name, grid=(T // tm, G),
        in_specs=[pl.BlockSpec((tm, Ps), lambda i, s: (i, s)),
                  pl.BlockSpec((None, D, Ps), lambda i, s: (s, 0, 0))],
        out_specs=[pl.BlockSpec((tm, D), lambda i, s: (i, 0))],
        out_shape=[jax.ShapeDtypeStruct((T, D), F32)],
        scratch_shapes=[pltpu.VMEM((tm, D), F32)],
        sem=("parallel", "arbitrary"), args=(dproj, w_in), riders=riders)


def out_proj(y, w_out, x, name):
    T, M = y.shape
    D = w_out.shape[1]
    tm = min(512, T)

    def body(y_ref, w_ref, x_ref, o_ref):
        o_ref[...] = x_ref[...] + _dot(y_ref[...], w_ref[...])

    return pl.pallas_call(
        body, name=name, grid=(T // tm,),
        in_specs=[pl.BlockSpec((tm, M), lambda i: (i, 0)), pl.BlockSpec((M, D), lambda i: (0, 0)),
                  pl.BlockSpec((tm, D), lambda i: (i, 0))],
        out_specs=pl.BlockSpec((tm, D), lambda i: (i, 0)),
        out_shape=jax.ShapeDtypeStruct((T, D), F32),
        compiler_params=_params("parallel"),
    )(y, w_out, x)


def out_norm(ya, yb, ga, gb, name):
    T, W = ya.shape
    tr = min(512, T)

    def body(a_ref, b_ref, ga_ref, gb_ref, o_ref):
        for src, gam, col in ((a_ref, ga_ref, 0), (b_ref, gb_ref, W)):
            v = src[...]
            r = lax.rsqrt(jnp.mean(v * v, axis=-1, keepdims=True) + EPS)
            o_ref[:, col:col + W] = ((v * r) * gam[...]).astype(BF16)

    row = pl.BlockSpec((tr, W), lambda i: (i, 0))
    vec = pl.BlockSpec((1, W), lambda i: (0, 0))
    return pl.pallas_call(
        body, name=name, grid=(T // tr,),
        in_specs=[row, row, vec, vec],
        out_specs=pl.BlockSpec((tr, 2 * W), lambda i: (i, 0)),
        out_shape=jax.ShapeDtypeStruct((T, 2 * W), BF16),
        compiler_params=_params("parallel"),
    )(ya, yb, ga, gb)


def out_proj_bwd(dx, w_out, ya, yb, ga, gb, name):
    T, D = dx.shape
    M = w_out.shape[0]
    W = M // 2
    tm = min(256, T)

    def body(dx_ref, w_ref, a_ref, b_ref, ga_ref, gb_ref, da_ref, db_ref, dga_ref, dgb_ref):
        dy = _dot_nt(dx_ref[...].astype(BF16), w_ref[...])

        @pl.when(pl.program_id(0) == 0)
        def _():
            dga_ref[...] = jnp.zeros_like(dga_ref)
            dgb_ref[...] = jnp.zeros_like(dgb_ref)

        d_a, dg_a = _rms_bwd(a_ref[...], ga_ref[...], dy[:, :W])
        d_b, dg_b = _rms_bwd(b_ref[...], gb_ref[...], dy[:, W:])
        da_ref[...] = d_a
        db_ref[...] = d_b
        dga_ref[...] += dg_a
        dgb_ref[...] += dg_b

    row = pl.BlockSpec((tm, W), lambda i: (i, 0))
    vec = pl.BlockSpec((1, W), lambda i: (0, 0))
    return pl.pallas_call(
        body, name=name, grid=(T // tm,),
        in_specs=[pl.BlockSpec((tm, D), lambda i: (i, 0)), pl.BlockSpec((M, D), lambda i: (0, 0)),
                  row, row, vec, vec],
        out_specs=[row, row, vec, vec],
        out_shape=[jax.ShapeDtypeStruct((T, W), F32), jax.ShapeDtypeStruct((T, W), F32),
                   jax.ShapeDtypeStruct((1, W), F32), jax.ShapeDtypeStruct((1, W), F32)],
        compiler_params=_params("arbitrary"),
    )(dx, w_out, ya, yb, ga, gb)


def final_loss(x, gamma, target, name):
    T, D = x.shape
    tr = min(256, T)

    def body(x_ref, g_ref, t_ref, dx_ref, dg_ref, loss_ref):
        xv = x_ref[...]
        gam = g_ref[...]
        r = lax.rsqrt(jnp.mean(xv * xv, axis=-1, keepdims=True) + EPS)
        err = (xv * r) * gam - t_ref[...]
        part = 0.5 * jnp.sum(jnp.mean(err * err, axis=-1, keepdims=True), axis=0, keepdims=True)
        dx, dgamma = _rms_bwd(xv, gam, err / D)
        dx_ref[...] = dx

        @pl.when(pl.program_id(0) == 0)
        def _():
            dg_ref[...] = jnp.zeros_like(dg_ref)
            loss_ref[...] = jnp.zeros_like(loss_ref)

        dg_ref[...] += dgamma
        loss_ref[...] += jnp.broadcast_to(part, loss_ref.shape)

    row = pl.BlockSpec((tr, D), lambda i: (i, 0))
    vec = pl.BlockSpec((1, D), lambda i: (0, 0))
    return pl.pallas_call(
        body, name=name, grid=(T // tr,),
        in_specs=[row, vec, row],
        out_specs=[row, vec, pl.BlockSpec((1, LANES), lambda i: (0, 0))],
        out_shape=[jax.ShapeDtypeStruct((T, D), F32), jax.ShapeDtypeStruct((1, D), F32),
                   jax.ShapeDtypeStruct((1, LANES), F32)],
        compiler_params=_params("arbitrary"),
    )(x, gamma, target)


def _lane_select(src_half, dst_halves):
    r = lax.broadcasted_iota(jnp.int32, (LANES, LANES), 0)
    c = lax.broadcasted_iota(jnp.int32, (LANES, LANES), 1)
    hit = ((r >> 6) == src_half) & ((r & (HEAD_DIM - 1)) == (c & (HEAD_DIM - 1)))
    ok = functools.reduce(jnp.logical_or, [(c >> 6) == d for d in dst_halves])
    return jnp.where(hit & ok, 1.0, 0.0).astype(BF16)


def _swa_slope(kvp, local_head):
    lo = 2.0 ** (-8.0 * (local_head + 1) / N_SWA_HEADS)
    hi = 2.0 ** (-8.0 * (8 + local_head + 1) / N_SWA_HEADS)
    return jnp.where(kvp == 0, jnp.float32(lo), jnp.float32(hi))


def _swa_probs(qblk, kwin, n, start, slope, sink):
    z = _dot_nt(qblk, kwin) * ATT_SCALE
    t = n * SWA_BLOCK + lax.broadcasted_iota(jnp.int32, z.shape, 0)
    s = start + lax.broadcasted_iota(jnp.int32, z.shape, 1)
    dist = t - s
    valid = (dist >= 0) & (dist < WINDOW)
    z = jnp.where(valid, z - slope * dist.astype(F32), NEG_BIG)
    m = jnp.maximum(jnp.max(z, axis=-1, keepdims=True), sink)
    e = jnp.exp(z - m)
    den = jnp.sum(e, axis=-1, keepdims=True) + jnp.exp(sink - m)
    return e / den, m, den


def swa_fwd(proj, sinks, b_loc, seq, name, riders=()):
    T = proj.shape[0]
    nblk = seq // SWA_BLOCK

    def body(sink_ref, q_ref, k_ref, v_ref, o_ref, qs, vv):
        kvp = pl.program_id(1)
        lane = lax.broadcasted_iota(jnp.int32, (SWA_BLOCK, LANES), 1)
        for kvh in range(2):
            both = _lane_select(kvh, (0, 1))
            vv[...] = _dot(v_ref[...], both).astype(BF16)
            for qp in (2 * kvh, 2 * kvh + 1):
                cols = slice(LANES * qp, LANES * (qp + 1))
                for a in range(2):
                    qs[a] = _dot(q_ref[:, cols], _lane_select(a, (kvh,))).astype(BF16)
                slopes = [_swa_slope(kvp, 2 * qp + a) for a in range(2)]
                snk = [sink_ref[0, 8 * kvp + 2 * qp + a] for a in range(2)]

                def blk(n, carry):
                    start = pl.multiple_of(jnp.maximum(n - 1, 0) * SWA_BLOCK, SWA_BLOCK)
                    rows = pl.ds(pl.multiple_of(n * SWA_BLOCK, SWA_BLOCK), SWA_BLOCK)
                    win = pl.ds(start, 2 * SWA_BLOCK)
                    outs = []
                    for a in range(2):
                        p, _, _ = _swa_probs(qs[a, rows, :], k_ref[win, :], n, start, slopes[a], snk[a])
                        outs.append(_dot(p.astype(BF16), vv[win, :]))
                    o_ref[rows, cols] = jnp.where(lane < HEAD_DIM, outs[0], outs[1])
                    return carry

                lax.fori_loop(0, nblk, blk, 0)

    return _call(
        body, name=name, grid=(b_loc, 2),
        in_specs=[pl.BlockSpec(memory_space=pltpu.SMEM),
                  pl.BlockSpec((seq, 4 * LANES), lambda b, p: (b, p)),
                  pl.BlockSpec((seq, LANES), lambda b, p: (b, 8 + p)),
                  pl.BlockSpec((seq, LANES), lambda b, p: (b, 10 + p))],
        out_specs=[pl.BlockSpec((seq, 4 * LANES), lambda b, p: (b, p))],
        out_shape=[jax.ShapeDtypeStruct((T, N_SWA_HEADS * HEAD_DIM), F32)],
        scratch_shapes=[pltpu.VMEM((2, seq, LANES), BF16), pltpu.VMEM((seq, LANES), BF16)],
        sem=("parallel", "parallel"), args=(sinks, proj, proj, proj), riders=riders)


def swa_bwd(proj, sinks, dya, b_loc, seq, name, riders=()):
    T = proj.shape[0]
    nblk = seq // SWA_BLOCK

    def body(sink_ref, q_ref, k_ref, v_ref, do_ref, dq_ref, dk_ref, dv_ref, ds_ref, qs, dos, kk, dk_acc, dv_acc):
        kvp = pl.program_id(0)
        b = pl.program_id(1)
        lane = lax.broadcasted_iota(jnp.int32, (SWA_BLOCK, LANES), 1)

        @pl.when(b == 0)
        def _():
            ds_ref[...] = jnp.zeros_like(ds_ref)

        dk_acc[...] = jnp.zeros_like(dk_acc)
        dv_acc[...] = jnp.zeros_like(dv_acc)
        for kvh in range(2):
            kk[...] = _dot(k_ref[...], _lane_select(kvh, (0, 1))).astype(BF16)
            for qp in (2 * kvh, 2 * kvh + 1):
                cols = slice(LANES * qp, LANES * (qp + 1))
                for a in range(2):
                    move = _lane_select(a, (kvh,))
                    qs[a] = _dot(q_ref[:, cols], move).astype(BF16)
                    dos[a] = _dot(do_ref[:, cols].astype(BF16), move).astype(BF16)
                slopes = [_swa_slope(kvp, 2 * qp + a) for a in range(2)]
                snk = [sink_ref[0, 8 * kvp + 2 * qp + a] for a in range(2)]

                def blk(n, carry):
                    start = pl.multiple_of(jnp.maximum(n - 1, 0) * SWA_BLOCK, SWA_BLOCK)
                    rows = pl.ds(pl.multiple_of(n * SWA_BLOCK, SWA_BLOCK), SWA_BLOCK)
                    win = pl.ds(start, 2 * SWA_BLOCK)
                    dqs = []
                    for a in range(2):
                        qblk = qs[a, rows, :]
                        doblk = dos[a, rows, :]
                        p, m, den = _swa_probs(qblk, k_ref[win, :], n, start, slopes[a], snk[a])
                        dp = _dot_nt(doblk, v_ref[win, :])
                        delta = jnp.sum(p * dp, axis=-1, keepdims=True)
                        p_sink = jnp.exp(snk[a] - m) / den
                        head = 2 * qp + a
                        ds_ref[head:head + 1, :] -= jnp.broadcast_to(
                            jnp.sum(p_sink * delta, axis=0, keepdims=True), (1, LANES))
                        dsc = ((p * (dp - delta)) * ATT_SCALE).astype(BF16)
                        dv_acc[win, :] += _dot_tn(p.astype(BF16), doblk)
                        dk_acc[win, :] += _dot_tn(dsc, qblk)
                        dqs.append(_dot(dsc, kk[win, :]))
                    dq_ref[rows, cols] = jnp.where(lane < HEAD_DIM, dqs[0], dqs[1]).astype(BF16)
                    return carry

                lax.fori_loop(0, nblk, blk, 0)
        dk_ref[...] = dk_acc[...].astype(BF16)
        dv_ref[...] = dv_acc[...].astype(BF16)

    return _call(
        body, name=name, grid=(2, b_loc),
        in_specs=[pl.BlockSpec(memory_space=pltpu.SMEM),
                  pl.BlockSpec((seq, 4 * LANES), lambda p, b: (b, p)),
                  pl.BlockSpec((seq, LANES), lambda p, b: (b, 8 + p)),
                  pl.BlockSpec((seq, LANES), lambda p, b: (b, 10 + p)),
                  pl.BlockSpec((seq, 4 * LANES), lambda p, b: (b, p))],
        out_specs=[pl.BlockSpec((seq, 4 * LANES), lambda p, b: (b, p)),
                   pl.BlockSpec((seq, LANES), lambda p, b: (b, p)),
                   pl.BlockSpec((seq, LANES), lambda p, b: (b, p)),
                   pl.BlockSpec((8, LANES), lambda p, b: (p, 0))],
        out_shape=[jax.ShapeDtypeStruct((T, N_SWA_HEADS * HEAD_DIM), BF16),
                   jax.ShapeDtypeStruct((T, N_SWA_KV * HEAD_DIM), BF16),
                   jax.ShapeDtypeStruct((T, N_SWA_KV * HEAD_DIM), BF16),
                   jax.ShapeDtypeStruct((N_SWA_HEADS, LANES), F32)],
        scratch_shapes=[pltpu.VMEM((2, seq, LANES), BF16), pltpu.VMEM((2, seq, LANES), BF16),
                        pltpu.VMEM((seq, LANES), BF16),
                        pltpu.VMEM((seq, LANES), F32), pltpu.VMEM((seq, LANES), F32)],
        sem=("arbitrary", "arbitrary"), args=(sinks, proj, proj, proj, dya), riders=riders)


def _softplus(z):
    return jnp.maximum(z, 0.0) + jnp.log(1.0 + jnp.exp(-jnp.abs(z)))


def _tri(n, rel):
    r = lax.broadcasted_iota(jnp.int32, (n, n), 0)
    c = lax.broadcasted_iota(jnp.int32, (n, n), 1)
    return jnp.where(rel(r, c), 1.0, 0.0).astype(BF16)


def sb_fwd(proj, b_loc, seq, name, riders=()):
    T = proj.shape[0]
    bq = min(SB_BLOCK, seq)
    nq = seq // bq
    npair = N_SB_HEADS // 2

    def body(q_ref, k_ref, v_ref, o_ref, lt_ref):
        i = pl.program_id(2)
        lane = lax.broadcasted_iota(jnp.int32, (bq, LANES), 1)
        after = _tri(bq, lambda r, c: r > c)
        row = lax.broadcasted_iota(jnp.int32, (bq, bq), 0)
        col = lax.broadcasted_iota(jnp.int32, (bq, bq), 1)
        qv = q_ref[...] * jnp.asarray(ATT_SCALE, BF16)
        blank = jnp.zeros((bq, LANES), BF16)
        q2 = jnp.concatenate([jnp.where((lane >> 6) == h, qv, blank) for h in range(2)], axis=0)


        def tile(j, state, mask):
            rows = pl.ds(pl.multiple_of(j * bq, bq), bq)
            kj, vj = k_ref[rows, :], v_ref[rows, :]
            zz = _dot_nt(q2, kj)
            log_beta, totals, pieces = [], [], []
            for h in range(2):
                z = zz[h * bq:(h + 1) * bq]
                sp = _softplus(z)
                lg = -sp if mask is None else jnp.where(mask, -sp, 0.0)
                hi = lg.astype(BF16)
                pieces += [hi, (lg - hi.astype(F32)).astype(BF16)]
                log_beta.append(z - sp if mask is None else jnp.where(mask, z - sp, NEG_BIG))
                totals.append(jnp.sum(lg, axis=-1, keepdims=True))
            cc = _dot(jnp.concatenate(pieces, axis=0), after)
            probs = []
            for h in range(2):
                inside = cc[2 * h * bq:(2 * h + 1) * bq] + cc[(2 * h + 1) * bq:(2 * h + 2) * bq]
                probs.append(jnp.exp(log_beta[h] + (inside + state[2 * h])).astype(BF16))
            pv = _dot(jnp.concatenate(probs, axis=0), vj)
            return (state[0] + totals[0], state[1] + pv[:bq], state[2] + totals[1], state[3] + pv[bq:])

        zero = (jnp.zeros((bq, 1), F32), jnp.zeros((bq, LANES), F32))
        state = tile(i, zero + zero, col < row)
        state = lax.fori_loop(0, i, lambda jj, st: tile(i - 1 - jj, st, None), state)
        o_ref[...] = jnp.where(lane < HEAD_DIM, state[1], state[3])
        lt_ref[...] = jnp.where(lane < HEAD_DIM, state[0], state[2])

    return _call(
        body, name=name, grid=(b_loc, npair, nq),
        in_specs=[pl.BlockSpec((bq, LANES), lambda b, p, i: (b * nq + i, 12 + p)),
                  pl.BlockSpec((seq, LANES), lambda b, p, i: (b, 20 + p)),
                  pl.BlockSpec((seq, LANES), lambda b, p, i: (b, 28 + p))],
        out_specs=[pl.BlockSpec((bq, LANES), lambda b, p, i: (b * nq + i, p)),
                   pl.BlockSpec((None, bq, LANES), lambda b, p, i: (b * npair + p, i, 0))],
        out_shape=[jax.ShapeDtypeStruct((T, N_SB_HEADS * HEAD_DIM), F32),
                   jax.ShapeDtypeStruct((b_loc * npair, seq, LANES), F32)],
        sem=("parallel", "parallel", "parallel"), args=(proj, proj, proj), riders=riders)


def sb_bwd(proj, dyb, ltot, b_loc, seq, name, riders=()):
    T = proj.shape[0]
    bq = min(SB_BLOCK, seq)
    nq = seq // bq
    npair = N_SB_HEADS // 2

    def body(q_ref, k_ref, v_ref, do_ref, lt_ref, dq_ref, dk_ref, dv_ref, dk_acc, dv_acc):
        lane = lax.broadcasted_iota(jnp.int32, (bq, LANES), 1)
        upto = _tri(bq, lambda r, c: r <= c)
        before = _tri(bq, lambda r, c: r < c)
        row = lax.broadcasted_iota(jnp.int32, (bq, bq), 0)
        col = lax.broadcasted_iota(jnp.int32, (bq, bq), 1)
        dk_acc[...] = jnp.zeros_like(dk_acc)
        dv_acc[...] = jnp.zeros_like(dv_acc)
        scale = jnp.asarray(ATT_SCALE, BF16)
        heads = [(lane >> 6) == h for h in range(2)]
        blank = jnp.zeros((bq, LANES), BF16)

        def qblock(i, carry):
            qrows = pl.ds(pl.multiple_of(i * bq, bq), bq)
            qv = q_ref[qrows, :] * scale
            dov = do_ref[qrows, :].astype(BF16)
            ltv = lt_ref[qrows, :]
            q2 = jnp.concatenate([jnp.where(m, qv, blank) for m in heads], axis=0)
            do2 = jnp.concatenate([jnp.where(m, dov, blank) for m in heads], axis=0)
            totals = [jnp.sum(jnp.where(lane == h * HEAD_DIM, ltv, 0.0), axis=-1, keepdims=True) for h in range(2)]

            def split2(xs):
                out = []
                for x in xs:
                    hi = x.astype(BF16)
                    out += [hi, (x - hi.astype(F32)).astype(BF16)]
                return jnp.concatenate(out, axis=0)

            def join2(cc, h):
                return cc[2 * h * bq:(2 * h + 1) * bq] + cc[(2 * h + 1) * bq:(2 * h + 2) * bq]

            def tile(j, state, mask):
                krows = pl.ds(pl.multiple_of(j * bq, bq), bq)
                kj, vj = k_ref[krows, :], v_ref[krows, :]
                kv = kj * scale
                k2 = jnp.concatenate([jnp.where(m, kv, blank) for m in heads], axis=0)
                zz = _dot_nt(q2, kj)
                dd = _dot_nt(do2, vj)
                lgs, log_betas = [], []
                for h in range(2):
                    z = zz[h * bq:(h + 1) * bq]
                    sp = _softplus(z)
                    lgs.append(-sp if mask is None else jnp.where(mask, -sp, 0.0))
                    log_betas.append(z - sp)
                cl = _dot(split2(lgs), upto)
                probs, gs = [], []
                for h in range(2):
                    a = jnp.exp(log_betas[h] + (totals[h] - (join2(cl, h) + state[2 * h])))
                    if mask is not None:
                        a = jnp.where(mask, a, 0.0)
                    probs.append(a.astype(BF16))
                    gs.append(a * dd[h * bq:(h + 1) * bq])
                cg = _dot(split2(gs), before)
                dzs = []
                for h in range(2):
                    beta = jnp.exp(log_betas[h])
                    dz = gs[h] * (1.0 - beta) - beta * (join2(cg, h) + state[2 * h + 1])
                    if mask is not None:
                        dz = jnp.where(mask, dz, 0.0)
                    dzs.append(dz.astype(BF16))
                dv_acc[krows, :] += _dot_tn(jnp.concatenate(probs, axis=0), do2)
                dk_acc[krows, :] += _dot_tn(jnp.concatenate(dzs, axis=0), q2)
                dq = state[4] + _dot(jnp.concatenate(dzs, axis=1), k2)
                new = []
                for h in range(2):
                    new += [state[2 * h] + jnp.sum(lgs[h], axis=-1, keepdims=True),
                            state[2 * h + 1] + jnp.sum(gs[h], axis=-1, keepdims=True)]
                return (*new, dq)

            zero = jnp.zeros((bq, 1), F32)
            state = lax.fori_loop(0, i, lambda j, st: tile(j, st, None),
                                  (zero, zero, zero, zero, jnp.zeros((bq, LANES), F32)))
            state = tile(i, state, col < row)
            dq_ref[qrows, :] = state[4].astype(BF16)
            return carry

        lax.fori_loop(0, nq, qblock, 0)
        dk_ref[...] = dk_acc[...].astype(BF16)
        dv_ref[...] = dv_acc[...].astype(BF16)

    pair = pl.BlockSpec((seq, LANES), lambda b, p: (b, p))
    shape = jax.ShapeDtypeStruct((T, N_SB_HEADS * HEAD_DIM), BF16)
    return _call(
        body, name=name, grid=(b_loc, npair),
        in_specs=[pl.BlockSpec((seq, LANES), lambda b, p: (b, 12 + p)),
                  pl.BlockSpec((seq, LANES), lambda b, p: (b, 20 + p)),
                  pl.BlockSpec((seq, LANES), lambda b, p: (b, 28 + p)),
                  pair,
                  pl.BlockSpec((None, seq, LANES), lambda b, p: (b * npair + p, 0, 0))],
        out_specs=[pair, pair, pair],
        out_shape=[shape, shape, shape],
        scratch_shapes=[pltpu.VMEM((seq, LANES), F32)] * 2,
        sem=("parallel", "parallel"), args=(proj, proj, proj, dyb, ltot), riders=riders)


def allreduce_small(v, name):
    R = v.shape[0]
    ndev = 8

    def body(v_ref, o_ref, land, send_sems, recv_sems):
        x, y, c = _place()
        me = 4 * x + 2 * y + c
        land[me] = v_ref[...]
        started = []
        for f in range(1, ndev):
            fx, fy, fc = (f >> 2) & 1, (f >> 1) & 1, f & 1
            peer = ((1 - x) if fx else x, (1 - y) if fy else y, (1 - c) if fc else c)
            cp = pltpu.make_async_remote_copy(
                src_ref=v_ref, dst_ref=land.at[me], send_sem=send_sems.at[f - 1], recv_sem=recv_sems.at[f - 1],
                device_id=peer, device_id_type=MESH)
            cp.start()
            started.append(cp)
        for cp in started:
            cp.wait()
        total = land[0]
        for d in range(1, ndev):
            total = total + land[d]
        o_ref[...] = total

    return pl.pallas_call(
        body, name=name,
        in_specs=[VMEM_WHOLE], out_specs=VMEM_WHOLE,
        out_shape=jax.ShapeDtypeStruct(v.shape, F32),
        scratch_shapes=[pltpu.VMEM((ndev, R, LANES), F32),
                        pltpu.SemaphoreType.DMA((ndev - 1,)), pltpu.SemaphoreType.DMA((ndev - 1,))],
    )(v)


def sum_cores(grads, theirs, name):
    _, _, r, c = grads.shape
    tr, tc = _tiles(r, c, 1 << 18)
    core = lax.axis_index("c").astype(jnp.int32).reshape(1)

    def body(core_ref, g_ref, t_ref, o_ref):
        o_ref[...] = (g_ref[...].astype(F32) + t_ref[...].astype(F32)).astype(BF16)

    blk = pl.BlockSpec((N_CHIPS, tr, tc), lambda i, j, core_ref: (0, i, j))
    return pl.pallas_call(
        body, name=name,
        grid_spec=pltpu.PrefetchScalarGridSpec(
            num_scalar_prefetch=1, grid=(r // tr, c // tc),
            in_specs=[pl.BlockSpec((None, N_CHIPS, tr, tc), lambda i, j, core_ref: (core_ref[0], 0, i, j)), blk],
            out_specs=blk),
        out_shape=jax.ShapeDtypeStruct((N_CHIPS, r, c), BF16),
        compiler_params=_params("parallel", "parallel"),
    )(core, grads, theirs)


def sum_chips(halves, landed, name):
    _, r, c = landed.shape
    tr, tc = _tiles(r, c, 1 << 18)
    chip = (2 * lax.axis_index("x") + lax.axis_index("y")).astype(jnp.int32).reshape(1)

    def body(chip_ref, own_ref, p_ref, o_ref):
        total = own_ref[...].astype(F32)
        for j in range(N_CHIPS - 1):
            total = total + p_ref[j].astype(F32)
        o_ref[...] = total

    return pl.pallas_call(
        body, name=name,
        grid_spec=pltpu.PrefetchScalarGridSpec(
            num_scalar_prefetch=1, grid=(r // tr, c // tc),
            in_specs=[pl.BlockSpec((None, tr, tc), lambda i, j, chip_ref: (chip_ref[0], i, j)),
                      pl.BlockSpec((N_CHIPS - 1, tr, tc), lambda i, j, chip_ref: (0, i, j))],
            out_specs=pl.BlockSpec((tr, tc), lambda i, j, chip_ref: (i, j))),
        out_shape=jax.ShapeDtypeStruct((r, c), F32),
        compiler_params=_params("parallel", "parallel"),
    )(chip, halves, landed)


def _adamw_math(w, g, m, v):
    m = ADAM_B1 * m + (1.0 - ADAM_B1) * g
    v = ADAM_B2 * v + (1.0 - ADAM_B2) * (g * g)
    m_hat = m / (1.0 - ADAM_B1 ** ADAM_STEP)
    v_hat = v / (1.0 - ADAM_B2 ** ADAM_STEP)
    delta = -ADAM_LR * (m_hat / (jnp.sqrt(v_hat) + ADAM_EPS) + ADAM_WD * w)
    return delta, m, v


def adamw(w, m, v, g, name):
    r, c = w.shape
    tr, tc = _tiles(r, c, 1 << 18, multiple=8)

    def body(w_ref, m_ref, v_ref, gin_ref, g_ref, d_ref, nm_ref, nv_ref):
        g = gin_ref[...]
        delta, nm, nv = _adamw_math(w_ref[...], g, m_ref[...], v_ref[...])
        g_ref[...] = g
        d_ref[...] = delta
        nm_ref[...] = nm
        nv_ref[...] = nv

    blk = pl.BlockSpec((tr, tc), lambda i, j: (i, j))
    shape = jax.ShapeDtypeStruct((r, c), F32)
    return pl.pallas_call(
        body, name=name, grid=(r // tr, c // tc),
        in_specs=[blk] * 4, out_specs=[blk] * 4, out_shape=[shape] * 4,
        compiler_params=_params("parallel", "parallel"),
    )(w, m, v, g)


def adamw_halves(w, m, v, mine, theirs, name, col_halves=False):
    r, c = w.shape
    hr, hc = (r, c // 2) if col_halves else (r // 2, c)
    tr, tc = _tiles(hr, hc, 1 << 18, multiple=8)
    g_cols = mine.shape[-1]
    padded = g_cols != hc
    if padded and tc != hc:
        raise ValueError("a gradient with padded columns is read in full-width row blocks")
    per_half = hc // tc if col_halves else hr // tr
    core = lax.axis_index("c").astype(jnp.int32).reshape(1)

    def body(core_ref, w_ref, m_ref, v_ref, mine_ref, theirs_ref, g_ref, d_ref, nm_ref, nv_ref):
        half = pl.program_id(1 if col_halves else 0) // per_half
        cols = slice(0, hc) if padded else slice(None)
        g = jnp.where(half == core_ref[0], mine_ref[:, cols], theirs_ref[:, cols])
        delta, nm, nv = _adamw_math(w_ref[...], g, m_ref[...], v_ref[...])
        g_ref[...] = g
        d_ref[...] = delta
        nm_ref[...] = nm
        nv_ref[...] = nv

    blk = pl.BlockSpec((tr, tc), lambda i, j, core_ref: (i, j))
    if col_halves:
        g_spec = pl.BlockSpec((tr, tc), lambda i, j, core_ref: (i, j % per_half))
    else:
        g_spec = pl.BlockSpec((tr, g_cols if padded else tc), lambda i, j, core_ref: (i % per_half, j))
    shape = jax.ShapeDtypeStruct((r, c), F32)
    return pl.pallas_call(
        body, name=name,
        grid_spec=pltpu.PrefetchScalarGridSpec(
            num_scalar_prefetch=1, grid=(r // tr, c // tc),
            in_specs=[blk, blk, blk, g_spec, g_spec], out_specs=[blk] * 4),
        out_shape=[shape] * 4,
        compiler_params=_params("parallel", "parallel"),
    )(core, w, m, v, mine, theirs)


MATRICES = ("ffn1_w_gate", "ffn1_w_up", "ffn1_w_down", "w_in", "w_out", "ffn2_w_gate", "ffn2_w_up", "ffn2_w_down")
VECTORS = ("ffn1_norm", "mix_norm", "swa_sinks", "swa_out_norm", "sb_out_norm", "ffn2_norm", "final_norm")
WEIGHTS = ("ffn1_norm", "ffn1_w_gate", "ffn1_w_up", "ffn1_w_down", "mix_norm", "w_in", "swa_sinks", "swa_out_norm",
           "sb_out_norm", "w_out", "ffn2_norm", "ffn2_w_gate", "ffn2_w_up", "ffn2_w_down", "final_norm")


def _pack_rows(vec):
    flat = vec.reshape(-1).astype(F32)
    rows = -(-flat.shape[0] // LANES)
    rows8 = -(-rows // 8) * 8
    flat = jnp.pad(flat, (0, rows8 * LANES - flat.shape[0]))
    return flat.reshape(rows8, LANES)


class GradPipe:
    def __init__(self, name, grads):
        self.name, self.step, self.value = name, 0, grads

    def rider(self):
        make = (half_swap_rider, scatter_rider, final_swap_rider)[self.step]
        return make([self.value])

    def landed(self, outs):
        if self.step == 0:
            self.value = sum_cores(self.value, outs[0], "sum_cores_" + self.name)
        elif self.step == 1:
            self.value = sum_chips(self.value, outs[0], "sum_chips_" + self.name)
        else:
            self.value = (self.value, outs[0])
        self.step += 1


def _hosted(fn, *args, advance=(), **kw):
    outs, routs = fn(*args, riders=[p.rider() for p in advance], **kw)
    for p, ro in zip(advance, routs):
        p.landed(ro)
    return outs


def kernel(x, ffn1_norm, ffn1_w_gate, ffn1_w_up, ffn1_w_down, mix_norm, w_in, swa_sinks, swa_out_norm, sb_out_norm, w_out, ffn2_norm, ffn2_w_gate, ffn2_w_up, ffn2_w_down, final_norm, loss_target, m_ffn1_norm, m_ffn1_w_gate, m_ffn1_w_up, m_ffn1_w_down, m_mix_norm, m_w_in, m_swa_sinks, m_swa_out_norm, m_sb_out_norm, m_w_out, m_ffn2_norm, m_ffn2_w_gate, m_ffn2_w_up, m_ffn2_w_down, m_final_norm, v_ffn1_norm, v_ffn1_w_gate, v_ffn1_w_up, v_ffn1_w_down, v_mix_norm, v_w_in, v_swa_sinks, v_swa_out_norm, v_sb_out_norm, v_w_out, v_ffn2_norm, v_ffn2_w_gate, v_ffn2_w_up, v_ffn2_w_down, v_final_norm):
    given = dict(locals())
    b_loc, seq, D = x.shape
    T = b_loc * seq
    x0 = x.reshape(T, D)
    target = loss_target.reshape(T, D)
    final_g = final_norm.reshape(1, D)

    gathered = {}

    def gather(*names):
        shards = []
        for nm in names:
            w = given[nm][0].astype(BF16)
            hidden_axis = 0 if nm.endswith("w_down") else 1 if nm.startswith("ffn") else None
            if hidden_axis is not None:
                pad = -w.shape[hidden_axis] % LANES
                w = jnp.pad(w, [(0, pad) if a == hidden_axis else (0, 0) for a in range(2)])
            shards.append(w.reshape(2, w.shape[0] // 2, w.shape[1]))
        return names, gather_rider(shards)

    def hosting(fn, *args, fetch):
        names, rider = fetch
        outs, landed = fn(*args, riders=[rider])
        for nm, o in zip(names, landed[0]):
            gathered[nm] = o.reshape(N_CHIPS, 2 * o.shape[2], o.shape[3])
        return outs

    hosting(lambda riders: ([], exchange_now(riders, "gather_ffn1_gate")), fetch=gather("ffn1_w_gate"))

    h1 = rmsnorm_cast(x0, ffn1_norm, "ffn1_norm_fwd")
    (g1,) = hosting(ffn_gate, h1, gathered["ffn1_w_gate"], "ffn1_gate", fetch=gather("ffn1_w_up"))
    u1, a1 = hosting(ffn_up_act, h1, gathered["ffn1_w_up"], g1, "ffn1_up_act", fetch=gather("ffn1_w_down"))
    (x1,) = hosting(ffn_down, a1, gathered["ffn1_w_down"], x0, "ffn1_down", fetch=gather("w_in"))
    h2 = rmsnorm_cast(x1, mix_norm, "mix_norm_fwd")
    (proj,) = hosting(in_proj, h2, gathered["w_in"], "in_proj", fetch=gather("w_out"))
    w_out_all = gathered["w_out"].reshape(-1, D)
    (ya,) = hosting(swa_fwd, proj, swa_sinks, b_loc, seq, "swa_fwd", fetch=gather("ffn2_w_gate"))
    yb, ltot = hosting(sb_fwd, proj, b_loc, seq, "sb_fwd", fetch=gather("ffn2_w_up", "ffn2_w_down"))
    y = out_norm(ya, yb, swa_out_norm, sb_out_norm, "out_norm_fwd")
    x2 = out_proj(y, w_out_all, x1, "out_proj")
    h3 = rmsnorm_cast(x2, ffn2_norm, "ffn2_norm_fwd")
    (g2, u2, a2), _ = ffn_up(h3, gathered["ffn2_w_gate"], gathered["ffn2_w_up"], "ffn2_up")
    (x3,), _ = ffn_down(a2, gathered["ffn2_w_down"], x2, "ffn2_down")
    dx3, d_final, loss_part = final_loss(x3, final_g, target, "final_loss")

    Fs = gathered["ffn1_w_gate"].shape[2]
    Ps = gathered["w_in"].shape[2]
    Os = w_out_all.shape[0] // N_CHIPS
    pipes = {}

    def down_grad(tag, act, dx, advance):
        (grads,) = _hosted(wgrad, act, dx, tag + "_dwd", groups=N_CHIPS, m=Fs, n=D // 2,
                           a_block=lambda tk: (None, tk, Fs), a_map=lambda s, j, k: (s, k, 0),
                           b_block=lambda tk: (tk, D // 2), b_map=lambda s, j, k: (k, j), b_scale=0.5,
                           advance=advance)
        return grads

    def up_grad(name, h, d_act, advance):
        (grads,) = _hosted(wgrad, h, d_act, name, groups=N_CHIPS, m=D // 2, n=Fs,
                           a_block=lambda tk: (tk, D // 2), a_map=lambda s, j, k: (k, j),
                           b_block=lambda tk: (None, tk, Fs), b_map=lambda s, j, k: (s, k, 0), advance=advance)
        return grads

    (dg2, du2), _ = ffn_bwd_act(dx3, gathered["ffn2_w_down"], g2, u2, "ffn2_bwd_act")
    pipes["ffn2_w_down"] = p_d2 = GradPipe("ffn2_w_down", down_grad("ffn2", a2, dx3, []))
    pipes["ffn2_w_gate"] = p_g2 = GradPipe("ffn2_w_gate", up_grad("ffn2_dwg", h3, dg2, [p_d2]))
    pipes["ffn2_w_up"] = p_u2 = GradPipe("ffn2_w_up", up_grad("ffn2_dwu", h3, du2, [p_d2, p_g2]))
    (dh3,) = _hosted(ffn_bwd_dh, dg2, du2, gathered["ffn2_w_gate"], gathered["ffn2_w_up"], "ffn2_bwd_dh",
                     advance=[p_d2, p_g2, p_u2])
    (dx2, d_ffn2_norm), _ = rmsnorm_bwd_add(x2, ffn2_norm, dh3, dx3, "ffn2_norm_bwd")

    (d_w_out,) = _hosted(wgrad, y, dx2, "dw_out", groups=N_CHIPS, m=Os // 2, n=D,
                         a_block=lambda tk: (tk, Os // 2), a_map=lambda s, j, k: (k, 2 * s + j),
                         b_block=lambda tk: (tk, D), b_map=lambda s, j, k: (k, 0), advance=[p_g2])
    pipes["w_out"] = p_out = GradPipe("w_out", d_w_out)
    dya, dyb, d_swa_norm, d_sb_norm = out_proj_bwd(dx2, w_out_all, ya, yb, swa_out_norm, sb_out_norm, "out_proj_bwd")
    dqa, dka, dva, d_sinks = _hosted(swa_bwd, proj, swa_sinks, dya, b_loc, seq, "swa_bwd", advance=[p_u2, p_out])
    dqb, dkb, dvb = _hosted(sb_bwd, proj, dyb, ltot, b_loc, seq, "sb_bwd", advance=[p_u2, p_out])
    dproj = jnp.concatenate([dqa, dka, dva, dqb, dkb, dvb], axis=1)
    (d_w_in,) = _hosted(wgrad, h2, dproj, "dw_in", groups=N_CHIPS, m=D // 2, n=Ps,
                        a_block=lambda tk: (tk, D // 2), a_map=lambda s, j, k: (k, j),
                        b_block=lambda tk: (tk, Ps), b_map=lambda s, j, k: (k, s), advance=[p_out])
    pipes["w_in"] = p_in = GradPipe("w_in", d_w_in)
    (dh2,) = _hosted(in_proj_bwd, dproj, gathered["w_in"], "in_proj_bwd", advance=[p_in])
    (dx1, d_mix_norm), _ = rmsnorm_bwd_add(x1, mix_norm, dh2, dx2, "mix_norm_bwd")

    pipes["ffn1_w_down"] = p_d1 = GradPipe("ffn1_w_down", down_grad("ffn1", a1, dx1, [p_in]))
    dg1, du1 = _hosted(ffn_bwd_act, dx1, gathered["ffn1_w_down"], g1, u1, "ffn1_bwd_act", advance=[p_in, p_d1])
    pipes["ffn1_w_gate"] = p_g1 = GradPipe("ffn1_w_gate", up_grad("ffn1_dwg", h1, dg1, [p_d1]))
    pipes["ffn1_w_up"] = p_u1 = GradPipe("ffn1_w_up", up_grad("ffn1_dwu", h1, du1, [p_d1, p_g1]))
    (dh1,) = _hosted(ffn_bwd_dh, dg1, du1, gathered["ffn1_w_gate"], gathered["ffn1_w_up"], "ffn1_bwd_dh_a",
                     part=(0, 2), advance=[p_g1, p_u1])
    (dh1,) = _hosted(ffn_bwd_dh, dg1, du1, gathered["ffn1_w_gate"], gathered["ffn1_w_up"], "ffn1_bwd_dh_b",
                     part=(1, 2), prev=dh1, advance=[p_g1, p_u1])
    (dx0, d_ffn1_norm), _ = rmsnorm_bwd_add(x0, ffn1_norm, dh1, dx1, "ffn1_norm_bwd")
    (last,) = exchange_now([p_u1.rider()], "swap_ffn1_w_up")
    p_u1.landed(last)

    vec_grads = dict(ffn1_norm=d_ffn1_norm, mix_norm=d_mix_norm, swa_sinks=d_sinks[:, 0], swa_out_norm=d_swa_norm,
                     sb_out_norm=d_sb_norm, ffn2_norm=d_ffn2_norm, final_norm=d_final)
    packed = [_pack_rows(vec_grads[nm]) for nm in VECTORS] + [_pack_rows(loss_part[0, :1])]
    offsets = [0]
    for p in packed:
        offsets.append(offsets[-1] + p.shape[0])
    reduced = allreduce_small(jnp.concatenate(packed, axis=0), "reduce_vectors")
    loss = reduced[offsets[len(VECTORS)], 0]

    out = {}
    for nm in MATRICES:
        shape = given[nm].shape
        mine, theirs = pipes[nm].value
        res = adamw_halves(given[nm][0], given["m_" + nm][0], given["v_" + nm][0], mine, theirs, "adamw_" + nm,
                           col_halves=nm.endswith("w_down"))
        out[nm] = [r.reshape(shape) for r in res]
    w_rows = jnp.concatenate([_pack_rows(given[nm]) for nm in VECTORS], axis=0)
    m_rows = jnp.concatenate([_pack_rows(given["m_" + nm]) for nm in VECTORS], axis=0)
    v_rows = jnp.concatenate([_pack_rows(given["v_" + nm]) for nm in VECTORS], axis=0)
    g_rows = reduced[:offsets[len(VECTORS)]]
    small = adamw(w_rows, m_rows, v_rows, g_rows, "adamw_vectors")
    for i, nm in enumerate(VECTORS):
        shape = given[nm].shape
        size = math.prod(shape)
        out[nm] = [r[offsets[i]:offsets[i + 1]].reshape(-1)[:size].reshape(shape) for r in small]

    grad_x = dx0.reshape(b_loc, seq, D)
    return (loss, grad_x, *[out[nm][0] for nm in WEIGHTS], *[out[nm][1] for nm in WEIGHTS],
            *[out[nm][2] for nm in WEIGHTS], *[out[nm][3] for nm in WEIGHTS])
```

```python
import functools
import math

import jax
import jax.numpy as jnp
from jax import lax
from jax.experimental import pallas as pl
from jax.experimental.pallas import tpu as pltpu

F32 = jnp.float32
BF16 = jnp.bfloat16
MESH = pl.DeviceIdType.MESH

EPS = 1e-6
HEAD_DIM = 64
LANES = 128
N_SWA_HEADS = 16
N_SWA_KV = 4
N_SB_HEADS = 16
WINDOW = 128
SWA_BLOCK = 128
SB_BLOCK = 256
N_CHIPS = 4
ATT_SCALE = HEAD_DIM ** -0.5

ADAM_LR = 0.001
ADAM_B1 = 0.9
ADAM_B2 = 0.999
ADAM_EPS = 1e-08
ADAM_WD = 0.01
ADAM_STEP = 10

VMEM_LIMIT = 56 * 1024 * 1024
NEG_BIG = -1e30

ANY = pl.BlockSpec(memory_space=pl.ANY)
VMEM_WHOLE = pl.BlockSpec(memory_space=pltpu.VMEM)


def _row_tile(rows, cap, multiple=16):
    best = None
    for t in range(multiple, min(rows, cap) + 1, multiple):
        if rows % t == 0:
            best = t
    if best is None:
        raise ValueError(f"no row tile for {rows} rows under {cap}")
    return best


def _params(*sem):
    return pltpu.CompilerParams(dimension_semantics=sem, vmem_limit_bytes=VMEM_LIMIT)


def _dot(a, b):
    return jnp.dot(a, b, preferred_element_type=F32)


def _dot_nt(a, b):
    return lax.dot_general(a, b, (((1,), (1,)), ((), ())), preferred_element_type=F32)


def _dot_tn(a, b):
    return lax.dot_general(a, b, (((0,), (0,)), ((), ())), preferred_element_type=F32)


def _sigmoid(x):
    return 1.0 / (1.0 + jnp.exp(-x))


def _split_dot(x, u):
    hi = x.astype(BF16)
    lo = (x - hi.astype(F32)).astype(BF16)
    return _dot(hi, u) + _dot(lo, u)


def _tiles(r, c, max_elems, multiple=16):
    col_opts = [c] + [t for t in (2048, 1024, 512, 256, 128) if t < c and c % t == 0]
    best = None
    for tc in col_opts:
        for tr in [r] + list(range(multiple, r, multiple)):
            if r % tr == 0 and tr * tc <= max_elems and (best is None or tr * tc > best[0] * best[1]):
                best = (tr, tc)
    if best is None:
        raise ValueError(f"no tile for [{r}, {c}] under {max_elems} elements")
    return best


class Rider:
    def __init__(self, arrays, out_shapes, sem_shapes, start, finish):
        self.arrays, self.out_shapes, self.sem_shapes = list(arrays), list(out_shapes), list(sem_shapes)
        self.start, self.finish = start, finish


def _place():
    return lax.axis_index("x"), lax.axis_index("y"), lax.axis_index("c")


def _other_chips(x, y):
    return [(1 - x, y), (x, 1 - y), (1 - x, 1 - y)]


def _split(ref):
    rows = ref.shape[-2]
    n = next(k for k in (4, 2, 1) if rows % (16 * k) == 0)
    step = rows // n
    lead = [()] if len(ref.shape) == 2 else [(s,) for s in range(ref.shape[0])]
    return [ref.at[(*idx, pl.ds(i * step, step))] for idx in lead for i in range(n)]


class _Copy:
    def __init__(self, src, dst, make):
        self.src, self.dst, self.make = src, dst, make

    def start(self):
        for s, d in zip(_split(self.src), _split(self.dst)):
            self.make(s, d).start()

    def wait(self):
        self.make(self.src, self.dst).wait()

    def wait_send(self):
        self.make(self.src, self.dst).wait_send()

    def wait_recv(self):
        self.make(self.src, self.dst).wait_recv()


def _remote(src, dst, send_sem, recv_sem, device):
    return _Copy(src, dst, lambda s, d: pltpu.make_async_remote_copy(
        src_ref=s, dst_ref=d, send_sem=send_sem, recv_sem=recv_sem, device_id=device, device_id_type=MESH))


def gather_rider(shards):
    n = len(shards)

    def copies(ins, outs, sems):
        ici_s, ici_r, d2d_s, d2d_r, own_s, own_r = sems
        x, y, c = _place()
        me = 2 * x + y
        own, ici, fwd = [], [], []
        for k in range(n):
            own.append(_remote(ins[k], outs[k].at[me], own_s.at[k], own_r.at[k], (x, y, 1 - c)))
            for j, (px, py) in enumerate(_other_chips(x, y)):
                i = 3 * k + j
                ici.append(_remote(ins[k].at[c], outs[k].at[me, c], ici_s.at[i], ici_r.at[i], (px, py, c)))
                got = outs[k].at[2 * px + py, c]
                fwd.append(_remote(got, got, d2d_s.at[i], d2d_r.at[i], (x, y, 1 - c)))
        return own, ici, fwd

    def start(ins, outs, sems):
        own, ici, _ = copies(ins, outs, sems)
        for cp in own + ici:
            cp.start()

    def finish(ins, outs, sems):
        own, ici, fwd = copies(ins, outs, sems)
        for a, f in zip(ici, fwd):
            a.wait_recv()
            f.start()
        for a in ici:
            a.wait_send()
        for f in fwd:
            f.wait()
        for o in own:
            o.wait()

    dma = pltpu.SemaphoreType.DMA
    return Rider(shards, [jax.ShapeDtypeStruct((N_CHIPS,) + s.shape, s.dtype) for s in shards],
                 [dma((3 * n,)), dma((3 * n,)), dma((3 * n,)), dma((3 * n,)), dma((n,)), dma((n,))], start, finish)


def half_swap_rider(grads):
    n = len(grads)

    def copies(ins, outs, sems):
        x, y, c = _place()
        return [_remote(ins[k].at[1 - c], outs[k], sems[0].at[k], sems[1].at[k], (x, y, 1 - c)) for k in range(n)]

    def start(ins, outs, sems):
        for cp in copies(ins, outs, sems):
            cp.start()

    def finish(ins, outs, sems):
        for cp in copies(ins, outs, sems):
            cp.wait()

    dma = pltpu.SemaphoreType.DMA
    return Rider(grads, [jax.ShapeDtypeStruct(g.shape[1:], g.dtype) for g in grads], [dma((n,)), dma((n,))],
                 start, finish)


def scatter_rider(halves):
    n = len(halves)

    def copies(ins, outs, sems):
        send, recv = sems
        x, y, c = _place()
        cps = []
        for k in range(n):
            for j, (px, py) in enumerate(_other_chips(x, y)):
                i = 3 * k + j
                cps.append(_remote(ins[k].at[2 * px + py], outs[k].at[j], send.at[i], recv.at[i], (px, py, c)))
        return cps

    def start(ins, outs, sems):
        for cp in copies(ins, outs, sems):
            cp.start()

    def finish(ins, outs, sems):
        for cp in copies(ins, outs, sems):
            cp.wait()

    dma = pltpu.SemaphoreType.DMA
    return Rider(halves, [jax.ShapeDtypeStruct((3,) + h.shape[1:], h.dtype) for h in halves],
                 [dma((3 * n,)), dma((3 * n,))], start, finish)


def final_swap_rider(sums):
    n = len(sums)

    def copies(ins, outs, sems):
        send, recv = sems
        x, y, c = _place()
        return [_remote(ins[k], outs[k], send.at[k], recv.at[k], (x, y, 1 - c)) for k in range(n)]

    def start(ins, outs, sems):
        for cp in copies(ins, outs, sems):
            cp.start()

    def finish(ins, outs, sems):
        for cp in copies(ins, outs, sems):
            cp.wait()

    dma = pltpu.SemaphoreType.DMA
    return Rider(sums, [jax.ShapeDtypeStruct(s.shape, s.dtype) for s in sums], [dma((n,)), dma((n,))],
                 start, finish)


def _call(body, *, name, grid, in_specs, out_specs, out_shape, scratch_shapes=(), sem, args, riders=(), aliases=None):
    in_specs, out_specs, out_shape = list(in_specs), list(out_specs), list(out_shape)
    scratch_shapes = list(scratch_shapes)
    aliases = aliases or {}
    if not riders:
        outs = pl.pallas_call(body, name=name, grid=grid, in_specs=in_specs, out_specs=out_specs, out_shape=out_shape,
                              scratch_shapes=scratch_shapes, input_output_aliases=aliases,
                              compiler_params=_params(*sem))(*args)
        return list(outs), []
    n_in, n_out, n_scr = len(in_specs), len(out_specs), len(scratch_shapes)
    r_in = [len(r.arrays) for r in riders]
    r_out = [len(r.out_shapes) for r in riders]
    r_sem = [len(r.sem_shapes) for r in riders]

    def wrapped(*refs):
        pos = 0

        def take(count):
            nonlocal pos
            got = refs[pos:pos + count]
            pos += count
            return got

        ins = take(n_in)
        rins = [take(k) for k in r_in]
        outs = take(n_out)
        routs = [take(k) for k in r_out]
        scr = take(n_scr)
        rsems = [take(k) for k in r_sem]
        first = functools.reduce(jnp.logical_and, [pl.program_id(a) == 0 for a in range(len(grid))])
        last = functools.reduce(jnp.logical_and, [pl.program_id(a) == grid[a] - 1 for a in range(len(grid))])

        @pl.when(first)
        def _():
            for r, ri, ro, rs in zip(riders, rins, routs, rsems):
                r.start(ri, ro, rs)

        body(*ins, *outs, *scr)

        @pl.when(last)
        def _():
            for r, ri, ro, rs in zip(riders, rins, routs, rsems):
                r.finish(ri, ro, rs)

    any_spec = pl.BlockSpec(memory_space=pl.ANY)
    res = pl.pallas_call(
        wrapped, name=name, grid=grid,
        in_specs=in_specs + [any_spec] * sum(r_in),
        out_specs=out_specs + [any_spec] * sum(r_out),
        out_shape=out_shape + [s for r in riders for s in r.out_shapes],
        scratch_shapes=scratch_shapes + [s for r in riders for s in r.sem_shapes],
        input_output_aliases=aliases,
        compiler_params=_params(*(["arbitrary"] * len(grid))),
    )(*args, *[a for r in riders for a in r.arrays])
    res = list(res)
    outs, rest = res[:n_out], res[n_out:]
    routs = []
    for k in r_out:
        routs.append(rest[:k])
        rest = rest[k:]
    return outs, routs


def exchange_now(riders, name):
    n_in = [len(r.arrays) for r in riders]
    n_out = [len(r.out_shapes) for r in riders]
    n_sem = [len(r.sem_shapes) for r in riders]

    def body(*refs):
        pos = 0
        groups = []
        for counts in (n_in, n_out, n_sem):
            part = []
            for k in counts:
                part.append(refs[pos:pos + k])
                pos += k
            groups.append(part)
        for r, ri, ro, rs in zip(riders, *groups):
            r.start(ri, ro, rs)
        for r, ri, ro, rs in zip(riders, *groups):
            r.finish(ri, ro, rs)

    any_spec = pl.BlockSpec(memory_space=pl.ANY)
    res = pl.pallas_call(
        body, name=name,
        in_specs=[any_spec] * sum(n_in), out_specs=[any_spec] * sum(n_out),
        out_shape=[s for r in riders for s in r.out_shapes],
        scratch_shapes=[s for r in riders for s in r.sem_shapes],
    )(*[a for r in riders for a in r.arrays])
    res = list(res)
    routs = []
    for k in n_out:
        routs.append(res[:k])
        res = res[k:]
    return routs


def rmsnorm_cast(x, gamma, name):
    T, D = x.shape
    tr = min(512, T)

    def body(x_ref, g_ref, o_ref):
        xv = x_ref[...]
        r = lax.rsqrt(jnp.mean(xv * xv, axis=-1, keepdims=True) + EPS)
        o_ref[...] = ((xv * r) * g_ref[...]).astype(BF16)

    return pl.pallas_call(
        body, name=name, grid=(T // tr,),
        in_specs=[pl.BlockSpec((tr, D), lambda i: (i, 0)), pl.BlockSpec((1, D), lambda i: (0, 0))],
        out_specs=pl.BlockSpec((tr, D), lambda i: (i, 0)),
        out_shape=jax.ShapeDtypeStruct((T, D), BF16),
        compiler_params=_params("parallel"),
    )(x, gamma)


def _rms_bwd(xv, gamma, dy):
    d = xv.shape[-1]
    r = lax.rsqrt(jnp.mean(xv * xv, axis=-1, keepdims=True) + EPS)
    gdy = dy * gamma
    dot = jnp.sum(gdy * xv, axis=-1, keepdims=True)
    dx = r * gdy - xv * ((r * r * r) * (dot / d))
    dgamma = jnp.sum(dy * (xv * r), axis=0, keepdims=True)
    return dx, dgamma


def rmsnorm_bwd_add(x, gamma, dy, dx_up, name, riders=()):
    T, D = x.shape
    tr = min(256, T)

    def body(x_ref, g_ref, dy_ref, up_ref, dx_ref, dg_ref):
        dx, dgamma = _rms_bwd(x_ref[...], g_ref[...], dy_ref[...])
        dx_ref[...] = up_ref[...] + dx

        @pl.when(pl.program_id(0) == 0)
        def _():
            dg_ref[...] = jnp.zeros_like(dg_ref)

        dg_ref[...] += dgamma

    row = pl.BlockSpec((tr, D), lambda i: (i, 0))
    vec = pl.BlockSpec((1, D), lambda i: (0, 0))
    return _call(
        body, name=name, grid=(T // tr,),
        in_specs=[row, vec, row, row], out_specs=[row, vec],
        out_shape=[jax.ShapeDtypeStruct((T, D), F32), jax.ShapeDtypeStruct((1, D), F32)],
        sem=("arbitrary",), args=(x, gamma, dy, dx_up), riders=riders)


def ffn_up(h, wg, wu, name, riders=()):
    T, D = h.shape
    G, _, Fs = wg.shape
    tm = min(512, T)

    def body(h_ref, wg_ref, wu_ref, g_ref, u_ref, a_ref):
        hv = h_ref[...]
        g = _dot(hv, wg_ref[...])
        u = _dot(hv, wu_ref[...])
        g_ref[...] = g.astype(BF16)
        u_ref[...] = u.astype(BF16)
        a_ref[...] = ((g * _sigmoid(g)) * u).astype(BF16)

    w_spec = pl.BlockSpec((None, D, Fs), lambda s, i: (s, 0, 0), pipeline_mode=pl.Buffered(1))
    o_spec = pl.BlockSpec((None, tm, Fs), lambda s, i: (s, i, 0))
    return _call(
        body, name=name, grid=(G, T // tm),
        in_specs=[pl.BlockSpec((tm, D), lambda s, i: (i, 0)), w_spec, w_spec],
        out_specs=[o_spec, o_spec, o_spec],
        out_shape=[jax.ShapeDtypeStruct((G, T, Fs), BF16)] * 3,
        sem=("parallel", "parallel"), args=(h, wg, wu), riders=riders)


def ffn_gate(h, wg, name, riders=()):
    T, D = h.shape
    G, _, Fs = wg.shape
    tm = min(512, T)

    def body(h_ref, wg_ref, g_ref):
        g_ref[...] = _dot(h_ref[...], wg_ref[...]).astype(BF16)

    return _call(
        body, name=name, grid=(G, T // tm),
        in_specs=[pl.BlockSpec((tm, D), lambda s, i: (i, 0)),
                  pl.BlockSpec((None, D, Fs), lambda s, i: (s, 0, 0), pipeline_mode=pl.Buffered(1))],
        out_specs=[pl.BlockSpec((None, tm, Fs), lambda s, i: (s, i, 0))],
        out_shape=[jax.ShapeDtypeStruct((G, T, Fs), BF16)],
        sem=("parallel", "parallel"), args=(h, wg), riders=riders)


def ffn_up_act(h, wu, g, name, riders=()):
    T, D = h.shape
    G, _, Fs = wu.shape
    tm = min(512, T)

    def body(h_ref, wu_ref, g_ref, u_ref, a_ref):
        u = _dot(h_ref[...], wu_ref[...])
        g = g_ref[...].astype(F32)
        u_ref[...] = u.astype(BF16)
        a_ref[...] = ((g * _sigmoid(g)) * u).astype(BF16)

    act = pl.BlockSpec((None, tm, Fs), lambda s, i: (s, i, 0))
    return _call(
        body, name=name, grid=(G, T // tm),
        in_specs=[pl.BlockSpec((tm, D), lambda s, i: (i, 0)),
                  pl.BlockSpec((None, D, Fs), lambda s, i: (s, 0, 0), pipeline_mode=pl.Buffered(1)), act],
        out_specs=[act, act],
        out_shape=[jax.ShapeDtypeStruct((G, T, Fs), BF16)] * 2,
        sem=("parallel", "parallel"), args=(h, wu, g), riders=riders)


def ffn_down(a, wd, x, name, riders=()):
    G, T, Fs = a.shape
    D = wd.shape[2]
    tm = min(512, T)

    def body(a_ref, w_ref, x_ref, o_ref, acc):
        s = pl.program_id(1)

        @pl.when(s == 0)
        def _():
            acc[...] = jnp.zeros_like(acc)

        acc[...] += _dot(a_ref[...], w_ref[...])

        @pl.when(s == G - 1)
        def _():
            o_ref[...] = x_ref[...] + 0.5 * acc[...]

    return _call(
        body, name=name, grid=(T // tm, G),
        in_specs=[pl.BlockSpec((None, tm, Fs), lambda i, s: (s, i, 0)),
                  pl.BlockSpec((None, Fs, D), lambda i, s: (s, 0, 0)),
                  pl.BlockSpec((tm, D), lambda i, s: (i, 0))],
        out_specs=[pl.BlockSpec((tm, D), lambda i, s: (i, 0))],
        out_shape=[jax.ShapeDtypeStruct((T, D), F32)],
        scratch_shapes=[pltpu.VMEM((tm, D), F32)],
        sem=("parallel", "arbitrary"), args=(a, wd, x), riders=riders)


def ffn_bwd_act(dx, wd, g, u, name, riders=()):
    T, D = dx.shape
    G, Fs, _ = wd.shape
    tm = min(512, T)

    def body(dx_ref, w_ref, g_ref, u_ref, dg_ref, du_ref):
        da = _dot_nt((0.5 * dx_ref[...]).astype(BF16), w_ref[...])
        gv = g_ref[...].astype(F32)
        sig = _sigmoid(gv)
        silu = gv * sig
        du_ref[...] = (da * silu).astype(BF16)
        dg_ref[...] = ((da * u_ref[...].astype(F32)) * (sig * (1.0 + gv * (1.0 - sig)))).astype(BF16)

    act = pl.BlockSpec((None, tm, Fs), lambda s, i: (s, i, 0))
    return _call(
        body, name=name, grid=(G, T // tm),
        in_specs=[pl.BlockSpec((tm, D), lambda s, i: (i, 0)),
                  pl.BlockSpec((None, Fs, D), lambda s, i: (s, 0, 0), pipeline_mode=pl.Buffered(1)),
                  act, act],
        out_specs=[act, act],
        out_shape=[jax.ShapeDtypeStruct((G, T, Fs), BF16), jax.ShapeDtypeStruct((G, T, Fs), BF16)],
        sem=("parallel", "parallel"), args=(dx, wd, g, u), riders=riders)


def ffn_bwd_dh(dg, du, wg, wu, name, riders=(), part=(0, 1), prev=None):
    G, T, Fs = dg.shape
    D = wg.shape[1]
    tm = min(512, T // part[1])
    nblk = T // tm // part[1]
    off = part[0] * nblk

    def body(dg_ref, du_ref, wg_ref, wu_ref, *rest):
        o_ref, acc = rest[-2:]
        s = pl.program_id(1)

        @pl.when(s == 0)
        def _():
            acc[...] = jnp.zeros_like(acc)

        acc[...] += _dot_nt(dg_ref[...], wg_ref[...])
        acc[...] += _dot_nt(du_ref[...], wu_ref[...])

        @pl.when(s == G - 1)
        def _():
            o_ref[...] = acc[...]

    act = pl.BlockSpec((None, tm, Fs), lambda i, s: (s, i + off, 0))
    w_spec = pl.BlockSpec((None, D, Fs), lambda i, s: (s, 0, 0))
    extra = [] if prev is None else [prev]
    return _call(
        body, name=name, grid=(nblk, G),
        in_specs=[act, act, w_spec, w_spec] + [pl.BlockSpec(memory_space=pl.ANY)] * len(extra),
        out_specs=[pl.BlockSpec((tm, D), lambda i, s: (i + off, 0))],
        out_shape=[jax.ShapeDtypeStruct((T, D), F32)],
        scratch_shapes=[pltpu.VMEM((tm, D), F32)],
        sem=("parallel", "arbitrary"), args=(dg, du, wg, wu, *extra), riders=riders,
        aliases={4: 0} if extra else None)


def wgrad(a, b, name, *, groups, m, n, a_block, a_map, b_block, b_map, b_scale=None, tk=512, riders=()):
    T = a.shape[-2]
    tk = min(tk, T)
    nk = T // tk

    def body(a_ref, b_ref, o_ref, acc):
        k = pl.program_id(2)

        @pl.when(k == 0)
        def _():
            acc[...] = jnp.zeros_like(acc)

        bv = b_ref[...]
        if b_scale is not None:
            bv = b_scale * bv
        acc[...] += _dot_tn(a_ref[...].astype(BF16), bv.astype(BF16))

        @pl.when(k == nk - 1)
        def _():
            o_ref[...] = acc[...].astype(BF16)

    return _call(
        body, name=name, grid=(groups, 2, nk),
        in_specs=[pl.BlockSpec(a_block(tk), a_map), pl.BlockSpec(b_block(tk), b_map)],
        out_specs=[pl.BlockSpec((None, None, m, n), lambda s, j, k: (j, s, 0, 0))],
        out_shape=[jax.ShapeDtypeStruct((2, groups, m, n), BF16)],
        scratch_shapes=[pltpu.VMEM((m, n), F32)],
        sem=("parallel", "parallel", "arbitrary"), args=(a, b), riders=riders)


def in_proj(h, w_in, name, riders=()):
    T, D = h.shape
    G, _, Ps = w_in.shape
    tm = min(512, T)

    def body(h_ref, w_ref, o_ref):
        o_ref[...] = _dot(h_ref[...], w_ref[...]).astype(BF16)

    return _call(
        body, name=name, grid=(G, T // tm),
        in_specs=[pl.BlockSpec((tm, D), lambda s, i: (i, 0)),
                  pl.BlockSpec((None, D, Ps), lambda s, i: (s, 0, 0))],
        out_specs=[pl.BlockSpec((tm, Ps), lambda s, i: (i, s))],
        out_shape=[jax.ShapeDtypeStruct((T, G * Ps), BF16)],
        sem=("parallel", "parallel"), args=(h, w_in), riders=riders)


def in_proj_bwd(dproj, w_in, name, riders=()):
    T = dproj.shape[0]
    G, D, Ps = w_in.shape
    tm = min(512, T)

    def body(d_ref, w_ref, o_ref, acc):
        s = pl.program_id(1)

        @pl.when(s == 0)
        def _():
            acc[...] = jnp.zeros_like(acc)

        acc[...] += _dot_nt(d_ref[...], w_ref[...])

        @pl.when(s == G - 1)
        def _():
            o_ref[...] = acc[...]

    return _call(
        body, name=name, grid=(T // tm, G),
        in_specs=[pl.BlockSpec((tm, Ps), lambda i, s: (i, s)),
                  pl.BlockSpec((None, D, Ps), lambda i, s: (s, 0, 0))],
        out_specs=[pl.BlockSpec((tm, D), lambda i, s: (i, 0))],
        out_shape=[jax.ShapeDtypeStruct((T, D), F32)],
        scratch_shapes=[pltpu.VMEM((tm, D), F32)],
        sem=("parallel", "arbitrary"), args=(dproj, w_in), riders=riders)


def out_proj(y, w_out, x, name):
    T, M = y.shape
    D = w_out.shape[1]
    tm = min(512, T)

    def body(y_ref, w_ref, x_ref, o_ref):
        o_ref[...] = x_ref[...] + _dot(y_ref[...], w_ref[...])

    return pl.pallas_call(
        body, name=name, grid=(T // tm,),
        in_specs=[pl.BlockSpec((tm, M), lambda i: (i, 0)), pl.BlockSpec((M, D), lambda i: (0, 0)),
                  pl.BlockSpec((tm, D), lambda i: (i, 0))],
        out_specs=pl.BlockSpec((tm, D), lambda i: (i, 0)),
        out_shape=jax.ShapeDtypeStruct((T, D), F32),
        compiler_params=_params("parallel"),
    )(y, w_out, x)


def out_norm(ya, yb, ga, gb, name):
    T, W = ya.shape
    tr = min(512, T)

    def body(a_ref, b_ref, ga_ref, gb_ref, o_ref):
        for src, gam, col in ((a_ref, ga_ref, 0), (b_ref, gb_ref, W)):
            v = src[...]
            r = lax.rsqrt(jnp.mean(v * v, axis=-1, keepdims=True) + EPS)
            o_ref[:, col:col + W] = ((v * r) * gam[...]).astype(BF16)

    row = pl.BlockSpec((tr, W), lambda i: (i, 0))
    vec = pl.BlockSpec((1, W), lambda i: (0, 0))
    return pl.pallas_call(
        body, name=name, grid=(T // tr,),
        in_specs=[row, row, vec, vec],
        out_specs=pl.BlockSpec((tr, 2 * W), lambda i: (i, 0)),
        out_shape=jax.ShapeDtypeStruct((T, 2 * W), BF16),
        compiler_params=_params("parallel"),
    )(ya, yb, ga, gb)


def out_proj_bwd(dx, w_out, ya, yb, ga, gb, name):
    T, D = dx.shape
    M = w_out.shape[0]
    W = M // 2
    tm = min(256, T)

    def body(dx_ref, w_ref, a_ref, b_ref, ga_ref, gb_ref, da_ref, db_ref, dga_ref, dgb_ref):
        dy = _dot_nt(dx_ref[...].astype(BF16), w_ref[...])

        @pl.when(pl.program_id(0) == 0)
        def _():
            dga_ref[...] = jnp.zeros_like(dga_ref)
            dgb_ref[...] = jnp.zeros_like(dgb_ref)

        d_a, dg_a = _rms_bwd(a_ref[...], ga_ref[...], dy[:, :W])
        d_b, dg_b = _rms_bwd(b_ref[...], gb_ref[...], dy[:, W:])
        da_ref[...] = d_a
        db_ref[...] = d_b
        dga_ref[...] += dg_a
        dgb_ref[...] += dg_b

    row = pl.BlockSpec((tm, W), lambda i: (i, 0))
    vec = pl.BlockSpec((1, W), lambda i: (0, 0))
    return pl.pallas_call(
        body, name=name, grid=(T // tm,),
        in_specs=[pl.BlockSpec((tm, D), lambda i: (i, 0)), pl.BlockSpec((M, D), lambda i: (0, 0)),
                  row, row, vec, vec],
        out_specs=[row, row, vec, vec],
        out_shape=[jax.ShapeDtypeStruct((T, W), F32), jax.ShapeDtypeStruct((T, W), F32),
                   jax.ShapeDtypeStruct((1, W), F32), jax.ShapeDtypeStruct((1, W), F32)],
        compiler_params=_params("arbitrary"),
    )(dx, w_out, ya, yb, ga, gb)


def final_loss(x, gamma, target, name):
    T, D = x.shape
    tr = min(256, T)

    def body(x_ref, g_ref, t_ref, dx_ref, dg_ref, loss_ref):
        xv = x_ref[...]
        gam = g_ref[...]
        r = lax.rsqrt(jnp.mean(xv * xv, axis=-1, keepdims=True) + EPS)
        err = (xv * r) * gam - t_ref[...]
        part = 0.5 * jnp.sum(jnp.mean(err * err, axis=-1, keepdims=True), axis=0, keepdims=True)
        dx, dgamma = _rms_bwd(xv, gam, err / D)
        dx_ref[...] = dx

        @pl.when(pl.program_id(0) == 0)
        def _():
            dg_ref[...] = jnp.zeros_like(dg_ref)
            loss_ref[...] = jnp.zeros_like(loss_ref)

        dg_ref[...] += dgamma
        loss_ref[...] += jnp.broadcast_to(part, loss_ref.shape)

    row = pl.BlockSpec((tr, D), lambda i: (i, 0))
    vec = pl.BlockSpec((1, D), lambda i: (0, 0))
    return pl.pallas_call(
        body, name=name, grid=(T // tr,),
        in_specs=[row, vec, row],
        out_specs=[row, vec, pl.BlockSpec((1, LANES), lambda i: (0, 0))],
        out_shape=[jax.ShapeDtypeStruct((T, D), F32), jax.ShapeDtypeStruct((1, D), F32),
                   jax.ShapeDtypeStruct((1, LANES), F32)],
        compiler_params=_params("arbitrary"),
    )(x, gamma, target)


def _lane_select(src_half, dst_halves):
    r = lax.broadcasted_iota(jnp.int32, (LANES, LANES), 0)
    c = lax.broadcasted_iota(jnp.int32, (LANES, LANES), 1)
    hit = ((r >> 6) == src_half) & ((r & (HEAD_DIM - 1)) == (c & (HEAD_DIM - 1)))
    ok = functools.reduce(jnp.logical_or, [(c >> 6) == d for d in dst_halves])
    return jnp.where(hit & ok, 1.0, 0.0).astype(BF16)


def _swa_slope(kvp, local_head):
    lo = 2.0 ** (-8.0 * (local_head + 1) / N_SWA_HEADS)
    hi = 2.0 ** (-8.0 * (8 + local_head + 1) / N_SWA_HEADS)
    return jnp.where(kvp == 0, jnp.float32(lo), jnp.float32(hi))


GQA = N_SWA_HEADS // N_SWA_KV


def _per_head_rows(values):
    return jnp.concatenate([jnp.zeros((SWA_BLOCK, 1), F32) + v for v in values], axis=0)


def _swa_probs(qblk, kwin, n, start, slope, sink):
    z = _dot_nt(qblk, kwin) * ATT_SCALE
    t = n * SWA_BLOCK + (lax.broadcasted_iota(jnp.int32, z.shape, 0) & (SWA_BLOCK - 1))
    s = start + lax.broadcasted_iota(jnp.int32, z.shape, 1)
    dist = t - s
    valid = (dist >= 0) & (dist < WINDOW)
    z = jnp.where(valid, z - slope * dist.astype(F32), NEG_BIG)
    m = jnp.maximum(jnp.max(z, axis=-1, keepdims=True), sink)
    e = jnp.exp(z - m)
    den = jnp.sum(e, axis=-1, keepdims=True) + jnp.exp(sink - m)
    return e / den, m, den


def swa_fwd(proj, sinks, b_loc, seq, name, riders=()):
    T = proj.shape[0]
    nblk = seq // SWA_BLOCK

    def body(sink_ref, q_ref, k_ref, v_ref, o_ref, qs, vv):
        kvp = pl.program_id(1)
        lane = lax.broadcasted_iota(jnp.int32, (SWA_BLOCK, LANES), 1)
        for kvh in range(2):
            vv[...] = _dot(v_ref[...], _lane_select(kvh, (0, 1))).astype(BF16)
            for g in range(GQA):
                cols = slice(LANES * (2 * kvh + g // 2), LANES * (2 * kvh + g // 2 + 1))
                qs[g] = _dot(q_ref[:, cols], _lane_select(g % 2, (kvh,))).astype(BF16)
            slope = _per_head_rows([_swa_slope(kvp, GQA * kvh + g) for g in range(GQA)])
            sink = _per_head_rows([sink_ref[0, 8 * kvp + GQA * kvh + g] for g in range(GQA)])

            def blk(n, carry):
                start = pl.multiple_of(jnp.maximum(n - 1, 0) * SWA_BLOCK, SWA_BLOCK)
                rows = pl.ds(pl.multiple_of(n * SWA_BLOCK, SWA_BLOCK), SWA_BLOCK)
                win = pl.ds(start, 2 * SWA_BLOCK)
                qblk = jnp.concatenate([qs[g, rows, :] for g in range(GQA)], axis=0)
                p, _, _ = _swa_probs(qblk, k_ref[win, :], n, start, slope, sink)
                out = _dot(p.astype(BF16), vv[win, :])
                for pair in range(GQA // 2):
                    cols = slice(LANES * (2 * kvh + pair), LANES * (2 * kvh + pair + 1))
                    first = out[2 * pair * SWA_BLOCK:(2 * pair + 1) * SWA_BLOCK]
                    second = out[(2 * pair + 1) * SWA_BLOCK:(2 * pair + 2) * SWA_BLOCK]
                    o_ref[rows, cols] = jnp.where(lane < HEAD_DIM, first, second)
                return carry

            lax.fori_loop(0, nblk, blk, 0)

    return _call(
        body, name=name, grid=(b_loc, 2),
        in_specs=[pl.BlockSpec(memory_space=pltpu.SMEM),
                  pl.BlockSpec((seq, 4 * LANES), lambda b, p: (b, p)),
                  pl.BlockSpec((seq, LANES), lambda b, p: (b, 8 + p)),
                  pl.BlockSpec((seq, LANES), lambda b, p: (b, 10 + p))],
        out_specs=[pl.BlockSpec((seq, 4 * LANES), lambda b, p: (b, p))],
        out_shape=[jax.ShapeDtypeStruct((T, N_SWA_HEADS * HEAD_DIM), F32)],
        scratch_shapes=[pltpu.VMEM((GQA, seq, LANES), BF16), pltpu.VMEM((seq, LANES), BF16)],
        sem=("parallel", "parallel"), args=(sinks, proj, proj, proj), riders=riders)


def swa_bwd(proj, sinks, dya, b_loc, seq, name, riders=()):
    T = proj.shape[0]
    nblk = seq // SWA_BLOCK

    def body(sink_ref, q_ref, k_ref, v_ref, do_ref, dq_ref, dk_ref, dv_ref, ds_ref, qs, dos, kk, dk_acc, dv_acc):
        kvp = pl.program_id(0)
        b = pl.program_id(1)
        lane = lax.broadcasted_iota(jnp.int32, (SWA_BLOCK, LANES), 1)

        @pl.when(b == 0)
        def _():
            ds_ref[...] = jnp.zeros_like(ds_ref)

        dk_acc[...] = jnp.zeros_like(dk_acc)
        dv_acc[...] = jnp.zeros_like(dv_acc)
        for kvh in range(2):
            kk[...] = _dot(k_ref[...], _lane_select(kvh, (0, 1))).astype(BF16)
            for g in range(GQA):
                cols = slice(LANES * (2 * kvh + g // 2), LANES * (2 * kvh + g // 2 + 1))
                move = _lane_select(g % 2, (kvh,))
                qs[g] = _dot(q_ref[:, cols], move).astype(BF16)
                dos[g] = _dot(do_ref[:, cols].astype(BF16), move).astype(BF16)
            slope = _per_head_rows([_swa_slope(kvp, GQA * kvh + g) for g in range(GQA)])
            sink = _per_head_rows([sink_ref[0, 8 * kvp + GQA * kvh + g] for g in range(GQA)])

            def blk(n, carry):
                start = pl.multiple_of(jnp.maximum(n - 1, 0) * SWA_BLOCK, SWA_BLOCK)
                rows = pl.ds(pl.multiple_of(n * SWA_BLOCK, SWA_BLOCK), SWA_BLOCK)
                win = pl.ds(start, 2 * SWA_BLOCK)
                qblk = jnp.concatenate([qs[g, rows, :] for g in range(GQA)], axis=0)
                doblk = jnp.concatenate([dos[g, rows, :] for g in range(GQA)], axis=0)
                p, m, den = _swa_probs(qblk, k_ref[win, :], n, start, slope, sink)
                dp = _dot_nt(doblk, v_ref[win, :])
                delta = jnp.sum(p * dp, axis=-1, keepdims=True)
                leak = (jnp.exp(sink - m) / den) * delta
                for g in range(GQA):
                    head = GQA * kvh + g
                    ds_ref[head:head + 1, :] -= jnp.broadcast_to(
                        jnp.sum(leak[g * SWA_BLOCK:(g + 1) * SWA_BLOCK], axis=0, keepdims=True), (1, LANES))
                dsc = ((p * (dp - delta)) * ATT_SCALE).astype(BF16)
                dv_acc[win, :] += _dot_tn(p.astype(BF16), doblk)
                dk_acc[win, :] += _dot_tn(dsc, qblk)
                dq = _dot(dsc, kk[win, :])
                for pair in range(GQA // 2):
                    cols = slice(LANES * (2 * kvh + pair), LANES * (2 * kvh + pair + 1))
                    first = dq[2 * pair * SWA_BLOCK:(2 * pair + 1) * SWA_BLOCK]
                    second = dq[(2 * pair + 1) * SWA_BLOCK:(2 * pair + 2) * SWA_BLOCK]
                    dq_ref[rows, cols] = jnp.where(lane < HEAD_DIM, first, second).astype(BF16)
                return carry

            lax.fori_loop(0, nblk, blk, 0)
        dk_ref[...] = dk_acc[...].astype(BF16)
        dv_ref[...] = dv_acc[...].astype(BF16)

    return _call(
        body, name=name, grid=(2, b_loc),
        in_specs=[pl.BlockSpec(memory_space=pltpu.SMEM),
                  pl.BlockSpec((seq, 4 * LANES), lambda p, b: (b, p)),
                  pl.BlockSpec((seq, LANES), lambda p, b: (b, 8 + p)),
                  pl.BlockSpec((seq, LANES), lambda p, b: (b, 10 + p)),
                  pl.BlockSpec((seq, 4 * LANES), lambda p, b: (b, p))],
        out_specs=[pl.BlockSpec((seq, 4 * LANES), lambda p, b: (b, p)),
                   pl.BlockSpec((seq, LANES), lambda p, b: (b, p)),
                   pl.BlockSpec((seq, LANES), lambda p, b: (b, p)),
                   pl.BlockSpec((8, LANES), lambda p, b: (p, 0))],
        out_shape=[jax.ShapeDtypeStruct((T, N_SWA_HEADS * HEAD_DIM), BF16),
                   jax.ShapeDtypeStruct((T, N_SWA_KV * HEAD_DIM), BF16),
                   jax.ShapeDtypeStruct((T, N_SWA_KV * HEAD_DIM), BF16),
                   jax.ShapeDtypeStruct((N_SWA_HEADS, LANES), F32)],
        scratch_shapes=[pltpu.VMEM((GQA, seq, LANES), BF16), pltpu.VMEM((GQA, seq, LANES), BF16),
                        pltpu.VMEM((seq, LANES), BF16),
                        pltpu.VMEM((seq, LANES), F32), pltpu.VMEM((seq, LANES), F32)],
        sem=("arbitrary", "arbitrary"), args=(sinks, proj, proj, proj, dya), riders=riders)


def _softplus(z):
    return jnp.maximum(z, 0.0) + jnp.log(1.0 + jnp.exp(-jnp.abs(z)))


def _tri(n, rel):
    r = lax.broadcasted_iota(jnp.int32, (n, n), 0)
    c = lax.broadcasted_iota(jnp.int32, (n, n), 1)
    return jnp.where(rel(r, c), 1.0, 0.0).astype(BF16)


def sb_fwd(proj, b_loc, seq, name, riders=()):
    T = proj.shape[0]
    bq = min(SB_BLOCK, seq)
    nq = seq // bq
    npair = N_SB_HEADS // 2

    def body(q_ref, k_ref, v_ref, o_ref, lt_ref):
        i = pl.program_id(2)
        lane = lax.broadcasted_iota(jnp.int32, (bq, LANES), 1)
        after = _tri(bq, lambda r, c: r > c)
        row = lax.broadcasted_iota(jnp.int32, (bq, bq), 0)
        col = lax.broadcasted_iota(jnp.int32, (bq, bq), 1)
        qv = q_ref[...] * jnp.asarray(ATT_SCALE, BF16)
        blank = jnp.zeros((bq, LANES), BF16)
        q2 = jnp.concatenate([jnp.where((lane >> 6) == h, qv, blank) for h in range(2)], axis=0)


        def tile(j, state, mask):
            rows = pl.ds(pl.multiple_of(j * bq, bq), bq)
            kj, vj = k_ref[rows, :], v_ref[rows, :]
            zz = _dot_nt(q2, kj)
            log_beta, totals, pieces = [], [], []
            for h in range(2):
                z = zz[h * bq:(h + 1) * bq]
                sp = _softplus(z)
                lg = -sp if mask is None else jnp.where(mask, -sp, 0.0)
                hi = lg.astype(BF16)
                pieces += [hi, (lg - hi.astype(F32)).astype(BF16)]
                log_beta.append(z - sp if mask is None else jnp.where(mask, z - sp, NEG_BIG))
                totals.append(jnp.sum(lg, axis=-1, keepdims=True))
            cc = _dot(jnp.concatenate(pieces, axis=0), after)
            probs = []
            for h in range(2):
                inside = cc[2 * h * bq:(2 * h + 1) * bq] + cc[(2 * h + 1) * bq:(2 * h + 2) * bq]
                probs.append(jnp.exp(log_beta[h] + (inside + state[2 * h])).astype(BF16))
            pv = _dot(jnp.concatenate(probs, axis=0), vj)
            return (state[0] + totals[0], state[1] + pv[:bq], state[2] + totals[1], state[3] + pv[bq:])

        zero = (jnp.zeros((bq, 1), F32), jnp.zeros((bq, LANES), F32))
        state = tile(i, zero + zero, col < row)
        state = lax.fori_loop(0, i, lambda jj, st: tile(i - 1 - jj, st, None), state)
        o_ref[...] = jnp.where(lane < HEAD_DIM, state[1], state[3])
        lt_ref[...] = jnp.where(lane < HEAD_DIM, state[0], state[2])

    return _call(
        body, name=name, grid=(b_loc, npair, nq),
        in_specs=[pl.BlockSpec((bq, LANES), lambda b, p, i: (b * nq + i, 12 + p)),
                  pl.BlockSpec((seq, LANES), lambda b, p, i: (b, 20 + p)),
                  pl.BlockSpec((seq, LANES), lambda b, p, i: (b, 28 + p))],
        out_specs=[pl.BlockSpec((bq, LANES), lambda b, p, i: (b * nq + i, p)),
                   pl.BlockSpec((None, bq, LANES), lambda b, p, i: (b * npair + p, i, 0))],
        out_shape=[jax.ShapeDtypeStruct((T, N_SB_HEADS * HEAD_DIM), F32),
                   jax.ShapeDtypeStruct((b_loc * npair, seq, LANES), F32)],
        sem=("parallel", "parallel", "parallel"), args=(proj, proj, proj), riders=riders)


def sb_bwd(proj, dyb, ltot, b_loc, seq, name, riders=()):
    T = proj.shape[0]
    bq = min(SB_BLOCK, seq)
    nq = seq // bq
    npair = N_SB_HEADS // 2

    def body(q_ref, k_ref, v_ref, do_ref, lt_ref, dq_ref, dk_ref, dv_ref, dk_acc, dv_acc):
        lane = lax.broadcasted_iota(jnp.int32, (bq, LANES), 1)
        upto = _tri(bq, lambda r, c: r <= c)
        before = _tri(bq, lambda r, c: r < c)
        row = lax.broadcasted_iota(jnp.int32, (bq, bq), 0)
        col = lax.broadcasted_iota(jnp.int32, (bq, bq), 1)
        dk_acc[...] = jnp.zeros_like(dk_acc)
        dv_acc[...] = jnp.zeros_like(dv_acc)
        scale = jnp.asarray(ATT_SCALE, BF16)
        heads = [(lane >> 6) == h for h in range(2)]
        blank = jnp.zeros((bq, LANES), BF16)

        def qblock(i, carry):
            qrows = pl.ds(pl.multiple_of(i * bq, bq), bq)
            qv = q_ref[qrows, :] * scale
            dov = do_ref[qrows, :].astype(BF16)
            ltv = lt_ref[qrows, :]
            q2 = jnp.concatenate([jnp.where(m, qv, blank) for m in heads], axis=0)
            do2 = jnp.concatenate([jnp.where(m, dov, blank) for m in heads], axis=0)
            totals = [jnp.sum(jnp.where(lane == h * HEAD_DIM, ltv, 0.0), axis=-1, keepdims=True) for h in range(2)]

            def split2(xs):
                out = []
                for x in xs:
                    hi = x.astype(BF16)
                    out += [hi, (x - hi.astype(F32)).astype(BF16)]
                return jnp.concatenate(out, axis=0)

            def join2(cc, h):
                return cc[2 * h * bq:(2 * h + 1) * bq] + cc[(2 * h + 1) * bq:(2 * h + 2) * bq]

            def tile(j, state, mask):
                krows = pl.ds(pl.multiple_of(j * bq, bq), bq)
                kj, vj = k_ref[krows, :], v_ref[krows, :]
                kv = kj * scale
                k2 = jnp.concatenate([jnp.where(m, kv, blank) for m in heads], axis=0)
                zz = _dot_nt(q2, kj)
                dd = _dot_nt(do2, vj)
                lgs, log_betas = [], []
                for h in range(2):
                    z = zz[h * bq:(h + 1) * bq]
                    sp = _softplus(z)
                    lgs.append(-sp if mask is None else jnp.where(mask, -sp, 0.0))
                    log_betas.append(z - sp)
                cl = _dot(split2(lgs), upto)
                probs, gs = [], []
                for h in range(2):
                    a = jnp.exp(log_betas[h] + (totals[h] - (join2(cl, h) + state[2 * h])))
                    if mask is not None:
                        a = jnp.where(mask, a, 0.0)
                    probs.append(a.astype(BF16))
                    gs.append(a * dd[h * bq:(h + 1) * bq])
                cg = _dot(split2(gs), before)
                dzs = []
                for h in range(2):
                    beta = jnp.exp(log_betas[h])
                    dz = gs[h] * (1.0 - beta) - beta * (join2(cg, h) + state[2 * h + 1])
                    if mask is not None:
                        dz = jnp.where(mask, dz, 0.0)
                    dzs.append(dz.astype(BF16))
                dv_acc[krows, :] += _dot_tn(jnp.concatenate(probs, axis=0), do2)
                dk_acc[krows, :] += _dot_tn(jnp.concatenate(dzs, axis=0), q2)
                dq = state[4] + _dot(jnp.concatenate(dzs, axis=1), k2)
                new = []
                for h in range(2):
                    new += [state[2 * h] + jnp.sum(lgs[h], axis=-1, keepdims=True),
                            state[2 * h + 1] + jnp.sum(gs[h], axis=-1, keepdims=True)]
                return (*new, dq)

            zero = jnp.zeros((bq, 1), F32)
            state = lax.fori_loop(0, i, lambda j, st: tile(j, st, None),
                                  (zero, zero, zero, zero, jnp.zeros((bq, LANES), F32)))
            state = tile(i, state, col < row)
            dq_ref[qrows, :] = state[4].astype(BF16)
            return carry

        lax.fori_loop(0, nq, qblock, 0)
        dk_ref[...] = dk_acc[...].astype(BF16)
        dv_ref[...] = dv_acc[...].astype(BF16)

    pair = pl.BlockSpec((seq, LANES), lambda b, p: (b, p))
    shape = jax.ShapeDtypeStruct((T, N_SB_HEADS * HEAD_DIM), BF16)
    return _call(
        body, name=name, grid=(b_loc, npair),
        in_specs=[pl.BlockSpec((seq, LANES), lambda b, p: (b, 12 + p)),
                  pl.BlockSpec((seq, LANES), lambda b, p: (b, 20 + p)),
                  pl.BlockSpec((seq, LANES), lambda b, p: (b, 28 + p)),
                  pair,
                  pl.BlockSpec((None, seq, LANES), lambda b, p: (b * npair + p, 0, 0))],
        out_specs=[pair, pair, pair],
        out_shape=[shape, shape, shape],
        scratch_shapes=[pltpu.VMEM((seq, LANES), F32)] * 2,
        sem=("parallel", "parallel"), args=(proj, proj, proj, dyb, ltot), riders=riders)


def allreduce_small(v, name):
    R = v.shape[0]
    ndev = 8

    def body(v_ref, o_ref, land, send_sems, recv_sems):
        x, y, c = _place()
        me = 4 * x + 2 * y + c
        land[me] = v_ref[...]
        started = []
        for f in range(1, ndev):
            fx, fy, fc = (f >> 2) & 1, (f >> 1) & 1, f & 1
            peer = ((1 - x) if fx else x, (1 - y) if fy else y, (1 - c) if fc else c)
            cp = pltpu.make_async_remote_copy(
                src_ref=v_ref, dst_ref=land.at[me], send_sem=send_sems.at[f - 1], recv_sem=recv_sems.at[f - 1],
                device_id=peer, device_id_type=MESH)
            cp.start()
            started.append(cp)
        for cp in started:
            cp.wait()
        total = land[0]
        for d in range(1, ndev):
            total = total + land[d]
        o_ref[...] = total

    return pl.pallas_call(
        body, name=name,
        in_specs=[VMEM_WHOLE], out_specs=VMEM_WHOLE,
        out_shape=jax.ShapeDtypeStruct(v.shape, F32),
        scratch_shapes=[pltpu.VMEM((ndev, R, LANES), F32),
                        pltpu.SemaphoreType.DMA((ndev - 1,)), pltpu.SemaphoreType.DMA((ndev - 1,))],
    )(v)


def sum_cores(grads, theirs, name):
    _, _, r, c = grads.shape
    tr, tc = _tiles(r, c, 1 << 18)
    core = lax.axis_index("c").astype(jnp.int32).reshape(1)

    def body(core_ref, g_ref, t_ref, o_ref):
        o_ref[...] = (g_ref[...].astype(F32) + t_ref[...].astype(F32)).astype(BF16)

    blk = pl.BlockSpec((N_CHIPS, tr, tc), lambda i, j, core_ref: (0, i, j))
    return pl.pallas_call(
        body, name=name,
        grid_spec=pltpu.PrefetchScalarGridSpec(
            num_scalar_prefetch=1, grid=(r // tr, c // tc),
            in_specs=[pl.BlockSpec((None, N_CHIPS, tr, tc), lambda i, j, core_ref: (core_ref[0], 0, i, j)), blk],
            out_specs=blk),
        out_shape=jax.ShapeDtypeStruct((N_CHIPS, r, c), BF16),
        compiler_params=_params("parallel", "parallel"),
    )(core, grads, theirs)


def sum_chips(halves, landed, name):
    _, r, c = landed.shape
    tr, tc = _tiles(r, c, 1 << 18)
    chip = (2 * lax.axis_index("x") + lax.axis_index("y")).astype(jnp.int32).reshape(1)

    def body(chip_ref, own_ref, p_ref, o_ref):
        total = own_ref[...].astype(F32)
        for j in range(N_CHIPS - 1):
            total = total + p_ref[j].astype(F32)
        o_ref[...] = total

    return pl.pallas_call(
        body, name=name,
        grid_spec=pltpu.PrefetchScalarGridSpec(
            num_scalar_prefetch=1, grid=(r // tr, c // tc),
            in_specs=[pl.BlockSpec((None, tr, tc), lambda i, j, chip_ref: (chip_ref[0], i, j)),
                      pl.BlockSpec((N_CHIPS - 1, tr, tc), lambda i, j, chip_ref: (0, i, j))],
            out_specs=pl.BlockSpec((tr, tc), lambda i, j, chip_ref: (i, j))),
        out_shape=jax.ShapeDtypeStruct((r, c), F32),
        compiler_params=_params("parallel", "parallel"),
    )(chip, halves, landed)


def _adamw_math(w, g, m, v):
    m = ADAM_B1 * m + (1.0 - ADAM_B1) * g
    v = ADAM_B2 * v + (1.0 - ADAM_B2) * (g * g)
    m_hat = m / (1.0 - ADAM_B1 ** ADAM_STEP)
    v_hat = v / (1.0 - ADAM_B2 ** ADAM_STEP)
    delta = -ADAM_LR * (m_hat / (jnp.sqrt(v_hat) + ADAM_EPS) + ADAM_WD * w)
    return delta, m, v


def adamw(w, m, v, g, name):
    r, c = w.shape
    tr, tc = _tiles(r, c, 1 << 18, multiple=8)

    def body(w_ref, m_ref, v_ref, gin_ref, g_ref, d_ref, nm_ref, nv_ref):
        g = gin_ref[...]
        delta, nm, nv = _adamw_math(w_ref[...], g, m_ref[...], v_ref[...])
        g_ref[...] = g
        d_ref[...] = delta
        nm_ref[...] = nm
        nv_ref[...] = nv

    blk = pl.BlockSpec((tr, tc), lambda i, j: (i, j))
    shape = jax.ShapeDtypeStruct((r, c), F32)
    return pl.pallas_call(
        body, name=name, grid=(r // tr, c // tc),
        in_specs=[blk] * 4, out_specs=[blk] * 4, out_shape=[shape] * 4,
        compiler_params=_params("parallel", "parallel"),
    )(w, m, v, g)


def adamw_halves(w, m, v, mine, theirs, name, col_halves=False):
    r, c = w.shape
    hr, hc = (r, c // 2) if col_halves else (r // 2, c)
    tr, tc = _tiles(hr, hc, 1 << 18, multiple=8)
    g_cols = mine.shape[-1]
    padded = g_cols != hc
    if padded and tc != hc:
        raise ValueError("a gradient with padded columns is read in full-width row blocks")
    per_half = hc // tc if col_halves else hr // tr
    core = lax.axis_index("c").astype(jnp.int32).reshape(1)

    def body(core_ref, w_ref, m_ref, v_ref, mine_ref, theirs_ref, g_ref, d_ref, nm_ref, nv_ref):
        half = pl.program_id(1 if col_halves else 0) // per_half
        cols = slice(0, hc) if padded else slice(None)
        g = jnp.where(half == core_ref[0], mine_ref[:, cols], theirs_ref[:, cols])
        delta, nm, nv = _adamw_math(w_ref[...], g, m_ref[...], v_ref[...])
        g_ref[...] = g
        d_ref[...] = delta
        nm_ref[...] = nm
        nv_ref[...] = nv

    blk = pl.BlockSpec((tr, tc), lambda i, j, core_ref: (i, j))
    if col_halves:
        g_spec = pl.BlockSpec((tr, tc), lambda i, j, core_ref: (i, j % per_half))
    else:
        g_spec = pl.BlockSpec((tr, g_cols if padded else tc), lambda i, j, core_ref: (i % per_half, j))
    shape = jax.ShapeDtypeStruct((r, c), F32)
    return pl.pallas_call(
        body, name=name,
        grid_spec=pltpu.PrefetchScalarGridSpec(
            num_scalar_prefetch=1, grid=(r // tr, c // tc),
            in_specs=[blk, blk, blk, g_spec, g_spec], out_specs=[blk] * 4),
        out_shape=[shape] * 4,
        compiler_params=_params("parallel", "parallel"),
    )(core, w, m, v, mine, theirs)


MATRICES = ("ffn1_w_gate", "ffn1_w_up", "ffn1_w_down", "w_in", "w_out", "ffn2_w_gate", "ffn2_w_up", "ffn2_w_down")
VECTORS = ("ffn1_norm", "mix_norm", "swa_sinks", "swa_out_norm", "sb_out_norm", "ffn2_norm", "final_norm")
WEIGHTS = ("ffn1_norm", "ffn1_w_gate", "ffn1_w_up", "ffn1_w_down", "mix_norm", "w_in", "swa_sinks", "swa_out_norm",
           "sb_out_norm", "w_out", "ffn2_norm", "ffn2_w_gate", "ffn2_w_up", "ffn2_w_down", "final_norm")


def _pack_rows(vec):
    flat = vec.reshape(-1).astype(F32)
    rows = -(-flat.shape[0] // LANES)
    rows8 = -(-rows // 8) * 8
    flat = jnp.pad(flat, (0, rows8 * LANES - flat.shape[0]))
    return flat.reshape(rows8, LANES)


class GradPipe:
    def __init__(self, name, grads):
        self.name, self.step, self.value = name, 0, grads

    def rider(self):
        make = (half_swap_rider, scatter_rider, final_swap_rider)[self.step]
        return make([self.value])

    def landed(self, outs):
        if self.step == 0:
            self.value = sum_cores(self.value, outs[0], "sum_cores_" + self.name)
        elif self.step == 1:
            self.value = sum_chips(self.value, outs[0], "sum_chips_" + self.name)
        else:
            self.value = (self.value, outs[0])
        self.step += 1


def _hosted(fn, *args, advance=(), **kw):
    outs, routs = fn(*args, riders=[p.rider() for p in advance], **kw)
    for p, ro in zip(advance, routs):
        p.landed(ro)
    return outs


def kernel(x, ffn1_norm, ffn1_w_gate, ffn1_w_up, ffn1_w_down, mix_norm, w_in, swa_sinks, swa_out_norm, sb_out_norm, w_out, ffn2_norm, ffn2_w_gate, ffn2_w_up, ffn2_w_down, final_norm, loss_target, m_ffn1_norm, m_ffn1_w_gate, m_ffn1_w_up, m_ffn1_w_down, m_mix_norm, m_w_in, m_swa_sinks, m_swa_out_norm, m_sb_out_norm, m_w_out, m_ffn2_norm, m_ffn2_w_gate, m_ffn2_w_up, m_ffn2_w_down, m_final_norm, v_ffn1_norm, v_ffn1_w_gate, v_ffn1_w_up, v_ffn1_w_down, v_mix_norm, v_w_in, v_swa_sinks, v_swa_out_norm, v_sb_out_norm, v_w_out, v_ffn2_norm, v_ffn2_w_gate, v_ffn2_w_up, v_ffn2_w_down, v_final_norm):
    given = dict(locals())
    b_loc, seq, D = x.shape
    T = b_loc * seq
    x0 = x.reshape(T, D)
    target = loss_target.reshape(T, D)
    final_g = final_norm.reshape(1, D)

    gathered = {}

    def gather(*names):
        shards = []
        for nm in names:
            w = given[nm][0].astype(BF16)
            hidden_axis = 0 if nm.endswith("w_down") else 1 if nm.startswith("ffn") else None
            if hidden_axis is not None:
                pad = -w.shape[hidden_axis] % LANES
                w = jnp.pad(w, [(0, pad) if a == hidden_axis else (0, 0) for a in range(2)])
            shards.append(w.reshape(2, w.shape[0] // 2, w.shape[1]))
        return names, gather_rider(shards)

    def hosting(fn, *args, fetch):
        names, rider = fetch
        outs, landed = fn(*args, riders=[rider])
        for nm, o in zip(names, landed[0]):
            gathered[nm] = o.reshape(N_CHIPS, 2 * o.shape[2], o.shape[3])
        return outs

    hosting(lambda riders: ([], exchange_now(riders, "gather_ffn1_gate")), fetch=gather("ffn1_w_gate"))

    h1 = rmsnorm_cast(x0, ffn1_norm, "ffn1_norm_fwd")
    (g1,) = hosting(ffn_gate, h1, gathered["ffn1_w_gate"], "ffn1_gate", fetch=gather("ffn1_w_up"))
    u1, a1 = hosting(ffn_up_act, h1, gathered["ffn1_w_up"], g1, "ffn1_up_act", fetch=gather("ffn1_w_down"))
    (x1,) = hosting(ffn_down, a1, gathered["ffn1_w_down"], x0, "ffn1_down", fetch=gather("w_in"))
    h2 = rmsnorm_cast(x1, mix_norm, "mix_norm_fwd")
    (proj,) = hosting(in_proj, h2, gathered["w_in"], "in_proj", fetch=gather("w_out"))
    w_out_all = gathered["w_out"].reshape(-1, D)
    (ya,) = hosting(swa_fwd, proj, swa_sinks, b_loc, seq, "swa_fwd", fetch=gather("ffn2_w_gate"))
    yb, ltot = hosting(sb_fwd, proj, b_loc, seq, "sb_fwd", fetch=gather("ffn2_w_up", "ffn2_w_down"))
    y = out_norm(ya, yb, swa_out_norm, sb_out_norm, "out_norm_fwd")
    x2 = out_proj(y, w_out_all, x1, "out_proj")
    h3 = rmsnorm_cast(x2, ffn2_norm, "ffn2_norm_fwd")
    (g2, u2, a2), _ = ffn_up(h3, gathered["ffn2_w_gate"], gathered["ffn2_w_up"], "ffn2_up")
    (x3,), _ = ffn_down(a2, gathered["ffn2_w_down"], x2, "ffn2_down")
    dx3, d_final, loss_part = final_loss(x3, final_g, target, "final_loss")

    Fs = gathered["ffn1_w_gate"].shape[2]
    Ps = gathered["w_in"].shape[2]
    Os = w_out_all.shape[0] // N_CHIPS
    pipes = {}

    def down_grad(tag, act, dx, advance):
        (grads,) = _hosted(wgrad, act, dx, tag + "_dwd", groups=N_CHIPS, m=Fs, n=D // 2,
                           a_block=lambda tk: (None, tk, Fs), a_map=lambda s, j, k: (s, k, 0),
                           b_block=lambda tk: (tk, D // 2), b_map=lambda s, j, k: (k, j), b_scale=0.5,
                           advance=advance)
        return grads

    def up_grad(name, h, d_act, advance):
        (grads,) = _hosted(wgrad, h, d_act, name, groups=N_CHIPS, m=D // 2, n=Fs,
                           a_block=lambda tk: (tk, D // 2), a_map=lambda s, j, k: (k, j),
                           b_block=lambda tk: (None, tk, Fs), b_map=lambda s, j, k: (s, k, 0), advance=advance)
        return grads

    (dg2, du2), _ = ffn_bwd_act(dx3, gathered["ffn2_w_down"], g2, u2, "ffn2_bwd_act")
    pipes["ffn2_w_down"] = p_d2 = GradPipe("ffn2_w_down", down_grad("ffn2", a2, dx3, []))
    pipes["ffn2_w_gate"] = p_g2 = GradPipe("ffn2_w_gate", up_grad("ffn2_dwg", h3, dg2, [p_d2]))
    pipes["ffn2_w_up"] = p_u2 = GradPipe("ffn2_w_up", up_grad("ffn2_dwu", h3, du2, [p_d2, p_g2]))
    (dh3,) = _hosted(ffn_bwd_dh, dg2, du2, gathered["ffn2_w_gate"], gathered["ffn2_w_up"], "ffn2_bwd_dh",
                     advance=[p_d2, p_g2, p_u2])
    (dx2, d_ffn2_norm), _ = rmsnorm_bwd_add(x2, ffn2_norm, dh3, dx3, "ffn2_norm_bwd")

    (d_w_out,) = _hosted(wgrad, y, dx2, "dw_out", groups=N_CHIPS, m=Os // 2, n=D,
                         a_block=lambda tk: (tk, Os // 2), a_map=lambda s, j, k: (k, 2 * s + j),
                         b_block=lambda tk: (tk, D), b_map=lambda s, j, k: (k, 0), advance=[p_g2])
    pipes["w_out"] = p_out = GradPipe("w_out", d_w_out)
    dya, dyb, d_swa_norm, d_sb_norm = out_proj_bwd(dx2, w_out_all, ya, yb, swa_out_norm, sb_out_norm, "out_proj_bwd")
    dqa, dka, dva, d_sinks = _hosted(swa_bwd, proj, swa_sinks, dya, b_loc, seq, "swa_bwd", advance=[p_u2, p_out])
    dqb, dkb, dvb = _hosted(sb_bwd, proj, dyb, ltot, b_loc, seq, "sb_bwd", advance=[p_u2, p_out])
    dproj = jnp.concatenate([dqa, dka, dva, dqb, dkb, dvb], axis=1)
    (d_w_in,) = _hosted(wgrad, h2, dproj, "dw_in", groups=N_CHIPS, m=D // 2, n=Ps,
                        a_block=lambda tk: (tk, D // 2), a_map=lambda s, j, k: (k, j),
                        b_block=lambda tk: (tk, Ps), b_map=lambda s, j, k: (k, s), advance=[p_out])
    pipes["w_in"] = p_in = GradPipe("w_in", d_w_in)
    (dh2,) = _hosted(in_proj_bwd, dproj, gathered["w_in"], "in_proj_bwd", advance=[p_in])
    (dx1, d_mix_norm), _ = rmsnorm_bwd_add(x1, mix_norm, dh2, dx2, "mix_norm_bwd")

    pipes["ffn1_w_down"] = p_d1 = GradPipe("ffn1_w_down", down_grad("ffn1", a1, dx1, [p_in]))
    dg1, du1 = _hosted(ffn_bwd_act, dx1, gathered["ffn1_w_down"], g1, u1, "ffn1_bwd_act", advance=[p_in, p_d1])
    pipes["ffn1_w_gate"] = p_g1 = GradPipe("ffn1_w_gate", up_grad("ffn1_dwg", h1, dg1, [p_d1]))
    pipes["ffn1_w_up"] = p_u1 = GradPipe("ffn1_w_up", up_grad("ffn1_dwu", h1, du1, [p_d1, p_g1]))
    (dh1,) = _hosted(ffn_bwd_dh, dg1, du1, gathered["ffn1_w_gate"], gathered["ffn1_w_up"], "ffn1_bwd_dh_a",
                     part=(0, 2), advance=[p_g1, p_u1])
    (dh1,) = _hosted(ffn_bwd_dh, dg1, du1, gathered["ffn1_w_gate"], gathered["ffn1_w_up"], "ffn1_bwd_dh_b",
                     part=(1, 2), prev=dh1, advance=[p_g1, p_u1])
    (dx0, d_ffn1_norm), _ = rmsnorm_bwd_add(x0, ffn1_norm, dh1, dx1, "ffn1_norm_bwd")
    (last,) = exchange_now([p_u1.rider()], "swap_ffn1_w_up")
    p_u1.landed(last)

    vec_grads = dict(ffn1_norm=d_ffn1_norm, mix_norm=d_mix_norm, swa_sinks=d_sinks[:, 0], swa_out_norm=d_swa_norm,
                     sb_out_norm=d_sb_norm, ffn2_norm=d_ffn2_norm, final_norm=d_final)
    packed = [_pack_rows(vec_grads[nm]) for nm in VECTORS] + [_pack_rows(loss_part[0, :1])]
    offsets = [0]
    for p in packed:
        offsets.append(offsets[-1] + p.shape[0])
    reduced = allreduce_small(jnp.concatenate(packed, axis=0), "reduce_vectors")
    loss = reduced[offsets[len(VECTORS)], 0]

    out = {}
    for nm in MATRICES:
        shape = given[nm].shape
        mine, theirs = pipes[nm].value
        res = adamw_halves(given[nm][0], given["m_" + nm][0], given["v_" + nm][0], mine, theirs, "adamw_" + nm,
                           col_halves=nm.endswith("w_down"))
        out[nm] = [r.reshape(shape) for r in res]
    w_rows = jnp.concatenate([_pack_rows(given[nm]) for nm in VECTORS], axis=0)
    m_rows = jnp.concatenate([_pack_rows(given["m_" + nm]) for nm in VECTORS], axis=0)
    v_rows = jnp.concatenate([_pack_rows(given["v_" + nm]) for nm in VECTORS], axis=0)
    g_rows = reduced[:offsets[len(VECTORS)]]
    small = adamw(w_rows, m_rows, v_rows, g_rows, "adamw_vectors")
    for i, nm in enumerate(VECTORS):
        shape = given[nm].shape
        size = math.prod(shape)
        out[nm] = [r[offsets[i]:offsets[i + 1]].reshape(-1)[:size].reshape(shape) for r in small]

    grad_x = dx0.reshape(b_loc, seq, D)
    return (loss, grad_x, *[out[nm][0] for nm in WEIGHTS], *[out[nm][1] for nm in WEIGHTS],
            *[out[nm][2] for nm in WEIGHTS], *[out[nm][3] for nm in WEIGHTS])
```

```python
import functools
import math

import jax
import jax.numpy as jnp
from jax import lax
from jax.experimental import pallas as pl
from jax.experimental.pallas import tpu as pltpu

F32 = jnp.float32
BF16 = jnp.bfloat16
MESH = pl.DeviceIdType.MESH

EPS = 1e-6
HEAD_DIM = 64
LANES = 128
N_SWA_HEADS = 16
N_SWA_KV = 4
N_SB_HEADS = 16
WINDOW = 128
SWA_BLOCK = 128
SB_BLOCK = 256
N_CHIPS = 4
ATT_SCALE = HEAD_DIM ** -0.5

ADAM_LR = 0.001
ADAM_B1 = 0.9
ADAM_B2 = 0.999
ADAM_EPS = 1e-08
ADAM_WD = 0.01
ADAM_STEP = 10

VMEM_LIMIT = 56 * 1024 * 1024
NEG_BIG = -1e30

ANY = pl.BlockSpec(memory_space=pl.ANY)
VMEM_WHOLE = pl.BlockSpec(memory_space=pltpu.VMEM)


def _row_tile(rows, cap, multiple=16):
    best = None
    for t in range(multiple, min(rows, cap) + 1, multiple):
        if rows % t == 0:
            best = t
    if best is None:
        raise ValueError(f"no row tile for {rows} rows under {cap}")
    return best


def _params(*sem):
    return pltpu.CompilerParams(dimension_semantics=sem, vmem_limit_bytes=VMEM_LIMIT)


def _dot(a, b):
    return jnp.dot(a, b, preferred_element_type=F32)


def _dot_nt(a, b):
    return lax.dot_general(a, b, (((1,), (1,)), ((), ())), preferred_element_type=F32)


def _dot_tn(a, b):
    return lax.dot_general(a, b, (((0,), (0,)), ((), ())), preferred_element_type=F32)


def _sigmoid(x):
    return 1.0 / (1.0 + jnp.exp(-x))


def _split_dot(x, u):
    hi = x.astype(BF16)
    lo = (x - hi.astype(F32)).astype(BF16)
    return _dot(hi, u) + _dot(lo, u)


def _tiles(r, c, max_elems, multiple=16):
    col_opts = [c] + [t for t in (2048, 1024, 512, 256, 128) if t < c and c % t == 0]
    best = None
    for tc in col_opts:
        for tr in [r] + list(range(multiple, r, multiple)):
            if r % tr == 0 and tr * tc <= max_elems and (best is None or tr * tc > best[0] * best[1]):
                best = (tr, tc)
    if best is None:
        raise ValueError(f"no tile for [{r}, {c}] under {max_elems} elements")
    return best


class Rider:
    def __init__(self, arrays, out_shapes, sem_shapes, start, finish):
        self.arrays, self.out_shapes, self.sem_shapes = list(arrays), list(out_shapes), list(sem_shapes)
        self.start, self.finish = start, finish


def _place():
    return lax.axis_index("x"), lax.axis_index("y"), lax.axis_index("c")


def _other_chips(x, y):
    return [(1 - x, y), (x, 1 - y), (1 - x, 1 - y)]


def _split(ref):
    rows = ref.shape[-2]
    n = next(k for k in (4, 2, 1) if rows % (16 * k) == 0)
    step = rows // n
    lead = [()] if len(ref.shape) == 2 else [(s,) for s in range(ref.shape[0])]
    return [ref.at[(*idx, pl.ds(i * step, step))] for idx in lead for i in range(n)]


class _Copy:
    def __init__(self, src, dst, make):
        self.src, self.dst, self.make = src, dst, make

    def start(self):
        for s, d in zip(_split(self.src), _split(self.dst)):
            self.make(s, d).start()

    def wait(self):
        self.make(self.src, self.dst).wait()

    def wait_send(self):
        self.make(self.src, self.dst).wait_send()

    def wait_recv(self):
        self.make(self.src, self.dst).wait_recv()


def _remote(src, dst, send_sem, recv_sem, device):
    return _Copy(src, dst, lambda s, d: pltpu.make_async_remote_copy(
        src_ref=s, dst_ref=d, send_sem=send_sem, recv_sem=recv_sem, device_id=device, device_id_type=MESH))


def gather_rider(shards):
    n = len(shards)

    def copies(ins, outs, sems):
        ici_s, ici_r, d2d_s, d2d_r, own_s, own_r = sems
        x, y, c = _place()
        me = 2 * x + y
        own, ici, fwd = [], [], []
        for k in range(n):
            own.append(_remote(ins[k], outs[k].at[me], own_s.at[k], own_r.at[k], (x, y, 1 - c)))
            for j, (px, py) in enumerate(_other_chips(x, y)):
                i = 3 * k + j
                ici.append(_remote(ins[k].at[c], outs[k].at[me, c], ici_s.at[i], ici_r.at[i], (px, py, c)))
                got = outs[k].at[2 * px + py, c]
                fwd.append(_remote(got, got, d2d_s.at[i], d2d_r.at[i], (x, y, 1 - c)))
        return own, ici, fwd

    def start(ins, outs, sems):
        own, ici, _ = copies(ins, outs, sems)
        for cp in own + ici:
            cp.start()

    def finish(ins, outs, sems):
        own, ici, fwd = copies(ins, outs, sems)
        for a, f in zip(ici, fwd):
            a.wait_recv()
            f.start()
        for a in ici:
            a.wait_send()
        for f in fwd:
            f.wait()
        for o in own:
            o.wait()

    dma = pltpu.SemaphoreType.DMA
    return Rider(shards, [jax.ShapeDtypeStruct((N_CHIPS,) + s.shape, s.dtype) for s in shards],
                 [dma((3 * n,)), dma((3 * n,)), dma((3 * n,)), dma((3 * n,)), dma((n,)), dma((n,))], start, finish)


def half_swap_rider(grads):
    n = len(grads)

    def copies(ins, outs, sems):
        x, y, c = _place()
        return [_remote(ins[k].at[1 - c], outs[k], sems[0].at[k], sems[1].at[k], (x, y, 1 - c)) for k in range(n)]

    def start(ins, outs, sems):
        for cp in copies(ins, outs, sems):
            cp.start()

    def finish(ins, outs, sems):
        for cp in copies(ins, outs, sems):
            cp.wait()

    dma = pltpu.SemaphoreType.DMA
    return Rider(grads, [jax.ShapeDtypeStruct(g.shape[1:], g.dtype) for g in grads], [dma((n,)), dma((n,))],
                 start, finish)


def scatter_rider(halves):
    n = len(halves)

    def copies(ins, outs, sems):
        send, recv = sems
        x, y, c = _place()
        cps = []
        for k in range(n):
            for j, (px, py) in enumerate(_other_chips(x, y)):
                i = 3 * k + j
                cps.append(_remote(ins[k].at[2 * px + py], outs[k].at[j], send.at[i], recv.at[i], (px, py, c)))
        return cps

    def start(ins, outs, sems):
        for cp in copies(ins, outs, sems):
            cp.start()

    def finish(ins, outs, sems):
        for cp in copies(ins, outs, sems):
            cp.wait()

    dma = pltpu.SemaphoreType.DMA
    return Rider(halves, [jax.ShapeDtypeStruct((3,) + h.shape[1:], h.dtype) for h in halves],
                 [dma((3 * n,)), dma((3 * n,))], start, finish)


def final_swap_rider(sums):
    n = len(sums)

    def copies(ins, outs, sems):
        send, recv = sems
        x, y, c = _place()
        return [_remote(ins[k], outs[k], send.at[k], recv.at[k], (x, y, 1 - c)) for k in range(n)]

    def start(ins, outs, sems):
        for cp in copies(ins, outs, sems):
            cp.start()

    def finish(ins, outs, sems):
        for cp in copies(ins, outs, sems):
            cp.wait()

    dma = pltpu.SemaphoreType.DMA
    return Rider(sums, [jax.ShapeDtypeStruct(s.shape, s.dtype) for s in sums], [dma((n,)), dma((n,))],
                 start, finish)


def _call(body, *, name, grid, in_specs, out_specs, out_shape, scratch_shapes=(), sem, args, riders=(), aliases=None):
    in_specs, out_specs, out_shape = list(in_specs), list(out_specs), list(out_shape)
    scratch_shapes = list(scratch_shapes)
    aliases = aliases or {}
    if not riders:
        outs = pl.pallas_call(body, name=name, grid=grid, in_specs=in_specs, out_specs=out_specs, out_shape=out_shape,
                              scratch_shapes=scratch_shapes, input_output_aliases=aliases,
                              compiler_params=_params(*sem))(*args)
        return list(outs), []
    n_in, n_out, n_scr = len(in_specs), len(out_specs), len(scratch_shapes)
    r_in = [len(r.arrays) for r in riders]
    r_out = [len(r.out_shapes) for r in riders]
    r_sem = [len(r.sem_shapes) for r in riders]

    def wrapped(*refs):
        pos = 0

        def take(count):
            nonlocal pos
            got = refs[pos:pos + count]
            pos += count
            return got

        ins = take(n_in)
        rins = [take(k) for k in r_in]
        outs = take(n_out)
        routs = [take(k) for k in r_out]
        scr = take(n_scr)
        rsems = [take(k) for k in r_sem]
        first = functools.reduce(jnp.logical_and, [pl.program_id(a) == 0 for a in range(len(grid))])
        last = functools.reduce(jnp.logical_and, [pl.program_id(a) == grid[a] - 1 for a in range(len(grid))])

        @pl.when(first)
        def _():
            for r, ri, ro, rs in zip(riders, rins, routs, rsems):
                r.start(ri, ro, rs)

        body(*ins, *outs, *scr)

        @pl.when(last)
        def _():
            for r, ri, ro, rs in zip(riders, rins, routs, rsems):
                r.finish(ri, ro, rs)

    any_spec = pl.BlockSpec(memory_space=pl.ANY)
    res = pl.pallas_call(
        wrapped, name=name, grid=grid,
        in_specs=in_specs + [any_spec] * sum(r_in),
        out_specs=out_specs + [any_spec] * sum(r_out),
        out_shape=out_shape + [s for r in riders for s in r.out_shapes],
        scratch_shapes=scratch_shapes + [s for r in riders for s in r.sem_shapes],
        input_output_aliases=aliases,
        compiler_params=_params(*(["arbitrary"] * len(grid))),
    )(*args, *[a for r in riders for a in r.arrays])
    res = list(res)
    outs, rest = res[:n_out], res[n_out:]
    routs = []
    for k in r_out:
        routs.append(rest[:k])
        rest = rest[k:]
    return outs, routs


def exchange_now(riders, name):
    n_in = [len(r.arrays) for r in riders]
    n_out = [len(r.out_shapes) for r in riders]
    n_sem = [len(r.sem_shapes) for r in riders]

    def body(*refs):
        pos = 0
        groups = []
        for counts in (n_in, n_out, n_sem):
            part = []
            for k in counts:
                part.append(refs[pos:pos + k])
                pos += k
            groups.append(part)
        for r, ri, ro, rs in zip(riders, *groups):
            r.start(ri, ro, rs)
        for r, ri, ro, rs in zip(riders, *groups):
            r.finish(ri, ro, rs)

    any_spec = pl.BlockSpec(memory_space=pl.ANY)
    res = pl.pallas_call(
        body, name=name,
        in_specs=[any_spec] * sum(n_in), out_specs=[any_spec] * sum(n_out),
        out_shape=[s for r in riders for s in r.out_shapes],
        scratch_shapes=[s for r in riders for s in r.sem_shapes],
    )(*[a for r in riders for a in r.arrays])
    res = list(res)
    routs = []
    for k in n_out:
        routs.append(res[:k])
        res = res[k:]
    return routs


def rmsnorm_cast(x, gamma, name):
    T, D = x.shape
    tr = min(512, T)

    def body(x_ref, g_ref, o_ref):
        xv = x_ref[...]
        r = lax.rsqrt(jnp.mean(xv * xv, axis=-1, keepdims=True) + EPS)
        o_ref[...] = ((xv * r) * g_ref[...]).astype(BF16)

    return pl.pallas_call(
        body, name=name, grid=(T // tr,),
        in_specs=[pl.BlockSpec((tr, D), lambda i: (i, 0)), pl.BlockSpec((1, D), lambda i: (0, 0))],
        out_specs=pl.BlockSpec((tr, D), lambda i: (i, 0)),
        out_shape=jax.ShapeDtypeStruct((T, D), BF16),
        compiler_params=_params("parallel"),
    )(x, gamma)


def _rms_bwd(xv, gamma, dy):
    d = xv.shape[-1]
    r = lax.rsqrt(jnp.mean(xv * xv, axis=-1, keepdims=True) + EPS)
    gdy = dy * gamma
    dot = jnp.sum(gdy * xv, axis=-1, keepdims=True)
    dx = r * gdy - xv * ((r * r * r) * (dot / d))
    dgamma = jnp.sum(dy * (xv * r), axis=0, keepdims=True)
    return dx, dgamma


def rmsnorm_bwd_add(x, gamma, dy, dx_up, name, riders=()):
    T, D = x.shape
    tr = min(256, T)

    def body(x_ref, g_ref, dy_ref, up_ref, dx_ref, dg_ref):
        dx, dgamma = _rms_bwd(x_ref[...], g_ref[...], dy_ref[...])
        dx_ref[...] = up_ref[...] + dx

        @pl.when(pl.program_id(0) == 0)
        def _():
            dg_ref[...] = jnp.zeros_like(dg_ref)

        dg_ref[...] += dgamma

    row = pl.BlockSpec((tr, D), lambda i: (i, 0))
    vec = pl.BlockSpec((1, D), lambda i: (0, 0))
    return _call(
        body, name=name, grid=(T // tr,),
        in_specs=[row, vec, row, row], out_specs=[row, vec],
        out_shape=[jax.ShapeDtypeStruct((T, D), F32), jax.ShapeDtypeStruct((1, D), F32)],
        sem=("arbitrary",), args=(x, gamma, dy, dx_up), riders=riders)


def ffn_up(h, wg, wu, name, riders=()):
    T, D = h.shape
    G, _, Fs = wg.shape
    tm = min(512, T)

    def body(h_ref, wg_ref, wu_ref, g_ref, u_ref, a_ref):
        hv = h_ref[...]
        g = _dot(hv, wg_ref[...])
        u = _dot(hv, wu_ref[...])
        g_ref[...] = g.astype(BF16)
        u_ref[...] = u.astype(BF16)
        a_ref[...] = ((g * _sigmoid(g)) * u).astype(BF16)

    w_spec = pl.BlockSpec((None, D, Fs), lambda s, i: (s, 0, 0), pipeline_mode=pl.Buffered(1))
    o_spec = pl.BlockSpec((None, tm, Fs), lambda s, i: (s, i, 0))
    return _call(
        body, name=name, grid=(G, T // tm),
        in_specs=[pl.BlockSpec((tm, D), lambda s, i: (i, 0)), w_spec, w_spec],
        out_specs=[o_spec, o_spec, o_spec],
        out_shape=[jax.ShapeDtypeStruct((G, T, Fs), BF16)] * 3,
        sem=("parallel", "parallel"), args=(h, wg, wu), riders=riders)


def ffn_gate(h, wg, name, riders=()):
    T, D = h.shape
    G, _, Fs = wg.shape
    tm = min(512, T)

    def body(h_ref, wg_ref, g_ref):
        g_ref[...] = _dot(h_ref[...], wg_ref[...]).astype(BF16)

    return _call(
        body, name=name, grid=(G, T // tm),
        in_specs=[pl.BlockSpec((tm, D), lambda s, i: (i, 0)),
                  pl.BlockSpec((None, D, Fs), lambda s, i: (s, 0, 0), pipeline_mode=pl.Buffered(1))],
        out_specs=[pl.BlockSpec((None, tm, Fs), lambda s, i: (s, i, 0))],
        out_shape=[jax.ShapeDtypeStruct((G, T, Fs), BF16)],
        sem=("parallel", "parallel"), args=(h, wg), riders=riders)


def ffn_up_act(h, wu, g, name, riders=()):
    T, D = h.shape
    G, _, Fs = wu.shape
    tm = min(512, T)

    def body(h_ref, wu_ref, g_ref, u_ref, a_ref):
        u = _dot(h_ref[...], wu_ref[...])
        g = g_ref[...].astype(F32)
        u_ref[...] = u.astype(BF16)
        a_ref[...] = ((g * _sigmoid(g)) * u).astype(BF16)

    act = pl.BlockSpec((None, tm, Fs), lambda s, i: (s, i, 0))
    return _call(
        body, name=name, grid=(G, T // tm),
        in_specs=[pl.BlockSpec((tm, D), lambda s, i: (i, 0)),
                  pl.BlockSpec((None, D, Fs), lambda s, i: (s, 0, 0), pipeline_mode=pl.Buffered(1)), act],
        out_specs=[act, act],
        out_shape=[jax.ShapeDtypeStruct((G, T, Fs), BF16)] * 2,
        sem=("parallel", "parallel"), args=(h, wu, g), riders=riders)


def ffn_down(a, wd, x, name, riders=()):
    G, T, Fs = a.shape
    D = wd.shape[2]
    tm = min(512, T)

    def body(a_ref, w_ref, x_ref, o_ref, acc):
        s = pl.program_id(1)

        @pl.when(s == 0)
        def _():
            acc[...] = jnp.zeros_like(acc)

        acc[...] += _dot(a_ref[...], w_ref[...])

        @pl.when(s == G - 1)
        def _():
            o_ref[...] = x_ref[...] + 0.5 * acc[...]

    return _call(
        body, name=name, grid=(T // tm, G),
        in_specs=[pl.BlockSpec((None, tm, Fs), lambda i, s: (s, i, 0)),
                  pl.BlockSpec((None, Fs, D), lambda i, s: (s, 0, 0)),
                  pl.BlockSpec((tm, D), lambda i, s: (i, 0))],
        out_specs=[pl.BlockSpec((tm, D), lambda i, s: (i, 0))],
        out_shape=[jax.ShapeDtypeStruct((T, D), F32)],
        scratch_shapes=[pltpu.VMEM((tm, D), F32)],
        sem=("parallel", "arbitrary"), args=(a, wd, x), riders=riders)


def ffn_bwd_act(dx, wd, g, u, name, riders=()):
    T, D = dx.shape
    G, Fs, _ = wd.shape
    tm = min(512, T)

    def body(dx_ref, w_ref, g_ref, u_ref, dg_ref, du_ref):
        da = _dot_nt((0.5 * dx_ref[...]).astype(BF16), w_ref[...])
        gv = g_ref[...].astype(F32)
        sig = _sigmoid(gv)
        silu = gv * sig
        du_ref[...] = (da * silu).astype(BF16)
        dg_ref[...] = ((da * u_ref[...].astype(F32)) * (sig * (1.0 + gv * (1.0 - sig)))).astype(BF16)

    act = pl.BlockSpec((None, tm, Fs), lambda s, i: (s, i, 0))
    return _call(
        body, name=name, grid=(G, T // tm),
        in_specs=[pl.BlockSpec((tm, D), lambda s, i: (i, 0)),
                  pl.BlockSpec((None, Fs, D), lambda s, i: (s, 0, 0), pipeline_mode=pl.Buffered(1)),
                  act, act],
        out_specs=[act, act],
        out_shape=[jax.ShapeDtypeStruct((G, T, Fs), BF16), jax.ShapeDtypeStruct((G, T, Fs), BF16)],
        sem=("parallel", "parallel"), args=(dx, wd, g, u), riders=riders)


def ffn_bwd_dh(dg, du, wg, wu, name, riders=(), part=(0, 1), prev=None):
    G, T, Fs = dg.shape
    D = wg.shape[1]
    tm = min(512, T // part[1])
    nblk = T // tm // part[1]
    off = part[0] * nblk

    def body(dg_ref, du_ref, wg_ref, wu_ref, *rest):
        o_ref, acc = rest[-2:]
        s = pl.program_id(1)

        @pl.when(s == 0)
        def _():
            acc[...] = jnp.zeros_like(acc)

        acc[...] += _dot_nt(dg_ref[...], wg_ref[...])
        acc[...] += _dot_nt(du_ref[...], wu_ref[...])

        @pl.when(s == G - 1)
        def _():
            o_ref[...] = acc[...]

    act = pl.BlockSpec((None, tm, Fs), lambda i, s: (s, i + off, 0))
    w_spec = pl.BlockSpec((None, D, Fs), lambda i, s: (s, 0, 0))
    extra = [] if prev is None else [prev]
    return _call(
        body, name=name, grid=(nblk, G),
        in_specs=[act, act, w_spec, w_spec] + [pl.BlockSpec(memory_space=pl.ANY)] * len(extra),
        out_specs=[pl.BlockSpec((tm, D), lambda i, s: (i + off, 0))],
        out_shape=[jax.ShapeDtypeStruct((T, D), F32)],
        scratch_shapes=[pltpu.VMEM((tm, D), F32)],
        sem=("parallel", "arbitrary"), args=(dg, du, wg, wu, *extra), riders=riders,
        aliases={4: 0} if extra else None)


def wgrad(a, b, name, *, groups, m, n, a_block, a_map, b_block, b_map, b_scale=None, tk=1024, riders=()):
    T = a.shape[-2]
    tk = min(tk, T)
    nk = T // tk

    def body(a_ref, b_ref, o_ref, acc):
        k = pl.program_id(2)

        @pl.when(k == 0)
        def _():
            acc[...] = jnp.zeros_like(acc)

        bv = b_ref[...]
        if b_scale is not None:
            bv = b_scale * bv
        acc[...] += _dot_tn(a_ref[...].astype(BF16), bv.astype(BF16))

        @pl.when(k == nk - 1)
        def _():
            o_ref[...] = acc[...].astype(BF16)

    return _call(
        body, name=name, grid=(groups, 2, nk),
        in_specs=[pl.BlockSpec(a_block(tk), a_map), pl.BlockSpec(b_block(tk), b_map)],
        out_specs=[pl.BlockSpec((None, None, m, n), lambda s, j, k: (j, s, 0, 0))],
        out_shape=[jax.ShapeDtypeStruct((2, groups, m, n), BF16)],
        scratch_shapes=[pltpu.VMEM((m, n), F32)],
        sem=("parallel", "parallel", "arbitrary"), args=(a, b), riders=riders)


def in_proj(h, w_in, name, riders=()):
    T, D = h.shape
    G, _, Ps = w_in.shape
    tm = min(512, T)

    def body(h_ref, w_ref, o_ref):
        o_ref[...] = _dot(h_ref[...], w_ref[...]).astype(BF16)

    return _call(
        body, name=name, grid=(G, T // tm),
        in_specs=[pl.BlockSpec((tm, D), lambda s, i: (i, 0)),
                  pl.BlockSpec((None, D, Ps), lambda s, i: (s, 0, 0))],
        out_specs=[pl.BlockSpec((tm, Ps), lambda s, i: (i, s))],
        out_shape=[jax.ShapeDtypeStruct((T, G * Ps), BF16)],
        sem=("parallel", "parallel"), args=(h, w_in), riders=riders)


def in_proj_bwd(dproj, w_in, name, riders=()):
    T = dproj.shape[0]
    G, D, Ps = w_in.shape
    tm = min(512, T)

    def body(d_ref, w_ref, o_ref, acc):
        s = pl.program_id(1)

        @pl.when(s == 0)
        def _():
            acc[...] = jnp.zeros_like(acc)

        acc[...] += _dot_nt(d_ref[...], w_ref[...])

        @pl.when(s == G - 1)
        def _():
            o_ref[...] = acc[...]

    return _call(
        body, name=name, grid=(T // tm, G),
        in_specs=[pl.BlockSpec((tm, Ps), lambda i, s: (i, s)),
                  pl.BlockSpec((None, D, Ps), lambda i, s: (s, 0, 0))],
        out_specs=[pl.BlockSpec((tm, D), lambda i, s: (i, 0))],
        out_shape=[jax.ShapeDtypeStruct((T, D), F32)],
        scratch_shapes=[pltpu.VMEM((tm, D), F32)],
        sem=("parallel", "arbitrary"), args=(dproj, w_in), riders=riders)


def out_proj(y, w_out, x, name):
    T, M = y.shape
    D = w_out.shape[1]
    tm = min(512, T)

    def body(y_ref, w_ref, x_ref, o_ref):
        o_ref[...] = x_ref[...] + _dot(y_ref[...], w_ref[...])

    return pl.pallas_call(
        body, name=name, grid=(T // tm,),
        in_specs=[pl.BlockSpec((tm, M), lambda i: (i, 0)), pl.BlockSpec((M, D), lambda i: (0, 0)),
                  pl.BlockSpec((tm, D), lambda i: (i, 0))],
        out_specs=pl.BlockSpec((tm, D), lambda i: (i, 0)),
        out_shape=jax.ShapeDtypeStruct((T, D), F32),
        compiler_params=_params("parallel"),
    )(y, w_out, x)


def out_norm(ya, yb, ga, gb, name):
    T, W = ya.shape
    tr = min(512, T)

    def body(a_ref, b_ref, ga_ref, gb_ref, o_ref):
        for src, gam, col in ((a_ref, ga_ref, 0), (b_ref, gb_ref, W)):
            v = src[...]
            r = lax.rsqrt(jnp.mean(v * v, axis=-1, keepdims=True) + EPS)
            o_ref[:, col:col + W] = ((v * r) * gam[...]).astype(BF16)

    row = pl.BlockSpec((tr, W), lambda i: (i, 0))
    vec = pl.BlockSpec((1, W), lambda i: (0, 0))
    return pl.pallas_call(
        body, name=name, grid=(T // tr,),
        in_specs=[row, row, vec, vec],
        out_specs=pl.BlockSpec((tr, 2 * W), lambda i: (i, 0)),
        out_shape=jax.ShapeDtypeStruct((T, 2 * W), BF16),
        compiler_params=_params("parallel"),
    )(ya, yb, ga, gb)


def out_proj_bwd(dx, w_out, ya, yb, ga, gb, name):
    T, D = dx.shape
    M = w_out.shape[0]
    W = M // 2
    tm = min(256, T)

    def body(dx_ref, w_ref, a_ref, b_ref, ga_ref, gb_ref, da_ref, db_ref, dga_ref, dgb_ref):
        dy = _dot_nt(dx_ref[...].astype(BF16), w_ref[...])

        @pl.when(pl.program_id(0) == 0)
        def _():
            dga_ref[...] = jnp.zeros_like(dga_ref)
            dgb_ref[...] = jnp.zeros_like(dgb_ref)

        d_a, dg_a = _rms_bwd(a_ref[...], ga_ref[...], dy[:, :W])
        d_b, dg_b = _rms_bwd(b_ref[...], gb_ref[...], dy[:, W:])
        da_ref[...] = d_a
        db_ref[...] = d_b
        dga_ref[...] += dg_a
        dgb_ref[...] += dg_b

    row = pl.BlockSpec((tm, W), lambda i: (i, 0))
    vec = pl.BlockSpec((1, W), lambda i: (0, 0))
    return pl.pallas_call(
        body, name=name, grid=(T // tm,),
        in_specs=[pl.BlockSpec((tm, D), lambda i: (i, 0)), pl.BlockSpec((M, D), lambda i: (0, 0)),
                  row, row, vec, vec],
        out_specs=[row, row, vec, vec],
        out_shape=[jax.ShapeDtypeStruct((T, W), F32), jax.ShapeDtypeStruct((T, W), F32),
                   jax.ShapeDtypeStruct((1, W), F32), jax.ShapeDtypeStruct((1, W), F32)],
        compiler_params=_params("arbitrary"),
    )(dx, w_out, ya, yb, ga, gb)


def final_loss(x, gamma, target, name):
    T, D = x.shape
    tr = min(256, T)

    def body(x_ref, g_ref, t_ref, dx_ref, dg_ref, loss_ref):
        xv = x_ref[...]
        gam = g_ref[...]
        r = lax.rsqrt(jnp.mean(xv * xv, axis=-1, keepdims=True) + EPS)
        err = (xv * r) * gam - t_ref[...]
        part = 0.5 * jnp.sum(jnp.mean(err * err, axis=-1, keepdims=True), axis=0, keepdims=True)
        dx, dgamma = _rms_bwd(xv, gam, err / D)
        dx_ref[...] = dx

        @pl.when(pl.program_id(0) == 0)
        def _():
            dg_ref[...] = jnp.zeros_like(dg_ref)
            loss_ref[...] = jnp.zeros_like(loss_ref)

        dg_ref[...] += dgamma
        loss_ref[...] += jnp.broadcast_to(part, loss_ref.shape)

    row = pl.BlockSpec((tr, D), lambda i: (i, 0))
    vec = pl.BlockSpec((1, D), lambda i: (0, 0))
    return pl.pallas_call(
        body, name=name, grid=(T // tr,),
        in_specs=[row, vec, row],
        out_specs=[row, vec, pl.BlockSpec((1, LANES), lambda i: (0, 0))],
        out_shape=[jax.ShapeDtypeStruct((T, D), F32), jax.ShapeDtypeStruct((1, D), F32),
                   jax.ShapeDtypeStruct((1, LANES), F32)],
        compiler_params=_params("arbitrary"),
    )(x, gamma, target)


def _lane_select(src_half, dst_halves):
    r = lax.broadcasted_iota(jnp.int32, (LANES, LANES), 0)
    c = lax.broadcasted_iota(jnp.int32, (LANES, LANES), 1)
    hit = ((r >> 6) == src_half) & ((r & (HEAD_DIM - 1)) == (c & (HEAD_DIM - 1)))
    ok = functools.reduce(jnp.logical_or, [(c >> 6) == d for d in dst_halves])
    return jnp.where(hit & ok, 1.0, 0.0).astype(BF16)


def _swa_slope(kvp, local_head):
    lo = 2.0 ** (-8.0 * (local_head + 1) / N_SWA_HEADS)
    hi = 2.0 ** (-8.0 * (8 + local_head + 1) / N_SWA_HEADS)
    return jnp.where(kvp == 0, jnp.float32(lo), jnp.float32(hi))


GQA = N_SWA_HEADS // N_SWA_KV


def _per_head_rows(values):
    return jnp.concatenate([jnp.zeros((SWA_BLOCK, 1), F32) + v for v in values], axis=0)


def _swa_probs(qblk, kwin, n, start, slope, sink):
    z = _dot_nt(qblk, kwin) * ATT_SCALE
    t = n * SWA_BLOCK + (lax.broadcasted_iota(jnp.int32, z.shape, 0) & (SWA_BLOCK - 1))
    s = start + lax.broadcasted_iota(jnp.int32, z.shape, 1)
    dist = t - s
    valid = (dist >= 0) & (dist < WINDOW)
    z = jnp.where(valid, z - slope * dist.astype(F32), NEG_BIG)
    m = jnp.maximum(jnp.max(z, axis=-1, keepdims=True), sink)
    e = jnp.exp(z - m)
    den = jnp.sum(e, axis=-1, keepdims=True) + jnp.exp(sink - m)
    return e / den, m, den


def swa_fwd(proj, sinks, b_loc, seq, name, riders=()):
    T = proj.shape[0]
    nblk = seq // SWA_BLOCK

    def body(sink_ref, q_ref, k_ref, v_ref, o_ref, qs, vv):
        kvp = pl.program_id(1)
        lane = lax.broadcasted_iota(jnp.int32, (SWA_BLOCK, LANES), 1)
        for kvh in range(2):
            vv[...] = _dot(v_ref[...], _lane_select(kvh, (0, 1))).astype(BF16)
            for g in range(GQA):
                cols = slice(LANES * (2 * kvh + g // 2), LANES * (2 * kvh + g // 2 + 1))
                qs[g] = _dot(q_ref[:, cols], _lane_select(g % 2, (kvh,))).astype(BF16)
            slope = _per_head_rows([_swa_slope(kvp, GQA * kvh + g) for g in range(GQA)])
            sink = _per_head_rows([sink_ref[0, 8 * kvp + GQA * kvh + g] for g in range(GQA)])

            def blk(n, carry):
                start = pl.multiple_of(jnp.maximum(n - 1, 0) * SWA_BLOCK, SWA_BLOCK)
                rows = pl.ds(pl.multiple_of(n * SWA_BLOCK, SWA_BLOCK), SWA_BLOCK)
                win = pl.ds(start, 2 * SWA_BLOCK)
                qblk = jnp.concatenate([qs[g, rows, :] for g in range(GQA)], axis=0)
                p, _, _ = _swa_probs(qblk, k_ref[win, :], n, start, slope, sink)
                out = _dot(p.astype(BF16), vv[win, :])
                for pair in range(GQA // 2):
                    cols = slice(LANES * (2 * kvh + pair), LANES * (2 * kvh + pair + 1))
                    first = out[2 * pair * SWA_BLOCK:(2 * pair + 1) * SWA_BLOCK]
                    second = out[(2 * pair + 1) * SWA_BLOCK:(2 * pair + 2) * SWA_BLOCK]
                    o_ref[rows, cols] = jnp.where(lane < HEAD_DIM, first, second)
                return carry

            lax.fori_loop(0, nblk, blk, 0)

    return _call(
        body, name=name, grid=(b_loc, 2),
        in_specs=[pl.BlockSpec(memory_space=pltpu.SMEM),
                  pl.BlockSpec((seq, 4 * LANES), lambda b, p: (b, p)),
                  pl.BlockSpec((seq, LANES), lambda b, p: (b, 8 + p)),
                  pl.BlockSpec((seq, LANES), lambda b, p: (b, 10 + p))],
        out_specs=[pl.BlockSpec((seq, 4 * LANES), lambda b, p: (b, p))],
        out_shape=[jax.ShapeDtypeStruct((T, N_SWA_HEADS * HEAD_DIM), F32)],
        scratch_shapes=[pltpu.VMEM((GQA, seq, LANES), BF16), pltpu.VMEM((seq, LANES), BF16)],
        sem=("parallel", "parallel"), args=(sinks, proj, proj, proj), riders=riders)


def swa_bwd(proj, sinks, dya, b_loc, seq, name, riders=()):
    T = proj.shape[0]
    nblk = seq // SWA_BLOCK

    def body(sink_ref, q_ref, k_ref, v_ref, do_ref, out_ref, ds_ref, stage, qs, dos, kk, dk_acc, dv_acc):
        piece = pl.program_id(2)

        @pl.when(piece == 0)
        def _():
            compute(sink_ref, q_ref, k_ref, v_ref, do_ref, stage, ds_ref, qs, dos, kk, dk_acc, dv_acc)

        out_ref[...] = stage[piece]

    def compute(sink_ref, q_ref, k_ref, v_ref, do_ref, stage, ds_ref, qs, dos, kk, dk_acc, dv_acc):
        kvp = pl.program_id(0)
        b = pl.program_id(1)
        lane = lax.broadcasted_iota(jnp.int32, (SWA_BLOCK, LANES), 1)

        @pl.when(b == 0)
        def _():
            ds_ref[...] = jnp.zeros_like(ds_ref)

        dk_acc[...] = jnp.zeros_like(dk_acc)
        dv_acc[...] = jnp.zeros_like(dv_acc)
        for kvh in range(2):
            kk[...] = _dot(k_ref[...], _lane_select(kvh, (0, 1))).astype(BF16)
            for g in range(GQA):
                cols = slice(LANES * (2 * kvh + g // 2), LANES * (2 * kvh + g // 2 + 1))
                move = _lane_select(g % 2, (kvh,))
                qs[g] = _dot(q_ref[:, cols], move).astype(BF16)
                dos[g] = _dot(do_ref[:, cols].astype(BF16), move).astype(BF16)
            slope = _per_head_rows([_swa_slope(kvp, GQA * kvh + g) for g in range(GQA)])
            sink = _per_head_rows([sink_ref[0, 8 * kvp + GQA * kvh + g] for g in range(GQA)])

            def blk(n, carry):
                start = pl.multiple_of(jnp.maximum(n - 1, 0) * SWA_BLOCK, SWA_BLOCK)
                rows = pl.ds(pl.multiple_of(n * SWA_BLOCK, SWA_BLOCK), SWA_BLOCK)
                win = pl.ds(start, 2 * SWA_BLOCK)
                qblk = jnp.concatenate([qs[g, rows, :] for g in range(GQA)], axis=0)
                doblk = jnp.concatenate([dos[g, rows, :] for g in range(GQA)], axis=0)
                p, m, den = _swa_probs(qblk, k_ref[win, :], n, start, slope, sink)
                dp = _dot_nt(doblk, v_ref[win, :])
                delta = jnp.sum(p * dp, axis=-1, keepdims=True)
                leak = (jnp.exp(sink - m) / den) * delta
                for g in range(GQA):
                    head = GQA * kvh + g
                    ds_ref[head:head + 1, :] -= jnp.broadcast_to(
                        jnp.sum(leak[g * SWA_BLOCK:(g + 1) * SWA_BLOCK], axis=0, keepdims=True), (1, LANES))
                dsc = ((p * (dp - delta)) * ATT_SCALE).astype(BF16)
                dv_acc[win, :] += _dot_tn(p.astype(BF16), doblk)
                dk_acc[win, :] += _dot_tn(dsc, qblk)
                dq = _dot(dsc, kk[win, :])
                for pair in range(GQA // 2):
                    first = dq[2 * pair * SWA_BLOCK:(2 * pair + 1) * SWA_BLOCK]
                    second = dq[(2 * pair + 1) * SWA_BLOCK:(2 * pair + 2) * SWA_BLOCK]
                    stage[2 * kvh + pair, rows, :] = jnp.where(lane < HEAD_DIM, first, second).astype(BF16)
                return carry

            lax.fori_loop(0, nblk, blk, 0)
        stage[4] = dk_acc[...].astype(BF16)
        stage[5] = dv_acc[...].astype(BF16)

    def out_tile(p, b, j):
        return b, jnp.where(j < 4, 4 * p + j, jnp.where(j == 4, 8 + p, 10 + p))

    return _call(
        body, name=name, grid=(2, b_loc, 6),
        in_specs=[pl.BlockSpec(memory_space=pltpu.SMEM),
                  pl.BlockSpec((seq, 4 * LANES), lambda p, b, j: (b, p)),
                  pl.BlockSpec((seq, LANES), lambda p, b, j: (b, 8 + p)),
                  pl.BlockSpec((seq, LANES), lambda p, b, j: (b, 10 + p)),
                  pl.BlockSpec((seq, 4 * LANES), lambda p, b, j: (b, p))],
        out_specs=[pl.BlockSpec((seq, LANES), out_tile),
                   pl.BlockSpec((8, LANES), lambda p, b, j: (p, 0))],
        out_shape=[jax.ShapeDtypeStruct((T, proj.shape[1]), BF16),
                   jax.ShapeDtypeStruct((N_SWA_HEADS, LANES), F32)],
        scratch_shapes=[pltpu.VMEM((6, seq, LANES), BF16),
                        pltpu.VMEM((GQA, seq, LANES), BF16), pltpu.VMEM((GQA, seq, LANES), BF16),
                        pltpu.VMEM((seq, LANES), BF16),
                        pltpu.VMEM((seq, LANES), F32), pltpu.VMEM((seq, LANES), F32)],
        sem=("arbitrary", "arbitrary", "arbitrary"), args=(sinks, proj, proj, proj, dya), riders=riders)


def _softplus(z):
    return jnp.maximum(z, 0.0) + jnp.log(1.0 + jnp.exp(-jnp.abs(z)))


def _tri(n, rel):
    r = lax.broadcasted_iota(jnp.int32, (n, n), 0)
    c = lax.broadcasted_iota(jnp.int32, (n, n), 1)
    return jnp.where(rel(r, c), 1.0, 0.0).astype(BF16)


def sb_fwd(proj, b_loc, seq, name, riders=()):
    T = proj.shape[0]
    bq = min(SB_BLOCK, seq)
    nq = seq // bq
    npair = N_SB_HEADS // 2

    def body(q_ref, k_ref, v_ref, o_ref, lt_ref):
        i = pl.program_id(2)
        lane = lax.broadcasted_iota(jnp.int32, (bq, LANES), 1)
        after = _tri(bq, lambda r, c: r > c)
        row = lax.broadcasted_iota(jnp.int32, (bq, bq), 0)
        col = lax.broadcasted_iota(jnp.int32, (bq, bq), 1)
        qv = q_ref[...] * jnp.asarray(ATT_SCALE, BF16)
        blank = jnp.zeros((bq, LANES), BF16)
        q2 = jnp.concatenate([jnp.where((lane >> 6) == h, qv, blank) for h in range(2)], axis=0)


        def tile(j, state, mask):
            rows = pl.ds(pl.multiple_of(j * bq, bq), bq)
            kj, vj = k_ref[rows, :], v_ref[rows, :]
            zz = _dot_nt(q2, kj)
            log_beta, totals, pieces = [], [], []
            for h in range(2):
                z = zz[h * bq:(h + 1) * bq]
                sp = _softplus(z)
                lg = -sp if mask is None else jnp.where(mask, -sp, 0.0)
                hi = lg.astype(BF16)
                pieces += [hi, (lg - hi.astype(F32)).astype(BF16)]
                log_beta.append(z - sp if mask is None else jnp.where(mask, z - sp, NEG_BIG))
                totals.append(jnp.sum(lg, axis=-1, keepdims=True))
            cc = _dot(jnp.concatenate(pieces, axis=0), after)
            probs = []
            for h in range(2):
                inside = cc[2 * h * bq:(2 * h + 1) * bq] + cc[(2 * h + 1) * bq:(2 * h + 2) * bq]
                probs.append(jnp.exp(log_beta[h] + (inside + state[2 * h])).astype(BF16))
            pv = _dot(jnp.concatenate(probs, axis=0), vj)
            return (state[0] + totals[0], state[1] + pv[:bq], state[2] + totals[1], state[3] + pv[bq:])

        zero = (jnp.zeros((bq, 1), F32), jnp.zeros((bq, LANES), F32))
        state = tile(i, zero + zero, col < row)
        state = lax.fori_loop(0, i, lambda jj, st: tile(i - 1 - jj, st, None), state)
        o_ref[...] = jnp.where(lane < HEAD_DIM, state[1], state[3])
        lt_ref[...] = jnp.where(lane < HEAD_DIM, state[0], state[2])

    return _call(
        body, name=name, grid=(b_loc, npair, nq),
        in_specs=[pl.BlockSpec((bq, LANES), lambda b, p, i: (b * nq + i, 12 + p)),
                  pl.BlockSpec((seq, LANES), lambda b, p, i: (b, 20 + p)),
                  pl.BlockSpec((seq, LANES), lambda b, p, i: (b, 28 + p))],
        out_specs=[pl.BlockSpec((bq, LANES), lambda b, p, i: (b * nq + i, p)),
                   pl.BlockSpec((None, bq, LANES), lambda b, p, i: (b * npair + p, i, 0))],
        out_shape=[jax.ShapeDtypeStruct((T, N_SB_HEADS * HEAD_DIM), F32),
                   jax.ShapeDtypeStruct((b_loc * npair, seq, LANES), F32)],
        sem=("parallel", "parallel", "parallel"), args=(proj, proj, proj), riders=riders)


def sb_bwd(proj, dyb, ltot, dproj, b_loc, seq, name, riders=()):
    T = proj.shape[0]
    bq = min(SB_BLOCK, seq)
    nq = seq // bq
    npair = N_SB_HEADS // 2

    def body(q_ref, k_ref, v_ref, do_ref, lt_ref, _, out_ref, stage, dk_acc, dv_acc):
        piece = pl.program_id(2)

        @pl.when(piece == 0)
        def _():
            compute(q_ref, k_ref, v_ref, do_ref, lt_ref, stage, dk_acc, dv_acc)

        out_ref[...] = stage[piece]

    def compute(q_ref, k_ref, v_ref, do_ref, lt_ref, stage, dk_acc, dv_acc):
        lane = lax.broadcasted_iota(jnp.int32, (bq, LANES), 1)
        upto = _tri(bq, lambda r, c: r <= c)
        before = _tri(bq, lambda r, c: r < c)
        row = lax.broadcasted_iota(jnp.int32, (bq, bq), 0)
        col = lax.broadcasted_iota(jnp.int32, (bq, bq), 1)
        dk_acc[...] = jnp.zeros_like(dk_acc)
        dv_acc[...] = jnp.zeros_like(dv_acc)
        scale = jnp.asarray(ATT_SCALE, BF16)
        heads = [(lane >> 6) == h for h in range(2)]
        blank = jnp.zeros((bq, LANES), BF16)

        def qblock(i, carry):
            qrows = pl.ds(pl.multiple_of(i * bq, bq), bq)
            qv = q_ref[qrows, :] * scale
            dov = do_ref[qrows, :].astype(BF16)
            ltv = lt_ref[qrows, :]
            q2 = jnp.concatenate([jnp.where(m, qv, blank) for m in heads], axis=0)
            do2 = jnp.concatenate([jnp.where(m, dov, blank) for m in heads], axis=0)
            totals = [jnp.sum(jnp.where(lane == h * HEAD_DIM, ltv, 0.0), axis=-1, keepdims=True) for h in range(2)]

            def split2(xs):
                out = []
                for x in xs:
                    hi = x.astype(BF16)
                    out += [hi, (x - hi.astype(F32)).astype(BF16)]
                return jnp.concatenate(out, axis=0)

            def join2(cc, h):
                return cc[2 * h * bq:(2 * h + 1) * bq] + cc[(2 * h + 1) * bq:(2 * h + 2) * bq]

            def tile(j, state, mask):
                krows = pl.ds(pl.multiple_of(j * bq, bq), bq)
                kj, vj = k_ref[krows, :], v_ref[krows, :]
                kv = kj * scale
                k2 = jnp.concatenate([jnp.where(m, kv, blank) for m in heads], axis=0)
                zz = _dot_nt(q2, kj)
                dd = _dot_nt(do2, vj)
                lgs, log_betas = [], []
                for h in range(2):
                    z = zz[h * bq:(h + 1) * bq]
                    sp = _softplus(z)
                    lgs.append(-sp if mask is None else jnp.where(mask, -sp, 0.0))
                    log_betas.append(z - sp)
                cl = _dot(split2(lgs), upto)
                probs, gs = [], []
                for h in range(2):
                    a = jnp.exp(log_betas[h] + (totals[h] - (join2(cl, h) + state[2 * h])))
                    if mask is not None:
                        a = jnp.where(mask, a, 0.0)
                    probs.append(a.astype(BF16))
                    gs.append(a * dd[h * bq:(h + 1) * bq])
                cg = _dot(split2(gs), before)
                dzs = []
                for h in range(2):
                    beta = jnp.exp(log_betas[h])
                    dz = gs[h] * (1.0 - beta) - beta * (join2(cg, h) + state[2 * h + 1])
                    if mask is not None:
                        dz = jnp.where(mask, dz, 0.0)
                    dzs.append(dz.astype(BF16))
                dv_acc[krows, :] += _dot_tn(jnp.concatenate(probs, axis=0), do2)
                dk_acc[krows, :] += _dot_tn(jnp.concatenate(dzs, axis=0), q2)
                dq = state[4] + _dot(jnp.concatenate(dzs, axis=1), k2)
                new = []
                for h in range(2):
                    new += [state[2 * h] + jnp.sum(lgs[h], axis=-1, keepdims=True),
                            state[2 * h + 1] + jnp.sum(gs[h], axis=-1, keepdims=True)]
                return (*new, dq)

            zero = jnp.zeros((bq, 1), F32)
            state = lax.fori_loop(0, i, lambda j, st: tile(j, st, None),
                                  (zero, zero, zero, zero, jnp.zeros((bq, LANES), F32)))
            state = tile(i, state, col < row)
            stage[0, qrows, :] = state[4].astype(BF16)
            return carry

        lax.fori_loop(0, nq, qblock, 0)
        stage[1] = dk_acc[...].astype(BF16)
        stage[2] = dv_acc[...].astype(BF16)

    return _call(
        body, name=name, grid=(b_loc, npair, 3),
        in_specs=[pl.BlockSpec((seq, LANES), lambda b, p, j: (b, 12 + p)),
                  pl.BlockSpec((seq, LANES), lambda b, p, j: (b, 20 + p)),
                  pl.BlockSpec((seq, LANES), lambda b, p, j: (b, 28 + p)),
                  pl.BlockSpec((seq, LANES), lambda b, p, j: (b, p)),
                  pl.BlockSpec((None, seq, LANES), lambda b, p, j: (b * npair + p, 0, 0)),
                  pl.BlockSpec(memory_space=pl.ANY)],
        out_specs=[pl.BlockSpec((seq, LANES), lambda b, p, j: (b, 12 + p + npair * j))],
        out_shape=[jax.ShapeDtypeStruct(dproj.shape, BF16)],
        scratch_shapes=[pltpu.VMEM((3, seq, LANES), BF16)] + [pltpu.VMEM((seq, LANES), F32)] * 2,
        sem=("arbitrary", "arbitrary", "arbitrary"), args=(proj, proj, proj, dyb, ltot, dproj), riders=riders,
        aliases={5: 0})


def allreduce_small(v, name):
    R = v.shape[0]
    ndev = 8

    def body(v_ref, o_ref, land, send_sems, recv_sems):
        x, y, c = _place()
        me = 4 * x + 2 * y + c
        land[me] = v_ref[...]
        started = []
        for f in range(1, ndev):
            fx, fy, fc = (f >> 2) & 1, (f >> 1) & 1, f & 1
            peer = ((1 - x) if fx else x, (1 - y) if fy else y, (1 - c) if fc else c)
            cp = pltpu.make_async_remote_copy(
                src_ref=v_ref, dst_ref=land.at[me], send_sem=send_sems.at[f - 1], recv_sem=recv_sems.at[f - 1],
                device_id=peer, device_id_type=MESH)
            cp.start()
            started.append(cp)
        for cp in started:
            cp.wait()
        total = land[0]
        for d in range(1, ndev):
            total = total + land[d]
        o_ref[...] = total

    return pl.pallas_call(
        body, name=name,
        in_specs=[VMEM_WHOLE], out_specs=VMEM_WHOLE,
        out_shape=jax.ShapeDtypeStruct(v.shape, F32),
        scratch_shapes=[pltpu.VMEM((ndev, R, LANES), F32),
                        pltpu.SemaphoreType.DMA((ndev - 1,)), pltpu.SemaphoreType.DMA((ndev - 1,))],
    )(v)


def sum_cores(grads, theirs, name):
    _, _, r, c = grads.shape
    tr, tc = _tiles(r, c, 1 << 18)
    core = lax.axis_index("c").astype(jnp.int32).reshape(1)

    def body(core_ref, g_ref, t_ref, o_ref):
        o_ref[...] = (g_ref[...].astype(F32) + t_ref[...].astype(F32)).astype(BF16)

    blk = pl.BlockSpec((N_CHIPS, tr, tc), lambda i, j, core_ref: (0, i, j))
    return pl.pallas_call(
        body, name=name,
        grid_spec=pltpu.PrefetchScalarGridSpec(
            num_scalar_prefetch=1, grid=(r // tr, c // tc),
            in_specs=[pl.BlockSpec((None, N_CHIPS, tr, tc), lambda i, j, core_ref: (core_ref[0], 0, i, j)), blk],
            out_specs=blk),
        out_shape=jax.ShapeDtypeStruct((N_CHIPS, r, c), BF16),
        compiler_params=_params("parallel", "parallel"),
    )(core, grads, theirs)


def sum_chips(halves, landed, name):
    _, r, c = landed.shape
    tr, tc = _tiles(r, c, 1 << 18)
    chip = (2 * lax.axis_index("x") + lax.axis_index("y")).astype(jnp.int32).reshape(1)

    def body(chip_ref, own_ref, p_ref, o_ref):
        total = own_ref[...].astype(F32)
        for j in range(N_CHIPS - 1):
            total = total + p_ref[j].astype(F32)
        o_ref[...] = total

    return pl.pallas_call(
        body, name=name,
        grid_spec=pltpu.PrefetchScalarGridSpec(
            num_scalar_prefetch=1, grid=(r // tr, c // tc),
            in_specs=[pl.BlockSpec((None, tr, tc), lambda i, j, chip_ref: (chip_ref[0], i, j)),
                      pl.BlockSpec((N_CHIPS - 1, tr, tc), lambda i, j, chip_ref: (0, i, j))],
            out_specs=pl.BlockSpec((tr, tc), lambda i, j, chip_ref: (i, j))),
        out_shape=jax.ShapeDtypeStruct((r, c), F32),
        compiler_params=_params("parallel", "parallel"),
    )(chip, halves, landed)


def _adamw_math(w, g, m, v):
    m = ADAM_B1 * m + (1.0 - ADAM_B1) * g
    v = ADAM_B2 * v + (1.0 - ADAM_B2) * (g * g)
    m_hat = m / (1.0 - ADAM_B1 ** ADAM_STEP)
    v_hat = v / (1.0 - ADAM_B2 ** ADAM_STEP)
    delta = -ADAM_LR * (m_hat / (jnp.sqrt(v_hat) + ADAM_EPS) + ADAM_WD * w)
    return delta, m, v


def adamw(w, m, v, g, name):
    r, c = w.shape
    tr, tc = _tiles(r, c, 1 << 18, multiple=8)

    def body(w_ref, m_ref, v_ref, gin_ref, g_ref, d_ref, nm_ref, nv_ref):
        g = gin_ref[...]
        delta, nm, nv = _adamw_math(w_ref[...], g, m_ref[...], v_ref[...])
        g_ref[...] = g
        d_ref[...] = delta
        nm_ref[...] = nm
        nv_ref[...] = nv

    blk = pl.BlockSpec((tr, tc), lambda i, j: (i, j))
    shape = jax.ShapeDtypeStruct((r, c), F32)
    return pl.pallas_call(
        body, name=name, grid=(r // tr, c // tc),
        in_specs=[blk] * 4, out_specs=[blk] * 4, out_shape=[shape] * 4,
        compiler_params=_params("parallel", "parallel"),
    )(w, m, v, g)


def adamw_halves(w, m, v, mine, theirs, name, col_halves=False):
    r, c = w.shape
    hr, hc = (r, c // 2) if col_halves else (r // 2, c)
    tr, tc = _tiles(hr, hc, 1 << 18, multiple=8)
    g_cols = mine.shape[-1]
    padded = g_cols != hc
    if padded and tc != hc:
        raise ValueError("a gradient with padded columns is read in full-width row blocks")
    per_half = hc // tc if col_halves else hr // tr
    core = lax.axis_index("c").astype(jnp.int32).reshape(1)

    def body(core_ref, w_ref, m_ref, v_ref, mine_ref, theirs_ref, g_ref, d_ref, nm_ref, nv_ref):
        half = pl.program_id(1 if col_halves else 0) // per_half
        cols = slice(0, hc) if padded else slice(None)
        g = jnp.where(half == core_ref[0], mine_ref[:, cols], theirs_ref[:, cols])
        delta, nm, nv = _adamw_math(w_ref[...], g, m_ref[...], v_ref[...])
        g_ref[...] = g
        d_ref[...] = delta
        nm_ref[...] = nm
        nv_ref[...] = nv

    blk = pl.BlockSpec((tr, tc), lambda i, j, core_ref: (i, j))
    if col_halves:
        g_spec = pl.BlockSpec((tr, tc), lambda i, j, core_ref: (i, j % per_half))
    else:
        g_spec = pl.BlockSpec((tr, g_cols if padded else tc), lambda i, j, core_ref: (i % per_half, j))
    shape = jax.ShapeDtypeStruct((r, c), F32)
    return pl.pallas_call(
        body, name=name,
        grid_spec=pltpu.PrefetchScalarGridSpec(
            num_scalar_prefetch=1, grid=(r // tr, c // tc),
            in_specs=[blk, blk, blk, g_spec, g_spec], out_specs=[blk] * 4),
        out_shape=[shape] * 4,
        compiler_params=_params("parallel", "parallel"),
    )(core, w, m, v, mine, theirs)


MATRICES = ("ffn1_w_gate", "ffn1_w_up", "ffn1_w_down", "w_in", "w_out", "ffn2_w_gate", "ffn2_w_up", "ffn2_w_down")
VECTORS = ("ffn1_norm", "mix_norm", "swa_sinks", "swa_out_norm", "sb_out_norm", "ffn2_norm", "final_norm")
WEIGHTS = ("ffn1_norm", "ffn1_w_gate", "ffn1_w_up", "ffn1_w_down", "mix_norm", "w_in", "swa_sinks", "swa_out_norm",
           "sb_out_norm", "w_out", "ffn2_norm", "ffn2_w_gate", "ffn2_w_up", "ffn2_w_down", "final_norm")


def _pack_rows(vec):
    flat = vec.reshape(-1).astype(F32)
    rows = -(-flat.shape[0] // LANES)
    rows8 = -(-rows // 8) * 8
    flat = jnp.pad(flat, (0, rows8 * LANES - flat.shape[0]))
    return flat.reshape(rows8, LANES)


class GradPipe:
    def __init__(self, name, grads):
        self.name, self.step, self.value = name, 0, grads

    def rider(self):
        make = (half_swap_rider, scatter_rider, final_swap_rider)[self.step]
        return make([self.value])

    def landed(self, outs):
        if self.step == 0:
            self.value = sum_cores(self.value, outs[0], "sum_cores_" + self.name)
        elif self.step == 1:
            self.value = sum_chips(self.value, outs[0], "sum_chips_" + self.name)
        else:
            self.value = (self.value, outs[0])
        self.step += 1


def _hosted(fn, *args, advance=(), **kw):
    outs, routs = fn(*args, riders=[p.rider() for p in advance], **kw)
    for p, ro in zip(advance, routs):
        p.landed(ro)
    return outs


def kernel(x, ffn1_norm, ffn1_w_gate, ffn1_w_up, ffn1_w_down, mix_norm, w_in, swa_sinks, swa_out_norm, sb_out_norm, w_out, ffn2_norm, ffn2_w_gate, ffn2_w_up, ffn2_w_down, final_norm, loss_target, m_ffn1_norm, m_ffn1_w_gate, m_ffn1_w_up, m_ffn1_w_down, m_mix_norm, m_w_in, m_swa_sinks, m_swa_out_norm, m_sb_out_norm, m_w_out, m_ffn2_norm, m_ffn2_w_gate, m_ffn2_w_up, m_ffn2_w_down, m_final_norm, v_ffn1_norm, v_ffn1_w_gate, v_ffn1_w_up, v_ffn1_w_down, v_mix_norm, v_w_in, v_swa_sinks, v_swa_out_norm, v_sb_out_norm, v_w_out, v_ffn2_norm, v_ffn2_w_gate, v_ffn2_w_up, v_ffn2_w_down, v_final_norm):
    given = dict(locals())
    b_loc, seq, D = x.shape
    T = b_loc * seq
    x0 = x.reshape(T, D)
    target = loss_target.reshape(T, D)
    final_g = final_norm.reshape(1, D)

    gathered = {}

    def gather(*names):
        shards = []
        for nm in names:
            w = given[nm][0].astype(BF16)
            hidden_axis = 0 if nm.endswith("w_down") else 1 if nm.startswith("ffn") else None
            if hidden_axis is not None:
                pad = -w.shape[hidden_axis] % LANES
                w = jnp.pad(w, [(0, pad) if a == hidden_axis else (0, 0) for a in range(2)])
            shards.append(w.reshape(2, w.shape[0] // 2, w.shape[1]))
        return names, gather_rider(shards)

    def hosting(fn, *args, fetch):
        names, rider = fetch
        outs, landed = fn(*args, riders=[rider])
        for nm, o in zip(names, landed[0]):
            gathered[nm] = o.reshape(N_CHIPS, 2 * o.shape[2], o.shape[3])
        return outs

    hosting(lambda riders: ([], exchange_now(riders, "gather_ffn1_gate")), fetch=gather("ffn1_w_gate"))

    h1 = rmsnorm_cast(x0, ffn1_norm, "ffn1_norm_fwd")
    (g1,) = hosting(ffn_gate, h1, gathered["ffn1_w_gate"], "ffn1_gate", fetch=gather("ffn1_w_up"))
    u1, a1 = hosting(ffn_up_act, h1, gathered["ffn1_w_up"], g1, "ffn1_up_act", fetch=gather("ffn1_w_down"))
    (x1,) = hosting(ffn_down, a1, gathered["ffn1_w_down"], x0, "ffn1_down", fetch=gather("w_in"))
    h2 = rmsnorm_cast(x1, mix_norm, "mix_norm_fwd")
    (proj,) = hosting(in_proj, h2, gathered["w_in"], "in_proj", fetch=gather("w_out"))
    w_out_all = gathered["w_out"].reshape(-1, D)
    (ya,) = hosting(swa_fwd, proj, swa_sinks, b_loc, seq, "swa_fwd", fetch=gather("ffn2_w_gate"))
    yb, ltot = hosting(sb_fwd, proj, b_loc, seq, "sb_fwd", fetch=gather("ffn2_w_up", "ffn2_w_down"))
    y = out_norm(ya, yb, swa_out_norm, sb_out_norm, "out_norm_fwd")
    x2 = out_proj(y, w_out_all, x1, "out_proj")
    h3 = rmsnorm_cast(x2, ffn2_norm, "ffn2_norm_fwd")
    (g2, u2, a2), _ = ffn_up(h3, gathered["ffn2_w_gate"], gathered["ffn2_w_up"], "ffn2_up")
    (x3,), _ = ffn_down(a2, gathered["ffn2_w_down"], x2, "ffn2_down")
    dx3, d_final, loss_part = final_loss(x3, final_g, target, "final_loss")

    Fs = gathered["ffn1_w_gate"].shape[2]
    Ps = gathered["w_in"].shape[2]
    Os = w_out_all.shape[0] // N_CHIPS
    pipes = {}

    def down_grad(tag, act, dx, advance):
        (grads,) = _hosted(wgrad, act, dx, tag + "_dwd", groups=N_CHIPS, m=Fs, n=D // 2,
                           a_block=lambda tk: (None, tk, Fs), a_map=lambda s, j, k: (s, k, 0),
                           b_block=lambda tk: (tk, D // 2), b_map=lambda s, j, k: (k, j), b_scale=0.5,
                           advance=advance)
        return grads

    def up_grad(name, h, d_act, advance):
        (grads,) = _hosted(wgrad, h, d_act, name, groups=N_CHIPS, m=D // 2, n=Fs,
                           a_block=lambda tk: (tk, D // 2), a_map=lambda s, j, k: (k, j),
                           b_block=lambda tk: (None, tk, Fs), b_map=lambda s, j, k: (s, k, 0), advance=advance)
        return grads

    (dg2, du2), _ = ffn_bwd_act(dx3, gathered["ffn2_w_down"], g2, u2, "ffn2_bwd_act")
    pipes["ffn2_w_down"] = p_d2 = GradPipe("ffn2_w_down", down_grad("ffn2", a2, dx3, []))
    pipes["ffn2_w_gate"] = p_g2 = GradPipe("ffn2_w_gate", up_grad("ffn2_dwg", h3, dg2, [p_d2]))
    pipes["ffn2_w_up"] = p_u2 = GradPipe("ffn2_w_up", up_grad("ffn2_dwu", h3, du2, [p_d2, p_g2]))
    (dh3,) = _hosted(ffn_bwd_dh, dg2, du2, gathered["ffn2_w_gate"], gathered["ffn2_w_up"], "ffn2_bwd_dh",
                     advance=[p_d2, p_g2, p_u2])
    (dx2, d_ffn2_norm), _ = rmsnorm_bwd_add(x2, ffn2_norm, dh3, dx3, "ffn2_norm_bwd")

    (d_w_out,) = _hosted(wgrad, y, dx2, "dw_out", groups=N_CHIPS, m=Os // 2, n=D,
                         a_block=lambda tk: (tk, Os // 2), a_map=lambda s, j, k: (k, 2 * s + j),
                         b_block=lambda tk: (tk, D), b_map=lambda s, j, k: (k, 0), advance=[p_g2])
    pipes["w_out"] = p_out = GradPipe("w_out", d_w_out)
    dya, dyb, d_swa_norm, d_sb_norm = out_proj_bwd(dx2, w_out_all, ya, yb, swa_out_norm, sb_out_norm, "out_proj_bwd")
    dproj, d_sinks = _hosted(swa_bwd, proj, swa_sinks, dya, b_loc, seq, "swa_bwd", advance=[p_u2, p_out])
    (dproj,) = _hosted(sb_bwd, proj, dyb, ltot, dproj, b_loc, seq, "sb_bwd", advance=[p_u2, p_out])
    (d_w_in,) = _hosted(wgrad, h2, dproj, "dw_in", groups=N_CHIPS, m=D // 2, n=Ps,
                        a_block=lambda tk: (tk, D // 2), a_map=lambda s, j, k: (k, j),
                        b_block=lambda tk: (tk, Ps), b_map=lambda s, j, k: (k, s), advance=[p_out])
    pipes["w_in"] = p_in = GradPipe("w_in", d_w_in)
    (dh2,) = _hosted(in_proj_bwd, dproj, gathered["w_in"], "in_proj_bwd", advance=[p_in])
    (dx1, d_mix_norm), _ = rmsnorm_bwd_add(x1, mix_norm, dh2, dx2, "mix_norm_bwd")

    pipes["ffn1_w_down"] = p_d1 = GradPipe("ffn1_w_down", down_grad("ffn1", a1, dx1, [p_in]))
    dg1, du1 = _hosted(ffn_bwd_act, dx1, gathered["ffn1_w_down"], g1, u1, "ffn1_bwd_act", advance=[p_in, p_d1])
    pipes["ffn1_w_gate"] = p_g1 = GradPipe("ffn1_w_gate", up_grad("ffn1_dwg", h1, dg1, [p_d1]))
    pipes["ffn1_w_up"] = p_u1 = GradPipe("ffn1_w_up", up_grad("ffn1_dwu", h1, du1, [p_d1, p_g1]))
    (dh1,) = _hosted(ffn_bwd_dh, dg1, du1, gathered["ffn1_w_gate"], gathered["ffn1_w_up"], "ffn1_bwd_dh_a",
                     part=(0, 2), advance=[p_g1, p_u1])
    (dh1,) = _hosted(ffn_bwd_dh, dg1, du1, gathered["ffn1_w_gate"], gathered["ffn1_w_up"], "ffn1_bwd_dh_b",
                     part=(1, 2), prev=dh1, advance=[p_g1, p_u1])
    (dx0, d_ffn1_norm), _ = rmsnorm_bwd_add(x0, ffn1_norm, dh1, dx1, "ffn1_norm_bwd")
    (last,) = exchange_now([p_u1.rider()], "swap_ffn1_w_up")
    p_u1.landed(last)

    vec_grads = dict(ffn1_norm=d_ffn1_norm, mix_norm=d_mix_norm, swa_sinks=d_sinks[:, 0], swa_out_norm=d_swa_norm,
                     sb_out_norm=d_sb_norm, ffn2_norm=d_ffn2_norm, final_norm=d_final)
    packed = [_pack_rows(vec_grads[nm]) for nm in VECTORS] + [_pack_rows(loss_part[0, :1])]
    offsets = [0]
    for p in packed:
        offsets.append(offsets[-1] + p.shape[0])
    reduced = allreduce_small(jnp.concatenate(packed, axis=0), "reduce_vectors")
    loss = reduced[offsets[len(VECTORS)], 0]

    out = {}
    for nm in MATRICES:
        shape = given[nm].shape
        mine, theirs = pipes[nm].value
        res = adamw_halves(given[nm][0], given["m_" + nm][0], given["v_" + nm][0], mine, theirs, "adamw_" + nm,
                           col_halves=nm.endswith("w_down"))
        out[nm] = [r.reshape(shape) for r in res]
    w_rows = jnp.concatenate([_pack_rows(given[nm]) for nm in VECTORS], axis=0)
    m_rows = jnp.concatenate([_pack_rows(given["m_" + nm]) for nm in VECTORS], axis=0)
    v_rows = jnp.concatenate([_pack_rows(given["v_" + nm]) for nm in VECTORS], axis=0)
    g_rows = reduced[:offsets[len(VECTORS)]]
    small = adamw(w_rows, m_rows, v_rows, g_rows, "adamw_vectors")
    for i, nm in enumerate(VECTORS):
        shape = given[nm].shape
        size = math.prod(shape)
        out[nm] = [r[offsets[i]:offsets[i + 1]].reshape(-1)[:size].reshape(shape) for r in small]

    grad_x = dx0.reshape(b_loc, seq, D)
    return (loss, grad_x, *[out[nm][0] for nm in WEIGHTS], *[out[nm][1] for nm in WEIGHTS],
            *[out[nm][2] for nm in WEIGHTS], *[out[nm][3] for nm in WEIGHTS])
```

```python
import functools
import math

import jax
import jax.numpy as jnp
from jax import lax
from jax.experimental import pallas as pl
from jax.experimental.pallas import tpu as pltpu

F32 = jnp.float32
BF16 = jnp.bfloat16
MESH = pl.DeviceIdType.MESH

EPS = 1e-6
HEAD_DIM = 64
LANES = 128
N_SWA_HEADS = 16
N_SWA_KV = 4
N_SB_HEADS = 16
WINDOW = 128
SWA_BLOCK = 128
SB_BLOCK = 256
N_CHIPS = 4
ATT_SCALE = HEAD_DIM ** -0.5

ADAM_LR = 0.001
ADAM_B1 = 0.9
ADAM_B2 = 0.999
ADAM_EPS = 1e-08
ADAM_WD = 0.01
ADAM_STEP = 10

VMEM_LIMIT = 56 * 1024 * 1024
NEG_BIG = -1e30

ANY = pl.BlockSpec(memory_space=pl.ANY)
VMEM_WHOLE = pl.BlockSpec(memory_space=pltpu.VMEM)


def _row_tile(rows, cap, multiple=16):
    best = None
    for t in range(multiple, min(rows, cap) + 1, multiple):
        if rows % t == 0:
            best = t
    if best is None:
        raise ValueError(f"no row tile for {rows} rows under {cap}")
    return best


def _params(*sem):
    return pltpu.CompilerParams(dimension_semantics=sem, vmem_limit_bytes=VMEM_LIMIT)


def _dot(a, b):
    return jnp.dot(a, b, preferred_element_type=F32)


def _dot_nt(a, b):
    return lax.dot_general(a, b, (((1,), (1,)), ((), ())), preferred_element_type=F32)


def _dot_tn(a, b):
    return lax.dot_general(a, b, (((0,), (0,)), ((), ())), preferred_element_type=F32)


def _sigmoid(x):
    return 1.0 / (1.0 + jnp.exp(-x))


def _split_dot(x, u):
    hi = x.astype(BF16)
    lo = (x - hi.astype(F32)).astype(BF16)
    return _dot(hi, u) + _dot(lo, u)


def _tiles(r, c, max_elems, multiple=16):
    col_opts = [c] + [t for t in (2048, 1024, 512, 256, 128) if t < c and c % t == 0]
    best = None
    for tc in col_opts:
        for tr in [r] + list(range(multiple, r, multiple)):
            if r % tr == 0 and tr * tc <= max_elems and (best is None or tr * tc > best[0] * best[1]):
                best = (tr, tc)
    if best is None:
        raise ValueError(f"no tile for [{r}, {c}] under {max_elems} elements")
    return best


class Rider:
    def __init__(self, arrays, out_shapes, sem_shapes, start, finish):
        self.arrays, self.out_shapes, self.sem_shapes = list(arrays), list(out_shapes), list(sem_shapes)
        self.start, self.finish = start, finish


def _place():
    return lax.axis_index("x"), lax.axis_index("y"), lax.axis_index("c")


def _other_chips(x, y):
    return [(1 - x, y), (x, 1 - y), (1 - x, 1 - y)]


def _split(ref):
    rows = ref.shape[-2]
    n = next(k for k in (4, 2, 1) if rows % (16 * k) == 0)
    step = rows // n
    lead = [()] if len(ref.shape) == 2 else [(s,) for s in range(ref.shape[0])]
    return [ref.at[(*idx, pl.ds(i * step, step))] for idx in lead for i in range(n)]


class _Copy:
    def __init__(self, src, dst, make):
        self.src, self.dst, self.make = src, dst, make

    def start(self):
        for s, d in zip(_split(self.src), _split(self.dst)):
            self.make(s, d).start()

    def wait(self):
        self.make(self.src, self.dst).wait()

    def wait_send(self):
        self.make(self.src, self.dst).wait_send()

    def wait_recv(self):
        self.make(self.src, self.dst).wait_recv()


def _remote(src, dst, send_sem, recv_sem, device):
    return _Copy(src, dst, lambda s, d: pltpu.make_async_remote_copy(
        src_ref=s, dst_ref=d, send_sem=send_sem, recv_sem=recv_sem, device_id=device, device_id_type=MESH))


def gather_rider(shards):
    n = len(shards)

    def copies(ins, outs, sems):
        ici_s, ici_r, d2d_s, d2d_r, own_s, own_r = sems
        x, y, c = _place()
        me = 2 * x + y
        own, ici, fwd = [], [], []
        for k in range(n):
            own.append(_remote(ins[k], outs[k].at[me], own_s.at[k], own_r.at[k], (x, y, 1 - c)))
            for j, (px, py) in enumerate(_other_chips(x, y)):
                i = 3 * k + j
                ici.append(_remote(ins[k].at[c], outs[k].at[me, c], ici_s.at[i], ici_r.at[i], (px, py, c)))
                got = outs[k].at[2 * px + py, c]
                fwd.append(_remote(got, got, d2d_s.at[i], d2d_r.at[i], (x, y, 1 - c)))
        return own, ici, fwd

    def start(ins, outs, sems):
        own, ici, _ = copies(ins, outs, sems)
        for cp in own + ici:
            cp.start()

    def finish(ins, outs, sems):
        own, ici, fwd = copies(ins, outs, sems)
        for a, f in zip(ici, fwd):
            a.wait_recv()
            f.start()
        for a in ici:
            a.wait_send()
        for f in fwd:
            f.wait()
        for o in own:
            o.wait()

    dma = pltpu.SemaphoreType.DMA
    return Rider(shards, [jax.ShapeDtypeStruct((N_CHIPS,) + s.shape, s.dtype) for s in shards],
                 [dma((3 * n,)), dma((3 * n,)), dma((3 * n,)), dma((3 * n,)), dma((n,)), dma((n,))], start, finish)


def half_swap_rider(grads):
    n = len(grads)

    def copies(ins, outs, sems):
        x, y, c = _place()
        return [_remote(ins[k].at[1 - c], outs[k], sems[0].at[k], sems[1].at[k], (x, y, 1 - c)) for k in range(n)]

    def start(ins, outs, sems):
        for cp in copies(ins, outs, sems):
            cp.start()

    def finish(ins, outs, sems):
        for cp in copies(ins, outs, sems):
            cp.wait()

    dma = pltpu.SemaphoreType.DMA
    return Rider(grads, [jax.ShapeDtypeStruct(g.shape[1:], g.dtype) for g in grads], [dma((n,)), dma((n,))],
                 start, finish)


def scatter_rider(halves):
    n = len(halves)

    def copies(ins, outs, sems):
        send, recv = sems
        x, y, c = _place()
        cps = []
        for k in range(n):
            for j, (px, py) in enumerate(_other_chips(x, y)):
                i = 3 * k + j
                cps.append(_remote(ins[k].at[2 * px + py], outs[k].at[j], send.at[i], recv.at[i], (px, py, c)))
        return cps

    def start(ins, outs, sems):
        for cp in copies(ins, outs, sems):
            cp.start()

    def finish(ins, outs, sems):
        for cp in copies(ins, outs, sems):
            cp.wait()

    dma = pltpu.SemaphoreType.DMA
    return Rider(halves, [jax.ShapeDtypeStruct((3,) + h.shape[1:], h.dtype) for h in halves],
                 [dma((3 * n,)), dma((3 * n,))], start, finish)


def final_swap_rider(sums):
    n = len(sums)

    def copies(ins, outs, sems):
        send, recv = sems
        x, y, c = _place()
        return [_remote(ins[k], outs[k], send.at[k], recv.at[k], (x, y, 1 - c)) for k in range(n)]

    def start(ins, outs, sems):
        for cp in copies(ins, outs, sems):
            cp.start()

    def finish(ins, outs, sems):
        for cp in copies(ins, outs, sems):
            cp.wait()

    dma = pltpu.SemaphoreType.DMA
    return Rider(sums, [jax.ShapeDtypeStruct(s.shape, s.dtype) for s in sums], [dma((n,)), dma((n,))],
                 start, finish)


def _call(body, *, name, grid, in_specs, out_specs, out_shape, scratch_shapes=(), sem, args, riders=(), aliases=None):
    in_specs, out_specs, out_shape = list(in_specs), list(out_specs), list(out_shape)
    scratch_shapes = list(scratch_shapes)
    aliases = aliases or {}
    if not riders:
        outs = pl.pallas_call(body, name=name, grid=grid, in_specs=in_specs, out_specs=out_specs, out_shape=out_shape,
                              scratch_shapes=scratch_shapes, input_output_aliases=aliases,
                              compiler_params=_params(*sem))(*args)
        return list(outs), []
    n_in, n_out, n_scr = len(in_specs), len(out_specs), len(scratch_shapes)
    r_in = [len(r.arrays) for r in riders]
    r_out = [len(r.out_shapes) for r in riders]
    r_sem = [len(r.sem_shapes) for r in riders]

    def wrapped(*refs):
        pos = 0

        def take(count):
            nonlocal pos
            got = refs[pos:pos + count]
            pos += count
            return got

        ins = take(n_in)
        rins = [take(k) for k in r_in]
        outs = take(n_out)
        routs = [take(k) for k in r_out]
        scr = take(n_scr)
        rsems = [take(k) for k in r_sem]
        first = functools.reduce(jnp.logical_and, [pl.program_id(a) == 0 for a in range(len(grid))])
        last = functools.reduce(jnp.logical_and, [pl.program_id(a) == grid[a] - 1 for a in range(len(grid))])

        @pl.when(first)
        def _():
            for r, ri, ro, rs in zip(riders, rins, routs, rsems):
                r.start(ri, ro, rs)

        body(*ins, *outs, *scr)

        @pl.when(last)
        def _():
            for r, ri, ro, rs in zip(riders, rins, routs, rsems):
                r.finish(ri, ro, rs)

    any_spec = pl.BlockSpec(memory_space=pl.ANY)
    res = pl.pallas_call(
        wrapped, name=name, grid=grid,
        in_specs=in_specs + [any_spec] * sum(r_in),
        out_specs=out_specs + [any_spec] * sum(r_out),
        out_shape=out_shape + [s for r in riders for s in r.out_shapes],
        scratch_shapes=scratch_shapes + [s for r in riders for s in r.sem_shapes],
        input_output_aliases=aliases,
        compiler_params=_params(*(["arbitrary"] * len(grid))),
    )(*args, *[a for r in riders for a in r.arrays])
    res = list(res)
    outs, rest = res[:n_out], res[n_out:]
    routs = []
    for k in r_out:
        routs.append(rest[:k])
        rest = rest[k:]
    return outs, routs


def exchange_now(riders, name):
    n_in = [len(r.arrays) for r in riders]
    n_out = [len(r.out_shapes) for r in riders]
    n_sem = [len(r.sem_shapes) for r in riders]

    def body(*refs):
        pos = 0
        groups = []
        for counts in (n_in, n_out, n_sem):
            part = []
            for k in counts:
                part.append(refs[pos:pos + k])
                pos += k
            groups.append(part)
        for r, ri, ro, rs in zip(riders, *groups):
            r.start(ri, ro, rs)
        for r, ri, ro, rs in zip(riders, *groups):
            r.finish(ri, ro, rs)

    any_spec = pl.BlockSpec(memory_space=pl.ANY)
    res = pl.pallas_call(
        body, name=name,
        in_specs=[any_spec] * sum(n_in), out_specs=[any_spec] * sum(n_out),
        out_shape=[s for r in riders for s in r.out_shapes],
        scratch_shapes=[s for r in riders for s in r.sem_shapes],
    )(*[a for r in riders for a in r.arrays])
    res = list(res)
    routs = []
    for k in n_out:
        routs.append(res[:k])
        res = res[k:]
    return routs


def rmsnorm_cast(x, gamma, name):
    T, D = x.shape
    tr = min(512, T)

    def body(x_ref, g_ref, o_ref):
        xv = x_ref[...]
        r = lax.rsqrt(jnp.mean(xv * xv, axis=-1, keepdims=True) + EPS)
        o_ref[...] = ((xv * r) * g_ref[...]).astype(BF16)

    return pl.pallas_call(
        body, name=name, grid=(T // tr,),
        in_specs=[pl.BlockSpec((tr, D), lambda i: (i, 0)), pl.BlockSpec((1, D), lambda i: (0, 0))],
        out_specs=pl.BlockSpec((tr, D), lambda i: (i, 0)),
        out_shape=jax.ShapeDtypeStruct((T, D), BF16),
        compiler_params=_params("parallel"),
    )(x, gamma)


def _rms_bwd(xv, gamma, dy):
    d = xv.shape[-1]
    r = lax.rsqrt(jnp.mean(xv * xv, axis=-1, keepdims=True) + EPS)
    gdy = dy * gamma
    dot = jnp.sum(gdy * xv, axis=-1, keepdims=True)
    dx = r * gdy - xv * ((r * r * r) * (dot / d))
    dgamma = jnp.sum(dy * (xv * r), axis=0, keepdims=True)
    return dx, dgamma


def rmsnorm_bwd_add(x, gamma, dy, dx_up, name, riders=()):
    T, D = x.shape
    tr = min(256, T)

    def body(x_ref, g_ref, dy_ref, up_ref, dx_ref, dg_ref):
        dx, dgamma = _rms_bwd(x_ref[...], g_ref[...], dy_ref[...])
        dx_ref[...] = up_ref[...] + dx

        @pl.when(pl.program_id(0) == 0)
        def _():
            dg_ref[...] = jnp.zeros_like(dg_ref)

        dg_ref[...] += dgamma

    row = pl.BlockSpec((tr, D), lambda i: (i, 0))
    vec = pl.BlockSpec((1, D), lambda i: (0, 0))
    return _call(
        body, name=name, grid=(T // tr,),
        in_specs=[row, vec, row, row], out_specs=[row, vec],
        out_shape=[jax.ShapeDtypeStruct((T, D), F32), jax.ShapeDtypeStruct((1, D), F32)],
        sem=("arbitrary",), args=(x, gamma, dy, dx_up), riders=riders)


def ffn_up(h, wg, wu, name, riders=()):
    T, D = h.shape
    G, _, Fs = wg.shape
    tm = min(512, T)

    def body(h_ref, wg_ref, wu_ref, g_ref, u_ref, a_ref):
        hv = h_ref[...]
        g = _dot(hv, wg_ref[...])
        u = _dot(hv, wu_ref[...])
        g_ref[...] = g.astype(BF16)
        u_ref[...] = u.astype(BF16)
        a_ref[...] = ((g * _sigmoid(g)) * u).astype(BF16)

    w_spec = pl.BlockSpec((None, D, Fs), lambda s, i: (s, 0, 0), pipeline_mode=pl.Buffered(1))
    o_spec = pl.BlockSpec((None, tm, Fs), lambda s, i: (s, i, 0))
    return _call(
        body, name=name, grid=(G, T // tm),
        in_specs=[pl.BlockSpec((tm, D), lambda s, i: (i, 0)), w_spec, w_spec],
        out_specs=[o_spec, o_spec, o_spec],
        out_shape=[jax.ShapeDtypeStruct((G, T, Fs), BF16)] * 3,
        sem=("parallel", "parallel"), args=(h, wg, wu), riders=riders)


def ffn_gate(h, wg, name, riders=()):
    T, D = h.shape
    G, _, Fs = wg.shape
    tm = min(512, T)

    def body(h_ref, wg_ref, g_ref):
        g_ref[...] = _dot(h_ref[...], wg_ref[...]).astype(BF16)

    return _call(
        body, name=name, grid=(G, T // tm),
        in_specs=[pl.BlockSpec((tm, D), lambda s, i: (i, 0)),
                  pl.BlockSpec((None, D, Fs), lambda s, i: (s, 0, 0), pipeline_mode=pl.Buffered(1))],
        out_specs=[pl.BlockSpec((None, tm, Fs), lambda s, i: (s, i, 0))],
        out_shape=[jax.ShapeDtypeStruct((G, T, Fs), BF16)],
        sem=("parallel", "parallel"), args=(h, wg), riders=riders)


def ffn_up_act(h, wu, g, name, riders=()):
    T, D = h.shape
    G, _, Fs = wu.shape
    tm = min(512, T)

    def body(h_ref, wu_ref, g_ref, u_ref, a_ref):
        u = _dot(h_ref[...], wu_ref[...])
        g = g_ref[...].astype(F32)
        u_ref[...] = u.astype(BF16)
        a_ref[...] = ((g * _sigmoid(g)) * u).astype(BF16)

    act = pl.BlockSpec((None, tm, Fs), lambda s, i: (s, i, 0))
    return _call(
        body, name=name, grid=(G, T // tm),
        in_specs=[pl.BlockSpec((tm, D), lambda s, i: (i, 0)),
                  pl.BlockSpec((None, D, Fs), lambda s, i: (s, 0, 0), pipeline_mode=pl.Buffered(1)), act],
        out_specs=[act, act],
        out_shape=[jax.ShapeDtypeStruct((G, T, Fs), BF16)] * 2,
        sem=("parallel", "parallel"), args=(h, wu, g), riders=riders)


def ffn_down(a, wd, x, name, riders=()):
    G, T, Fs = a.shape
    D = wd.shape[2]
    tm = min(512, T)

    def body(a_ref, w_ref, x_ref, o_ref, acc):
        s = pl.program_id(1)

        @pl.when(s == 0)
        def _():
            acc[...] = jnp.zeros_like(acc)

        acc[...] += _dot(a_ref[...], w_ref[...])

        @pl.when(s == G - 1)
        def _():
            o_ref[...] = x_ref[...] + 0.5 * acc[...]

    return _call(
        body, name=name, grid=(T // tm, G),
        in_specs=[pl.BlockSpec((None, tm, Fs), lambda i, s: (s, i, 0)),
                  pl.BlockSpec((None, Fs, D), lambda i, s: (s, 0, 0)),
                  pl.BlockSpec((tm, D), lambda i, s: (i, 0))],
        out_specs=[pl.BlockSpec((tm, D), lambda i, s: (i, 0))],
        out_shape=[jax.ShapeDtypeStruct((T, D), F32)],
        scratch_shapes=[pltpu.VMEM((tm, D), F32)],
        sem=("parallel", "arbitrary"), args=(a, wd, x), riders=riders)


def ffn_bwd_act(dx, wd, g, u, name, riders=()):
    T, D = dx.shape
    G, Fs, _ = wd.shape
    tm = min(512, T)

    def body(dx_ref, w_ref, g_ref, u_ref, dg_ref, du_ref):
        da = _dot_nt((0.5 * dx_ref[...]).astype(BF16), w_ref[...])
        gv = g_ref[...].astype(F32)
        sig = _sigmoid(gv)
        silu = gv * sig
        du_ref[...] = (da * silu).astype(BF16)
        dg_ref[...] = ((da * u_ref[...].astype(F32)) * (sig * (1.0 + gv * (1.0 - sig)))).astype(BF16)

    act = pl.BlockSpec((None, tm, Fs), lambda s, i: (s, i, 0))
    return _call(
        body, name=name, grid=(G, T // tm),
        in_specs=[pl.BlockSpec((tm, D), lambda s, i: (i, 0)),
                  pl.BlockSpec((None, Fs, D), lambda s, i: (s, 0, 0), pipeline_mode=pl.Buffered(1)),
                  act, act],
        out_specs=[act, act],
        out_shape=[jax.ShapeDtypeStruct((G, T, Fs), BF16), jax.ShapeDtypeStruct((G, T, Fs), BF16)],
        sem=("parallel", "parallel"), args=(dx, wd, g, u), riders=riders)


def ffn_bwd_dh(dg, du, wg, wu, name, riders=(), part=(0, 1), prev=None):
    G, T, Fs = dg.shape
    D = wg.shape[1]
    tm = min(512, T // part[1])
    nblk = T // tm // part[1]
    off = part[0] * nblk

    def body(dg_ref, du_ref, wg_ref, wu_ref, *rest):
        o_ref, acc = rest[-2:]
        s = pl.program_id(1)

        @pl.when(s == 0)
        def _():
            acc[...] = jnp.zeros_like(acc)

        acc[...] += _dot_nt(dg_ref[...], wg_ref[...])
        acc[...] += _dot_nt(du_ref[...], wu_ref[...])

        @pl.when(s == G - 1)
        def _():
            o_ref[...] = acc[...]

    act = pl.BlockSpec((None, tm, Fs), lambda i, s: (s, i + off, 0))
    w_spec = pl.BlockSpec((None, D, Fs), lambda i, s: (s, 0, 0))
    extra = [] if prev is None else [prev]
    return _call(
        body, name=name, grid=(nblk, G),
        in_specs=[act, act, w_spec, w_spec] + [pl.BlockSpec(memory_space=pl.ANY)] * len(extra),
        out_specs=[pl.BlockSpec((tm, D), lambda i, s: (i + off, 0))],
        out_shape=[jax.ShapeDtypeStruct((T, D), F32)],
        scratch_shapes=[pltpu.VMEM((tm, D), F32)],
        sem=("parallel", "arbitrary"), args=(dg, du, wg, wu, *extra), riders=riders,
        aliases={4: 0} if extra else None)


def wgrad(a, b, name, *, groups, m, n, a_block, a_map, b_block, b_map, b_scale=None, tk=1024, riders=()):
    T = a.shape[-2]
    tk = min(tk, T)
    nk = T // tk

    def body(a_ref, b_ref, o_ref, acc):
        k = pl.program_id(2)

        @pl.when(k == 0)
        def _():
            acc[...] = jnp.zeros_like(acc)

        bv = b_ref[...]
        if b_scale is not None:
            bv = b_scale * bv
        acc[...] += _dot_tn(a_ref[...].astype(BF16), bv.astype(BF16))

        @pl.when(k == nk - 1)
        def _():
            o_ref[...] = acc[...].astype(BF16)

    return _call(
        body, name=name, grid=(groups, 2, nk),
        in_specs=[pl.BlockSpec(a_block(tk), a_map), pl.BlockSpec(b_block(tk), b_map)],
        out_specs=[pl.BlockSpec((None, None, m, n), lambda s, j, k: (j, s, 0, 0))],
        out_shape=[jax.ShapeDtypeStruct((2, groups, m, n), BF16)],
        scratch_shapes=[pltpu.VMEM((m, n), F32)],
        sem=("parallel", "parallel", "arbitrary"), args=(a, b), riders=riders)


def in_proj(h, w_in, name, riders=()):
    T, D = h.shape
    G, _, Ps = w_in.shape
    tm = min(512, T)

    def body(h_ref, w_ref, o_ref):
        o_ref[...] = _dot(h_ref[...], w_ref[...]).astype(BF16)

    return _call(
        body, name=name, grid=(G, T // tm),
        in_specs=[pl.BlockSpec((tm, D), lambda s, i: (i, 0)),
                  pl.BlockSpec((None, D, Ps), lambda s, i: (s, 0, 0))],
        out_specs=[pl.BlockSpec((tm, Ps), lambda s, i: (i, s))],
        out_shape=[jax.ShapeDtypeStruct((T, G * Ps), BF16)],
        sem=("parallel", "parallel"), args=(h, w_in), riders=riders)


def in_proj_bwd(dproj, w_in, name, riders=()):
    T = dproj.shape[0]
    G, D, Ps = w_in.shape
    tm = min(512, T)

    def body(d_ref, w_ref, o_ref, acc):
        s = pl.program_id(1)

        @pl.when(s == 0)
        def _():
            acc[...] = jnp.zeros_like(acc)

        acc[...] += _dot_nt(d_ref[...], w_ref[...])

        @pl.when(s == G - 1)
        def _():
            o_ref[...] = acc[...]

    return _call(
        body, name=name, grid=(T // tm, G),
        in_specs=[pl.BlockSpec((tm, Ps), lambda i, s: (i, s)),
                  pl.BlockSpec((None, D, Ps), lambda i, s: (s, 0, 0))],
        out_specs=[pl.BlockSpec((tm, D), lambda i, s: (i, 0))],
        out_shape=[jax.ShapeDtypeStruct((T, D), F32)],
        scratch_shapes=[pltpu.VMEM((tm, D), F32)],
        sem=("parallel", "arbitrary"), args=(dproj, w_in), riders=riders)


def out_proj(y, w_out, x, name):
    T, M = y.shape
    D = w_out.shape[1]
    tm = min(512, T)

    def body(y_ref, w_ref, x_ref, o_ref):
        o_ref[...] = x_ref[...] + _dot(y_ref[...], w_ref[...])

    return pl.pallas_call(
        body, name=name, grid=(T // tm,),
        in_specs=[pl.BlockSpec((tm, M), lambda i: (i, 0)), pl.BlockSpec((M, D), lambda i: (0, 0)),
                  pl.BlockSpec((tm, D), lambda i: (i, 0))],
        out_specs=pl.BlockSpec((tm, D), lambda i: (i, 0)),
        out_shape=jax.ShapeDtypeStruct((T, D), F32),
        compiler_params=_params("parallel"),
    )(y, w_out, x)


def out_norm(ya, yb, ga, gb, name):
    T, W = ya.shape
    tr = min(512, T)

    def body(a_ref, b_ref, ga_ref, gb_ref, o_ref):
        for src, gam, col in ((a_ref, ga_ref, 0), (b_ref, gb_ref, W)):
            v = src[...]
            r = lax.rsqrt(jnp.mean(v * v, axis=-1, keepdims=True) + EPS)
            o_ref[:, col:col + W] = ((v * r) * gam[...]).astype(BF16)

    row = pl.BlockSpec((tr, W), lambda i: (i, 0))
    vec = pl.BlockSpec((1, W), lambda i: (0, 0))
    return pl.pallas_call(
        body, name=name, grid=(T // tr,),
        in_specs=[row, row, vec, vec],
        out_specs=pl.BlockSpec((tr, 2 * W), lambda i: (i, 0)),
        out_shape=jax.ShapeDtypeStruct((T, 2 * W), BF16),
        compiler_params=_params("parallel"),
    )(ya, yb, ga, gb)


def out_proj_bwd(dx, w_out, ya, yb, ga, gb, name):
    T, D = dx.shape
    M = w_out.shape[0]
    W = M // 2
    tm = min(256, T)

    def body(dx_ref, w_ref, a_ref, b_ref, ga_ref, gb_ref, da_ref, db_ref, dga_ref, dgb_ref):
        dy = _dot_nt(dx_ref[...].astype(BF16), w_ref[...])

        @pl.when(pl.program_id(0) == 0)
        def _():
            dga_ref[...] = jnp.zeros_like(dga_ref)
            dgb_ref[...] = jnp.zeros_like(dgb_ref)

        d_a, dg_a = _rms_bwd(a_ref[...], ga_ref[...], dy[:, :W])
        d_b, dg_b = _rms_bwd(b_ref[...], gb_ref[...], dy[:, W:])
        da_ref[...] = d_a
        db_ref[...] = d_b
        dga_ref[...] += dg_a
        dgb_ref[...] += dg_b

    row = pl.BlockSpec((tm, W), lambda i: (i, 0))
    vec = pl.BlockSpec((1, W), lambda i: (0, 0))
    return pl.pallas_call(
        body, name=name, grid=(T // tm,),
        in_specs=[pl.BlockSpec((tm, D), lambda i: (i, 0)), pl.BlockSpec((M, D), lambda i: (0, 0)),
                  row, row, vec, vec],
        out_specs=[row, row, vec, vec],
        out_shape=[jax.ShapeDtypeStruct((T, W), F32), jax.ShapeDtypeStruct((T, W), F32),
                   jax.ShapeDtypeStruct((1, W), F32), jax.ShapeDtypeStruct((1, W), F32)],
        compiler_params=_params("arbitrary"),
    )(dx, w_out, ya, yb, ga, gb)


def final_loss(x, gamma, target, name):
    T, D = x.shape
    tr = min(256, T)

    def body(x_ref, g_ref, t_ref, dx_ref, dg_ref, loss_ref):
        xv = x_ref[...]
        gam = g_ref[...]
        r = lax.rsqrt(jnp.mean(xv * xv, axis=-1, keepdims=True) + EPS)
        err = (xv * r) * gam - t_ref[...]
        part = 0.5 * jnp.sum(jnp.mean(err * err, axis=-1, keepdims=True), axis=0, keepdims=True)
        dx, dgamma = _rms_bwd(xv, gam, err / D)
        dx_ref[...] = dx

        @pl.when(pl.program_id(0) == 0)
        def _():
            dg_ref[...] = jnp.zeros_like(dg_ref)
            loss_ref[...] = jnp.zeros_like(loss_ref)

        dg_ref[...] += dgamma
        loss_ref[...] += jnp.broadcast_to(part, loss_ref.shape)

    row = pl.BlockSpec((tr, D), lambda i: (i, 0))
    vec = pl.BlockSpec((1, D), lambda i: (0, 0))
    return pl.pallas_call(
        body, name=name, grid=(T // tr,),
        in_specs=[row, vec, row],
        out_specs=[row, vec, pl.BlockSpec((1, LANES), lambda i: (0, 0))],
        out_shape=[jax.ShapeDtypeStruct((T, D), F32), jax.ShapeDtypeStruct((1, D), F32),
                   jax.ShapeDtypeStruct((1, LANES), F32)],
        compiler_params=_params("arbitrary"),
    )(x, gamma, target)


def _lane_select(src_half, dst_halves):
    r = lax.broadcasted_iota(jnp.int32, (LANES, LANES), 0)
    c = lax.broadcasted_iota(jnp.int32, (LANES, LANES), 1)
    hit = ((r >> 6) == src_half) & ((r & (HEAD_DIM - 1)) == (c & (HEAD_DIM - 1)))
    ok = functools.reduce(jnp.logical_or, [(c >> 6) == d for d in dst_halves])
    return jnp.where(hit & ok, 1.0, 0.0).astype(BF16)


def _swa_slope(kvp, local_head):
    lo = 2.0 ** (-8.0 * (local_head + 1) / N_SWA_HEADS)
    hi = 2.0 ** (-8.0 * (8 + local_head + 1) / N_SWA_HEADS)
    return jnp.where(kvp == 0, jnp.float32(lo), jnp.float32(hi))


GQA = N_SWA_HEADS // N_SWA_KV


def _per_head_rows(values):
    return jnp.concatenate([jnp.zeros((SWA_BLOCK, 1), F32) + v for v in values], axis=0)


def _swa_probs(qblk, kwin, n, start, slope, sink):
    z = _dot_nt(qblk, kwin) * ATT_SCALE
    t = n * SWA_BLOCK + (lax.broadcasted_iota(jnp.int32, z.shape, 0) & (SWA_BLOCK - 1))
    s = start + lax.broadcasted_iota(jnp.int32, z.shape, 1)
    dist = t - s
    valid = (dist >= 0) & (dist < WINDOW)
    z = jnp.where(valid, z - slope * dist.astype(F32), NEG_BIG)
    m = jnp.maximum(jnp.max(z, axis=-1, keepdims=True), sink)
    e = jnp.exp(z - m)
    den = jnp.sum(e, axis=-1, keepdims=True) + jnp.exp(sink - m)
    return e / den, m, den


def swa_fwd(proj, sinks, b_loc, seq, name, riders=()):
    T = proj.shape[0]
    nblk = seq // SWA_BLOCK

    def body(sink_ref, q_ref, k_ref, v_ref, o_ref, qs, vv):
        kvp = pl.program_id(1)
        lane = lax.broadcasted_iota(jnp.int32, (SWA_BLOCK, LANES), 1)
        for kvh in range(2):
            vv[...] = _dot(v_ref[...], _lane_select(kvh, (0, 1))).astype(BF16)
            for g in range(GQA):
                cols = slice(LANES * (2 * kvh + g // 2), LANES * (2 * kvh + g // 2 + 1))
                qs[g] = _dot(q_ref[:, cols], _lane_select(g % 2, (kvh,))).astype(BF16)
            slope = _per_head_rows([_swa_slope(kvp, GQA * kvh + g) for g in range(GQA)])
            sink = _per_head_rows([sink_ref[0, 8 * kvp + GQA * kvh + g] for g in range(GQA)])

            def blk(n, carry):
                start = pl.multiple_of(jnp.maximum(n - 1, 0) * SWA_BLOCK, SWA_BLOCK)
                rows = pl.ds(pl.multiple_of(n * SWA_BLOCK, SWA_BLOCK), SWA_BLOCK)
                win = pl.ds(start, 2 * SWA_BLOCK)
                qblk = jnp.concatenate([qs[g, rows, :] for g in range(GQA)], axis=0)
                p, _, _ = _swa_probs(qblk, k_ref[win, :], n, start, slope, sink)
                out = _dot(p.astype(BF16), vv[win, :])
                for pair in range(GQA // 2):
                    cols = slice(LANES * (2 * kvh + pair), LANES * (2 * kvh + pair + 1))
                    first = out[2 * pair * SWA_BLOCK:(2 * pair + 1) * SWA_BLOCK]
                    second = out[(2 * pair + 1) * SWA_BLOCK:(2 * pair + 2) * SWA_BLOCK]
                    o_ref[rows, cols] = jnp.where(lane < HEAD_DIM, first, second)
                return carry

            lax.fori_loop(0, nblk, blk, 0)

    return _call(
        body, name=name, grid=(b_loc, 2),
        in_specs=[pl.BlockSpec(memory_space=pltpu.SMEM),
                  pl.BlockSpec((seq, 4 * LANES), lambda b, p: (b, p)),
                  pl.BlockSpec((seq, LANES), lambda b, p: (b, 8 + p)),
                  pl.BlockSpec((seq, LANES), lambda b, p: (b, 10 + p))],
        out_specs=[pl.BlockSpec((seq, 4 * LANES), lambda b, p: (b, p))],
        out_shape=[jax.ShapeDtypeStruct((T, N_SWA_HEADS * HEAD_DIM), F32)],
        scratch_shapes=[pltpu.VMEM((GQA, seq, LANES), BF16), pltpu.VMEM((seq, LANES), BF16)],
        sem=("parallel", "parallel"), args=(sinks, proj, proj, proj), riders=riders)


def swa_bwd(proj, sinks, dya, b_loc, seq, name, riders=()):
    T = proj.shape[0]
    nblk = seq // SWA_BLOCK

    def body(sink_ref, q_ref, k_ref, v_ref, do_ref, out_ref, ds_ref, stage, qs, dos, kk, dk_acc, dv_acc, sems):
        compute(sink_ref, q_ref, k_ref, v_ref, do_ref, stage, ds_ref, qs, dos, kk, dk_acc, dv_acc)
        kvp = pl.program_id(0)
        rows = pl.ds(pl.multiple_of(pl.program_id(1) * seq, seq), seq)
        tiles = [4 * kvp + j for j in range(4)] + [8 + kvp, 10 + kvp]
        copies = []
        for j, t in enumerate(tiles):
            cols = pl.ds(pl.multiple_of(t * LANES, LANES), LANES)
            copies.append(pltpu.make_async_copy(stage.at[j], out_ref.at[rows, cols], sems.at[j]))
        for cp in copies:
            cp.start()
        for cp in copies:
            cp.wait()

    def compute(sink_ref, q_ref, k_ref, v_ref, do_ref, stage, ds_ref, qs, dos, kk, dk_acc, dv_acc):
        kvp = pl.program_id(0)
        b = pl.program_id(1)
        lane = lax.broadcasted_iota(jnp.int32, (SWA_BLOCK, LANES), 1)

        @pl.when(b == 0)
        def _():
            ds_ref[...] = jnp.zeros_like(ds_ref)

        dk_acc[...] = jnp.zeros_like(dk_acc)
        dv_acc[...] = jnp.zeros_like(dv_acc)
        for kvh in range(2):
            kk[...] = _dot(k_ref[...], _lane_select(kvh, (0, 1))).astype(BF16)
            for g in range(GQA):
                cols = slice(LANES * (2 * kvh + g // 2), LANES * (2 * kvh + g // 2 + 1))
                move = _lane_select(g % 2, (kvh,))
                qs[g] = _dot(q_ref[:, cols], move).astype(BF16)
                dos[g] = _dot(do_ref[:, cols].astype(BF16), move).astype(BF16)
            slope = _per_head_rows([_swa_slope(kvp, GQA * kvh + g) for g in range(GQA)])
            sink = _per_head_rows([sink_ref[0, 8 * kvp + GQA * kvh + g] for g in range(GQA)])

            def blk(n, carry):
                start = pl.multiple_of(jnp.maximum(n - 1, 0) * SWA_BLOCK, SWA_BLOCK)
                rows = pl.ds(pl.multiple_of(n * SWA_BLOCK, SWA_BLOCK), SWA_BLOCK)
                win = pl.ds(start, 2 * SWA_BLOCK)
                qblk = jnp.concatenate([qs[g, rows, :] for g in range(GQA)], axis=0)
                doblk = jnp.concatenate([dos[g, rows, :] for g in range(GQA)], axis=0)
                p, m, den = _swa_probs(qblk, k_ref[win, :], n, start, slope, sink)
                dp = _dot_nt(doblk, v_ref[win, :])
                delta = jnp.sum(p * dp, axis=-1, keepdims=True)
                leak = (jnp.exp(sink - m) / den) * delta
                for g in range(GQA):
                    head = GQA * kvh + g
                    ds_ref[head:head + 1, :] -= jnp.broadcast_to(
                        jnp.sum(leak[g * SWA_BLOCK:(g + 1) * SWA_BLOCK], axis=0, keepdims=True), (1, LANES))
                dsc = ((p * (dp - delta)) * ATT_SCALE).astype(BF16)
                dv_acc[win, :] += _dot_tn(p.astype(BF16), doblk)
                dk_acc[win, :] += _dot_tn(dsc, qblk)
                dq = _dot(dsc, kk[win, :])
                for pair in range(GQA // 2):
                    first = dq[2 * pair * SWA_BLOCK:(2 * pair + 1) * SWA_BLOCK]
                    second = dq[(2 * pair + 1) * SWA_BLOCK:(2 * pair + 2) * SWA_BLOCK]
                    stage[2 * kvh + pair, rows, :] = jnp.where(lane < HEAD_DIM, first, second).astype(BF16)
                return carry

            lax.fori_loop(0, nblk, blk, 0)
        stage[4] = dk_acc[...].astype(BF16)
        stage[5] = dv_acc[...].astype(BF16)

    return _call(
        body, name=name, grid=(2, b_loc),
        in_specs=[pl.BlockSpec(memory_space=pltpu.SMEM),
                  pl.BlockSpec((seq, 4 * LANES), lambda p, b: (b, p)),
                  pl.BlockSpec((seq, LANES), lambda p, b: (b, 8 + p)),
                  pl.BlockSpec((seq, LANES), lambda p, b: (b, 10 + p)),
                  pl.BlockSpec((seq, 4 * LANES), lambda p, b: (b, p))],
        out_specs=[pl.BlockSpec(memory_space=pl.ANY),
                   pl.BlockSpec((8, LANES), lambda p, b: (p, 0))],
        out_shape=[jax.ShapeDtypeStruct((T, proj.shape[1]), BF16),
                   jax.ShapeDtypeStruct((N_SWA_HEADS, LANES), F32)],
        scratch_shapes=[pltpu.VMEM((6, seq, LANES), BF16),
                        pltpu.VMEM((GQA, seq, LANES), BF16), pltpu.VMEM((GQA, seq, LANES), BF16),
                        pltpu.VMEM((seq, LANES), BF16),
                        pltpu.VMEM((seq, LANES), F32), pltpu.VMEM((seq, LANES), F32),
                        pltpu.SemaphoreType.DMA((6,))],
        sem=("arbitrary", "arbitrary"), args=(sinks, proj, proj, proj, dya), riders=riders)


def _softplus(z):
    return jnp.maximum(z, 0.0) + jnp.log(1.0 + jnp.exp(-jnp.abs(z)))


def _tri(n, rel):
    r = lax.broadcasted_iota(jnp.int32, (n, n), 0)
    c = lax.broadcasted_iota(jnp.int32, (n, n), 1)
    return jnp.where(rel(r, c), 1.0, 0.0).astype(BF16)


def sb_fwd(proj, b_loc, seq, name, riders=()):
    T = proj.shape[0]
    bq = min(SB_BLOCK, seq)
    nq = seq // bq
    npair = N_SB_HEADS // 2

    def body(q_ref, k_ref, v_ref, o_ref, lt_ref):
        i = pl.program_id(2)
        lane = lax.broadcasted_iota(jnp.int32, (bq, LANES), 1)
        after = _tri(bq, lambda r, c: r > c)
        row = lax.broadcasted_iota(jnp.int32, (bq, bq), 0)
        col = lax.broadcasted_iota(jnp.int32, (bq, bq), 1)
        qv = q_ref[...] * jnp.asarray(ATT_SCALE, BF16)
        blank = jnp.zeros((bq, LANES), BF16)
        q2 = jnp.concatenate([jnp.where((lane >> 6) == h, qv, blank) for h in range(2)], axis=0)


        def tile(j, state, mask):
            rows = pl.ds(pl.multiple_of(j * bq, bq), bq)
            kj, vj = k_ref[rows, :], v_ref[rows, :]
            zz = _dot_nt(q2, kj)
            log_beta, totals, pieces = [], [], []
            for h in range(2):
                z = zz[h * bq:(h + 1) * bq]
                sp = _softplus(z)
                lg = -sp if mask is None else jnp.where(mask, -sp, 0.0)
                hi = lg.astype(BF16)
                pieces += [hi, (lg - hi.astype(F32)).astype(BF16)]
                log_beta.append(z - sp if mask is None else jnp.where(mask, z - sp, NEG_BIG))
                totals.append(jnp.sum(lg, axis=-1, keepdims=True))
            cc = _dot(jnp.concatenate(pieces, axis=0), after)
            probs = []
            for h in range(2):
                inside = cc[2 * h * bq:(2 * h + 1) * bq] + cc[(2 * h + 1) * bq:(2 * h + 2) * bq]
                probs.append(jnp.exp(log_beta[h] + (inside + state[2 * h])).astype(BF16))
            pv = _dot(jnp.concatenate(probs, axis=0), vj)
            return (state[0] + totals[0], state[1] + pv[:bq], state[2] + totals[1], state[3] + pv[bq:])

        zero = (jnp.zeros((bq, 1), F32), jnp.zeros((bq, LANES), F32))
        state = tile(i, zero + zero, col < row)
        state = lax.fori_loop(0, i, lambda jj, st: tile(i - 1 - jj, st, None), state)
        o_ref[...] = jnp.where(lane < HEAD_DIM, state[1], state[3])
        lt_ref[...] = jnp.where(lane < HEAD_DIM, state[0], state[2])

    return _call(
        body, name=name, grid=(b_loc, npair, nq),
        in_specs=[pl.BlockSpec((bq, LANES), lambda b, p, i: (b * nq + i, 12 + p)),
                  pl.BlockSpec((seq, LANES), lambda b, p, i: (b, 20 + p)),
                  pl.BlockSpec((seq, LANES), lambda b, p, i: (b, 28 + p))],
        out_specs=[pl.BlockSpec((bq, LANES), lambda b, p, i: (b * nq + i, p)),
                   pl.BlockSpec((None, bq, LANES), lambda b, p, i: (b * npair + p, i, 0))],
        out_shape=[jax.ShapeDtypeStruct((T, N_SB_HEADS * HEAD_DIM), F32),
                   jax.ShapeDtypeStruct((b_loc * npair, seq, LANES), F32)],
        sem=("parallel", "parallel", "parallel"), args=(proj, proj, proj), riders=riders)


def sb_bwd(proj, dyb, ltot, dproj, b_loc, seq, name, riders=()):
    T = proj.shape[0]
    bq = min(SB_BLOCK, seq)
    nq = seq // bq
    npair = N_SB_HEADS // 2

    def body(q_ref, k_ref, v_ref, do_ref, lt_ref, _, out_ref, stage, dk_acc, dv_acc, sems):
        compute(q_ref, k_ref, v_ref, do_ref, lt_ref, stage, dk_acc, dv_acc)
        rows = pl.ds(pl.multiple_of(pl.program_id(0) * seq, seq), seq)
        copies = []
        for j in range(3):
            tile = pl.multiple_of((12 + pl.program_id(1) + npair * j) * LANES, LANES)
            copies.append(pltpu.make_async_copy(stage.at[j], out_ref.at[rows, pl.ds(tile, LANES)], sems.at[j]))
        for cp in copies:
            cp.start()
        for cp in copies:
            cp.wait()

    def compute(q_ref, k_ref, v_ref, do_ref, lt_ref, stage, dk_acc, dv_acc):
        lane = lax.broadcasted_iota(jnp.int32, (bq, LANES), 1)
        upto = _tri(bq, lambda r, c: r <= c)
        before = _tri(bq, lambda r, c: r < c)
        row = lax.broadcasted_iota(jnp.int32, (bq, bq), 0)
        col = lax.broadcasted_iota(jnp.int32, (bq, bq), 1)
        dk_acc[...] = jnp.zeros_like(dk_acc)
        dv_acc[...] = jnp.zeros_like(dv_acc)
        scale = jnp.asarray(ATT_SCALE, BF16)
        heads = [(lane >> 6) == h for h in range(2)]
        blank = jnp.zeros((bq, LANES), BF16)

        def qblock(i, carry):
            qrows = pl.ds(pl.multiple_of(i * bq, bq), bq)
            qv = q_ref[qrows, :] * scale
            dov = do_ref[qrows, :].astype(BF16)
            ltv = lt_ref[qrows, :]
            q2 = jnp.concatenate([jnp.where(m, qv, blank) for m in heads], axis=0)
            do2 = jnp.concatenate([jnp.where(m, dov, blank) for m in heads], axis=0)
            totals = [jnp.sum(jnp.where(lane == h * HEAD_DIM, ltv, 0.0), axis=-1, keepdims=True) for h in range(2)]

            def split2(xs):
                out = []
                for x in xs:
                    hi = x.astype(BF16)
                    out += [hi, (x - hi.astype(F32)).astype(BF16)]
                return jnp.concatenate(out, axis=0)

            def join2(cc, h):
                return cc[2 * h * bq:(2 * h + 1) * bq] + cc[(2 * h + 1) * bq:(2 * h + 2) * bq]

            def tile(j, state, mask):
                krows = pl.ds(pl.multiple_of(j * bq, bq), bq)
                kj, vj = k_ref[krows, :], v_ref[krows, :]
                kv = kj * scale
                k2 = jnp.concatenate([jnp.where(m, kv, blank) for m in heads], axis=0)
                zz = _dot_nt(q2, kj)
                dd = _dot_nt(do2, vj)
                lgs, log_betas = [], []
                for h in range(2):
                    z = zz[h * bq:(h + 1) * bq]
                    sp = _softplus(z)
                    lgs.append(-sp if mask is None else jnp.where(mask, -sp, 0.0))
                    log_betas.append(z - sp)
                cl = _dot(split2(lgs), upto)
                probs, gs = [], []
                for h in range(2):
                    a = jnp.exp(log_betas[h] + (totals[h] - (join2(cl, h) + state[2 * h])))
                    if mask is not None:
                        a = jnp.where(mask, a, 0.0)
                    probs.append(a.astype(BF16))
                    gs.append(a * dd[h * bq:(h + 1) * bq])
                cg = _dot(split2(gs), before)
                dzs = []
                for h in range(2):
                    beta = jnp.exp(log_betas[h])
                    dz = gs[h] * (1.0 - beta) - beta * (join2(cg, h) + state[2 * h + 1])
                    if mask is not None:
                        dz = jnp.where(mask, dz, 0.0)
                    dzs.append(dz.astype(BF16))
                dv_acc[krows, :] += _dot_tn(jnp.concatenate(probs, axis=0), do2)
                dk_acc[krows, :] += _dot_tn(jnp.concatenate(dzs, axis=0), q2)
                dq = state[4] + _dot(jnp.concatenate(dzs, axis=1), k2)
                new = []
                for h in range(2):
                    new += [state[2 * h] + jnp.sum(lgs[h], axis=-1, keepdims=True),
                            state[2 * h + 1] + jnp.sum(gs[h], axis=-1, keepdims=True)]
                return (*new, dq)

            zero = jnp.zeros((bq, 1), F32)
            state = lax.fori_loop(0, i, lambda j, st: tile(j, st, None),
                                  (zero, zero, zero, zero, jnp.zeros((bq, LANES), F32)))
            state = tile(i, state, col < row)
            stage[0, qrows, :] = state[4].astype(BF16)
            return carry

        lax.fori_loop(0, nq, qblock, 0)
        stage[1] = dk_acc[...].astype(BF16)
        stage[2] = dv_acc[...].astype(BF16)

    return _call(
        body, name=name, grid=(b_loc, npair),
        in_specs=[pl.BlockSpec((seq, LANES), lambda b, p: (b, 12 + p)),
                  pl.BlockSpec((seq, LANES), lambda b, p: (b, 20 + p)),
                  pl.BlockSpec((seq, LANES), lambda b, p: (b, 28 + p)),
                  pl.BlockSpec((seq, LANES), lambda b, p: (b, p)),
                  pl.BlockSpec((None, seq, LANES), lambda b, p: (b * npair + p, 0, 0)),
                  pl.BlockSpec(memory_space=pl.ANY)],
        out_specs=[pl.BlockSpec(memory_space=pl.ANY)],
        out_shape=[jax.ShapeDtypeStruct(dproj.shape, BF16)],
        scratch_shapes=[pltpu.VMEM((3, seq, LANES), BF16)] + [pltpu.VMEM((seq, LANES), F32)] * 2
        + [pltpu.SemaphoreType.DMA((3,))],
        sem=("arbitrary", "arbitrary"), args=(proj, proj, proj, dyb, ltot, dproj), riders=riders,
        aliases={5: 0})


def allreduce_small(v, name):
    R = v.shape[0]
    ndev = 8

    def body(v_ref, o_ref, land, send_sems, recv_sems):
        x, y, c = _place()
        me = 4 * x + 2 * y + c
        land[me] = v_ref[...]
        started = []
        for f in range(1, ndev):
            fx, fy, fc = (f >> 2) & 1, (f >> 1) & 1, f & 1
            peer = ((1 - x) if fx else x, (1 - y) if fy else y, (1 - c) if fc else c)
            cp = pltpu.make_async_remote_copy(
                src_ref=v_ref, dst_ref=land.at[me], send_sem=send_sems.at[f - 1], recv_sem=recv_sems.at[f - 1],
                device_id=peer, device_id_type=MESH)
            cp.start()
            started.append(cp)
        for cp in started:
            cp.wait()
        total = land[0]
        for d in range(1, ndev):
            total = total + land[d]
        o_ref[...] = total

    return pl.pallas_call(
        body, name=name,
        in_specs=[VMEM_WHOLE], out_specs=VMEM_WHOLE,
        out_shape=jax.ShapeDtypeStruct(v.shape, F32),
        scratch_shapes=[pltpu.VMEM((ndev, R, LANES), F32),
                        pltpu.SemaphoreType.DMA((ndev - 1,)), pltpu.SemaphoreType.DMA((ndev - 1,))],
    )(v)


def sum_cores(grads, theirs, name):
    _, _, r, c = grads.shape
    tr, tc = _tiles(r, c, 1 << 18)
    core = lax.axis_index("c").astype(jnp.int32).reshape(1)

    def body(core_ref, g_ref, t_ref, o_ref):
        o_ref[...] = (g_ref[...].astype(F32) + t_ref[...].astype(F32)).astype(BF16)

    blk = pl.BlockSpec((N_CHIPS, tr, tc), lambda i, j, core_ref: (0, i, j))
    return pl.pallas_call(
        body, name=name,
        grid_spec=pltpu.PrefetchScalarGridSpec(
            num_scalar_prefetch=1, grid=(r // tr, c // tc),
            in_specs=[pl.BlockSpec((None, N_CHIPS, tr, tc), lambda i, j, core_ref: (core_ref[0], 0, i, j)), blk],
            out_specs=blk),
        out_shape=jax.ShapeDtypeStruct((N_CHIPS, r, c), BF16),
        compiler_params=_params("parallel", "parallel"),
    )(core, grads, theirs)


def sum_chips(halves, landed, name):
    _, r, c = landed.shape
    tr, tc = _tiles(r, c, 1 << 18)
    chip = (2 * lax.axis_index("x") + lax.axis_index("y")).astype(jnp.int32).reshape(1)

    def body(chip_ref, own_ref, p_ref, o_ref):
        total = own_ref[...].astype(F32)
        for j in range(N_CHIPS - 1):
            total = total + p_ref[j].astype(F32)
        o_ref[...] = total

    return pl.pallas_call(
        body, name=name,
        grid_spec=pltpu.PrefetchScalarGridSpec(
            num_scalar_prefetch=1, grid=(r // tr, c // tc),
            in_specs=[pl.BlockSpec((None, tr, tc), lambda i, j, chip_ref: (chip_ref[0], i, j)),
                      pl.BlockSpec((N_CHIPS - 1, tr, tc), lambda i, j, chip_ref: (0, i, j))],
            out_specs=pl.BlockSpec((tr, tc), lambda i, j, chip_ref: (i, j))),
        out_shape=jax.ShapeDtypeStruct((r, c), F32),
        compiler_params=_params("parallel", "parallel"),
    )(chip, halves, landed)


def _adamw_math(w, g, m, v):
    m = ADAM_B1 * m + (1.0 - ADAM_B1) * g
    v = ADAM_B2 * v + (1.0 - ADAM_B2) * (g * g)
    m_hat = m / (1.0 - ADAM_B1 ** ADAM_STEP)
    v_hat = v / (1.0 - ADAM_B2 ** ADAM_STEP)
    delta = -ADAM_LR * (m_hat / (jnp.sqrt(v_hat) + ADAM_EPS) + ADAM_WD * w)
    return delta, m, v


def adamw(w, m, v, g, name):
    r, c = w.shape
    tr, tc = _tiles(r, c, 1 << 18, multiple=8)

    def body(w_ref, m_ref, v_ref, gin_ref, g_ref, d_ref, nm_ref, nv_ref):
        g = gin_ref[...]
        delta, nm, nv = _adamw_math(w_ref[...], g, m_ref[...], v_ref[...])
        g_ref[...] = g
        d_ref[...] = delta
        nm_ref[...] = nm
        nv_ref[...] = nv

    blk = pl.BlockSpec((tr, tc), lambda i, j: (i, j))
    shape = jax.ShapeDtypeStruct((r, c), F32)
    return pl.pallas_call(
        body, name=name, grid=(r // tr, c // tc),
        in_specs=[blk] * 4, out_specs=[blk] * 4, out_shape=[shape] * 4,
        compiler_params=_params("parallel", "parallel"),
    )(w, m, v, g)


def adamw_halves(w, m, v, mine, theirs, name, col_halves=False):
    r, c = w.shape
    hr, hc = (r, c // 2) if col_halves else (r // 2, c)
    tr, tc = _tiles(hr, hc, 1 << 18, multiple=8)
    g_cols = mine.shape[-1]
    padded = g_cols != hc
    if padded and tc != hc:
        raise ValueError("a gradient with padded columns is read in full-width row blocks")
    per_half = hc // tc if col_halves else hr // tr
    core = lax.axis_index("c").astype(jnp.int32).reshape(1)

    def body(core_ref, w_ref, m_ref, v_ref, mine_ref, theirs_ref, g_ref, d_ref, nm_ref, nv_ref):
        half = pl.program_id(1 if col_halves else 0) // per_half
        cols = slice(0, hc) if padded else slice(None)
        g = jnp.where(half == core_ref[0], mine_ref[:, cols], theirs_ref[:, cols])
        delta, nm, nv = _adamw_math(w_ref[...], g, m_ref[...], v_ref[...])
        g_ref[...] = g
        d_ref[...] = delta
        nm_ref[...] = nm
        nv_ref[...] = nv

    blk = pl.BlockSpec((tr, tc), lambda i, j, core_ref: (i, j))

    def g_spec(wanted):
        def index(i, j, core_ref):
            use = ((j if col_halves else i) // per_half == core_ref[0]) == wanted
            if col_halves:
                return i, jnp.where(use, j % per_half, 0)
            return jnp.where(use, i % per_half, 0), j
        return pl.BlockSpec((tr, g_cols if padded else tc), index)

    shape = jax.ShapeDtypeStruct((r, c), F32)
    return pl.pallas_call(
        body, name=name,
        grid_spec=pltpu.PrefetchScalarGridSpec(
            num_scalar_prefetch=1, grid=(r // tr, c // tc),
            in_specs=[blk, blk, blk, g_spec(True), g_spec(False)], out_specs=[blk] * 4),
        out_shape=[shape] * 4,
        compiler_params=_params("parallel", "parallel"),
    )(core, w, m, v, mine, theirs)


MATRICES = ("ffn1_w_gate", "ffn1_w_up", "ffn1_w_down", "w_in", "w_out", "ffn2_w_gate", "ffn2_w_up", "ffn2_w_down")
VECTORS = ("ffn1_norm", "mix_norm", "swa_sinks", "swa_out_norm", "sb_out_norm", "ffn2_norm", "final_norm")
WEIGHTS = ("ffn1_norm", "ffn1_w_gate", "ffn1_w_up", "ffn1_w_down", "mix_norm", "w_in", "swa_sinks", "swa_out_norm",
           "sb_out_norm", "w_out", "ffn2_norm", "ffn2_w_gate", "ffn2_w_up", "ffn2_w_down", "final_norm")


def _pack_rows(vec):
    flat = vec.reshape(-1).astype(F32)
    rows = -(-flat.shape[0] // LANES)
    rows8 = -(-rows // 8) * 8
    flat = jnp.pad(flat, (0, rows8 * LANES - flat.shape[0]))
    return flat.reshape(rows8, LANES)


class GradPipe:
    def __init__(self, name, grads):
        self.name, self.step, self.value = name, 0, grads

    def rider(self):
        make = (half_swap_rider, scatter_rider, final_swap_rider)[self.step]
        return make([self.value])

    def landed(self, outs):
        if self.step == 0:
            self.value = sum_cores(self.value, outs[0], "sum_cores_" + self.name)
        elif self.step == 1:
            self.value = sum_chips(self.value, outs[0], "sum_chips_" + self.name)
        else:
            self.value = (self.value, outs[0])
        self.step += 1


def _hosted(fn, *args, advance=(), **kw):
    outs, routs = fn(*args, riders=[p.rider() for p in advance], **kw)
    for p, ro in zip(advance, routs):
        p.landed(ro)
    return outs


def kernel(x, ffn1_norm, ffn1_w_gate, ffn1_w_up, ffn1_w_down, mix_norm, w_in, swa_sinks, swa_out_norm, sb_out_norm, w_out, ffn2_norm, ffn2_w_gate, ffn2_w_up, ffn2_w_down, final_norm, loss_target, m_ffn1_norm, m_ffn1_w_gate, m_ffn1_w_up, m_ffn1_w_down, m_mix_norm, m_w_in, m_swa_sinks, m_swa_out_norm, m_sb_out_norm, m_w_out, m_ffn2_norm, m_ffn2_w_gate, m_ffn2_w_up, m_ffn2_w_down, m_final_norm, v_ffn1_norm, v_ffn1_w_gate, v_ffn1_w_up, v_ffn1_w_down, v_mix_norm, v_w_in, v_swa_sinks, v_swa_out_norm, v_sb_out_norm, v_w_out, v_ffn2_norm, v_ffn2_w_gate, v_ffn2_w_up, v_ffn2_w_down, v_final_norm):
    given = dict(locals())
    b_loc, seq, D = x.shape
    T = b_loc * seq
    x0 = x.reshape(T, D)
    target = loss_target.reshape(T, D)
    final_g = final_norm.reshape(1, D)

    gathered = {}

    def gather(*names):
        shards = []
        for nm in names:
            w = given[nm][0].astype(BF16)
            hidden_axis = 0 if nm.endswith("w_down") else 1 if nm.startswith("ffn") else None
            if hidden_axis is not None:
                pad = -w.shape[hidden_axis] % LANES
                w = jnp.pad(w, [(0, pad) if a == hidden_axis else (0, 0) for a in range(2)])
            shards.append(w.reshape(2, w.shape[0] // 2, w.shape[1]))
        return names, gather_rider(shards)

    def hosting(fn, *args, fetch):
        names, rider = fetch
        outs, landed = fn(*args, riders=[rider])
        for nm, o in zip(names, landed[0]):
            gathered[nm] = o.reshape(N_CHIPS, 2 * o.shape[2], o.shape[3])
        return outs

    hosting(lambda riders: ([], exchange_now(riders, "gather_ffn1_gate")), fetch=gather("ffn1_w_gate"))

    h1 = rmsnorm_cast(x0, ffn1_norm, "ffn1_norm_fwd")
    (g1,) = hosting(ffn_gate, h1, gathered["ffn1_w_gate"], "ffn1_gate", fetch=gather("ffn1_w_up"))
    u1, a1 = hosting(ffn_up_act, h1, gathered["ffn1_w_up"], g1, "ffn1_up_act", fetch=gather("ffn1_w_down"))
    (x1,) = hosting(ffn_down, a1, gathered["ffn1_w_down"], x0, "ffn1_down", fetch=gather("w_in"))
    h2 = rmsnorm_cast(x1, mix_norm, "mix_norm_fwd")
    (proj,) = hosting(in_proj, h2, gathered["w_in"], "in_proj", fetch=gather("w_out"))
    w_out_all = gathered["w_out"].reshape(-1, D)
    (ya,) = hosting(swa_fwd, proj, swa_sinks, b_loc, seq, "swa_fwd", fetch=gather("ffn2_w_gate"))
    yb, ltot = hosting(sb_fwd, proj, b_loc, seq, "sb_fwd", fetch=gather("ffn2_w_up", "ffn2_w_down"))
    y = out_norm(ya, yb, swa_out_norm, sb_out_norm, "out_norm_fwd")
    x2 = out_proj(y, w_out_all, x1, "out_proj")
    h3 = rmsnorm_cast(x2, ffn2_norm, "ffn2_norm_fwd")
    (g2, u2, a2), _ = ffn_up(h3, gathered["ffn2_w_gate"], gathered["ffn2_w_up"], "ffn2_up")
    (x3,), _ = ffn_down(a2, gathered["ffn2_w_down"], x2, "ffn2_down")
    dx3, d_final, loss_part = final_loss(x3, final_g, target, "final_loss")

    Fs = gathered["ffn1_w_gate"].shape[2]
    Ps = gathered["w_in"].shape[2]
    Os = w_out_all.shape[0] // N_CHIPS
    pipes = {}

    def down_grad(tag, act, dx, advance):
        (grads,) = _hosted(wgrad, act, dx, tag + "_dwd", groups=N_CHIPS, m=Fs, n=D // 2,
                           a_block=lambda tk: (None, tk, Fs), a_map=lambda s, j, k: (s, k, 0),
                           b_block=lambda tk: (tk, D // 2), b_map=lambda s, j, k: (k, j), b_scale=0.5,
                           advance=advance)
        return grads

    def up_grad(name, h, d_act, advance):
        (grads,) = _hosted(wgrad, h, d_act, name, groups=N_CHIPS, m=D // 2, n=Fs,
                           a_block=lambda tk: (tk, D // 2), a_map=lambda s, j, k: (k, j),
                           b_block=lambda tk: (None, tk, Fs), b_map=lambda s, j, k: (s, k, 0), advance=advance)
        return grads

    (dg2, du2), _ = ffn_bwd_act(dx3, gathered["ffn2_w_down"], g2, u2, "ffn2_bwd_act")
    pipes["ffn2_w_down"] = p_d2 = GradPipe("ffn2_w_down", down_grad("ffn2", a2, dx3, []))
    pipes["ffn2_w_gate"] = p_g2 = GradPipe("ffn2_w_gate", up_grad("ffn2_dwg", h3, dg2, [p_d2]))
    pipes["ffn2_w_up"] = p_u2 = GradPipe("ffn2_w_up", up_grad("ffn2_dwu", h3, du2, [p_d2, p_g2]))
    (dh3,) = _hosted(ffn_bwd_dh, dg2, du2, gathered["ffn2_w_gate"], gathered["ffn2_w_up"], "ffn2_bwd_dh",
                     advance=[p_d2, p_g2, p_u2])
    (dx2, d_ffn2_norm), _ = rmsnorm_bwd_add(x2, ffn2_norm, dh3, dx3, "ffn2_norm_bwd")

    (d_w_out,) = _hosted(wgrad, y, dx2, "dw_out", groups=N_CHIPS, m=Os // 2, n=D,
                         a_block=lambda tk: (tk, Os // 2), a_map=lambda s, j, k: (k, 2 * s + j),
                         b_block=lambda tk: (tk, D), b_map=lambda s, j, k: (k, 0), advance=[p_g2])
    pipes["w_out"] = p_out = GradPipe("w_out", d_w_out)
    dya, dyb, d_swa_norm, d_sb_norm = out_proj_bwd(dx2, w_out_all, ya, yb, swa_out_norm, sb_out_norm, "out_proj_bwd")
    dproj, d_sinks = _hosted(swa_bwd, proj, swa_sinks, dya, b_loc, seq, "swa_bwd", advance=[p_u2, p_out])
    (dproj,) = _hosted(sb_bwd, proj, dyb, ltot, dproj, b_loc, seq, "sb_bwd", advance=[p_u2, p_out])
    (d_w_in,) = _hosted(wgrad, h2, dproj, "dw_in", groups=N_CHIPS, m=D // 2, n=Ps,
                        a_block=lambda tk: (tk, D // 2), a_map=lambda s, j, k: (k, j),
                        b_block=lambda tk: (tk, Ps), b_map=lambda s, j, k: (k, s), advance=[p_out])
    pipes["w_in"] = p_in = GradPipe("w_in", d_w_in)
    (dh2,) = _hosted(in_proj_bwd, dproj, gathered["w_in"], "in_proj_bwd", advance=[p_in])
    (dx1, d_mix_norm), _ = rmsnorm_bwd_add(x1, mix_norm, dh2, dx2, "mix_norm_bwd")

    pipes["ffn1_w_down"] = p_d1 = GradPipe("ffn1_w_down", down_grad("ffn1", a1, dx1, [p_in]))
    dg1, du1 = _hosted(ffn_bwd_act, dx1, gathered["ffn1_w_down"], g1, u1, "ffn1_bwd_act", advance=[p_in, p_d1])
    pipes["ffn1_w_gate"] = p_g1 = GradPipe("ffn1_w_gate", up_grad("ffn1_dwg", h1, dg1, [p_d1]))
    pipes["ffn1_w_up"] = p_u1 = GradPipe("ffn1_w_up", up_grad("ffn1_dwu", h1, du1, [p_d1, p_g1]))
    (dh1,) = _hosted(ffn_bwd_dh, dg1, du1, gathered["ffn1_w_gate"], gathered["ffn1_w_up"], "ffn1_bwd_dh_a",
                     part=(0, 2), advance=[p_g1, p_u1])
    (dh1,) = _hosted(ffn_bwd_dh, dg1, du1, gathered["ffn1_w_gate"], gathered["ffn1_w_up"], "ffn1_bwd_dh_b",
                     part=(1, 2), prev=dh1, advance=[p_g1, p_u1])
    (dx0, d_ffn1_norm), _ = rmsnorm_bwd_add(x0, ffn1_norm, dh1, dx1, "ffn1_norm_bwd")
    (last,) = exchange_now([p_u1.rider()], "swap_ffn1_w_up")
    p_u1.landed(last)

    vec_grads = dict(ffn1_norm=d_ffn1_norm, mix_norm=d_mix_norm, swa_sinks=d_sinks[:, 0], swa_out_norm=d_swa_norm,
                     sb_out_norm=d_sb_norm, ffn2_norm=d_ffn2_norm, final_norm=d_final)
    packed = [_pack_rows(vec_grads[nm]) for nm in VECTORS] + [_pack_rows(loss_part[0, :1])]
    offsets = [0]
    for p in packed:
        offsets.append(offsets[-1] + p.shape[0])
    reduced = allreduce_small(jnp.concatenate(packed, axis=0), "reduce_vectors")
    loss = reduced[offsets[len(VECTORS)], 0]

    out = {}
    for nm in MATRICES:
        shape = given[nm].shape
        mine, theirs = pipes[nm].value
        res = adamw_halves(given[nm][0], given["m_" + nm][0], given["v_" + nm][0], mine, theirs, "adamw_" + nm,
                           col_halves=nm.endswith("w_down"))
        out[nm] = [r.reshape(shape) for r in res]
    w_rows = jnp.concatenate([_pack_rows(given[nm]) for nm in VECTORS], axis=0)
    m_rows = jnp.concatenate([_pack_rows(given["m_" + nm]) for nm in VECTORS], axis=0)
    v_rows = jnp.concatenate([_pack_rows(given["v_" + nm]) for nm in VECTORS], axis=0)
    g_rows = reduced[:offsets[len(VECTORS)]]
    small = adamw(w_rows, m_rows, v_rows, g_rows, "adamw_vectors")
    for i, nm in enumerate(VECTORS):
        shape = given[nm].shape
        size = math.prod(shape)
        out[nm] = [r[offsets[i]:offsets[i + 1]].reshape(-1)[:size].reshape(shape) for r in small]

    grad_x = dx0.reshape(b_loc, seq, D)
    return (loss, grad_x, *[out[nm][0] for nm in WEIGHTS], *[out[nm][1] for nm in WEIGHTS],
            *[out[nm][2] for nm in WEIGHTS], *[out[nm][3] for nm in WEIGHTS])
```

```python
import functools
import math

import jax
import jax.numpy as jnp
from jax import lax
from jax.experimental import pallas as pl
from jax.experimental.pallas import tpu as pltpu

F32 = jnp.float32
BF16 = jnp.bfloat16
MESH = pl.DeviceIdType.MESH

EPS = 1e-6
HEAD_DIM = 64
LANES = 128
N_SWA_HEADS = 16
N_SWA_KV = 4
N_SB_HEADS = 16
WINDOW = 128
SWA_BLOCK = 128
SB_BLOCK = 256
N_CHIPS = 4
ATT_SCALE = HEAD_DIM ** -0.5

ADAM_LR = 0.001
ADAM_B1 = 0.9
ADAM_B2 = 0.999
ADAM_EPS = 1e-08
ADAM_WD = 0.01
ADAM_STEP = 10

VMEM_LIMIT = 56 * 1024 * 1024
NEG_BIG = -1e30

ANY = pl.BlockSpec(memory_space=pl.ANY)
VMEM_WHOLE = pl.BlockSpec(memory_space=pltpu.VMEM)


def _row_tile(rows, cap, multiple=16):
    best = None
    for t in range(multiple, min(rows, cap) + 1, multiple):
        if rows % t == 0:
            best = t
    if best is None:
        raise ValueError(f"no row tile for {rows} rows under {cap}")
    return best


def _params(*sem):
    return pltpu.CompilerParams(dimension_semantics=sem, vmem_limit_bytes=VMEM_LIMIT)


def _dot(a, b):
    return jnp.dot(a, b, preferred_element_type=F32)


def _dot_nt(a, b):
    return lax.dot_general(a, b, (((1,), (1,)), ((), ())), preferred_element_type=F32)


def _dot_tn(a, b):
    return lax.dot_general(a, b, (((0,), (0,)), ((), ())), preferred_element_type=F32)


def _sigmoid(x):
    return 1.0 / (1.0 + jnp.exp(-x))


def _split_dot(x, u):
    hi = x.astype(BF16)
    lo = (x - hi.astype(F32)).astype(BF16)
    return _dot(hi, u) + _dot(lo, u)


def _tiles(r, c, max_elems, multiple=16):
    col_opts = [c] + [t for t in (2048, 1024, 512, 256, 128) if t < c and c % t == 0]
    best = None
    for tc in col_opts:
        for tr in [r] + list(range(multiple, r, multiple)):
            if r % tr == 0 and tr * tc <= max_elems and (best is None or tr * tc > best[0] * best[1]):
                best = (tr, tc)
    if best is None:
        raise ValueError(f"no tile for [{r}, {c}] under {max_elems} elements")
    return best


class Rider:
    def __init__(self, arrays, out_shapes, sem_shapes, start, finish):
        self.arrays, self.out_shapes, self.sem_shapes = list(arrays), list(out_shapes), list(sem_shapes)
        self.start, self.finish = start, finish


def _place():
    return lax.axis_index("x"), lax.axis_index("y"), lax.axis_index("c")


def _other_chips(x, y):
    return [(1 - x, y), (x, 1 - y), (1 - x, 1 - y)]


def _split(ref):
    rows = ref.shape[-2]
    n = next(k for k in (4, 2, 1) if rows % (16 * k) == 0)
    step = rows // n
    lead = [()] if len(ref.shape) == 2 else [(s,) for s in range(ref.shape[0])]
    return [ref.at[(*idx, pl.ds(i * step, step))] for idx in lead for i in range(n)]


class _Copy:
    def __init__(self, src, dst, make):
        self.src, self.dst, self.make = src, dst, make

    def start(self):
        for s, d in zip(_split(self.src), _split(self.dst)):
            self.make(s, d).start()

    def wait(self):
        self.make(self.src, self.dst).wait()

    def wait_send(self):
        self.make(self.src, self.dst).wait_send()

    def wait_recv(self):
        self.make(self.src, self.dst).wait_recv()


def _remote(src, dst, send_sem, recv_sem, device):
    return _Copy(src, dst, lambda s, d: pltpu.make_async_remote_copy(
        src_ref=s, dst_ref=d, send_sem=send_sem, recv_sem=recv_sem, device_id=device, device_id_type=MESH))


def gather_rider(shards):
    n = len(shards)

    def copies(ins, outs, sems):
        ici_s, ici_r, d2d_s, d2d_r, own_s, own_r = sems
        x, y, c = _place()
        me = 2 * x + y
        own, ici, fwd = [], [], []
        for k in range(n):
            own.append(_remote(ins[k], outs[k].at[me], own_s.at[k], own_r.at[k], (x, y, 1 - c)))
            for j, (px, py) in enumerate(_other_chips(x, y)):
                i = 3 * k + j
                ici.append(_remote(ins[k].at[c], outs[k].at[me, c], ici_s.at[i], ici_r.at[i], (px, py, c)))
                got = outs[k].at[2 * px + py, c]
                fwd.append(_remote(got, got, d2d_s.at[i], d2d_r.at[i], (x, y, 1 - c)))
        return own, ici, fwd

    def start(ins, outs, sems):
        own, ici, _ = copies(ins, outs, sems)
        for cp in own + ici:
            cp.start()

    def finish(ins, outs, sems):
        own, ici, fwd = copies(ins, outs, sems)
        for a, f in zip(ici, fwd):
            a.wait_recv()
            f.start()
        for a in ici:
            a.wait_send()
        for f in fwd:
            f.wait()
        for o in own:
            o.wait()

    dma = pltpu.SemaphoreType.DMA
    return Rider(shards, [jax.ShapeDtypeStruct((N_CHIPS,) + s.shape, s.dtype) for s in shards],
                 [dma((3 * n,)), dma((3 * n,)), dma((3 * n,)), dma((3 * n,)), dma((n,)), dma((n,))], start, finish)


def half_swap_rider(grads):
    n = len(grads)

    def copies(ins, outs, sems):
        x, y, c = _place()
        return [_remote(ins[k].at[1 - c], outs[k], sems[0].at[k], sems[1].at[k], (x, y, 1 - c)) for k in range(n)]

    def start(ins, outs, sems):
        for cp in copies(ins, outs, sems):
            cp.start()

    def finish(ins, outs, sems):
        for cp in copies(ins, outs, sems):
            cp.wait()

    dma = pltpu.SemaphoreType.DMA
    return Rider(grads, [jax.ShapeDtypeStruct(g.shape[1:], g.dtype) for g in grads], [dma((n,)), dma((n,))],
                 start, finish)


def scatter_rider(halves):
    n = len(halves)

    def copies(ins, outs, sems):
        send, recv = sems
        x, y, c = _place()
        cps = []
        for k in range(n):
            for j, (px, py) in enumerate(_other_chips(x, y)):
                i = 3 * k + j
                cps.append(_remote(ins[k].at[2 * px + py], outs[k].at[j], send.at[i], recv.at[i], (px, py, c)))
        return cps

    def start(ins, outs, sems):
        for cp in copies(ins, outs, sems):
            cp.start()

    def finish(ins, outs, sems):
        for cp in copies(ins, outs, sems):
            cp.wait()

    dma = pltpu.SemaphoreType.DMA
    return Rider(halves, [jax.ShapeDtypeStruct((3,) + h.shape[1:], h.dtype) for h in halves],
                 [dma((3 * n,)), dma((3 * n,))], start, finish)


def final_swap_rider(sums):
    n = len(sums)

    def copies(ins, outs, sems):
        send, recv = sems
        x, y, c = _place()
        return [_remote(ins[k], outs[k], send.at[k], recv.at[k], (x, y, 1 - c)) for k in range(n)]

    def start(ins, outs, sems):
        for cp in copies(ins, outs, sems):
            cp.start()

    def finish(ins, outs, sems):
        for cp in copies(ins, outs, sems):
            cp.wait()

    dma = pltpu.SemaphoreType.DMA
    return Rider(sums, [jax.ShapeDtypeStruct(s.shape, s.dtype) for s in sums], [dma((n,)), dma((n,))],
                 start, finish)


def _call(body, *, name, grid, in_specs, out_specs, out_shape, scratch_shapes=(), sem, args, riders=(), aliases=None):
    in_specs, out_specs, out_shape = list(in_specs), list(out_specs), list(out_shape)
    scratch_shapes = list(scratch_shapes)
    aliases = aliases or {}
    if not riders:
        outs = pl.pallas_call(body, name=name, grid=grid, in_specs=in_specs, out_specs=out_specs, out_shape=out_shape,
                              scratch_shapes=scratch_shapes, input_output_aliases=aliases,
                              compiler_params=_params(*sem))(*args)
        return list(outs), []
    n_in, n_out, n_scr = len(in_specs), len(out_specs), len(scratch_shapes)
    r_in = [len(r.arrays) for r in riders]
    r_out = [len(r.out_shapes) for r in riders]
    r_sem = [len(r.sem_shapes) for r in riders]

    def wrapped(*refs):
        pos = 0

        def take(count):
            nonlocal pos
            got = refs[pos:pos + count]
            pos += count
            return got

        ins = take(n_in)
        rins = [take(k) for k in r_in]
        outs = take(n_out)
        routs = [take(k) for k in r_out]
        scr = take(n_scr)
        rsems = [take(k) for k in r_sem]
        first = functools.reduce(jnp.logical_and, [pl.program_id(a) == 0 for a in range(len(grid))])
        last = functools.reduce(jnp.logical_and, [pl.program_id(a) == grid[a] - 1 for a in range(len(grid))])

        @pl.when(first)
        def _():
            for r, ri, ro, rs in zip(riders, rins, routs, rsems):
                r.start(ri, ro, rs)

        body(*ins, *outs, *scr)

        @pl.when(last)
        def _():
            for r, ri, ro, rs in zip(riders, rins, routs, rsems):
                r.finish(ri, ro, rs)

    any_spec = pl.BlockSpec(memory_space=pl.ANY)
    res = pl.pallas_call(
        wrapped, name=name, grid=grid,
        in_specs=in_specs + [any_spec] * sum(r_in),
        out_specs=out_specs + [any_spec] * sum(r_out),
        out_shape=out_shape + [s for r in riders for s in r.out_shapes],
        scratch_shapes=scratch_shapes + [s for r in riders for s in r.sem_shapes],
        input_output_aliases=aliases,
        compiler_params=_params(*(["arbitrary"] * len(grid))),
    )(*args, *[a for r in riders for a in r.arrays])
    res = list(res)
    outs, rest = res[:n_out], res[n_out:]
    routs = []
    for k in r_out:
        routs.append(rest[:k])
        rest = rest[k:]
    return outs, routs


def exchange_now(riders, name):
    n_in = [len(r.arrays) for r in riders]
    n_out = [len(r.out_shapes) for r in riders]
    n_sem = [len(r.sem_shapes) for r in riders]

    def body(*refs):
        pos = 0
        groups = []
        for counts in (n_in, n_out, n_sem):
            part = []
            for k in counts:
                part.append(refs[pos:pos + k])
                pos += k
            groups.append(part)
        for r, ri, ro, rs in zip(riders, *groups):
            r.start(ri, ro, rs)
        for r, ri, ro, rs in zip(riders, *groups):
            r.finish(ri, ro, rs)

    any_spec = pl.BlockSpec(memory_space=pl.ANY)
    res = pl.pallas_call(
        body, name=name,
        in_specs=[any_spec] * sum(n_in), out_specs=[any_spec] * sum(n_out),
        out_shape=[s for r in riders for s in r.out_shapes],
        scratch_shapes=[s for r in riders for s in r.sem_shapes],
    )(*[a for r in riders for a in r.arrays])
    res = list(res)
    routs = []
    for k in n_out:
        routs.append(res[:k])
        res = res[k:]
    return routs


def rmsnorm_cast(x, gamma, name):
    T, D = x.shape
    tr = min(512, T)

    def body(x_ref, g_ref, o_ref):
        xv = x_ref[...]
        r = lax.rsqrt(jnp.mean(xv * xv, axis=-1, keepdims=True) + EPS)
        o_ref[...] = ((xv * r) * g_ref[...]).astype(BF16)

    return pl.pallas_call(
        body, name=name, grid=(T // tr,),
        in_specs=[pl.BlockSpec((tr, D), lambda i: (i, 0)), pl.BlockSpec((1, D), lambda i: (0, 0))],
        out_specs=pl.BlockSpec((tr, D), lambda i: (i, 0)),
        out_shape=jax.ShapeDtypeStruct((T, D), BF16),
        compiler_params=_params("parallel"),
    )(x, gamma)


def _rms_bwd(xv, gamma, dy):
    d = xv.shape[-1]
    r = lax.rsqrt(jnp.mean(xv * xv, axis=-1, keepdims=True) + EPS)
    gdy = dy * gamma
    dot = jnp.sum(gdy * xv, axis=-1, keepdims=True)
    dx = r * gdy - xv * ((r * r * r) * (dot / d))
    dgamma = jnp.sum(dy * (xv * r), axis=0, keepdims=True)
    return dx, dgamma


def rmsnorm_bwd_add(x, gamma, dy, dx_up, name, riders=()):
    T, D = x.shape
    tr = min(256, T)

    def body(x_ref, g_ref, dy_ref, up_ref, dx_ref, dg_ref):
        dx, dgamma = _rms_bwd(x_ref[...], g_ref[...], dy_ref[...])
        dx_ref[...] = up_ref[...] + dx

        @pl.when(pl.program_id(0) == 0)
        def _():
            dg_ref[...] = jnp.zeros_like(dg_ref)

        dg_ref[...] += dgamma

    row = pl.BlockSpec((tr, D), lambda i: (i, 0))
    vec = pl.BlockSpec((1, D), lambda i: (0, 0))
    return _call(
        body, name=name, grid=(T // tr,),
        in_specs=[row, vec, row, row], out_specs=[row, vec],
        out_shape=[jax.ShapeDtypeStruct((T, D), F32), jax.ShapeDtypeStruct((1, D), F32)],
        sem=("arbitrary",), args=(x, gamma, dy, dx_up), riders=riders)


def _column_halves(width):
    tiles = width // LANES
    cut = (tiles - tiles // 2) * LANES
    return (slice(0, cut), slice(cut, width)) if 0 < cut < width else (slice(0, width),)


def ffn_up(h, wg, wu, name, riders=()):
    T, D = h.shape
    G, _, Fs = wg.shape
    tm = min(512, T)

    def body(h_ref, wg_ref, wu_ref, g_ref, u_ref, a_ref):
        hv = h_ref[...]
        for cols in _column_halves(Fs):
            g = _dot(hv, wg_ref[:, cols])
            u = _dot(hv, wu_ref[:, cols])
            g_ref[:, cols] = g.astype(BF16)
            u_ref[:, cols] = u.astype(BF16)
            a_ref[:, cols] = ((g * _sigmoid(g)) * u).astype(BF16)

    w_spec = pl.BlockSpec((None, D, Fs), lambda s, i: (s, 0, 0), pipeline_mode=pl.Buffered(1))
    o_spec = pl.BlockSpec((None, tm, Fs), lambda s, i: (s, i, 0))
    return _call(
        body, name=name, grid=(G, T // tm),
        in_specs=[pl.BlockSpec((tm, D), lambda s, i: (i, 0)), w_spec, w_spec],
        out_specs=[o_spec, o_spec, o_spec],
        out_shape=[jax.ShapeDtypeStruct((G, T, Fs), BF16)] * 3,
        sem=("parallel", "parallel"), args=(h, wg, wu), riders=riders)


def ffn_gate(h, wg, name, riders=()):
    T, D = h.shape
    G, _, Fs = wg.shape
    tm = min(512, T)

    def body(h_ref, wg_ref, g_ref):
        g_ref[...] = _dot(h_ref[...], wg_ref[...]).astype(BF16)

    return _call(
        body, name=name, grid=(G, T // tm),
        in_specs=[pl.BlockSpec((tm, D), lambda s, i: (i, 0)),
                  pl.BlockSpec((None, D, Fs), lambda s, i: (s, 0, 0), pipeline_mode=pl.Buffered(1))],
        out_specs=[pl.BlockSpec((None, tm, Fs), lambda s, i: (s, i, 0))],
        out_shape=[jax.ShapeDtypeStruct((G, T, Fs), BF16)],
        sem=("parallel", "parallel"), args=(h, wg), riders=riders)


def ffn_up_act(h, wu, g, name, riders=()):
    T, D = h.shape
    G, _, Fs = wu.shape
    tm = min(512, T)

    def body(h_ref, wu_ref, g_ref, u_ref, a_ref):
        u = _dot(h_ref[...], wu_ref[...])
        g = g_ref[...].astype(F32)
        u_ref[...] = u.astype(BF16)
        a_ref[...] = ((g * _sigmoid(g)) * u).astype(BF16)

    act = pl.BlockSpec((None, tm, Fs), lambda s, i: (s, i, 0))
    return _call(
        body, name=name, grid=(G, T // tm),
        in_specs=[pl.BlockSpec((tm, D), lambda s, i: (i, 0)),
                  pl.BlockSpec((None, D, Fs), lambda s, i: (s, 0, 0), pipeline_mode=pl.Buffered(1)), act],
        out_specs=[act, act],
        out_shape=[jax.ShapeDtypeStruct((G, T, Fs), BF16)] * 2,
        sem=("parallel", "parallel"), args=(h, wu, g), riders=riders)


def ffn_down(a, wd, x, name, riders=()):
    G, T, Fs = a.shape
    D = wd.shape[2]
    tm = min(512, T)

    def body(a_ref, w_ref, x_ref, o_ref, acc):
        s = pl.program_id(1)

        @pl.when(s == 0)
        def _():
            acc[...] = jnp.zeros_like(acc)

        acc[...] += _dot(a_ref[...], w_ref[...])

        @pl.when(s == G - 1)
        def _():
            o_ref[...] = x_ref[...] + 0.5 * acc[...]

    return _call(
        body, name=name, grid=(T // tm, G),
        in_specs=[pl.BlockSpec((None, tm, Fs), lambda i, s: (s, i, 0)),
                  pl.BlockSpec((None, Fs, D), lambda i, s: (s, 0, 0)),
                  pl.BlockSpec((tm, D), lambda i, s: (i, 0))],
        out_specs=[pl.BlockSpec((tm, D), lambda i, s: (i, 0))],
        out_shape=[jax.ShapeDtypeStruct((T, D), F32)],
        scratch_shapes=[pltpu.VMEM((tm, D), F32)],
        sem=("parallel", "arbitrary"), args=(a, wd, x), riders=riders)


def ffn_bwd_act(dx, wd, g, u, name, riders=()):
    T, D = dx.shape
    G, Fs, _ = wd.shape
    tm = min(512, T)

    def body(dx_ref, w_ref, g_ref, u_ref, dg_ref, du_ref):
        dxh = (0.5 * dx_ref[...]).astype(BF16)
        for cols in _column_halves(Fs):
            da = _dot_nt(dxh, w_ref[cols, :])
            gv = g_ref[:, cols].astype(F32)
            sig = _sigmoid(gv)
            silu = gv * sig
            du_ref[:, cols] = (da * silu).astype(BF16)
            dg_ref[:, cols] = ((da * u_ref[:, cols].astype(F32)) * (sig * (1.0 + gv * (1.0 - sig)))).astype(BF16)

    act = pl.BlockSpec((None, tm, Fs), lambda s, i: (s, i, 0))
    return _call(
        body, name=name, grid=(G, T // tm),
        in_specs=[pl.BlockSpec((tm, D), lambda s, i: (i, 0)),
                  pl.BlockSpec((None, Fs, D), lambda s, i: (s, 0, 0), pipeline_mode=pl.Buffered(1)),
                  act, act],
        out_specs=[act, act],
        out_shape=[jax.ShapeDtypeStruct((G, T, Fs), BF16), jax.ShapeDtypeStruct((G, T, Fs), BF16)],
        sem=("parallel", "parallel"), args=(dx, wd, g, u), riders=riders)


def ffn_bwd_dh(dg, du, wg, wu, name, riders=(), part=(0, 1), prev=None):
    G, T, Fs = dg.shape
    D = wg.shape[1]
    tm = min(512, T // part[1])
    nblk = T // tm // part[1]
    off = part[0] * nblk

    def body(dg_ref, du_ref, wg_ref, wu_ref, *rest):
        o_ref, acc = rest[-2:]
        s = pl.program_id(1)

        @pl.when(s == 0)
        def _():
            acc[...] = jnp.zeros_like(acc)

        acc[...] += _dot_nt(dg_ref[...], wg_ref[...])
        acc[...] += _dot_nt(du_ref[...], wu_ref[...])

        @pl.when(s == G - 1)
        def _():
            o_ref[...] = acc[...]

    act = pl.BlockSpec((None, tm, Fs), lambda i, s: (s, i + off, 0))
    w_spec = pl.BlockSpec((None, D, Fs), lambda i, s: (s, 0, 0))
    extra = [] if prev is None else [prev]
    return _call(
        body, name=name, grid=(nblk, G),
        in_specs=[act, act, w_spec, w_spec] + [pl.BlockSpec(memory_space=pl.ANY)] * len(extra),
        out_specs=[pl.BlockSpec((tm, D), lambda i, s: (i + off, 0))],
        out_shape=[jax.ShapeDtypeStruct((T, D), F32)],
        scratch_shapes=[pltpu.VMEM((tm, D), F32)],
        sem=("parallel", "arbitrary"), args=(dg, du, wg, wu, *extra), riders=riders,
        aliases={4: 0} if extra else None)


def wgrad(a, b, name, *, groups, m, n, a_block, a_map, b_block, b_map, b_scale=None, tk=1024, riders=()):
    T = a.shape[-2]
    tk = min(tk, T)
    nk = T // tk

    def body(a_ref, b_ref, o_ref, acc):
        k = pl.program_id(2)

        @pl.when(k == 0)
        def _():
            acc[...] = jnp.zeros_like(acc)

        bv = b_ref[...]
        if b_scale is not None:
            bv = b_scale * bv
        acc[...] += _dot_tn(a_ref[...].astype(BF16), bv.astype(BF16))

        @pl.when(k == nk - 1)
        def _():
            o_ref[...] = acc[...].astype(BF16)

    return _call(
        body, name=name, grid=(groups, 2, nk),
        in_specs=[pl.BlockSpec(a_block(tk), a_map), pl.BlockSpec(b_block(tk), b_map)],
        out_specs=[pl.BlockSpec((None, None, m, n), lambda s, j, k: (j, s, 0, 0))],
        out_shape=[jax.ShapeDtypeStruct((2, groups, m, n), BF16)],
        scratch_shapes=[pltpu.VMEM((m, n), F32)],
        sem=("parallel", "parallel", "arbitrary"), args=(a, b), riders=riders)


def in_proj(h, w_in, name, riders=()):
    T, D = h.shape
    G, _, Ps = w_in.shape
    tm = min(512, T)

    def body(h_ref, w_ref, o_ref):
        o_ref[...] = _dot(h_ref[...], w_ref[...]).astype(BF16)

    return _call(
        body, name=name, grid=(G, T // tm),
        in_specs=[pl.BlockSpec((tm, D), lambda s, i: (i, 0)),
                  pl.BlockSpec((None, D, Ps), lambda s, i: (s, 0, 0))],
        out_specs=[pl.BlockSpec((tm, Ps), lambda s, i: (i, s))],
        out_shape=[jax.ShapeDtypeStruct((T, G * Ps), BF16)],
        sem=("parallel", "parallel"), args=(h, w_in), riders=riders)


def in_proj_bwd(dproj, w_in, name, riders=()):
    T = dproj.shape[0]
    G, D, Ps = w_in.shape
    tm = min(512, T)

    def body(d_ref, w_ref, o_ref, acc):
        s = pl.program_id(1)

        @pl.when(s == 0)
        def _():
            acc[...] = jnp.zeros_like(acc)

        acc[...] += _dot_nt(d_ref[...], w_ref[...])

        @pl.when(s == G - 1)
        def _():
            o_ref[...] = acc[...]

    return _call(
        body, name=name, grid=(T // tm, G),
        in_specs=[pl.BlockSpec((tm, Ps), lambda i, s: (i, s)),
                  pl.BlockSpec((None, D, Ps), lambda i, s: (s, 0, 0))],
        out_specs=[pl.BlockSpec((tm, D), lambda i, s: (i, 0))],
        out_shape=[jax.ShapeDtypeStruct((T, D), F32)],
        scratch_shapes=[pltpu.VMEM((tm, D), F32)],
        sem=("parallel", "arbitrary"), args=(dproj, w_in), riders=riders)


def out_proj(y, w_out, x, name):
    T, M = y.shape
    D = w_out.shape[1]
    tm = min(512, T)

    def body(y_ref, w_ref, x_ref, o_ref):
        o_ref[...] = x_ref[...] + _dot(y_ref[...], w_ref[...])

    return pl.pallas_call(
        body, name=name, grid=(T // tm,),
        in_specs=[pl.BlockSpec((tm, M), lambda i: (i, 0)), pl.BlockSpec((M, D), lambda i: (0, 0)),
                  pl.BlockSpec((tm, D), lambda i: (i, 0))],
        out_specs=pl.BlockSpec((tm, D), lambda i: (i, 0)),
        out_shape=jax.ShapeDtypeStruct((T, D), F32),
        compiler_params=_params("parallel"),
    )(y, w_out, x)


def out_norm(ya, yb, ga, gb, name):
    T, W = ya.shape
    tr = min(512, T)

    def body(a_ref, b_ref, ga_ref, gb_ref, o_ref):
        for src, gam, col in ((a_ref, ga_ref, 0), (b_ref, gb_ref, W)):
            v = src[...]
            r = lax.rsqrt(jnp.mean(v * v, axis=-1, keepdims=True) + EPS)
            o_ref[:, col:col + W] = ((v * r) * gam[...]).astype(BF16)

    row = pl.BlockSpec((tr, W), lambda i: (i, 0))
    vec = pl.BlockSpec((1, W), lambda i: (0, 0))
    return pl.pallas_call(
        body, name=name, grid=(T // tr,),
        in_specs=[row, row, vec, vec],
        out_specs=pl.BlockSpec((tr, 2 * W), lambda i: (i, 0)),
        out_shape=jax.ShapeDtypeStruct((T, 2 * W), BF16),
        compiler_params=_params("parallel"),
    )(ya, yb, ga, gb)


def out_proj_bwd(dx, w_out, ya, yb, ga, gb, name):
    T, D = dx.shape
    M = w_out.shape[0]
    W = M // 2
    tm = min(256, T)

    def body(dx_ref, w_ref, a_ref, b_ref, ga_ref, gb_ref, da_ref, db_ref, dga_ref, dgb_ref):
        dy = _dot_nt(dx_ref[...].astype(BF16), w_ref[...])

        @pl.when(pl.program_id(0) == 0)
        def _():
            dga_ref[...] = jnp.zeros_like(dga_ref)
            dgb_ref[...] = jnp.zeros_like(dgb_ref)

        d_a, dg_a = _rms_bwd(a_ref[...], ga_ref[...], dy[:, :W])
        d_b, dg_b = _rms_bwd(b_ref[...], gb_ref[...], dy[:, W:])
        da_ref[...] = d_a
        db_ref[...] = d_b
        dga_ref[...] += dg_a
        dgb_ref[...] += dg_b

    row = pl.BlockSpec((tm, W), lambda i: (i, 0))
    vec = pl.BlockSpec((1, W), lambda i: (0, 0))
    return pl.pallas_call(
        body, name=name, grid=(T // tm,),
        in_specs=[pl.BlockSpec((tm, D), lambda i: (i, 0)), pl.BlockSpec((M, D), lambda i: (0, 0)),
                  row, row, vec, vec],
        out_specs=[row, row, vec, vec],
        out_shape=[jax.ShapeDtypeStruct((T, W), F32), jax.ShapeDtypeStruct((T, W), F32),
                   jax.ShapeDtypeStruct((1, W), F32), jax.ShapeDtypeStruct((1, W), F32)],
        compiler_params=_params("arbitrary"),
    )(dx, w_out, ya, yb, ga, gb)


def final_loss(x, gamma, target, name):
    T, D = x.shape
    tr = min(256, T)

    def body(x_ref, g_ref, t_ref, dx_ref, dg_ref, loss_ref):
        xv = x_ref[...]
        gam = g_ref[...]
        r = lax.rsqrt(jnp.mean(xv * xv, axis=-1, keepdims=True) + EPS)
        err = (xv * r) * gam - t_ref[...]
        part = 0.5 * jnp.sum(jnp.mean(err * err, axis=-1, keepdims=True), axis=0, keepdims=True)
        dx, dgamma = _rms_bwd(xv, gam, err / D)
        dx_ref[...] = dx

        @pl.when(pl.program_id(0) == 0)
        def _():
            dg_ref[...] = jnp.zeros_like(dg_ref)
            loss_ref[...] = jnp.zeros_like(loss_ref)

        dg_ref[...] += dgamma
        loss_ref[...] += jnp.broadcast_to(part, loss_ref.shape)

    row = pl.BlockSpec((tr, D), lambda i: (i, 0))
    vec = pl.BlockSpec((1, D), lambda i: (0, 0))
    return pl.pallas_call(
        body, name=name, grid=(T // tr,),
        in_specs=[row, vec, row],
        out_specs=[row, vec, pl.BlockSpec((1, LANES), lambda i: (0, 0))],
        out_shape=[jax.ShapeDtypeStruct((T, D), F32), jax.ShapeDtypeStruct((1, D), F32),
                   jax.ShapeDtypeStruct((1, LANES), F32)],
        compiler_params=_params("arbitrary"),
    )(x, gamma, target)


def _lane_select(src_half, dst_halves):
    r = lax.broadcasted_iota(jnp.int32, (LANES, LANES), 0)
    c = lax.broadcasted_iota(jnp.int32, (LANES, LANES), 1)
    hit = ((r >> 6) == src_half) & ((r & (HEAD_DIM - 1)) == (c & (HEAD_DIM - 1)))
    ok = functools.reduce(jnp.logical_or, [(c >> 6) == d for d in dst_halves])
    return jnp.where(hit & ok, 1.0, 0.0).astype(BF16)


def _swa_slope(kvp, local_head):
    lo = 2.0 ** (-8.0 * (local_head + 1) / N_SWA_HEADS)
    hi = 2.0 ** (-8.0 * (8 + local_head + 1) / N_SWA_HEADS)
    return jnp.where(kvp == 0, jnp.float32(lo), jnp.float32(hi))


GQA = N_SWA_HEADS // N_SWA_KV


def _per_head_rows(values):
    return jnp.concatenate([jnp.zeros((SWA_BLOCK, 1), F32) + v for v in values], axis=0)


def _swa_probs(qblk, kwin, n, start, slope, sink):
    z = _dot_nt(qblk, kwin) * ATT_SCALE
    t = n * SWA_BLOCK + (lax.broadcasted_iota(jnp.int32, z.shape, 0) & (SWA_BLOCK - 1))
    s = start + lax.broadcasted_iota(jnp.int32, z.shape, 1)
    dist = t - s
    valid = (dist >= 0) & (dist < WINDOW)
    z = jnp.where(valid, z - slope * dist.astype(F32), NEG_BIG)
    m = jnp.maximum(jnp.max(z, axis=-1, keepdims=True), sink)
    e = jnp.exp(z - m)
    den = jnp.sum(e, axis=-1, keepdims=True) + jnp.exp(sink - m)
    return e / den, m, den


def swa_fwd(proj, sinks, b_loc, seq, name, riders=()):
    T = proj.shape[0]
    nblk = seq // SWA_BLOCK

    def body(sink_ref, q_ref, k_ref, v_ref, o_ref, qs, vv):
        kvp = pl.program_id(1)
        lane = lax.broadcasted_iota(jnp.int32, (SWA_BLOCK, LANES), 1)
        for kvh in range(2):
            vv[...] = _dot(v_ref[...], _lane_select(kvh, (0, 1))).astype(BF16)
            for g in range(GQA):
                cols = slice(LANES * (2 * kvh + g // 2), LANES * (2 * kvh + g // 2 + 1))
                qs[g] = _dot(q_ref[:, cols], _lane_select(g % 2, (kvh,))).astype(BF16)
            slope = _per_head_rows([_swa_slope(kvp, GQA * kvh + g) for g in range(GQA)])
            sink = _per_head_rows([sink_ref[0, 8 * kvp + GQA * kvh + g] for g in range(GQA)])

            def blk(n, carry):
                start = pl.multiple_of(jnp.maximum(n - 1, 0) * SWA_BLOCK, SWA_BLOCK)
                rows = pl.ds(pl.multiple_of(n * SWA_BLOCK, SWA_BLOCK), SWA_BLOCK)
                win = pl.ds(start, 2 * SWA_BLOCK)
                qblk = jnp.concatenate([qs[g, rows, :] for g in range(GQA)], axis=0)
                p, _, _ = _swa_probs(qblk, k_ref[win, :], n, start, slope, sink)
                out = _dot(p.astype(BF16), vv[win, :])
                for pair in range(GQA // 2):
                    cols = slice(LANES * (2 * kvh + pair), LANES * (2 * kvh + pair + 1))
                    first = out[2 * pair * SWA_BLOCK:(2 * pair + 1) * SWA_BLOCK]
                    second = out[(2 * pair + 1) * SWA_BLOCK:(2 * pair + 2) * SWA_BLOCK]
                    o_ref[rows, cols] = jnp.where(lane < HEAD_DIM, first, second)
                return carry

            lax.fori_loop(0, nblk, blk, 0)

    return _call(
        body, name=name, grid=(b_loc, 2),
        in_specs=[pl.BlockSpec(memory_space=pltpu.SMEM),
                  pl.BlockSpec((seq, 4 * LANES), lambda b, p: (b, p)),
                  pl.BlockSpec((seq, LANES), lambda b, p: (b, 8 + p)),
                  pl.BlockSpec((seq, LANES), lambda b, p: (b, 10 + p))],
        out_specs=[pl.BlockSpec((seq, 4 * LANES), lambda b, p: (b, p))],
        out_shape=[jax.ShapeDtypeStruct((T, N_SWA_HEADS * HEAD_DIM), F32)],
        scratch_shapes=[pltpu.VMEM((GQA, seq, LANES), BF16), pltpu.VMEM((seq, LANES), BF16)],
        sem=("parallel", "parallel"), args=(sinks, proj, proj, proj), riders=riders)


def swa_bwd(proj, sinks, dya, b_loc, seq, name, riders=()):
    T = proj.shape[0]
    nblk = seq // SWA_BLOCK

    def body(sink_ref, q_ref, k_ref, v_ref, do_ref, out_ref, ds_ref, stage, qs, dos, kk, dk_acc, dv_acc, sems):
        compute(sink_ref, q_ref, k_ref, v_ref, do_ref, stage, ds_ref, qs, dos, kk, dk_acc, dv_acc)
        kvp = pl.program_id(0)
        rows = pl.ds(pl.multiple_of(pl.program_id(1) * seq, seq), seq)
        tiles = [4 * kvp + j for j in range(4)] + [8 + kvp, 10 + kvp]
        copies = []
        for j, t in enumerate(tiles):
            cols = pl.ds(pl.multiple_of(t * LANES, LANES), LANES)
            copies.append(pltpu.make_async_copy(stage.at[j], out_ref.at[rows, cols], sems.at[j]))
        for cp in copies:
            cp.start()
        for cp in copies:
            cp.wait()

    def compute(sink_ref, q_ref, k_ref, v_ref, do_ref, stage, ds_ref, qs, dos, kk, dk_acc, dv_acc):
        kvp = pl.program_id(0)
        b = pl.program_id(1)
        lane = lax.broadcasted_iota(jnp.int32, (SWA_BLOCK, LANES), 1)

        @pl.when(b == 0)
        def _():
            ds_ref[...] = jnp.zeros_like(ds_ref)

        dk_acc[...] = jnp.zeros_like(dk_acc)
        dv_acc[...] = jnp.zeros_like(dv_acc)
        for kvh in range(2):
            kk[...] = _dot(k_ref[...], _lane_select(kvh, (0, 1))).astype(BF16)
            for g in range(GQA):
                cols = slice(LANES * (2 * kvh + g // 2), LANES * (2 * kvh + g // 2 + 1))
                move = _lane_select(g % 2, (kvh,))
                qs[g] = _dot(q_ref[:, cols], move).astype(BF16)
                dos[g] = _dot(do_ref[:, cols].astype(BF16), move).astype(BF16)
            slope = _per_head_rows([_swa_slope(kvp, GQA * kvh + g) for g in range(GQA)])
            sink = _per_head_rows([sink_ref[0, 8 * kvp + GQA * kvh + g] for g in range(GQA)])

            def blk(n, carry):
                start = pl.multiple_of(jnp.maximum(n - 1, 0) * SWA_BLOCK, SWA_BLOCK)
                rows = pl.ds(pl.multiple_of(n * SWA_BLOCK, SWA_BLOCK), SWA_BLOCK)
                win = pl.ds(start, 2 * SWA_BLOCK)
                qblk = jnp.concatenate([qs[g, rows, :] for g in range(GQA)], axis=0)
                doblk = jnp.concatenate([dos[g, rows, :] for g in range(GQA)], axis=0)
                p, m, den = _swa_probs(qblk, k_ref[win, :], n, start, slope, sink)
                dp = _dot_nt(doblk, v_ref[win, :])
                delta = jnp.sum(p * dp, axis=-1, keepdims=True)
                leak = (jnp.exp(sink - m) / den) * delta
                for g in range(GQA):
                    head = GQA * kvh + g
                    ds_ref[head:head + 1, :] -= jnp.broadcast_to(
                        jnp.sum(leak[g * SWA_BLOCK:(g + 1) * SWA_BLOCK], axis=0, keepdims=True), (1, LANES))
                dsc = ((p * (dp - delta)) * ATT_SCALE).astype(BF16)
                dv_acc[win, :] += _dot_tn(p.astype(BF16), doblk)
                dk_acc[win, :] += _dot_tn(dsc, qblk)
                dq = _dot(dsc, kk[win, :])
                for pair in range(GQA // 2):
                    first = dq[2 * pair * SWA_BLOCK:(2 * pair + 1) * SWA_BLOCK]
                    second = dq[(2 * pair + 1) * SWA_BLOCK:(2 * pair + 2) * SWA_BLOCK]
                    stage[2 * kvh + pair, rows, :] = jnp.where(lane < HEAD_DIM, first, second).astype(BF16)
                return carry

            lax.fori_loop(0, nblk, blk, 0)
        stage[4] = dk_acc[...].astype(BF16)
        stage[5] = dv_acc[...].astype(BF16)

    return _call(
        body, name=name, grid=(2, b_loc),
        in_specs=[pl.BlockSpec(memory_space=pltpu.SMEM),
                  pl.BlockSpec((seq, 4 * LANES), lambda p, b: (b, p)),
                  pl.BlockSpec((seq, LANES), lambda p, b: (b, 8 + p)),
                  pl.BlockSpec((seq, LANES), lambda p, b: (b, 10 + p)),
                  pl.BlockSpec((seq, 4 * LANES), lambda p, b: (b, p))],
        out_specs=[pl.BlockSpec(memory_space=pl.ANY),
                   pl.BlockSpec((8, LANES), lambda p, b: (p, 0))],
        out_shape=[jax.ShapeDtypeStruct((T, proj.shape[1]), BF16),
                   jax.ShapeDtypeStruct((N_SWA_HEADS, LANES), F32)],
        scratch_shapes=[pltpu.VMEM((6, seq, LANES), BF16),
                        pltpu.VMEM((GQA, seq, LANES), BF16), pltpu.VMEM((GQA, seq, LANES), BF16),
                        pltpu.VMEM((seq, LANES), BF16),
                        pltpu.VMEM((seq, LANES), F32), pltpu.VMEM((seq, LANES), F32),
                        pltpu.SemaphoreType.DMA((6,))],
        sem=("arbitrary", "arbitrary"), args=(sinks, proj, proj, proj, dya), riders=riders)


def _softplus(z):
    return jnp.maximum(z, 0.0) + jnp.log(1.0 + jnp.exp(-jnp.abs(z)))


def _tri(n, rel):
    r = lax.broadcasted_iota(jnp.int32, (n, n), 0)
    c = lax.broadcasted_iota(jnp.int32, (n, n), 1)
    return jnp.where(rel(r, c), 1.0, 0.0).astype(BF16)


def sb_fwd(proj, b_loc, seq, name, riders=()):
    T = proj.shape[0]
    bq = min(SB_BLOCK, seq)
    nq = seq // bq
    npair = N_SB_HEADS // 2

    def body(q_ref, k_ref, v_ref, o_ref, lt_ref):
        i = pl.program_id(2)
        lane = lax.broadcasted_iota(jnp.int32, (bq, LANES), 1)
        after = _tri(bq, lambda r, c: r > c)
        row = lax.broadcasted_iota(jnp.int32, (bq, bq), 0)
        col = lax.broadcasted_iota(jnp.int32, (bq, bq), 1)
        qv = q_ref[...] * jnp.asarray(ATT_SCALE, BF16)
        blank = jnp.zeros((bq, LANES), BF16)
        q2 = jnp.concatenate([jnp.where((lane >> 6) == h, qv, blank) for h in range(2)], axis=0)


        def tile(j, state, mask):
            rows = pl.ds(pl.multiple_of(j * bq, bq), bq)
            kj, vj = k_ref[rows, :], v_ref[rows, :]
            zz = _dot_nt(q2, kj)
            log_beta, totals, pieces = [], [], []
            for h in range(2):
                z = zz[h * bq:(h + 1) * bq]
                sp = _softplus(z)
                lg = -sp if mask is None else jnp.where(mask, -sp, 0.0)
                hi = lg.astype(BF16)
                pieces += [hi, (lg - hi.astype(F32)).astype(BF16)]
                log_beta.append(z - sp if mask is None else jnp.where(mask, z - sp, NEG_BIG))
                totals.append(jnp.sum(lg, axis=-1, keepdims=True))
            cc = _dot(jnp.concatenate(pieces, axis=0), after)
            probs = []
            for h in range(2):
                inside = cc[2 * h * bq:(2 * h + 1) * bq] + cc[(2 * h + 1) * bq:(2 * h + 2) * bq]
                probs.append(jnp.exp(log_beta[h] + (inside + state[2 * h])).astype(BF16))
            pv = _dot(jnp.concatenate(probs, axis=0), vj)
            return (state[0] + totals[0], state[1] + pv[:bq], state[2] + totals[1], state[3] + pv[bq:])

        zero = (jnp.zeros((bq, 1), F32), jnp.zeros((bq, LANES), F32))
        state = tile(i, zero + zero, col < row)
        state = lax.fori_loop(0, i, lambda jj, st: tile(i - 1 - jj, st, None), state)
        o_ref[...] = jnp.where(lane < HEAD_DIM, state[1], state[3])
        lt_ref[...] = jnp.where(lane < HEAD_DIM, state[0], state[2])

    return _call(
        body, name=name, grid=(b_loc, npair, nq),
        in_specs=[pl.BlockSpec((bq, LANES), lambda b, p, i: (b * nq + i, 12 + p)),
                  pl.BlockSpec((seq, LANES), lambda b, p, i: (b, 20 + p)),
                  pl.BlockSpec((seq, LANES), lambda b, p, i: (b, 28 + p))],
        out_specs=[pl.BlockSpec((bq, LANES), lambda b, p, i: (b * nq + i, p)),
                   pl.BlockSpec((None, bq, LANES), lambda b, p, i: (b * npair + p, i, 0))],
        out_shape=[jax.ShapeDtypeStruct((T, N_SB_HEADS * HEAD_DIM), F32),
                   jax.ShapeDtypeStruct((b_loc * npair, seq, LANES), F32)],
        sem=("parallel", "parallel", "parallel"), args=(proj, proj, proj), riders=riders)


def sb_bwd(proj, dyb, ltot, dproj, b_loc, seq, name, riders=()):
    T = proj.shape[0]
    bq = min(SB_BLOCK, seq)
    nq = seq // bq
    npair = N_SB_HEADS // 2

    def body(q_ref, k_ref, v_ref, do_ref, lt_ref, _, out_ref, stage, dk_acc, dv_acc, sems):
        compute(q_ref, k_ref, v_ref, do_ref, lt_ref, stage, dk_acc, dv_acc)
        rows = pl.ds(pl.multiple_of(pl.program_id(0) * seq, seq), seq)
        copies = []
        for j in range(3):
            tile = pl.multiple_of((12 + pl.program_id(1) + npair * j) * LANES, LANES)
            copies.append(pltpu.make_async_copy(stage.at[j], out_ref.at[rows, pl.ds(tile, LANES)], sems.at[j]))
        for cp in copies:
            cp.start()
        for cp in copies:
            cp.wait()

    def compute(q_ref, k_ref, v_ref, do_ref, lt_ref, stage, dk_acc, dv_acc):
        lane = lax.broadcasted_iota(jnp.int32, (bq, LANES), 1)
        upto = _tri(bq, lambda r, c: r <= c)
        before = _tri(bq, lambda r, c: r < c)
        row = lax.broadcasted_iota(jnp.int32, (bq, bq), 0)
        col = lax.broadcasted_iota(jnp.int32, (bq, bq), 1)
        dk_acc[...] = jnp.zeros_like(dk_acc)
        dv_acc[...] = jnp.zeros_like(dv_acc)
        scale = jnp.asarray(ATT_SCALE, BF16)
        heads = [(lane >> 6) == h for h in range(2)]
        blank = jnp.zeros((bq, LANES), BF16)

        def qblock(i, carry):
            qrows = pl.ds(pl.multiple_of(i * bq, bq), bq)
            qv = q_ref[qrows, :] * scale
            dov = do_ref[qrows, :].astype(BF16)
            ltv = lt_ref[qrows, :]
            q2 = jnp.concatenate([jnp.where(m, qv, blank) for m in heads], axis=0)
            do2 = jnp.concatenate([jnp.where(m, dov, blank) for m in heads], axis=0)
            totals = [jnp.sum(jnp.where(lane == h * HEAD_DIM, ltv, 0.0), axis=-1, keepdims=True) for h in range(2)]

            def split2(xs):
                out = []
                for x in xs:
                    hi = x.astype(BF16)
                    out += [hi, (x - hi.astype(F32)).astype(BF16)]
                return jnp.concatenate(out, axis=0)

            def join2(cc, h):
                return cc[2 * h * bq:(2 * h + 1) * bq] + cc[(2 * h + 1) * bq:(2 * h + 2) * bq]

            def tile(j, state, mask):
                krows = pl.ds(pl.multiple_of(j * bq, bq), bq)
                kj, vj = k_ref[krows, :], v_ref[krows, :]
                kv = kj * scale
                k2 = jnp.concatenate([jnp.where(m, kv, blank) for m in heads], axis=0)
                zz = _dot_nt(q2, kj)
                dd = _dot_nt(do2, vj)
                lgs, log_betas = [], []
                for h in range(2):
                    z = zz[h * bq:(h + 1) * bq]
                    sp = _softplus(z)
                    lgs.append(-sp if mask is None else jnp.where(mask, -sp, 0.0))
                    log_betas.append(z - sp)
                cl = _dot(split2(lgs), upto)
                probs, gs = [], []
                for h in range(2):
                    a = jnp.exp(log_betas[h] + (totals[h] - (join2(cl, h) + state[2 * h])))
                    if mask is not None:
                        a = jnp.where(mask, a, 0.0)
                    probs.append(a.astype(BF16))
                    gs.append(a * dd[h * bq:(h + 1) * bq])
                cg = _dot(split2(gs), before)
                dzs = []
                for h in range(2):
                    beta = jnp.exp(log_betas[h])
                    dz = gs[h] * (1.0 - beta) - beta * (join2(cg, h) + state[2 * h + 1])
                    if mask is not None:
                        dz = jnp.where(mask, dz, 0.0)
                    dzs.append(dz.astype(BF16))
                dv_acc[krows, :] += _dot_tn(jnp.concatenate(probs, axis=0), do2)
                dk_acc[krows, :] += _dot_tn(jnp.concatenate(dzs, axis=0), q2)
                dq = state[4] + _dot(jnp.concatenate(dzs, axis=1), k2)
                new = []
                for h in range(2):
                    new += [state[2 * h] + jnp.sum(lgs[h], axis=-1, keepdims=True),
                            state[2 * h + 1] + jnp.sum(gs[h], axis=-1, keepdims=True)]
                return (*new, dq)

            zero = jnp.zeros((bq, 1), F32)
            state = lax.fori_loop(0, i, lambda j, st: tile(j, st, None),
                                  (zero, zero, zero, zero, jnp.zeros((bq, LANES), F32)))
            state = tile(i, state, col < row)
            stage[0, qrows, :] = state[4].astype(BF16)
            return carry

        lax.fori_loop(0, nq, qblock, 0)
        stage[1] = dk_acc[...].astype(BF16)
        stage[2] = dv_acc[...].astype(BF16)

    return _call(
        body, name=name, grid=(b_loc, npair),
        in_specs=[pl.BlockSpec((seq, LANES), lambda b, p: (b, 12 + p)),
                  pl.BlockSpec((seq, LANES), lambda b, p: (b, 20 + p)),
                  pl.BlockSpec((seq, LANES), lambda b, p: (b, 28 + p)),
                  pl.BlockSpec((seq, LANES), lambda b, p: (b, p)),
                  pl.BlockSpec((None, seq, LANES), lambda b, p: (b * npair + p, 0, 0)),
                  pl.BlockSpec(memory_space=pl.ANY)],
        out_specs=[pl.BlockSpec(memory_space=pl.ANY)],
        out_shape=[jax.ShapeDtypeStruct(dproj.shape, BF16)],
        scratch_shapes=[pltpu.VMEM((3, seq, LANES), BF16)] + [pltpu.VMEM((seq, LANES), F32)] * 2
        + [pltpu.SemaphoreType.DMA((3,))],
        sem=("arbitrary", "arbitrary"), args=(proj, proj, proj, dyb, ltot, dproj), riders=riders,
        aliases={5: 0})


def allreduce_small(v, name):
    R = v.shape[0]
    ndev = 8

    def body(v_ref, o_ref, land, send_sems, recv_sems):
        x, y, c = _place()
        me = 4 * x + 2 * y + c
        land[me] = v_ref[...]
        started = []
        for f in range(1, ndev):
            fx, fy, fc = (f >> 2) & 1, (f >> 1) & 1, f & 1
            peer = ((1 - x) if fx else x, (1 - y) if fy else y, (1 - c) if fc else c)
            cp = pltpu.make_async_remote_copy(
                src_ref=v_ref, dst_ref=land.at[me], send_sem=send_sems.at[f - 1], recv_sem=recv_sems.at[f - 1],
                device_id=peer, device_id_type=MESH)
            cp.start()
            started.append(cp)
        for cp in started:
            cp.wait()
        total = land[0]
        for d in range(1, ndev):
            total = total + land[d]
        o_ref[...] = total

    return pl.pallas_call(
        body, name=name,
        in_specs=[VMEM_WHOLE], out_specs=VMEM_WHOLE,
        out_shape=jax.ShapeDtypeStruct(v.shape, F32),
        scratch_shapes=[pltpu.VMEM((ndev, R, LANES), F32),
                        pltpu.SemaphoreType.DMA((ndev - 1,)), pltpu.SemaphoreType.DMA((ndev - 1,))],
    )(v)


def sum_cores(grads, theirs, name):
    _, _, r, c = grads.shape
    tr, tc = _tiles(r, c, 1 << 18)
    core = lax.axis_index("c").astype(jnp.int32).reshape(1)

    def body(core_ref, g_ref, t_ref, o_ref):
        o_ref[...] = (g_ref[...].astype(F32) + t_ref[...].astype(F32)).astype(BF16)

    blk = pl.BlockSpec((N_CHIPS, tr, tc), lambda i, j, core_ref: (0, i, j))
    return pl.pallas_call(
        body, name=name,
        grid_spec=pltpu.PrefetchScalarGridSpec(
            num_scalar_prefetch=1, grid=(r // tr, c // tc),
            in_specs=[pl.BlockSpec((None, N_CHIPS, tr, tc), lambda i, j, core_ref: (core_ref[0], 0, i, j)), blk],
            out_specs=blk),
        out_shape=jax.ShapeDtypeStruct((N_CHIPS, r, c), BF16),
        compiler_params=_params("parallel", "parallel"),
    )(core, grads, theirs)


def sum_chips(halves, landed, name):
    _, r, c = landed.shape
    tr, tc = _tiles(r, c, 1 << 18)
    chip = (2 * lax.axis_index("x") + lax.axis_index("y")).astype(jnp.int32).reshape(1)

    def body(chip_ref, own_ref, p_ref, o_ref):
        total = own_ref[...].astype(F32)
        for j in range(N_CHIPS - 1):
            total = total + p_ref[j].astype(F32)
        o_ref[...] = total

    return pl.pallas_call(
        body, name=name,
        grid_spec=pltpu.PrefetchScalarGridSpec(
            num_scalar_prefetch=1, grid=(r // tr, c // tc),
            in_specs=[pl.BlockSpec((None, tr, tc), lambda i, j, chip_ref: (chip_ref[0], i, j)),
                      pl.BlockSpec((N_CHIPS - 1, tr, tc), lambda i, j, chip_ref: (0, i, j))],
            out_specs=pl.BlockSpec((tr, tc), lambda i, j, chip_ref: (i, j))),
        out_shape=jax.ShapeDtypeStruct((r, c), F32),
        compiler_params=_params("parallel", "parallel"),
    )(chip, halves, landed)


def _adamw_math(w, g, m, v):
    m = ADAM_B1 * m + (1.0 - ADAM_B1) * g
    v = ADAM_B2 * v + (1.0 - ADAM_B2) * (g * g)
    m_hat = m / (1.0 - ADAM_B1 ** ADAM_STEP)
    v_hat = v / (1.0 - ADAM_B2 ** ADAM_STEP)
    delta = -ADAM_LR * (m_hat / (jnp.sqrt(v_hat) + ADAM_EPS) + ADAM_WD * w)
    return delta, m, v


def adamw(w, m, v, g, name):
    r, c = w.shape
    tr, tc = _tiles(r, c, 1 << 18, multiple=8)

    def body(w_ref, m_ref, v_ref, gin_ref, g_ref, d_ref, nm_ref, nv_ref):
        g = gin_ref[...]
        delta, nm, nv = _adamw_math(w_ref[...], g, m_ref[...], v_ref[...])
        g_ref[...] = g
        d_ref[...] = delta
        nm_ref[...] = nm
        nv_ref[...] = nv

    blk = pl.BlockSpec((tr, tc), lambda i, j: (i, j))
    shape = jax.ShapeDtypeStruct((r, c), F32)
    return pl.pallas_call(
        body, name=name, grid=(r // tr, c // tc),
        in_specs=[blk] * 4, out_specs=[blk] * 4, out_shape=[shape] * 4,
        compiler_params=_params("parallel", "parallel"),
    )(w, m, v, g)


def adamw_halves(w, m, v, mine, theirs, name, col_halves=False):
    r, c = w.shape
    hr, hc = (r, c // 2) if col_halves else (r // 2, c)
    tr, tc = _tiles(hr, hc, 1 << 18, multiple=8)
    g_cols = mine.shape[-1]
    padded = g_cols != hc
    if padded and tc != hc:
        raise ValueError("a gradient with padded columns is read in full-width row blocks")
    per_half = hc // tc if col_halves else hr // tr
    core = lax.axis_index("c").astype(jnp.int32).reshape(1)

    def body(core_ref, w_ref, m_ref, v_ref, mine_ref, theirs_ref, g_ref, d_ref, nm_ref, nv_ref):
        half = pl.program_id(1 if col_halves else 0) // per_half
        cols = slice(0, hc) if padded else slice(None)
        g = jnp.where(half == core_ref[0], mine_ref[:, cols], theirs_ref[:, cols])
        delta, nm, nv = _adamw_math(w_ref[...], g, m_ref[...], v_ref[...])
        g_ref[...] = g
        d_ref[...] = delta
        nm_ref[...] = nm
        nv_ref[...] = nv

    blk = pl.BlockSpec((tr, tc), lambda i, j, core_ref: (i, j))

    def g_spec(wanted):
        def index(i, j, core_ref):
            use = ((j if col_halves else i) // per_half == core_ref[0]) == wanted
            if col_halves:
                return i, jnp.where(use, j % per_half, 0)
            return jnp.where(use, i % per_half, 0), j
        return pl.BlockSpec((tr, g_cols if padded else tc), index)

    shape = jax.ShapeDtypeStruct((r, c), F32)
    return pl.pallas_call(
        body, name=name,
        grid_spec=pltpu.PrefetchScalarGridSpec(
            num_scalar_prefetch=1, grid=(r // tr, c // tc),
            in_specs=[blk, blk, blk, g_spec(True), g_spec(False)], out_specs=[blk] * 4),
        out_shape=[shape] * 4,
        compiler_params=_params("parallel", "parallel"),
    )(core, w, m, v, mine, theirs)


MATRICES = ("ffn1_w_gate", "ffn1_w_up", "ffn1_w_down", "w_in", "w_out", "ffn2_w_gate", "ffn2_w_up", "ffn2_w_down")
VECTORS = ("ffn1_norm", "mix_norm", "swa_sinks", "swa_out_norm", "sb_out_norm", "ffn2_norm", "final_norm")
WEIGHTS = ("ffn1_norm", "ffn1_w_gate", "ffn1_w_up", "ffn1_w_down", "mix_norm", "w_in", "swa_sinks", "swa_out_norm",
           "sb_out_norm", "w_out", "ffn2_norm", "ffn2_w_gate", "ffn2_w_up", "ffn2_w_down", "final_norm")


def _pack_rows(vec):
    flat = vec.reshape(-1).astype(F32)
    rows = -(-flat.shape[0] // LANES)
    rows8 = -(-rows // 8) * 8
    flat = jnp.pad(flat, (0, rows8 * LANES - flat.shape[0]))
    return flat.reshape(rows8, LANES)


class GradPipe:
    def __init__(self, name, grads):
        self.name, self.step, self.value = name, 0, grads

    def rider(self):
        make = (half_swap_rider, scatter_rider, final_swap_rider)[self.step]
        return make([self.value])

    def landed(self, outs):
        if self.step == 0:
            self.value = sum_cores(self.value, outs[0], "sum_cores_" + self.name)
        elif self.step == 1:
            self.value = sum_chips(self.value, outs[0], "sum_chips_" + self.name)
        else:
            self.value = (self.value, outs[0])
        self.step += 1


def _hosted(fn, *args, advance=(), **kw):
    outs, routs = fn(*args, riders=[p.rider() for p in advance], **kw)
    for p, ro in zip(advance, routs):
        p.landed(ro)
    return outs


def kernel(x, ffn1_norm, ffn1_w_gate, ffn1_w_up, ffn1_w_down, mix_norm, w_in, swa_sinks, swa_out_norm, sb_out_norm, w_out, ffn2_norm, ffn2_w_gate, ffn2_w_up, ffn2_w_down, final_norm, loss_target, m_ffn1_norm, m_ffn1_w_gate, m_ffn1_w_up, m_ffn1_w_down, m_mix_norm, m_w_in, m_swa_sinks, m_swa_out_norm, m_sb_out_norm, m_w_out, m_ffn2_norm, m_ffn2_w_gate, m_ffn2_w_up, m_ffn2_w_down, m_final_norm, v_ffn1_norm, v_ffn1_w_gate, v_ffn1_w_up, v_ffn1_w_down, v_mix_norm, v_w_in, v_swa_sinks, v_swa_out_norm, v_sb_out_norm, v_w_out, v_ffn2_norm, v_ffn2_w_gate, v_ffn2_w_up, v_ffn2_w_down, v_final_norm):
    given = dict(locals())
    b_loc, seq, D = x.shape
    T = b_loc * seq
    x0 = x.reshape(T, D)
    target = loss_target.reshape(T, D)
    final_g = final_norm.reshape(1, D)

    gathered = {}

    def gather(*names):
        shards = []
        for nm in names:
            w = given[nm][0].astype(BF16)
            hidden_axis = 0 if nm.endswith("w_down") else 1 if nm.startswith("ffn") else None
            if hidden_axis is not None:
                pad = -w.shape[hidden_axis] % LANES
                w = jnp.pad(w, [(0, pad) if a == hidden_axis else (0, 0) for a in range(2)])
            shards.append(w.reshape(2, w.shape[0] // 2, w.shape[1]))
        return names, gather_rider(shards)

    def hosting(fn, *args, fetch):
        names, rider = fetch
        outs, landed = fn(*args, riders=[rider])
        for nm, o in zip(names, landed[0]):
            gathered[nm] = o.reshape(N_CHIPS, 2 * o.shape[2], o.shape[3])
        return outs

    hosting(lambda riders: ([], exchange_now(riders, "gather_ffn1_gate")), fetch=gather("ffn1_w_gate"))

    h1 = rmsnorm_cast(x0, ffn1_norm, "ffn1_norm_fwd")
    (g1,) = hosting(ffn_gate, h1, gathered["ffn1_w_gate"], "ffn1_gate", fetch=gather("ffn1_w_up"))
    u1, a1 = hosting(ffn_up_act, h1, gathered["ffn1_w_up"], g1, "ffn1_up_act", fetch=gather("ffn1_w_down"))
    (x1,) = hosting(ffn_down, a1, gathered["ffn1_w_down"], x0, "ffn1_down", fetch=gather("w_in"))
    h2 = rmsnorm_cast(x1, mix_norm, "mix_norm_fwd")
    (proj,) = hosting(in_proj, h2, gathered["w_in"], "in_proj", fetch=gather("w_out"))
    w_out_all = gathered["w_out"].reshape(-1, D)
    (ya,) = hosting(swa_fwd, proj, swa_sinks, b_loc, seq, "swa_fwd", fetch=gather("ffn2_w_gate"))
    yb, ltot = hosting(sb_fwd, proj, b_loc, seq, "sb_fwd", fetch=gather("ffn2_w_up", "ffn2_w_down"))
    y = out_norm(ya, yb, swa_out_norm, sb_out_norm, "out_norm_fwd")
    x2 = out_proj(y, w_out_all, x1, "out_proj")
    h3 = rmsnorm_cast(x2, ffn2_norm, "ffn2_norm_fwd")
    (g2, u2, a2), _ = ffn_up(h3, gathered["ffn2_w_gate"], gathered["ffn2_w_up"], "ffn2_up")
    (x3,), _ = ffn_down(a2, gathered["ffn2_w_down"], x2, "ffn2_down")
    dx3, d_final, loss_part = final_loss(x3, final_g, target, "final_loss")

    Fs = gathered["ffn1_w_gate"].shape[2]
    Ps = gathered["w_in"].shape[2]
    Os = w_out_all.shape[0] // N_CHIPS
    pipes = {}

    def down_grad(tag, act, dx, advance):
        (grads,) = _hosted(wgrad, act, dx, tag + "_dwd", groups=N_CHIPS, m=Fs, n=D // 2,
                           a_block=lambda tk: (None, tk, Fs), a_map=lambda s, j, k: (s, k, 0),
                           b_block=lambda tk: (tk, D // 2), b_map=lambda s, j, k: (k, j), b_scale=0.5,
                           advance=advance)
        return grads

    def up_grad(name, h, d_act, advance):
        (grads,) = _hosted(wgrad, h, d_act, name, groups=N_CHIPS, m=D // 2, n=Fs,
                           a_block=lambda tk: (tk, D // 2), a_map=lambda s, j, k: (k, j),
                           b_block=lambda tk: (None, tk, Fs), b_map=lambda s, j, k: (s, k, 0), advance=advance)
        return grads

    (dg2, du2), _ = ffn_bwd_act(dx3, gathered["ffn2_w_down"], g2, u2, "ffn2_bwd_act")
    pipes["ffn2_w_down"] = p_d2 = GradPipe("ffn2_w_down", down_grad("ffn2", a2, dx3, []))
    pipes["ffn2_w_gate"] = p_g2 = GradPipe("ffn2_w_gate", up_grad("ffn2_dwg", h3, dg2, [p_d2]))
    pipes["ffn2_w_up"] = p_u2 = GradPipe("ffn2_w_up", up_grad("ffn2_dwu", h3, du2, [p_d2, p_g2]))
    (dh3,) = _hosted(ffn_bwd_dh, dg2, du2, gathered["ffn2_w_gate"], gathered["ffn2_w_up"], "ffn2_bwd_dh",
                     advance=[p_d2, p_g2, p_u2])
    (dx2, d_ffn2_norm), _ = rmsnorm_bwd_add(x2, ffn2_norm, dh3, dx3, "ffn2_norm_bwd")

    (d_w_out,) = _hosted(wgrad, y, dx2, "dw_out", groups=N_CHIPS, m=Os // 2, n=D,
                         a_block=lambda tk: (tk, Os // 2), a_map=lambda s, j, k: (k, 2 * s + j),
                         b_block=lambda tk: (tk, D), b_map=lambda s, j, k: (k, 0), advance=[p_g2])
    pipes["w_out"] = p_out = GradPipe("w_out", d_w_out)
    dya, dyb, d_swa_norm, d_sb_norm = out_proj_bwd(dx2, w_out_all, ya, yb, swa_out_norm, sb_out_norm, "out_proj_bwd")
    dproj, d_sinks = _hosted(swa_bwd, proj, swa_sinks, dya, b_loc, seq, "swa_bwd", advance=[p_u2, p_out])
    (dproj,) = _hosted(sb_bwd, proj, dyb, ltot, dproj, b_loc, seq, "sb_bwd", advance=[p_u2, p_out])
    (d_w_in,) = _hosted(wgrad, h2, dproj, "dw_in", groups=N_CHIPS, m=D // 2, n=Ps,
                        a_block=lambda tk: (tk, D // 2), a_map=lambda s, j, k: (k, j),
                        b_block=lambda tk: (tk, Ps), b_map=lambda s, j, k: (k, s), advance=[p_out])
    pipes["w_in"] = p_in = GradPipe("w_in", d_w_in)
    (dh2,) = _hosted(in_proj_bwd, dproj, gathered["w_in"], "in_proj_bwd", advance=[p_in])
    (dx1, d_mix_norm), _ = rmsnorm_bwd_add(x1, mix_norm, dh2, dx2, "mix_norm_bwd")

    pipes["ffn1_w_down"] = p_d1 = GradPipe("ffn1_w_down", down_grad("ffn1", a1, dx1, [p_in]))
    dg1, du1 = _hosted(ffn_bwd_act, dx1, gathered["ffn1_w_down"], g1, u1, "ffn1_bwd_act", advance=[p_in, p_d1])
    pipes["ffn1_w_gate"] = p_g1 = GradPipe("ffn1_w_gate", up_grad("ffn1_dwg", h1, dg1, [p_d1]))
    pipes["ffn1_w_up"] = p_u1 = GradPipe("ffn1_w_up", up_grad("ffn1_dwu", h1, du1, [p_d1, p_g1]))
    (dh1,) = _hosted(ffn_bwd_dh, dg1, du1, gathered["ffn1_w_gate"], gathered["ffn1_w_up"], "ffn1_bwd_dh_a",
                     part=(0, 2), advance=[p_g1, p_u1])
    (dh1,) = _hosted(ffn_bwd_dh, dg1, du1, gathered["ffn1_w_gate"], gathered["ffn1_w_up"], "ffn1_bwd_dh_b",
                     part=(1, 2), prev=dh1, advance=[p_g1, p_u1])
    (dx0, d_ffn1_norm), _ = rmsnorm_bwd_add(x0, ffn1_norm, dh1, dx1, "ffn1_norm_bwd")
    (last,) = exchange_now([p_u1.rider()], "swap_ffn1_w_up")
    p_u1.landed(last)

    vec_grads = dict(ffn1_norm=d_ffn1_norm, mix_norm=d_mix_norm, swa_sinks=d_sinks[:, 0], swa_out_norm=d_swa_norm,
                     sb_out_norm=d_sb_norm, ffn2_norm=d_ffn2_norm, final_norm=d_final)
    packed = [_pack_rows(vec_grads[nm]) for nm in VECTORS] + [_pack_rows(loss_part[0, :1])]
    offsets = [0]
    for p in packed:
        offsets.append(offsets[-1] + p.shape[0])
    reduced = allreduce_small(jnp.concatenate(packed, axis=0), "reduce_vectors")
    loss = reduced[offsets[len(VECTORS)], 0]

    out = {}
    for nm in MATRICES:
        shape = given[nm].shape
        mine, theirs = pipes[nm].value
        res = adamw_halves(given[nm][0], given["m_" + nm][0], given["v_" + nm][0], mine, theirs, "adamw_" + nm,
                           col_halves=nm.endswith("w_down"))
        out[nm] = [r.reshape(shape) for r in res]
    w_rows = jnp.concatenate([_pack_rows(given[nm]) for nm in VECTORS], axis=0)
    m_rows = jnp.concatenate([_pack_rows(given["m_" + nm]) for nm in VECTORS], axis=0)
    v_rows = jnp.concatenate([_pack_rows(given["v_" + nm]) for nm in VECTORS], axis=0)
    g_rows = reduced[:offsets[len(VECTORS)]]
    small = adamw(w_rows, m_rows, v_rows, g_rows, "adamw_vectors")
    for i, nm in enumerate(VECTORS):
        shape = given[nm].shape
        size = math.prod(shape)
        out[nm] = [r[offsets[i]:offsets[i + 1]].reshape(-1)[:size].reshape(shape) for r in small]

    grad_x = dx0.reshape(b_loc, seq, D)
    return (loss, grad_x, *[out[nm][0] for nm in WEIGHTS], *[out[nm][1] for nm in WEIGHTS],
            *[out[nm][2] for nm in WEIGHTS], *[out[nm][3] for nm in WEIGHTS])
```

```python
import functools
import math

import jax
import jax.numpy as jnp
from jax import lax
from jax.experimental import pallas as pl
from jax.experimental.pallas import tpu as pltpu

F32 = jnp.float32
BF16 = jnp.bfloat16
MESH = pl.DeviceIdType.MESH

EPS = 1e-6
HEAD_DIM = 64
LANES = 128
N_SWA_HEADS = 16
N_SWA_KV = 4
N_SB_HEADS = 16
WINDOW = 128
SWA_BLOCK = 128
SB_BLOCK = 256
N_CHIPS = 4
ATT_SCALE = HEAD_DIM ** -0.5

ADAM_LR = 0.001
ADAM_B1 = 0.9
ADAM_B2 = 0.999
ADAM_EPS = 1e-08
ADAM_WD = 0.01
ADAM_STEP = 10

VMEM_LIMIT = 56 * 1024 * 1024
NEG_BIG = -1e30

ANY = pl.BlockSpec(memory_space=pl.ANY)
VMEM_WHOLE = pl.BlockSpec(memory_space=pltpu.VMEM)


def _row_tile(rows, cap, multiple=16):
    best = None
    for t in range(multiple, min(rows, cap) + 1, multiple):
        if rows % t == 0:
            best = t
    if best is None:
        raise ValueError(f"no row tile for {rows} rows under {cap}")
    return best


def _params(*sem):
    return pltpu.CompilerParams(dimension_semantics=sem, vmem_limit_bytes=VMEM_LIMIT)


def _dot(a, b):
    return jnp.dot(a, b, preferred_element_type=F32)


def _dot_nt(a, b):
    return lax.dot_general(a, b, (((1,), (1,)), ((), ())), preferred_element_type=F32)


def _dot_tn(a, b):
    return lax.dot_general(a, b, (((0,), (0,)), ((), ())), preferred_element_type=F32)


def _sigmoid(x):
    return 1.0 / (1.0 + jnp.exp(-x))


def _split_dot(x, u):
    hi = x.astype(BF16)
    lo = (x - hi.astype(F32)).astype(BF16)
    return _dot(hi, u) + _dot(lo, u)


def _tiles(r, c, max_elems, multiple=16):
    col_opts = [c] + [t for t in (2048, 1024, 512, 256, 128) if t < c and c % t == 0]
    best = None
    for tc in col_opts:
        for tr in [r] + list(range(multiple, r, multiple)):
            if r % tr == 0 and tr * tc <= max_elems and (best is None or tr * tc > best[0] * best[1]):
                best = (tr, tc)
    if best is None:
        raise ValueError(f"no tile for [{r}, {c}] under {max_elems} elements")
    return best


class Rider:
    def __init__(self, arrays, out_shapes, sem_shapes, start, finish):
        self.arrays, self.out_shapes, self.sem_shapes = list(arrays), list(out_shapes), list(sem_shapes)
        self.start, self.finish = start, finish


def _place():
    return lax.axis_index("x"), lax.axis_index("y"), lax.axis_index("c")


def _other_chips(x, y):
    return [(1 - x, y), (x, 1 - y), (1 - x, 1 - y)]


def _split(ref):
    rows = ref.shape[-2]
    n = next(k for k in (4, 2, 1) if rows % (16 * k) == 0)
    step = rows // n
    lead = [()] if len(ref.shape) == 2 else [(s,) for s in range(ref.shape[0])]
    return [ref.at[(*idx, pl.ds(i * step, step))] for idx in lead for i in range(n)]


class _Copy:
    def __init__(self, src, dst, make):
        self.src, self.dst, self.make = src, dst, make

    def start(self):
        for s, d in zip(_split(self.src), _split(self.dst)):
            self.make(s, d).start()

    def wait(self):
        self.make(self.src, self.dst).wait()

    def wait_send(self):
        self.make(self.src, self.dst).wait_send()

    def wait_recv(self):
        self.make(self.src, self.dst).wait_recv()


def _remote(src, dst, send_sem, recv_sem, device):
    return _Copy(src, dst, lambda s, d: pltpu.make_async_remote_copy(
        src_ref=s, dst_ref=d, send_sem=send_sem, recv_sem=recv_sem, device_id=device, device_id_type=MESH))


def gather_rider(shards):
    n = len(shards)

    def copies(ins, outs, sems):
        ici_s, ici_r, d2d_s, d2d_r, own_s, own_r = sems
        x, y, c = _place()
        me = 2 * x + y
        own, ici, fwd = [], [], []
        for k in range(n):
            own.append(_remote(ins[k], outs[k].at[me], own_s.at[k], own_r.at[k], (x, y, 1 - c)))
            for j, (px, py) in enumerate(_other_chips(x, y)):
                i = 3 * k + j
                ici.append(_remote(ins[k].at[c], outs[k].at[me, c], ici_s.at[i], ici_r.at[i], (px, py, c)))
                got = outs[k].at[2 * px + py, c]
                fwd.append(_remote(got, got, d2d_s.at[i], d2d_r.at[i], (x, y, 1 - c)))
        return own, ici, fwd

    def start(ins, outs, sems):
        own, ici, _ = copies(ins, outs, sems)
        for cp in own + ici:
            cp.start()

    def finish(ins, outs, sems):
        own, ici, fwd = copies(ins, outs, sems)
        for a, f in zip(ici, fwd):
            a.wait_recv()
            f.start()
        for a in ici:
            a.wait_send()
        for f in fwd:
            f.wait()
        for o in own:
            o.wait()

    dma = pltpu.SemaphoreType.DMA
    return Rider(shards, [jax.ShapeDtypeStruct((N_CHIPS,) + s.shape, s.dtype) for s in shards],
                 [dma((3 * n,)), dma((3 * n,)), dma((3 * n,)), dma((3 * n,)), dma((n,)), dma((n,))], start, finish)


def half_swap_rider(grads):
    n = len(grads)

    def copies(ins, outs, sems):
        x, y, c = _place()
        return [_remote(ins[k].at[1 - c], outs[k], sems[0].at[k], sems[1].at[k], (x, y, 1 - c)) for k in range(n)]

    def start(ins, outs, sems):
        for cp in copies(ins, outs, sems):
            cp.start()

    def finish(ins, outs, sems):
        for cp in copies(ins, outs, sems):
            cp.wait()

    dma = pltpu.SemaphoreType.DMA
    return Rider(grads, [jax.ShapeDtypeStruct(g.shape[1:], g.dtype) for g in grads], [dma((n,)), dma((n,))],
                 start, finish)


def scatter_rider(halves):
    n = len(halves)

    def copies(ins, outs, sems):
        send, recv = sems
        x, y, c = _place()
        cps = []
        for k in range(n):
            for j, (px, py) in enumerate(_other_chips(x, y)):
                i = 3 * k + j
                cps.append(_remote(ins[k].at[2 * px + py], outs[k].at[j], send.at[i], recv.at[i], (px, py, c)))
        return cps

    def start(ins, outs, sems):
        for cp in copies(ins, outs, sems):
            cp.start()

    def finish(ins, outs, sems):
        for cp in copies(ins, outs, sems):
            cp.wait()

    dma = pltpu.SemaphoreType.DMA
    return Rider(halves, [jax.ShapeDtypeStruct((3,) + h.shape[1:], h.dtype) for h in halves],
                 [dma((3 * n,)), dma((3 * n,))], start, finish)


def final_swap_rider(sums):
    n = len(sums)

    def copies(ins, outs, sems):
        send, recv = sems
        x, y, c = _place()
        return [_remote(ins[k], outs[k], send.at[k], recv.at[k], (x, y, 1 - c)) for k in range(n)]

    def start(ins, outs, sems):
        for cp in copies(ins, outs, sems):
            cp.start()

    def finish(ins, outs, sems):
        for cp in copies(ins, outs, sems):
            cp.wait()

    dma = pltpu.SemaphoreType.DMA
    return Rider(sums, [jax.ShapeDtypeStruct(s.shape, s.dtype) for s in sums], [dma((n,)), dma((n,))],
                 start, finish)


def _call(body, *, name, grid, in_specs, out_specs, out_shape, scratch_shapes=(), sem, args, riders=(), aliases=None):
    in_specs, out_specs, out_shape = list(in_specs), list(out_specs), list(out_shape)
    scratch_shapes = list(scratch_shapes)
    aliases = aliases or {}
    if not riders:
        outs = pl.pallas_call(body, name=name, grid=grid, in_specs=in_specs, out_specs=out_specs, out_shape=out_shape,
                              scratch_shapes=scratch_shapes, input_output_aliases=aliases,
                              compiler_params=_params(*sem))(*args)
        return list(outs), []
    n_in, n_out, n_scr = len(in_specs), len(out_specs), len(scratch_shapes)
    r_in = [len(r.arrays) for r in riders]
    r_out = [len(r.out_shapes) for r in riders]
    r_sem = [len(r.sem_shapes) for r in riders]

    def wrapped(*refs):
        pos = 0

        def take(count):
            nonlocal pos
            got = refs[pos:pos + count]
            pos += count
            return got

        ins = take(n_in)
        rins = [take(k) for k in r_in]
        outs = take(n_out)
        routs = [take(k) for k in r_out]
        scr = take(n_scr)
        rsems = [take(k) for k in r_sem]
        first = functools.reduce(jnp.logical_and, [pl.program_id(a) == 0 for a in range(len(grid))])
        last = functools.reduce(jnp.logical_and, [pl.program_id(a) == grid[a] - 1 for a in range(len(grid))])

        @pl.when(first)
        def _():
            for r, ri, ro, rs in zip(riders, rins, routs, rsems):
                r.start(ri, ro, rs)

        body(*ins, *outs, *scr)

        @pl.when(last)
        def _():
            for r, ri, ro, rs in zip(riders, rins, routs, rsems):
                r.finish(ri, ro, rs)

    any_spec = pl.BlockSpec(memory_space=pl.ANY)
    res = pl.pallas_call(
        wrapped, name=name, grid=grid,
        in_specs=in_specs + [any_spec] * sum(r_in),
        out_specs=out_specs + [any_spec] * sum(r_out),
        out_shape=out_shape + [s for r in riders for s in r.out_shapes],
        scratch_shapes=scratch_shapes + [s for r in riders for s in r.sem_shapes],
        input_output_aliases=aliases,
        compiler_params=_params(*(["arbitrary"] * len(grid))),
    )(*args, *[a for r in riders for a in r.arrays])
    res = list(res)
    outs, rest = res[:n_out], res[n_out:]
    routs = []
    for k in r_out:
        routs.append(rest[:k])
        rest = rest[k:]
    return outs, routs


def exchange_now(riders, name):
    n_in = [len(r.arrays) for r in riders]
    n_out = [len(r.out_shapes) for r in riders]
    n_sem = [len(r.sem_shapes) for r in riders]

    def body(*refs):
        pos = 0
        groups = []
        for counts in (n_in, n_out, n_sem):
            part = []
            for k in counts:
                part.append(refs[pos:pos + k])
                pos += k
            groups.append(part)
        for r, ri, ro, rs in zip(riders, *groups):
            r.start(ri, ro, rs)
        for r, ri, ro, rs in zip(riders, *groups):
            r.finish(ri, ro, rs)

    any_spec = pl.BlockSpec(memory_space=pl.ANY)
    res = pl.pallas_call(
        body, name=name,
        in_specs=[any_spec] * sum(n_in), out_specs=[any_spec] * sum(n_out),
        out_shape=[s for r in riders for s in r.out_shapes],
        scratch_shapes=[s for r in riders for s in r.sem_shapes],
    )(*[a for r in riders for a in r.arrays])
    res = list(res)
    routs = []
    for k in n_out:
        routs.append(res[:k])
        res = res[k:]
    return routs


def rmsnorm_cast(x, gamma, name, riders=None):
    T, D = x.shape
    tr = min(512, T)

    def body(x_ref, g_ref, o_ref):
        xv = x_ref[...]
        r = lax.rsqrt(jnp.mean(xv * xv, axis=-1, keepdims=True) + EPS)
        o_ref[...] = ((xv * r) * g_ref[...]).astype(BF16)

    outs, landed = _call(
        body, name=name, grid=(T // tr,),
        in_specs=[pl.BlockSpec((tr, D), lambda i: (i, 0)), pl.BlockSpec((1, D), lambda i: (0, 0))],
        out_specs=[pl.BlockSpec((tr, D), lambda i: (i, 0))],
        out_shape=[jax.ShapeDtypeStruct((T, D), BF16)],
        sem=("parallel",), args=(x, gamma), riders=riders or ())
    return outs[0] if riders is None else (outs, landed)


def _rms_bwd(xv, gamma, dy):
    d = xv.shape[-1]
    r = lax.rsqrt(jnp.mean(xv * xv, axis=-1, keepdims=True) + EPS)
    gdy = dy * gamma
    dot = jnp.sum(gdy * xv, axis=-1, keepdims=True)
    dx = r * gdy - xv * ((r * r * r) * (dot / d))
    dgamma = jnp.sum(dy * (xv * r), axis=0, keepdims=True)
    return dx, dgamma


def rmsnorm_bwd_add(x, gamma, dy, dx_up, name, riders=()):
    T, D = x.shape
    tr = min(256, T)

    def body(x_ref, g_ref, dy_ref, up_ref, dx_ref, dg_ref):
        dx, dgamma = _rms_bwd(x_ref[...], g_ref[...], dy_ref[...])
        dx_ref[...] = up_ref[...] + dx

        @pl.when(pl.program_id(0) == 0)
        def _():
            dg_ref[...] = jnp.zeros_like(dg_ref)

        dg_ref[...] += dgamma

    row = pl.BlockSpec((tr, D), lambda i: (i, 0))
    vec = pl.BlockSpec((1, D), lambda i: (0, 0))
    return _call(
        body, name=name, grid=(T // tr,),
        in_specs=[row, vec, row, row], out_specs=[row, vec],
        out_shape=[jax.ShapeDtypeStruct((T, D), F32), jax.ShapeDtypeStruct((1, D), F32)],
        sem=("arbitrary",), args=(x, gamma, dy, dx_up), riders=riders)


def _column_halves(width):
    tiles = width // LANES
    cut = (tiles - tiles // 2) * LANES
    return (slice(0, cut), slice(cut, width)) if 0 < cut < width else (slice(0, width),)


def ffn_up(h, wg, wu, name, riders=()):
    T, D = h.shape
    G, _, Fs = wg.shape
    tm = min(512, T)

    def body(h_ref, wg_ref, wu_ref, g_ref, u_ref, a_ref):
        hv = h_ref[...]
        for cols in _column_halves(Fs):
            g = _dot(hv, wg_ref[:, cols])
            u = _dot(hv, wu_ref[:, cols])
            g_ref[:, cols] = g.astype(BF16)
            u_ref[:, cols] = u.astype(BF16)
            a_ref[:, cols] = ((g * _sigmoid(g)) * u).astype(BF16)

    w_spec = pl.BlockSpec((None, D, Fs), lambda s, i: (s, 0, 0), pipeline_mode=pl.Buffered(1))
    o_spec = pl.BlockSpec((None, tm, Fs), lambda s, i: (s, i, 0))
    return _call(
        body, name=name, grid=(G, T // tm),
        in_specs=[pl.BlockSpec((tm, D), lambda s, i: (i, 0)), w_spec, w_spec],
        out_specs=[o_spec, o_spec, o_spec],
        out_shape=[jax.ShapeDtypeStruct((G, T, Fs), BF16)] * 3,
        sem=("parallel", "parallel"), args=(h, wg, wu), riders=riders)


def ffn_gate(h, wg, name, riders=()):
    T, D = h.shape
    G, _, Fs = wg.shape
    tm = min(512, T)

    def body(h_ref, wg_ref, g_ref):
        g_ref[...] = _dot(h_ref[...], wg_ref[...]).astype(BF16)

    return _call(
        body, name=name, grid=(G, T // tm),
        in_specs=[pl.BlockSpec((tm, D), lambda s, i: (i, 0)),
                  pl.BlockSpec((None, D, Fs), lambda s, i: (s, 0, 0), pipeline_mode=pl.Buffered(1))],
        out_specs=[pl.BlockSpec((None, tm, Fs), lambda s, i: (s, i, 0))],
        out_shape=[jax.ShapeDtypeStruct((G, T, Fs), BF16)],
        sem=("parallel", "parallel"), args=(h, wg), riders=riders)


def ffn_up_act(h, wu, g, name, riders=()):
    T, D = h.shape
    G, _, Fs = wu.shape
    tm = min(512, T)

    def body(h_ref, wu_ref, g_ref, u_ref, a_ref):
        u = _dot(h_ref[...], wu_ref[...])
        g = g_ref[...].astype(F32)
        u_ref[...] = u.astype(BF16)
        a_ref[...] = ((g * _sigmoid(g)) * u).astype(BF16)

    act = pl.BlockSpec((None, tm, Fs), lambda s, i: (s, i, 0))
    return _call(
        body, name=name, grid=(G, T // tm),
        in_specs=[pl.BlockSpec((tm, D), lambda s, i: (i, 0)),
                  pl.BlockSpec((None, D, Fs), lambda s, i: (s, 0, 0), pipeline_mode=pl.Buffered(1)), act],
        out_specs=[act, act],
        out_shape=[jax.ShapeDtypeStruct((G, T, Fs), BF16)] * 2,
        sem=("parallel", "parallel"), args=(h, wu, g), riders=riders)


def ffn_down(a, wd, x, name, riders=()):
    G, T, Fs = a.shape
    D = wd.shape[2]
    tm = min(512, T)

    def body(a_ref, w_ref, x_ref, o_ref, acc):
        s = pl.program_id(1)

        @pl.when(s == 0)
        def _():
            acc[...] = jnp.zeros_like(acc)

        acc[...] += _dot(a_ref[...], w_ref[...])

        @pl.when(s == G - 1)
        def _():
            o_ref[...] = x_ref[...] + 0.5 * acc[...]

    return _call(
        body, name=name, grid=(T // tm, G),
        in_specs=[pl.BlockSpec((None, tm, Fs), lambda i, s: (s, i, 0)),
                  pl.BlockSpec((None, Fs, D), lambda i, s: (s, 0, 0)),
                  pl.BlockSpec((tm, D), lambda i, s: (i, 0))],
        out_specs=[pl.BlockSpec((tm, D), lambda i, s: (i, 0))],
        out_shape=[jax.ShapeDtypeStruct((T, D), F32)],
        scratch_shapes=[pltpu.VMEM((tm, D), F32)],
        sem=("parallel", "arbitrary"), args=(a, wd, x), riders=riders)


def ffn_bwd_act(dx, wd, g, u, name, riders=()):
    T, D = dx.shape
    G, Fs, _ = wd.shape
    tm = min(512, T)

    def body(dx_ref, w_ref, g_ref, u_ref, dg_ref, du_ref):
        dxh = (0.5 * dx_ref[...]).astype(BF16)
        for cols in _column_halves(Fs):
            da = _dot_nt(dxh, w_ref[cols, :])
            gv = g_ref[:, cols].astype(F32)
            sig = _sigmoid(gv)
            silu = gv * sig
            du_ref[:, cols] = (da * silu).astype(BF16)
            dg_ref[:, cols] = ((da * u_ref[:, cols].astype(F32)) * (sig * (1.0 + gv * (1.0 - sig)))).astype(BF16)

    act = pl.BlockSpec((None, tm, Fs), lambda s, i: (s, i, 0))
    return _call(
        body, name=name, grid=(G, T // tm),
        in_specs=[pl.BlockSpec((tm, D), lambda s, i: (i, 0)),
                  pl.BlockSpec((None, Fs, D), lambda s, i: (s, 0, 0), pipeline_mode=pl.Buffered(1)),
                  act, act],
        out_specs=[act, act],
        out_shape=[jax.ShapeDtypeStruct((G, T, Fs), BF16), jax.ShapeDtypeStruct((G, T, Fs), BF16)],
        sem=("parallel", "parallel"), args=(dx, wd, g, u), riders=riders)


def ffn_bwd_dh(dg, du, wg, wu, name, riders=(), part=(0, 1), prev=None):
    G, T, Fs = dg.shape
    D = wg.shape[1]
    tm = min(512, T // part[1])
    nblk = T // tm // part[1]
    off = part[0] * nblk

    def body(dg_ref, du_ref, wg_ref, wu_ref, *rest):
        o_ref, acc = rest[-2:]
        s = pl.program_id(1)

        @pl.when(s == 0)
        def _():
            acc[...] = jnp.zeros_like(acc)

        acc[...] += _dot_nt(dg_ref[...], wg_ref[...])
        acc[...] += _dot_nt(du_ref[...], wu_ref[...])

        @pl.when(s == G - 1)
        def _():
            o_ref[...] = acc[...]

    act = pl.BlockSpec((None, tm, Fs), lambda i, s: (s, i + off, 0))
    w_spec = pl.BlockSpec((None, D, Fs), lambda i, s: (s, 0, 0))
    extra = [] if prev is None else [prev]
    return _call(
        body, name=name, grid=(nblk, G),
        in_specs=[act, act, w_spec, w_spec] + [pl.BlockSpec(memory_space=pl.ANY)] * len(extra),
        out_specs=[pl.BlockSpec((tm, D), lambda i, s: (i + off, 0))],
        out_shape=[jax.ShapeDtypeStruct((T, D), F32)],
        scratch_shapes=[pltpu.VMEM((tm, D), F32)],
        sem=("parallel", "arbitrary"), args=(dg, du, wg, wu, *extra), riders=riders,
        aliases={4: 0} if extra else None)


def wgrad(a, b, name, *, groups, m, n, a_block, a_map, b_block, b_map, b_scale=None, tk=1024, riders=()):
    T = a.shape[-2]
    tk = min(tk, T)
    nk = T // tk

    def body(a_ref, b_ref, o_ref, acc):
        k = pl.program_id(2)

        @pl.when(k == 0)
        def _():
            acc[...] = jnp.zeros_like(acc)

        bv = b_ref[...]
        if b_scale is not None:
            bv = b_scale * bv
        acc[...] += _dot_tn(a_ref[...].astype(BF16), bv.astype(BF16))

        @pl.when(k == nk - 1)
        def _():
            o_ref[...] = acc[...].astype(BF16)

    return _call(
        body, name=name, grid=(groups, 2, nk),
        in_specs=[pl.BlockSpec(a_block(tk), a_map), pl.BlockSpec(b_block(tk), b_map)],
        out_specs=[pl.BlockSpec((None, None, m, n), lambda s, j, k: (j, s, 0, 0))],
        out_shape=[jax.ShapeDtypeStruct((2, groups, m, n), BF16)],
        scratch_shapes=[pltpu.VMEM((m, n), F32)],
        sem=("parallel", "parallel", "arbitrary"), args=(a, b), riders=riders)


def in_proj(h, w_in, name, riders=()):
    T, D = h.shape
    G, _, Ps = w_in.shape
    tm = min(1024, T)

    def body(h_ref, w_ref, o_ref):
        o_ref[...] = _dot(h_ref[...], w_ref[...]).astype(BF16)

    return _call(
        body, name=name, grid=(G, T // tm),
        in_specs=[pl.BlockSpec((tm, D), lambda s, i: (i, 0)),
                  pl.BlockSpec((None, D, Ps), lambda s, i: (s, 0, 0))],
        out_specs=[pl.BlockSpec((tm, Ps), lambda s, i: (i, s))],
        out_shape=[jax.ShapeDtypeStruct((T, G * Ps), BF16)],
        sem=("parallel", "parallel"), args=(h, w_in), riders=riders)


def in_proj_bwd(dproj, w_in, name, riders=()):
    T = dproj.shape[0]
    G, D, Ps = w_in.shape
    tm = min(512, T)

    def body(d_ref, w_ref, o_ref, acc):
        s = pl.program_id(1)

        @pl.when(s == 0)
        def _():
            acc[...] = jnp.zeros_like(acc)

        acc[...] += _dot_nt(d_ref[...], w_ref[...])

        @pl.when(s == G - 1)
        def _():
            o_ref[...] = acc[...]

    return _call(
        body, name=name, grid=(T // tm, G),
        in_specs=[pl.BlockSpec((tm, Ps), lambda i, s: (i, s)),
                  pl.BlockSpec((None, D, Ps), lambda i, s: (s, 0, 0))],
        out_specs=[pl.BlockSpec((tm, D), lambda i, s: (i, 0))],
        out_shape=[jax.ShapeDtypeStruct((T, D), F32)],
        scratch_shapes=[pltpu.VMEM((tm, D), F32)],
        sem=("parallel", "arbitrary"), args=(dproj, w_in), riders=riders)


def out_proj(y, w_out, x, name):
    T, M = y.shape
    D = w_out.shape[1]
    tm = min(512, T)

    def body(y_ref, w_ref, x_ref, o_ref):
        o_ref[...] = x_ref[...] + _dot(y_ref[...], w_ref[...])

    return pl.pallas_call(
        body, name=name, grid=(T // tm,),
        in_specs=[pl.BlockSpec((tm, M), lambda i: (i, 0)), pl.BlockSpec((M, D), lambda i: (0, 0)),
                  pl.BlockSpec((tm, D), lambda i: (i, 0))],
        out_specs=pl.BlockSpec((tm, D), lambda i: (i, 0)),
        out_shape=jax.ShapeDtypeStruct((T, D), F32),
        compiler_params=_params("parallel"),
    )(y, w_out, x)


def out_norm(ya, yb, ga, gb, name):
    T, W = ya.shape
    tr = min(512, T)

    def body(a_ref, b_ref, ga_ref, gb_ref, o_ref):
        for src, gam, col in ((a_ref, ga_ref, 0), (b_ref, gb_ref, W)):
            v = src[...]
            r = lax.rsqrt(jnp.mean(v * v, axis=-1, keepdims=True) + EPS)
            o_ref[:, col:col + W] = ((v * r) * gam[...]).astype(BF16)

    row = pl.BlockSpec((tr, W), lambda i: (i, 0))
    vec = pl.BlockSpec((1, W), lambda i: (0, 0))
    return pl.pallas_call(
        body, name=name, grid=(T // tr,),
        in_specs=[row, row, vec, vec],
        out_specs=pl.BlockSpec((tr, 2 * W), lambda i: (i, 0)),
        out_shape=jax.ShapeDtypeStruct((T, 2 * W), BF16),
        compiler_params=_params("parallel"),
    )(ya, yb, ga, gb)


def out_proj_bwd(dx, w_out, ya, yb, ga, gb, name):
    T, D = dx.shape
    M = w_out.shape[0]
    W = M // 2
    tm = min(256, T)

    def body(dx_ref, w_ref, a_ref, b_ref, ga_ref, gb_ref, da_ref, db_ref, dga_ref, dgb_ref):
        dy = _dot_nt(dx_ref[...].astype(BF16), w_ref[...])

        @pl.when(pl.program_id(0) == 0)
        def _():
            dga_ref[...] = jnp.zeros_like(dga_ref)
            dgb_ref[...] = jnp.zeros_like(dgb_ref)

        d_a, dg_a = _rms_bwd(a_ref[...], ga_ref[...], dy[:, :W])
        d_b, dg_b = _rms_bwd(b_ref[...], gb_ref[...], dy[:, W:])
        da_ref[...] = d_a
        db_ref[...] = d_b
        dga_ref[...] += dg_a
        dgb_ref[...] += dg_b

    row = pl.BlockSpec((tm, W), lambda i: (i, 0))
    vec = pl.BlockSpec((1, W), lambda i: (0, 0))
    return pl.pallas_call(
        body, name=name, grid=(T // tm,),
        in_specs=[pl.BlockSpec((tm, D), lambda i: (i, 0)), pl.BlockSpec((M, D), lambda i: (0, 0)),
                  row, row, vec, vec],
        out_specs=[row, row, vec, vec],
        out_shape=[jax.ShapeDtypeStruct((T, W), F32), jax.ShapeDtypeStruct((T, W), F32),
                   jax.ShapeDtypeStruct((1, W), F32), jax.ShapeDtypeStruct((1, W), F32)],
        compiler_params=_params("arbitrary"),
    )(dx, w_out, ya, yb, ga, gb)


def final_loss(x, gamma, target, name):
    T, D = x.shape
    tr = min(256, T)

    def body(x_ref, g_ref, t_ref, dx_ref, dg_ref, loss_ref):
        xv = x_ref[...]
        gam = g_ref[...]
        r = lax.rsqrt(jnp.mean(xv * xv, axis=-1, keepdims=True) + EPS)
        err = (xv * r) * gam - t_ref[...]
        part = 0.5 * jnp.sum(jnp.mean(err * err, axis=-1, keepdims=True), axis=0, keepdims=True)
        dx, dgamma = _rms_bwd(xv, gam, err / D)
        dx_ref[...] = dx

        @pl.when(pl.program_id(0) == 0)
        def _():
            dg_ref[...] = jnp.zeros_like(dg_ref)
            loss_ref[...] = jnp.zeros_like(loss_ref)

        dg_ref[...] += dgamma
        loss_ref[...] += jnp.broadcast_to(part, loss_ref.shape)

    row = pl.BlockSpec((tr, D), lambda i: (i, 0))
    vec = pl.BlockSpec((1, D), lambda i: (0, 0))
    return pl.pallas_call(
        body, name=name, grid=(T // tr,),
        in_specs=[row, vec, row],
        out_specs=[row, vec, pl.BlockSpec((1, LANES), lambda i: (0, 0))],
        out_shape=[jax.ShapeDtypeStruct((T, D), F32), jax.ShapeDtypeStruct((1, D), F32),
                   jax.ShapeDtypeStruct((1, LANES), F32)],
        compiler_params=_params("arbitrary"),
    )(x, gamma, target)


def _lane_select(src_half, dst_halves):
    r = lax.broadcasted_iota(jnp.int32, (LANES, LANES), 0)
    c = lax.broadcasted_iota(jnp.int32, (LANES, LANES), 1)
    hit = ((r >> 6) == src_half) & ((r & (HEAD_DIM - 1)) == (c & (HEAD_DIM - 1)))
    ok = functools.reduce(jnp.logical_or, [(c >> 6) == d for d in dst_halves])
    return jnp.where(hit & ok, 1.0, 0.0).astype(BF16)


def _swa_slope(kvp, local_head):
    lo = 2.0 ** (-8.0 * (local_head + 1) / N_SWA_HEADS)
    hi = 2.0 ** (-8.0 * (8 + local_head + 1) / N_SWA_HEADS)
    return jnp.where(kvp == 0, jnp.float32(lo), jnp.float32(hi))


GQA = N_SWA_HEADS // N_SWA_KV


def _per_head_rows(values):
    return jnp.concatenate([jnp.zeros((SWA_BLOCK, 1), F32) + v for v in values], axis=0)


def _swa_probs(qblk, kwin, n, start, slope, sink):
    z = _dot_nt(qblk, kwin) * ATT_SCALE
    t = n * SWA_BLOCK + (lax.broadcasted_iota(jnp.int32, z.shape, 0) & (SWA_BLOCK - 1))
    s = start + lax.broadcasted_iota(jnp.int32, z.shape, 1)
    dist = t - s
    valid = (dist >= 0) & (dist < WINDOW)
    z = jnp.where(valid, z - slope * dist.astype(F32), NEG_BIG)
    m = jnp.maximum(jnp.max(z, axis=-1, keepdims=True), sink)
    e = jnp.exp(z - m)
    den = jnp.sum(e, axis=-1, keepdims=True) + jnp.exp(sink - m)
    return e / den, m, den


def swa_fwd(proj, sinks, b_loc, seq, name, riders=()):
    T = proj.shape[0]
    nblk = seq // SWA_BLOCK

    def body(sink_ref, q_ref, k_ref, v_ref, o_ref, qs, vv):
        kvp = pl.program_id(1)
        lane = lax.broadcasted_iota(jnp.int32, (SWA_BLOCK, LANES), 1)
        for kvh in range(2):
            vv[...] = _dot(v_ref[...], _lane_select(kvh, (0, 1))).astype(BF16)
            for g in range(GQA):
                cols = slice(LANES * (2 * kvh + g // 2), LANES * (2 * kvh + g // 2 + 1))
                qs[g] = _dot(q_ref[:, cols], _lane_select(g % 2, (kvh,))).astype(BF16)
            slope = _per_head_rows([_swa_slope(kvp, GQA * kvh + g) for g in range(GQA)])
            sink = _per_head_rows([sink_ref[0, 8 * kvp + GQA * kvh + g] for g in range(GQA)])

            def blk(n, carry):
                start = pl.multiple_of(jnp.maximum(n - 1, 0) * SWA_BLOCK, SWA_BLOCK)
                rows = pl.ds(pl.multiple_of(n * SWA_BLOCK, SWA_BLOCK), SWA_BLOCK)
                win = pl.ds(start, 2 * SWA_BLOCK)
                qblk = jnp.concatenate([qs[g, rows, :] for g in range(GQA)], axis=0)
                p, _, _ = _swa_probs(qblk, k_ref[win, :], n, start, slope, sink)
                out = _dot(p.astype(BF16), vv[win, :])
                for pair in range(GQA // 2):
                    cols = slice(LANES * (2 * kvh + pair), LANES * (2 * kvh + pair + 1))
                    first = out[2 * pair * SWA_BLOCK:(2 * pair + 1) * SWA_BLOCK]
                    second = out[(2 * pair + 1) * SWA_BLOCK:(2 * pair + 2) * SWA_BLOCK]
                    o_ref[rows, cols] = jnp.where(lane < HEAD_DIM, first, second)
                return carry

            lax.fori_loop(0, nblk, blk, 0)

    return _call(
        body, name=name, grid=(b_loc, 2),
        in_specs=[pl.BlockSpec(memory_space=pltpu.SMEM),
                  pl.BlockSpec((seq, 4 * LANES), lambda b, p: (b, p)),
                  pl.BlockSpec((seq, LANES), lambda b, p: (b, 8 + p)),
                  pl.BlockSpec((seq, LANES), lambda b, p: (b, 10 + p))],
        out_specs=[pl.BlockSpec((seq, 4 * LANES), lambda b, p: (b, p))],
        out_shape=[jax.ShapeDtypeStruct((T, N_SWA_HEADS * HEAD_DIM), F32)],
        scratch_shapes=[pltpu.VMEM((GQA, seq, LANES), BF16), pltpu.VMEM((seq, LANES), BF16)],
        sem=("parallel", "parallel"), args=(sinks, proj, proj, proj), riders=riders)


def swa_bwd(proj, sinks, dya, b_loc, seq, name, riders=()):
    T = proj.shape[0]
    nblk = seq // SWA_BLOCK

    def body(sink_ref, q_ref, k_ref, v_ref, do_ref, out_ref, ds_ref, stage, qs, dos, kk, dk_acc, dv_acc, sems):
        compute(sink_ref, q_ref, k_ref, v_ref, do_ref, stage, ds_ref, qs, dos, kk, dk_acc, dv_acc)
        kvp = pl.program_id(0)
        rows = pl.ds(pl.multiple_of(pl.program_id(1) * seq, seq), seq)
        tiles = [4 * kvp + j for j in range(4)] + [8 + kvp, 10 + kvp]
        copies = []
        for j, t in enumerate(tiles):
            cols = pl.ds(pl.multiple_of(t * LANES, LANES), LANES)
            copies.append(pltpu.make_async_copy(stage.at[j], out_ref.at[rows, cols], sems.at[j]))
        for cp in copies:
            cp.start()
        for cp in copies:
            cp.wait()

    def compute(sink_ref, q_ref, k_ref, v_ref, do_ref, stage, ds_ref, qs, dos, kk, dk_acc, dv_acc):
        kvp = pl.program_id(0)
        b = pl.program_id(1)
        lane = lax.broadcasted_iota(jnp.int32, (SWA_BLOCK, LANES), 1)

        @pl.when(b == 0)
        def _():
            ds_ref[...] = jnp.zeros_like(ds_ref)

        dk_acc[...] = jnp.zeros_like(dk_acc)
        dv_acc[...] = jnp.zeros_like(dv_acc)
        for kvh in range(2):
            kk[...] = _dot(k_ref[...], _lane_select(kvh, (0, 1))).astype(BF16)
            for g in range(GQA):
                cols = slice(LANES * (2 * kvh + g // 2), LANES * (2 * kvh + g // 2 + 1))
                move = _lane_select(g % 2, (kvh,))
                qs[g] = _dot(q_ref[:, cols], move).astype(BF16)
                dos[g] = _dot(do_ref[:, cols].astype(BF16), move).astype(BF16)
            slope = _per_head_rows([_swa_slope(kvp, GQA * kvh + g) for g in range(GQA)])
            sink = _per_head_rows([sink_ref[0, 8 * kvp + GQA * kvh + g] for g in range(GQA)])

            def blk(n, carry):
                start = pl.multiple_of(jnp.maximum(n - 1, 0) * SWA_BLOCK, SWA_BLOCK)
                rows = pl.ds(pl.multiple_of(n * SWA_BLOCK, SWA_BLOCK), SWA_BLOCK)
                win = pl.ds(start, 2 * SWA_BLOCK)
                qblk = jnp.concatenate([qs[g, rows, :] for g in range(GQA)], axis=0)
                doblk = jnp.concatenate([dos[g, rows, :] for g in range(GQA)], axis=0)
                p, m, den = _swa_probs(qblk, k_ref[win, :], n, start, slope, sink)
                dp = _dot_nt(doblk, v_ref[win, :])
                delta = jnp.sum(p * dp, axis=-1, keepdims=True)
                leak = (jnp.exp(sink - m) / den) * delta
                for g in range(GQA):
                    head = GQA * kvh + g
                    ds_ref[head:head + 1, :] -= jnp.broadcast_to(
                        jnp.sum(leak[g * SWA_BLOCK:(g + 1) * SWA_BLOCK], axis=0, keepdims=True), (1, LANES))
                dsc = ((p * (dp - delta)) * ATT_SCALE).astype(BF16)
                dv_acc[win, :] += _dot_tn(p.astype(BF16), doblk)
                dk_acc[win, :] += _dot_tn(dsc, qblk)
                dq = _dot(dsc, kk[win, :])
                for pair in range(GQA // 2):
                    first = dq[2 * pair * SWA_BLOCK:(2 * pair + 1) * SWA_BLOCK]
                    second = dq[(2 * pair + 1) * SWA_BLOCK:(2 * pair + 2) * SWA_BLOCK]
                    stage[2 * kvh + pair, rows, :] = jnp.where(lane < HEAD_DIM, first, second).astype(BF16)
                return carry

            lax.fori_loop(0, nblk, blk, 0)
        stage[4] = dk_acc[...].astype(BF16)
        stage[5] = dv_acc[...].astype(BF16)

    return _call(
        body, name=name, grid=(2, b_loc),
        in_specs=[pl.BlockSpec(memory_space=pltpu.SMEM),
                  pl.BlockSpec((seq, 4 * LANES), lambda p, b: (b, p)),
                  pl.BlockSpec((seq, LANES), lambda p, b: (b, 8 + p)),
                  pl.BlockSpec((seq, LANES), lambda p, b: (b, 10 + p)),
                  pl.BlockSpec((seq, 4 * LANES), lambda p, b: (b, p))],
        out_specs=[pl.BlockSpec(memory_space=pl.ANY),
                   pl.BlockSpec((8, LANES), lambda p, b: (p, 0))],
        out_shape=[jax.ShapeDtypeStruct((T, proj.shape[1]), BF16),
                   jax.ShapeDtypeStruct((N_SWA_HEADS, LANES), F32)],
        scratch_shapes=[pltpu.VMEM((6, seq, LANES), BF16),
                        pltpu.VMEM((GQA, seq, LANES), BF16), pltpu.VMEM((GQA, seq, LANES), BF16),
                        pltpu.VMEM((seq, LANES), BF16),
                        pltpu.VMEM((seq, LANES), F32), pltpu.VMEM((seq, LANES), F32),
                        pltpu.SemaphoreType.DMA((6,))],
        sem=("arbitrary", "arbitrary"), args=(sinks, proj, proj, proj, dya), riders=riders)


def _softplus(z):
    return jnp.maximum(z, 0.0) + jnp.log(1.0 + jnp.exp(-jnp.abs(z)))


def _tri(n, rel):
    r = lax.broadcasted_iota(jnp.int32, (n, n), 0)
    c = lax.broadcasted_iota(jnp.int32, (n, n), 1)
    return jnp.where(rel(r, c), 1.0, 0.0).astype(BF16)


def sb_fwd(proj, b_loc, seq, name, riders=()):
    T = proj.shape[0]
    bq = min(SB_BLOCK, seq)
    nq = seq // bq
    npair = N_SB_HEADS // 2

    def body(q_ref, k_ref, v_ref, o_ref, lt_ref):
        i = pl.program_id(2)
        lane = lax.broadcasted_iota(jnp.int32, (bq, LANES), 1)
        after = _tri(bq, lambda r, c: r > c)
        row = lax.broadcasted_iota(jnp.int32, (bq, bq), 0)
        col = lax.broadcasted_iota(jnp.int32, (bq, bq), 1)
        qv = q_ref[...] * jnp.asarray(ATT_SCALE, BF16)
        blank = jnp.zeros((bq, LANES), BF16)
        q2 = jnp.concatenate([jnp.where((lane >> 6) == h, qv, blank) for h in range(2)], axis=0)


        def tile(j, state, mask):
            rows = pl.ds(pl.multiple_of(j * bq, bq), bq)
            kj, vj = k_ref[rows, :], v_ref[rows, :]
            zz = _dot_nt(q2, kj)
            log_beta, totals, pieces = [], [], []
            for h in range(2):
                z = zz[h * bq:(h + 1) * bq]
                sp = _softplus(z)
                lg = -sp if mask is None else jnp.where(mask, -sp, 0.0)
                hi = lg.astype(BF16)
                pieces += [hi, (lg - hi.astype(F32)).astype(BF16)]
                log_beta.append(z - sp if mask is None else jnp.where(mask, z - sp, NEG_BIG))
                totals.append(jnp.sum(lg, axis=-1, keepdims=True))
            cc = _dot(jnp.concatenate(pieces, axis=0), after)
            probs = []
            for h in range(2):
                inside = cc[2 * h * bq:(2 * h + 1) * bq] + cc[(2 * h + 1) * bq:(2 * h + 2) * bq]
                probs.append(jnp.exp(log_beta[h] + (inside + state[2 * h])).astype(BF16))
            pv = _dot(jnp.concatenate(probs, axis=0), vj)
            return (state[0] + totals[0], state[1] + pv[:bq], state[2] + totals[1], state[3] + pv[bq:])

        zero = (jnp.zeros((bq, 1), F32), jnp.zeros((bq, LANES), F32))
        state = tile(i, zero + zero, col < row)
        state = lax.fori_loop(0, i, lambda jj, st: tile(i - 1 - jj, st, None), state)
        o_ref[...] = jnp.where(lane < HEAD_DIM, state[1], state[3])
        lt_ref[...] = jnp.where(lane < HEAD_DIM, state[0], state[2])

    return _call(
        body, name=name, grid=(b_loc, npair, nq),
        in_specs=[pl.BlockSpec((bq, LANES), lambda b, p, i: (b * nq + i, 12 + p)),
                  pl.BlockSpec((seq, LANES), lambda b, p, i: (b, 20 + p)),
                  pl.BlockSpec((seq, LANES), lambda b, p, i: (b, 28 + p))],
        out_specs=[pl.BlockSpec((bq, LANES), lambda b, p, i: (b * nq + i, p)),
                   pl.BlockSpec((None, bq, LANES), lambda b, p, i: (b * npair + p, i, 0))],
        out_shape=[jax.ShapeDtypeStruct((T, N_SB_HEADS * HEAD_DIM), F32),
                   jax.ShapeDtypeStruct((b_loc * npair, seq, LANES), F32)],
        sem=("parallel", "parallel", "parallel"), args=(proj, proj, proj), riders=riders)


def sb_bwd(proj, dyb, ltot, dproj, b_loc, seq, name, riders=()):
    T = proj.shape[0]
    bq = min(SB_BLOCK, seq)
    nq = seq // bq
    npair = N_SB_HEADS // 2

    def body(q_ref, k_ref, v_ref, do_ref, lt_ref, _, out_ref, stage, dk_acc, dv_acc, sems):
        compute(q_ref, k_ref, v_ref, do_ref, lt_ref, stage, dk_acc, dv_acc)
        rows = pl.ds(pl.multiple_of(pl.program_id(0) * seq, seq), seq)
        copies = []
        for j in range(3):
            tile = pl.multiple_of((12 + pl.program_id(1) + npair * j) * LANES, LANES)
            copies.append(pltpu.make_async_copy(stage.at[j], out_ref.at[rows, pl.ds(tile, LANES)], sems.at[j]))
        for cp in copies:
            cp.start()
        for cp in copies:
            cp.wait()

    def compute(q_ref, k_ref, v_ref, do_ref, lt_ref, stage, dk_acc, dv_acc):
        lane = lax.broadcasted_iota(jnp.int32, (bq, LANES), 1)
        upto = _tri(bq, lambda r, c: r <= c)
        before = _tri(bq, lambda r, c: r < c)
        row = lax.broadcasted_iota(jnp.int32, (bq, bq), 0)
        col = lax.broadcasted_iota(jnp.int32, (bq, bq), 1)
        dk_acc[...] = jnp.zeros_like(dk_acc)
        dv_acc[...] = jnp.zeros_like(dv_acc)
        scale = jnp.asarray(ATT_SCALE, BF16)
        heads = [(lane >> 6) == h for h in range(2)]
        blank = jnp.zeros((bq, LANES), BF16)

        def qblock(i, carry):
            qrows = pl.ds(pl.multiple_of(i * bq, bq), bq)
            qv = q_ref[qrows, :] * scale
            dov = do_ref[qrows, :].astype(BF16)
            ltv = lt_ref[qrows, :]
            q2 = jnp.concatenate([jnp.where(m, qv, blank) for m in heads], axis=0)
            do2 = jnp.concatenate([jnp.where(m, dov, blank) for m in heads], axis=0)
            totals = [jnp.sum(jnp.where(lane == h * HEAD_DIM, ltv, 0.0), axis=-1, keepdims=True) for h in range(2)]

            def split2(xs):
                out = []
                for x in xs:
                    hi = x.astype(BF16)
                    out += [hi, (x - hi.astype(F32)).astype(BF16)]
                return jnp.concatenate(out, axis=0)

            def join2(cc, h):
                return cc[2 * h * bq:(2 * h + 1) * bq] + cc[(2 * h + 1) * bq:(2 * h + 2) * bq]

            def tile(j, state, mask):
                krows = pl.ds(pl.multiple_of(j * bq, bq), bq)
                kj, vj = k_ref[krows, :], v_ref[krows, :]
                kv = kj * scale
                k2 = jnp.concatenate([jnp.where(m, kv, blank) for m in heads], axis=0)
                zz = _dot_nt(q2, kj)
                dd = _dot_nt(do2, vj)
                lgs, log_betas = [], []
                for h in range(2):
                    z = zz[h * bq:(h + 1) * bq]
                    sp = _softplus(z)
                    lgs.append(-sp if mask is None else jnp.where(mask, -sp, 0.0))
                    log_betas.append(z - sp)
                cl = _dot(split2(lgs), upto)
                probs, gs = [], []
                for h in range(2):
                    a = jnp.exp(log_betas[h] + (totals[h] - (join2(cl, h) + state[2 * h])))
                    if mask is not None:
                        a = jnp.where(mask, a, 0.0)
                    probs.append(a.astype(BF16))
                    gs.append(a * dd[h * bq:(h + 1) * bq])
                cg = _dot(split2(gs), before)
                dzs = []
                for h in range(2):
                    beta = jnp.exp(log_betas[h])
                    dz = gs[h] * (1.0 - beta) - beta * (join2(cg, h) + state[2 * h + 1])
                    if mask is not None:
                        dz = jnp.where(mask, dz, 0.0)
                    dzs.append(dz.astype(BF16))
                dv_acc[krows, :] += _dot_tn(jnp.concatenate(probs, axis=0), do2)
                dk_acc[krows, :] += _dot_tn(jnp.concatenate(dzs, axis=0), q2)
                dq = state[4] + _dot(jnp.concatenate(dzs, axis=1), k2)
                new = []
                for h in range(2):
                    new += [state[2 * h] + jnp.sum(lgs[h], axis=-1, keepdims=True),
                            state[2 * h + 1] + jnp.sum(gs[h], axis=-1, keepdims=True)]
                return (*new, dq)

            zero = jnp.zeros((bq, 1), F32)
            state = lax.fori_loop(0, i, lambda j, st: tile(j, st, None),
                                  (zero, zero, zero, zero, jnp.zeros((bq, LANES), F32)))
            state = tile(i, state, col < row)
            stage[0, qrows, :] = state[4].astype(BF16)
            return carry

        lax.fori_loop(0, nq, qblock, 0)
        stage[1] = dk_acc[...].astype(BF16)
        stage[2] = dv_acc[...].astype(BF16)

    return _call(
        body, name=name, grid=(b_loc, npair),
        in_specs=[pl.BlockSpec((seq, LANES), lambda b, p: (b, 12 + p)),
                  pl.BlockSpec((seq, LANES), lambda b, p: (b, 20 + p)),
                  pl.BlockSpec((seq, LANES), lambda b, p: (b, 28 + p)),
                  pl.BlockSpec((seq, LANES), lambda b, p: (b, p)),
                  pl.BlockSpec((None, seq, LANES), lambda b, p: (b * npair + p, 0, 0)),
                  pl.BlockSpec(memory_space=pl.ANY)],
        out_specs=[pl.BlockSpec(memory_space=pl.ANY)],
        out_shape=[jax.ShapeDtypeStruct(dproj.shape, BF16)],
        scratch_shapes=[pltpu.VMEM((3, seq, LANES), BF16)] + [pltpu.VMEM((seq, LANES), F32)] * 2
        + [pltpu.SemaphoreType.DMA((3,))],
        sem=("arbitrary", "arbitrary"), args=(proj, proj, proj, dyb, ltot, dproj), riders=riders,
        aliases={5: 0})


def allreduce_small(v, name):
    R = v.shape[0]
    ndev = 8

    def body(v_ref, o_ref, land, send_sems, recv_sems):
        x, y, c = _place()
        me = 4 * x + 2 * y + c
        land[me] = v_ref[...]
        started = []
        for f in range(1, ndev):
            fx, fy, fc = (f >> 2) & 1, (f >> 1) & 1, f & 1
            peer = ((1 - x) if fx else x, (1 - y) if fy else y, (1 - c) if fc else c)
            cp = pltpu.make_async_remote_copy(
                src_ref=v_ref, dst_ref=land.at[me], send_sem=send_sems.at[f - 1], recv_sem=recv_sems.at[f - 1],
                device_id=peer, device_id_type=MESH)
            cp.start()
            started.append(cp)
        for cp in started:
            cp.wait()
        total = land[0]
        for d in range(1, ndev):
            total = total + land[d]
        o_ref[...] = total

    return pl.pallas_call(
        body, name=name,
        in_specs=[VMEM_WHOLE], out_specs=VMEM_WHOLE,
        out_shape=jax.ShapeDtypeStruct(v.shape, F32),
        scratch_shapes=[pltpu.VMEM((ndev, R, LANES), F32),
                        pltpu.SemaphoreType.DMA((ndev - 1,)), pltpu.SemaphoreType.DMA((ndev - 1,))],
    )(v)


def sum_cores(grads, theirs, name):
    _, _, r, c = grads.shape
    tr, tc = _tiles(r, c, 1 << 18)
    core = lax.axis_index("c").astype(jnp.int32).reshape(1)

    def body(core_ref, g_ref, t_ref, o_ref):
        o_ref[...] = (g_ref[...].astype(F32) + t_ref[...].astype(F32)).astype(BF16)

    blk = pl.BlockSpec((N_CHIPS, tr, tc), lambda i, j, core_ref: (0, i, j))
    return pl.pallas_call(
        body, name=name,
        grid_spec=pltpu.PrefetchScalarGridSpec(
            num_scalar_prefetch=1, grid=(r // tr, c // tc),
            in_specs=[pl.BlockSpec((None, N_CHIPS, tr, tc), lambda i, j, core_ref: (core_ref[0], 0, i, j)), blk],
            out_specs=blk),
        out_shape=jax.ShapeDtypeStruct((N_CHIPS, r, c), BF16),
        compiler_params=_params("parallel", "parallel"),
    )(core, grads, theirs)


def sum_chips(halves, landed, name):
    _, r, c = landed.shape
    tr, tc = _tiles(r, c, 1 << 18)
    chip = (2 * lax.axis_index("x") + lax.axis_index("y")).astype(jnp.int32).reshape(1)

    def body(chip_ref, own_ref, p_ref, o_ref):
        total = own_ref[...].astype(F32)
        for j in range(N_CHIPS - 1):
            total = total + p_ref[j].astype(F32)
        o_ref[...] = total

    return pl.pallas_call(
        body, name=name,
        grid_spec=pltpu.PrefetchScalarGridSpec(
            num_scalar_prefetch=1, grid=(r // tr, c // tc),
            in_specs=[pl.BlockSpec((None, tr, tc), lambda i, j, chip_ref: (chip_ref[0], i, j)),
                      pl.BlockSpec((N_CHIPS - 1, tr, tc), lambda i, j, chip_ref: (0, i, j))],
            out_specs=pl.BlockSpec((tr, tc), lambda i, j, chip_ref: (i, j))),
        out_shape=jax.ShapeDtypeStruct((r, c), F32),
        compiler_params=_params("parallel", "parallel"),
    )(chip, halves, landed)


def _adamw_math(w, g, m, v):
    m = ADAM_B1 * m + (1.0 - ADAM_B1) * g
    v = ADAM_B2 * v + (1.0 - ADAM_B2) * (g * g)
    m_hat = m / (1.0 - ADAM_B1 ** ADAM_STEP)
    v_hat = v / (1.0 - ADAM_B2 ** ADAM_STEP)
    delta = -ADAM_LR * (m_hat / (jnp.sqrt(v_hat) + ADAM_EPS) + ADAM_WD * w)
    return delta, m, v


def adamw(w, m, v, g, name):
    r, c = w.shape
    tr, tc = _tiles(r, c, 1 << 18, multiple=8)

    def body(w_ref, m_ref, v_ref, gin_ref, g_ref, d_ref, nm_ref, nv_ref):
        g = gin_ref[...]
        delta, nm, nv = _adamw_math(w_ref[...], g, m_ref[...], v_ref[...])
        g_ref[...] = g
        d_ref[...] = delta
        nm_ref[...] = nm
        nv_ref[...] = nv

    blk = pl.BlockSpec((tr, tc), lambda i, j: (i, j))
    shape = jax.ShapeDtypeStruct((r, c), F32)
    return pl.pallas_call(
        body, name=name, grid=(r // tr, c // tc),
        in_specs=[blk] * 4, out_specs=[blk] * 4, out_shape=[shape] * 4,
        compiler_params=_params("parallel", "parallel"),
    )(w, m, v, g)


def adamw_halves(w, m, v, mine, theirs, name, col_halves=False):
    r, c = w.shape
    hr, hc = (r, c // 2) if col_halves else (r // 2, c)
    tr, tc = _tiles(hr, hc, 1 << 18, multiple=8)
    g_cols = mine.shape[-1]
    padded = g_cols != hc
    if padded and tc != hc:
        raise ValueError("a gradient with padded columns is read in full-width row blocks")
    per_half = hc // tc if col_halves else hr // tr
    core = lax.axis_index("c").astype(jnp.int32).reshape(1)

    def body(core_ref, w_ref, m_ref, v_ref, mine_ref, theirs_ref, g_ref, d_ref, nm_ref, nv_ref):
        half = pl.program_id(1 if col_halves else 0) // per_half
        cols = slice(0, hc) if padded else slice(None)
        g = jnp.where(half == core_ref[0], mine_ref[:, cols], theirs_ref[:, cols])
        delta, nm, nv = _adamw_math(w_ref[...], g, m_ref[...], v_ref[...])
        g_ref[...] = g
        d_ref[...] = delta
        nm_ref[...] = nm
        nv_ref[...] = nv

    blk = pl.BlockSpec((tr, tc), lambda i, j, core_ref: (i, j))

    def g_spec(wanted):
        def index(i, j, core_ref):
            use = ((j if col_halves else i) // per_half == core_ref[0]) == wanted
            if col_halves:
                return i, jnp.where(use, j % per_half, 0)
            return jnp.where(use, i % per_half, 0), j
        return pl.BlockSpec((tr, g_cols if padded else tc), index)

    shape = jax.ShapeDtypeStruct((r, c), F32)
    return pl.pallas_call(
        body, name=name,
        grid_spec=pltpu.PrefetchScalarGridSpec(
            num_scalar_prefetch=1, grid=(r // tr, c // tc),
            in_specs=[blk, blk, blk, g_spec(True), g_spec(False)], out_specs=[blk] * 4),
        out_shape=[shape] * 4,
        compiler_params=_params("parallel", "parallel"),
    )(core, w, m, v, mine, theirs)


MATRICES = ("ffn1_w_gate", "ffn1_w_up", "ffn1_w_down", "w_in", "w_out", "ffn2_w_gate", "ffn2_w_up", "ffn2_w_down")
VECTORS = ("ffn1_norm", "mix_norm", "swa_sinks", "swa_out_norm", "sb_out_norm", "ffn2_norm", "final_norm")
WEIGHTS = ("ffn1_norm", "ffn1_w_gate", "ffn1_w_up", "ffn1_w_down", "mix_norm", "w_in", "swa_sinks", "swa_out_norm",
           "sb_out_norm", "w_out", "ffn2_norm", "ffn2_w_gate", "ffn2_w_up", "ffn2_w_down", "final_norm")


def _pack_rows(vec):
    flat = vec.reshape(-1).astype(F32)
    rows = -(-flat.shape[0] // LANES)
    rows8 = -(-rows // 8) * 8
    flat = jnp.pad(flat, (0, rows8 * LANES - flat.shape[0]))
    return flat.reshape(rows8, LANES)


class GradPipe:
    def __init__(self, name, grads):
        self.name, self.step, self.value = name, 0, grads

    def rider(self):
        make = (half_swap_rider, scatter_rider, final_swap_rider)[self.step]
        return make([self.value])

    def landed(self, outs):
        if self.step == 0:
            self.value = sum_cores(self.value, outs[0], "sum_cores_" + self.name)
        elif self.step == 1:
            self.value = sum_chips(self.value, outs[0], "sum_chips_" + self.name)
        else:
            self.value = (self.value, outs[0])
        self.step += 1


def _hosted(fn, *args, advance=(), **kw):
    outs, routs = fn(*args, riders=[p.rider() for p in advance], **kw)
    for p, ro in zip(advance, routs):
        p.landed(ro)
    return outs


def kernel(x, ffn1_norm, ffn1_w_gate, ffn1_w_up, ffn1_w_down, mix_norm, w_in, swa_sinks, swa_out_norm, sb_out_norm, w_out, ffn2_norm, ffn2_w_gate, ffn2_w_up, ffn2_w_down, final_norm, loss_target, m_ffn1_norm, m_ffn1_w_gate, m_ffn1_w_up, m_ffn1_w_down, m_mix_norm, m_w_in, m_swa_sinks, m_swa_out_norm, m_sb_out_norm, m_w_out, m_ffn2_norm, m_ffn2_w_gate, m_ffn2_w_up, m_ffn2_w_down, m_final_norm, v_ffn1_norm, v_ffn1_w_gate, v_ffn1_w_up, v_ffn1_w_down, v_mix_norm, v_w_in, v_swa_sinks, v_swa_out_norm, v_sb_out_norm, v_w_out, v_ffn2_norm, v_ffn2_w_gate, v_ffn2_w_up, v_ffn2_w_down, v_final_norm):
    given = dict(locals())
    b_loc, seq, D = x.shape
    T = b_loc * seq
    x0 = x.reshape(T, D)
    target = loss_target.reshape(T, D)
    final_g = final_norm.reshape(1, D)

    gathered = {}

    def gather(*names):
        shards = []
        for nm in names:
            w = given[nm][0].astype(BF16)
            hidden_axis = 0 if nm.endswith("w_down") else 1 if nm.startswith("ffn") else None
            if hidden_axis is not None:
                pad = -w.shape[hidden_axis] % LANES
                w = jnp.pad(w, [(0, pad) if a == hidden_axis else (0, 0) for a in range(2)])
            shards.append(w.reshape(2, w.shape[0] // 2, w.shape[1]))
        return names, gather_rider(shards)

    def hosting(fn, *args, fetch):
        names, rider = fetch
        outs, landed = fn(*args, riders=[rider])
        for nm, o in zip(names, landed[0]):
            gathered[nm] = o.reshape(N_CHIPS, 2 * o.shape[2], o.shape[3])
        return outs

    (h1,) = hosting(rmsnorm_cast, x0, ffn1_norm, "ffn1_norm_fwd", fetch=gather("ffn1_w_gate"))
    (g1,) = hosting(ffn_gate, h1, gathered["ffn1_w_gate"], "ffn1_gate", fetch=gather("ffn1_w_up"))
    u1, a1 = hosting(ffn_up_act, h1, gathered["ffn1_w_up"], g1, "ffn1_up_act", fetch=gather("ffn1_w_down"))
    (x1,) = hosting(ffn_down, a1, gathered["ffn1_w_down"], x0, "ffn1_down", fetch=gather("w_in"))
    h2 = rmsnorm_cast(x1, mix_norm, "mix_norm_fwd")
    (proj,) = hosting(in_proj, h2, gathered["w_in"], "in_proj", fetch=gather("w_out"))
    w_out_all = gathered["w_out"].reshape(-1, D)
    (ya,) = hosting(swa_fwd, proj, swa_sinks, b_loc, seq, "swa_fwd", fetch=gather("ffn2_w_gate"))
    yb, ltot = hosting(sb_fwd, proj, b_loc, seq, "sb_fwd", fetch=gather("ffn2_w_up", "ffn2_w_down"))
    y = out_norm(ya, yb, swa_out_norm, sb_out_norm, "out_norm_fwd")
    x2 = out_proj(y, w_out_all, x1, "out_proj")
    h3 = rmsnorm_cast(x2, ffn2_norm, "ffn2_norm_fwd")
    (g2, u2, a2), _ = ffn_up(h3, gathered["ffn2_w_gate"], gathered["ffn2_w_up"], "ffn2_up")
    (x3,), _ = ffn_down(a2, gathered["ffn2_w_down"], x2, "ffn2_down")
    dx3, d_final, loss_part = final_loss(x3, final_g, target, "final_loss")

    Fs = gathered["ffn1_w_gate"].shape[2]
    Ps = gathered["w_in"].shape[2]
    Os = w_out_all.shape[0] // N_CHIPS
    pipes = {}

    def down_grad(tag, act, dx, advance):
        (grads,) = _hosted(wgrad, act, dx, tag + "_dwd", groups=N_CHIPS, m=Fs, n=D // 2,
                           a_block=lambda tk: (None, tk, Fs), a_map=lambda s, j, k: (s, k, 0),
                           b_block=lambda tk: (tk, D // 2), b_map=lambda s, j, k: (k, j), b_scale=0.5,
                           advance=advance)
        return grads

    def up_grad(name, h, d_act, advance):
        (grads,) = _hosted(wgrad, h, d_act, name, groups=N_CHIPS, m=D // 2, n=Fs,
                           a_block=lambda tk: (tk, D // 2), a_map=lambda s, j, k: (k, j),
                           b_block=lambda tk: (None, tk, Fs), b_map=lambda s, j, k: (s, k, 0), advance=advance)
        return grads

    (dg2, du2), _ = ffn_bwd_act(dx3, gathered["ffn2_w_down"], g2, u2, "ffn2_bwd_act")
    pipes["ffn2_w_down"] = p_d2 = GradPipe("ffn2_w_down", down_grad("ffn2", a2, dx3, []))
    pipes["ffn2_w_gate"] = p_g2 = GradPipe("ffn2_w_gate", up_grad("ffn2_dwg", h3, dg2, [p_d2]))
    pipes["ffn2_w_up"] = p_u2 = GradPipe("ffn2_w_up", up_grad("ffn2_dwu", h3, du2, [p_d2, p_g2]))
    (dh3,) = _hosted(ffn_bwd_dh, dg2, du2, gathered["ffn2_w_gate"], gathered["ffn2_w_up"], "ffn2_bwd_dh",
                     advance=[p_d2, p_g2, p_u2])
    (dx2, d_ffn2_norm), _ = rmsnorm_bwd_add(x2, ffn2_norm, dh3, dx3, "ffn2_norm_bwd")

    (d_w_out,) = _hosted(wgrad, y, dx2, "dw_out", groups=N_CHIPS, m=Os // 2, n=D,
                         a_block=lambda tk: (tk, Os // 2), a_map=lambda s, j, k: (k, 2 * s + j),
                         b_block=lambda tk: (tk, D), b_map=lambda s, j, k: (k, 0), advance=[p_g2])
    pipes["w_out"] = p_out = GradPipe("w_out", d_w_out)
    dya, dyb, d_swa_norm, d_sb_norm = out_proj_bwd(dx2, w_out_all, ya, yb, swa_out_norm, sb_out_norm, "out_proj_bwd")
    dproj, d_sinks = _hosted(swa_bwd, proj, swa_sinks, dya, b_loc, seq, "swa_bwd", advance=[p_u2, p_out])
    (dproj,) = _hosted(sb_bwd, proj, dyb, ltot, dproj, b_loc, seq, "sb_bwd", advance=[p_u2, p_out])
    (d_w_in,) = _hosted(wgrad, h2, dproj, "dw_in", groups=N_CHIPS, m=D // 2, n=Ps,
                        a_block=lambda tk: (tk, D // 2), a_map=lambda s, j, k: (k, j),
                        b_block=lambda tk: (tk, Ps), b_map=lambda s, j, k: (k, s), advance=[p_out])
    pipes["w_in"] = p_in = GradPipe("w_in", d_w_in)
    (dh2,) = _hosted(in_proj_bwd, dproj, gathered["w_in"], "in_proj_bwd", advance=[p_in])
    (dx1, d_mix_norm), _ = rmsnorm_bwd_add(x1, mix_norm, dh2, dx2, "mix_norm_bwd")

    pipes["ffn1_w_down"] = p_d1 = GradPipe("ffn1_w_down", down_grad("ffn1", a1, dx1, [p_in]))
    dg1, du1 = _hosted(ffn_bwd_act, dx1, gathered["ffn1_w_down"], g1, u1, "ffn1_bwd_act", advance=[p_in, p_d1])
    pipes["ffn1_w_gate"] = p_g1 = GradPipe("ffn1_w_gate", up_grad("ffn1_dwg", h1, dg1, [p_d1]))
    pipes["ffn1_w_up"] = p_u1 = GradPipe("ffn1_w_up", up_grad("ffn1_dwu", h1, du1, [p_d1, p_g1]))
    (dh1,) = _hosted(ffn_bwd_dh, dg1, du1, gathered["ffn1_w_gate"], gathered["ffn1_w_up"], "ffn1_bwd_dh_a",
                     part=(0, 2), advance=[p_g1, p_u1])
    (dh1,) = _hosted(ffn_bwd_dh, dg1, du1, gathered["ffn1_w_gate"], gathered["ffn1_w_up"], "ffn1_bwd_dh_b",
                     part=(1, 2), prev=dh1, advance=[p_g1, p_u1])
    (dx0, d_ffn1_norm), _ = rmsnorm_bwd_add(x0, ffn1_norm, dh1, dx1, "ffn1_norm_bwd")
    (last,) = exchange_now([p_u1.rider()], "swap_ffn1_w_up")
    p_u1.landed(last)

    vec_grads = dict(ffn1_norm=d_ffn1_norm, mix_norm=d_mix_norm, swa_sinks=d_sinks[:, 0], swa_out_norm=d_swa_norm,
                     sb_out_norm=d_sb_norm, ffn2_norm=d_ffn2_norm, final_norm=d_final)
    packed = [_pack_rows(vec_grads[nm]) for nm in VECTORS] + [_pack_rows(loss_part[0, :1])]
    offsets = [0]
    for p in packed:
        offsets.append(offsets[-1] + p.shape[0])
    reduced = allreduce_small(jnp.concatenate(packed, axis=0), "reduce_vectors")
    loss = reduced[offsets[len(VECTORS)], 0]

    out = {}
    for nm in MATRICES:
        shape = given[nm].shape
        mine, theirs = pipes[nm].value
        res = adamw_halves(given[nm][0], given["m_" + nm][0], given["v_" + nm][0], mine, theirs, "adamw_" + nm,
                           col_halves=nm.endswith("w_down"))
        out[nm] = [r.reshape(shape) for r in res]
    w_rows = jnp.concatenate([_pack_rows(given[nm]) for nm in VECTORS], axis=0)
    m_rows = jnp.concatenate([_pack_rows(given["m_" + nm]) for nm in VECTORS], axis=0)
    v_rows = jnp.concatenate([_pack_rows(given["v_" + nm]) for nm in VECTORS], axis=0)
    g_rows = reduced[:offsets[len(VECTORS)]]
    small = adamw(w_rows, m_rows, v_rows, g_rows, "adamw_vectors")
    for i, nm in enumerate(VECTORS):
        shape = given[nm].shape
        size = math.prod(shape)
        out[nm] = [r[offsets[i]:offsets[i + 1]].reshape(-1)[:size].reshape(shape) for r in small]

    grad_x = dx0.reshape(b_loc, seq, D)
    return (loss, grad_x, *[out[nm][0] for nm in WEIGHTS], *[out[nm][1] for nm in WEIGHTS],
            *[out[nm][2] for nm in WEIGHTS], *[out[nm][3] for nm in WEIGHTS])
```

```python
import functools
import math

import jax
import jax.numpy as jnp
from jax import lax
from jax.experimental import pallas as pl
from jax.experimental.pallas import tpu as pltpu

F32 = jnp.float32
BF16 = jnp.bfloat16
MESH = pl.DeviceIdType.MESH

EPS = 1e-6
HEAD_DIM = 64
LANES = 128
N_SWA_HEADS = 16
N_SWA_KV = 4
N_SB_HEADS = 16
WINDOW = 128
SWA_BLOCK = 128
SB_BLOCK = 256
N_CHIPS = 4
ATT_SCALE = HEAD_DIM ** -0.5

ADAM_LR = 0.001
ADAM_B1 = 0.9
ADAM_B2 = 0.999
ADAM_EPS = 1e-08
ADAM_WD = 0.01
ADAM_STEP = 10

VMEM_LIMIT = 56 * 1024 * 1024
NEG_BIG = -1e30

ANY = pl.BlockSpec(memory_space=pl.ANY)
VMEM_WHOLE = pl.BlockSpec(memory_space=pltpu.VMEM)


def _row_tile(rows, cap, multiple=16):
    best = None
    for t in range(multiple, min(rows, cap) + 1, multiple):
        if rows % t == 0:
            best = t
    if best is None:
        raise ValueError(f"no row tile for {rows} rows under {cap}")
    return best


def _params(*sem):
    return pltpu.CompilerParams(dimension_semantics=sem, vmem_limit_bytes=VMEM_LIMIT)


def _dot(a, b):
    return jnp.dot(a, b, preferred_element_type=F32)


def _dot_nt(a, b):
    return lax.dot_general(a, b, (((1,), (1,)), ((), ())), preferred_element_type=F32)


def _dot_tn(a, b):
    return lax.dot_general(a, b, (((0,), (0,)), ((), ())), preferred_element_type=F32)


def _sigmoid(x):
    return 1.0 / (1.0 + jnp.exp(-x))


def _split_dot(x, u):
    hi = x.astype(BF16)
    lo = (x - hi.astype(F32)).astype(BF16)
    return _dot(hi, u) + _dot(lo, u)


def _tiles(r, c, max_elems, multiple=16):
    col_opts = [c] + [t for t in (2048, 1024, 512, 256, 128) if t < c and c % t == 0]
    best = None
    for tc in col_opts:
        for tr in [r] + list(range(multiple, r, multiple)):
            if r % tr == 0 and tr * tc <= max_elems and (best is None or tr * tc > best[0] * best[1]):
                best = (tr, tc)
    if best is None:
        raise ValueError(f"no tile for [{r}, {c}] under {max_elems} elements")
    return best


class Rider:
    def __init__(self, arrays, out_shapes, sem_shapes, start, finish):
        self.arrays, self.out_shapes, self.sem_shapes = list(arrays), list(out_shapes), list(sem_shapes)
        self.start, self.finish = start, finish


def _place():
    return lax.axis_index("x"), lax.axis_index("y"), lax.axis_index("c")


def _other_chips(x, y):
    return [(1 - x, y), (x, 1 - y), (1 - x, 1 - y)]


def _split(ref):
    rows = ref.shape[-2]
    n = next(k for k in (4, 2, 1) if rows % (16 * k) == 0)
    step = rows // n
    lead = [()] if len(ref.shape) == 2 else [(s,) for s in range(ref.shape[0])]
    return [ref.at[(*idx, pl.ds(i * step, step))] for idx in lead for i in range(n)]


class _Copy:
    def __init__(self, src, dst, make):
        self.src, self.dst, self.make = src, dst, make

    def start(self):
        for s, d in zip(_split(self.src), _split(self.dst)):
            self.make(s, d).start()

    def wait(self):
        self.make(self.src, self.dst).wait()

    def wait_send(self):
        self.make(self.src, self.dst).wait_send()

    def wait_recv(self):
        self.make(self.src, self.dst).wait_recv()


def _remote(src, dst, send_sem, recv_sem, device):
    return _Copy(src, dst, lambda s, d: pltpu.make_async_remote_copy(
        src_ref=s, dst_ref=d, send_sem=send_sem, recv_sem=recv_sem, device_id=device, device_id_type=MESH))


def gather_rider(shards):
    n = len(shards)

    def copies(ins, outs, sems):
        ici_s, ici_r, d2d_s, d2d_r, own_s, own_r = sems
        x, y, c = _place()
        me = 2 * x + y
        own, ici, fwd = [], [], []
        for k in range(n):
            own.append(_remote(ins[k], outs[k].at[me], own_s.at[k], own_r.at[k], (x, y, 1 - c)))
            for j, (px, py) in enumerate(_other_chips(x, y)):
                i = 3 * k + j
                ici.append(_remote(ins[k].at[c], outs[k].at[me, c], ici_s.at[i], ici_r.at[i], (px, py, c)))
                got = outs[k].at[2 * px + py, c]
                fwd.append(_remote(got, got, d2d_s.at[i], d2d_r.at[i], (x, y, 1 - c)))
        return own, ici, fwd

    def start(ins, outs, sems):
        own, ici, _ = copies(ins, outs, sems)
        for cp in own + ici:
            cp.start()

    def finish(ins, outs, sems):
        own, ici, fwd = copies(ins, outs, sems)
        for a, f in zip(ici, fwd):
            a.wait_recv()
            f.start()
        for a in ici:
            a.wait_send()
        for f in fwd:
            f.wait()
        for o in own:
            o.wait()

    dma = pltpu.SemaphoreType.DMA
    return Rider(shards, [jax.ShapeDtypeStruct((N_CHIPS,) + s.shape, s.dtype) for s in shards],
                 [dma((3 * n,)), dma((3 * n,)), dma((3 * n,)), dma((3 * n,)), dma((n,)), dma((n,))], start, finish)


def half_swap_rider(grads):
    n = len(grads)

    def copies(ins, outs, sems):
        x, y, c = _place()
        return [_remote(ins[k].at[1 - c], outs[k], sems[0].at[k], sems[1].at[k], (x, y, 1 - c)) for k in range(n)]

    def start(ins, outs, sems):
        for cp in copies(ins, outs, sems):
            cp.start()

    def finish(ins, outs, sems):
        for cp in copies(ins, outs, sems):
            cp.wait()

    dma = pltpu.SemaphoreType.DMA
    return Rider(grads, [jax.ShapeDtypeStruct(g.shape[1:], g.dtype) for g in grads], [dma((n,)), dma((n,))],
                 start, finish)


def scatter_rider(halves):
    n = len(halves)

    def copies(ins, outs, sems):
        send, recv = sems
        x, y, c = _place()
        cps = []
        for k in range(n):
            for j, (px, py) in enumerate(_other_chips(x, y)):
                i = 3 * k + j
                cps.append(_remote(ins[k].at[2 * px + py], outs[k].at[j], send.at[i], recv.at[i], (px, py, c)))
        return cps

    def start(ins, outs, sems):
        for cp in copies(ins, outs, sems):
            cp.start()

    def finish(ins, outs, sems):
        for cp in copies(ins, outs, sems):
            cp.wait()

    dma = pltpu.SemaphoreType.DMA
    return Rider(halves, [jax.ShapeDtypeStruct((3,) + h.shape[1:], h.dtype) for h in halves],
                 [dma((3 * n,)), dma((3 * n,))], start, finish)


def final_swap_rider(sums):
    n = len(sums)

    def copies(ins, outs, sems):
        send, recv = sems
        x, y, c = _place()
        return [_remote(ins[k], outs[k], send.at[k], recv.at[k], (x, y, 1 - c)) for k in range(n)]

    def start(ins, outs, sems):
        for cp in copies(ins, outs, sems):
            cp.start()

    def finish(ins, outs, sems):
        for cp in copies(ins, outs, sems):
            cp.wait()

    dma = pltpu.SemaphoreType.DMA
    return Rider(sums, [jax.ShapeDtypeStruct(s.shape, s.dtype) for s in sums], [dma((n,)), dma((n,))],
                 start, finish)


def _call(body, *, name, grid, in_specs, out_specs, out_shape, scratch_shapes=(), sem, args, riders=(), aliases=None):
    in_specs, out_specs, out_shape = list(in_specs), list(out_specs), list(out_shape)
    scratch_shapes = list(scratch_shapes)
    aliases = aliases or {}
    if not riders:
        outs = pl.pallas_call(body, name=name, grid=grid, in_specs=in_specs, out_specs=out_specs, out_shape=out_shape,
                              scratch_shapes=scratch_shapes, input_output_aliases=aliases,
                              compiler_params=_params(*sem))(*args)
        return list(outs), []
    n_in, n_out, n_scr = len(in_specs), len(out_specs), len(scratch_shapes)
    r_in = [len(r.arrays) for r in riders]
    r_out = [len(r.out_shapes) for r in riders]
    r_sem = [len(r.sem_shapes) for r in riders]

    def wrapped(*refs):
        pos = 0

        def take(count):
            nonlocal pos
            got = refs[pos:pos + count]
            pos += count
            return got

        ins = take(n_in)
        rins = [take(k) for k in r_in]
        outs = take(n_out)
        routs = [take(k) for k in r_out]
        scr = take(n_scr)
        rsems = [take(k) for k in r_sem]
        first = functools.reduce(jnp.logical_and, [pl.program_id(a) == 0 for a in range(len(grid))])
        last = functools.reduce(jnp.logical_and, [pl.program_id(a) == grid[a] - 1 for a in range(len(grid))])

        @pl.when(first)
        def _():
            for r, ri, ro, rs in zip(riders, rins, routs, rsems):
                r.start(ri, ro, rs)

        body(*ins, *outs, *scr)

        @pl.when(last)
        def _():
            for r, ri, ro, rs in zip(riders, rins, routs, rsems):
                r.finish(ri, ro, rs)

    any_spec = pl.BlockSpec(memory_space=pl.ANY)
    res = pl.pallas_call(
        wrapped, name=name, grid=grid,
        in_specs=in_specs + [any_spec] * sum(r_in),
        out_specs=out_specs + [any_spec] * sum(r_out),
        out_shape=out_shape + [s for r in riders for s in r.out_shapes],
        scratch_shapes=scratch_shapes + [s for r in riders for s in r.sem_shapes],
        input_output_aliases=aliases,
        compiler_params=_params(*(["arbitrary"] * len(grid))),
    )(*args, *[a for r in riders for a in r.arrays])
    res = list(res)
    outs, rest = res[:n_out], res[n_out:]
    routs = []
    for k in r_out:
        routs.append(rest[:k])
        rest = rest[k:]
    return outs, routs


def exchange_now(riders, name):
    n_in = [len(r.arrays) for r in riders]
    n_out = [len(r.out_shapes) for r in riders]
    n_sem = [len(r.sem_shapes) for r in riders]

    def body(*refs):
        pos = 0
        groups = []
        for counts in (n_in, n_out, n_sem):
            part = []
            for k in counts:
                part.append(refs[pos:pos + k])
                pos += k
            groups.append(part)
        for r, ri, ro, rs in zip(riders, *groups):
            r.start(ri, ro, rs)
        for r, ri, ro, rs in zip(riders, *groups):
            r.finish(ri, ro, rs)

    any_spec = pl.BlockSpec(memory_space=pl.ANY)
    res = pl.pallas_call(
        body, name=name,
        in_specs=[any_spec] * sum(n_in), out_specs=[any_spec] * sum(n_out),
        out_shape=[s for r in riders for s in r.out_shapes],
        scratch_shapes=[s for r in riders for s in r.sem_shapes],
    )(*[a for r in riders for a in r.arrays])
    res = list(res)
    routs = []
    for k in n_out:
        routs.append(res[:k])
        res = res[k:]
    return routs


def rmsnorm_cast(x, gamma, name, riders=None):
    T, D = x.shape
    tr = min(512, T)

    def body(x_ref, g_ref, o_ref):
        xv = x_ref[...]
        r = lax.rsqrt(jnp.mean(xv * xv, axis=-1, keepdims=True) + EPS)
        o_ref[...] = ((xv * r) * g_ref[...]).astype(BF16)

    outs, landed = _call(
        body, name=name, grid=(T // tr,),
        in_specs=[pl.BlockSpec((tr, D), lambda i: (i, 0)), pl.BlockSpec((1, D), lambda i: (0, 0))],
        out_specs=[pl.BlockSpec((tr, D), lambda i: (i, 0))],
        out_shape=[jax.ShapeDtypeStruct((T, D), BF16)],
        sem=("parallel",), args=(x, gamma), riders=riders or ())
    return outs[0] if riders is None else (outs, landed)


def _rms_bwd(xv, gamma, dy):
    d = xv.shape[-1]
    r = lax.rsqrt(jnp.mean(xv * xv, axis=-1, keepdims=True) + EPS)
    gdy = dy * gamma
    dot = jnp.sum(gdy * xv, axis=-1, keepdims=True)
    dx = r * gdy - xv * ((r * r * r) * (dot / d))
    dgamma = jnp.sum(dy * (xv * r), axis=0, keepdims=True)
    return dx, dgamma


def rmsnorm_bwd_add(x, gamma, dy, dx_up, name, riders=()):
    T, D = x.shape
    tr = min(256, T)

    def body(x_ref, g_ref, dy_ref, up_ref, dx_ref, dg_ref):
        dx, dgamma = _rms_bwd(x_ref[...], g_ref[...], dy_ref[...])
        dx_ref[...] = up_ref[...] + dx

        @pl.when(pl.program_id(0) == 0)
        def _():
            dg_ref[...] = jnp.zeros_like(dg_ref)

        dg_ref[...] += dgamma

    row = pl.BlockSpec((tr, D), lambda i: (i, 0))
    vec = pl.BlockSpec((1, D), lambda i: (0, 0))
    return _call(
        body, name=name, grid=(T // tr,),
        in_specs=[row, vec, row, row], out_specs=[row, vec],
        out_shape=[jax.ShapeDtypeStruct((T, D), F32), jax.ShapeDtypeStruct((1, D), F32)],
        sem=("arbitrary",), args=(x, gamma, dy, dx_up), riders=riders)


def _column_halves(width):
    tiles = width // LANES
    cut = (tiles - tiles // 2) * LANES
    return (slice(0, cut), slice(cut, width)) if 0 < cut < width else (slice(0, width),)


def ffn_up(h, wg, wu, name, riders=()):
    T, D = h.shape
    G, _, Fs = wg.shape
    tm = min(512, T)

    def body(h_ref, wg_ref, wu_ref, g_ref, u_ref, a_ref):
        hv = h_ref[...]
        for cols in _column_halves(Fs):
            g = _dot(hv, wg_ref[:, cols])
            u = _dot(hv, wu_ref[:, cols])
            g_ref[:, cols] = g.astype(BF16)
            u_ref[:, cols] = u.astype(BF16)
            a_ref[:, cols] = ((g * _sigmoid(g)) * u).astype(BF16)

    w_spec = pl.BlockSpec((None, D, Fs), lambda s, i: (s, 0, 0), pipeline_mode=pl.Buffered(1))
    o_spec = pl.BlockSpec((None, tm, Fs), lambda s, i: (s, i, 0))
    return _call(
        body, name=name, grid=(G, T // tm),
        in_specs=[pl.BlockSpec((tm, D), lambda s, i: (i, 0)), w_spec, w_spec],
        out_specs=[o_spec, o_spec, o_spec],
        out_shape=[jax.ShapeDtypeStruct((G, T, Fs), BF16)] * 3,
        sem=("parallel", "parallel"), args=(h, wg, wu), riders=riders)


def ffn_gate(h, wg, name, riders=()):
    T, D = h.shape
    G, _, Fs = wg.shape
    tm = min(512, T)

    def body(h_ref, wg_ref, g_ref):
        g_ref[...] = _dot(h_ref[...], wg_ref[...]).astype(BF16)

    return _call(
        body, name=name, grid=(G, T // tm),
        in_specs=[pl.BlockSpec((tm, D), lambda s, i: (i, 0)),
                  pl.BlockSpec((None, D, Fs), lambda s, i: (s, 0, 0), pipeline_mode=pl.Buffered(1))],
        out_specs=[pl.BlockSpec((None, tm, Fs), lambda s, i: (s, i, 0))],
        out_shape=[jax.ShapeDtypeStruct((G, T, Fs), BF16)],
        sem=("parallel", "parallel"), args=(h, wg), riders=riders)


def ffn_up_act(h, wu, g, name, riders=()):
    T, D = h.shape
    G, _, Fs = wu.shape
    tm = min(512, T)

    def body(h_ref, wu_ref, g_ref, u_ref, a_ref):
        u = _dot(h_ref[...], wu_ref[...])
        g = g_ref[...].astype(F32)
        u_ref[...] = u.astype(BF16)
        a_ref[...] = ((g * _sigmoid(g)) * u).astype(BF16)

    act = pl.BlockSpec((None, tm, Fs), lambda s, i: (s, i, 0))
    return _call(
        body, name=name, grid=(G, T // tm),
        in_specs=[pl.BlockSpec((tm, D), lambda s, i: (i, 0)),
                  pl.BlockSpec((None, D, Fs), lambda s, i: (s, 0, 0), pipeline_mode=pl.Buffered(1)), act],
        out_specs=[act, act],
        out_shape=[jax.ShapeDtypeStruct((G, T, Fs), BF16)] * 2,
        sem=("parallel", "parallel"), args=(h, wu, g), riders=riders)


def ffn_down(a, wd, x, name, riders=()):
    G, T, Fs = a.shape
    D = wd.shape[2]
    tm = min(512, T)

    def body(a_ref, w_ref, x_ref, o_ref, acc):
        s = pl.program_id(1)

        @pl.when(s == 0)
        def _():
            acc[...] = jnp.zeros_like(acc)

        acc[...] += _dot(a_ref[...], w_ref[...])

        @pl.when(s == G - 1)
        def _():
            o_ref[...] = x_ref[...] + 0.5 * acc[...]

    return _call(
        body, name=name, grid=(T // tm, G),
        in_specs=[pl.BlockSpec((None, tm, Fs), lambda i, s: (s, i, 0)),
                  pl.BlockSpec((None, Fs, D), lambda i, s: (s, 0, 0)),
                  pl.BlockSpec((tm, D), lambda i, s: (i, 0))],
        out_specs=[pl.BlockSpec((tm, D), lambda i, s: (i, 0))],
        out_shape=[jax.ShapeDtypeStruct((T, D), F32)],
        scratch_shapes=[pltpu.VMEM((tm, D), F32)],
        sem=("parallel", "arbitrary"), args=(a, wd, x), riders=riders)


def ffn_bwd_act(dx, wd, g, u, name, riders=()):
    T, D = dx.shape
    G, Fs, _ = wd.shape
    tm = min(512, T)

    def body(dx_ref, w_ref, g_ref, u_ref, dg_ref, du_ref):
        dxh = (0.5 * dx_ref[...]).astype(BF16)
        for cols in _column_halves(Fs):
            da = _dot_nt(dxh, w_ref[cols, :])
            gv = g_ref[:, cols].astype(F32)
            sig = _sigmoid(gv)
            silu = gv * sig
            du_ref[:, cols] = (da * silu).astype(BF16)
            dg_ref[:, cols] = ((da * u_ref[:, cols].astype(F32)) * (sig * (1.0 + gv * (1.0 - sig)))).astype(BF16)

    act = pl.BlockSpec((None, tm, Fs), lambda s, i: (s, i, 0))
    return _call(
        body, name=name, grid=(G, T // tm),
        in_specs=[pl.BlockSpec((tm, D), lambda s, i: (i, 0)),
                  pl.BlockSpec((None, Fs, D), lambda s, i: (s, 0, 0), pipeline_mode=pl.Buffered(1)),
                  act, act],
        out_specs=[act, act],
        out_shape=[jax.ShapeDtypeStruct((G, T, Fs), BF16), jax.ShapeDtypeStruct((G, T, Fs), BF16)],
        sem=("parallel", "parallel"), args=(dx, wd, g, u), riders=riders)


def ffn_bwd_dh(dg, du, wg, wu, name, riders=(), part=(0, 1), prev=None):
    G, T, Fs = dg.shape
    D = wg.shape[1]
    tm = min(512, T // part[1])
    nblk = T // tm // part[1]
    off = part[0] * nblk

    def body(dg_ref, du_ref, wg_ref, wu_ref, *rest):
        o_ref, acc = rest[-2:]
        s = pl.program_id(1)

        @pl.when(s == 0)
        def _():
            acc[...] = jnp.zeros_like(acc)

        acc[...] += _dot_nt(dg_ref[...], wg_ref[...])
        acc[...] += _dot_nt(du_ref[...], wu_ref[...])

        @pl.when(s == G - 1)
        def _():
            o_ref[...] = acc[...]

    act = pl.BlockSpec((None, tm, Fs), lambda i, s: (s, i + off, 0))
    w_spec = pl.BlockSpec((None, D, Fs), lambda i, s: (s, 0, 0))
    extra = [] if prev is None else [prev]
    return _call(
        body, name=name, grid=(nblk, G),
        in_specs=[act, act, w_spec, w_spec] + [pl.BlockSpec(memory_space=pl.ANY)] * len(extra),
        out_specs=[pl.BlockSpec((tm, D), lambda i, s: (i + off, 0))],
        out_shape=[jax.ShapeDtypeStruct((T, D), F32)],
        scratch_shapes=[pltpu.VMEM((tm, D), F32)],
        sem=("parallel", "arbitrary"), args=(dg, du, wg, wu, *extra), riders=riders,
        aliases={4: 0} if extra else None)


def wgrad(a, b, name, *, groups, m, n, a_block, a_map, b_block, b_map, b_scale=None, tk=2048, riders=()):
    T = a.shape[-2]
    tk = min(tk, T)
    nk = T // tk

    def body(a_ref, b_ref, o_ref, acc):
        k = pl.program_id(2)

        @pl.when(k == 0)
        def _():
            acc[...] = jnp.zeros_like(acc)

        bv = b_ref[...]
        if b_scale is not None:
            bv = b_scale * bv
        acc[...] += _dot_tn(a_ref[...].astype(BF16), bv.astype(BF16))

        @pl.when(k == nk - 1)
        def _():
            o_ref[...] = acc[...].astype(BF16)

    return _call(
        body, name=name, grid=(groups, 2, nk),
        in_specs=[pl.BlockSpec(a_block(tk), a_map), pl.BlockSpec(b_block(tk), b_map)],
        out_specs=[pl.BlockSpec((None, None, m, n), lambda s, j, k: (j, s, 0, 0))],
        out_shape=[jax.ShapeDtypeStruct((2, groups, m, n), BF16)],
        scratch_shapes=[pltpu.VMEM((m, n), F32)],
        sem=("parallel", "parallel", "arbitrary"), args=(a, b), riders=riders)


def in_proj(h, w_in, name, riders=()):
    T, D = h.shape
    G, _, Ps = w_in.shape
    tm = min(1024, T)

    def body(h_ref, w_ref, o_ref):
        o_ref[...] = _dot(h_ref[...], w_ref[...]).astype(BF16)

    return _call(
        body, name=name, grid=(G, T // tm),
        in_specs=[pl.BlockSpec((tm, D), lambda s, i: (i, 0)),
                  pl.BlockSpec((None, D, Ps), lambda s, i: (s, 0, 0))],
        out_specs=[pl.BlockSpec((tm, Ps), lambda s, i: (i, s))],
        out_shape=[jax.ShapeDtypeStruct((T, G * Ps), BF16)],
        sem=("parallel", "parallel"), args=(h, w_in), riders=riders)


def in_proj_bwd(dproj, w_in, name, riders=()):
    T = dproj.shape[0]
    G, D, Ps = w_in.shape
    tm = min(512, T)

    def body(d_ref, w_ref, o_ref, acc):
        s = pl.program_id(1)

        @pl.when(s == 0)
        def _():
            acc[...] = jnp.zeros_like(acc)

        acc[...] += _dot_nt(d_ref[...], w_ref[...])

        @pl.when(s == G - 1)
        def _():
            o_ref[...] = acc[...]

    return _call(
        body, name=name, grid=(T // tm, G),
        in_specs=[pl.BlockSpec((tm, Ps), lambda i, s: (i, s)),
                  pl.BlockSpec((None, D, Ps), lambda i, s: (s, 0, 0))],
        out_specs=[pl.BlockSpec((tm, D), lambda i, s: (i, 0))],
        out_shape=[jax.ShapeDtypeStruct((T, D), F32)],
        scratch_shapes=[pltpu.VMEM((tm, D), F32)],
        sem=("parallel", "arbitrary"), args=(dproj, w_in), riders=riders)


def out_proj(y, w_out, x, name):
    T, M = y.shape
    D = w_out.shape[1]
    tm = min(512, T)

    def body(y_ref, w_ref, x_ref, o_ref):
        o_ref[...] = x_ref[...] + _dot(y_ref[...], w_ref[...])

    return pl.pallas_call(
        body, name=name, grid=(T // tm,),
        in_specs=[pl.BlockSpec((tm, M), lambda i: (i, 0)), pl.BlockSpec((M, D), lambda i: (0, 0)),
                  pl.BlockSpec((tm, D), lambda i: (i, 0))],
        out_specs=pl.BlockSpec((tm, D), lambda i: (i, 0)),
        out_shape=jax.ShapeDtypeStruct((T, D), F32),
        compiler_params=_params("parallel"),
    )(y, w_out, x)


def out_norm(ya, yb, ga, gb, name):
    T, W = ya.shape
    tr = min(512, T)

    def body(a_ref, b_ref, ga_ref, gb_ref, o_ref):
        for src, gam, col in ((a_ref, ga_ref, 0), (b_ref, gb_ref, W)):
            v = src[...]
            r = lax.rsqrt(jnp.mean(v * v, axis=-1, keepdims=True) + EPS)
            o_ref[:, col:col + W] = ((v * r) * gam[...]).astype(BF16)

    row = pl.BlockSpec((tr, W), lambda i: (i, 0))
    vec = pl.BlockSpec((1, W), lambda i: (0, 0))
    return pl.pallas_call(
        body, name=name, grid=(T // tr,),
        in_specs=[row, row, vec, vec],
        out_specs=pl.BlockSpec((tr, 2 * W), lambda i: (i, 0)),
        out_shape=jax.ShapeDtypeStruct((T, 2 * W), BF16),
        compiler_params=_params("parallel"),
    )(ya, yb, ga, gb)


def out_proj_bwd(dx, w_out, ya, yb, ga, gb, name):
    T, D = dx.shape
    M = w_out.shape[0]
    W = M // 2
    tm = min(256, T)

    def body(dx_ref, w_ref, a_ref, b_ref, ga_ref, gb_ref, da_ref, db_ref, dga_ref, dgb_ref):
        dy = _dot_nt(dx_ref[...].astype(BF16), w_ref[...])

        @pl.when(pl.program_id(0) == 0)
        def _():
            dga_ref[...] = jnp.zeros_like(dga_ref)
            dgb_ref[...] = jnp.zeros_like(dgb_ref)

        d_a, dg_a = _rms_bwd(a_ref[...], ga_ref[...], dy[:, :W])
        d_b, dg_b = _rms_bwd(b_ref[...], gb_ref[...], dy[:, W:])
        da_ref[...] = d_a
        db_ref[...] = d_b
        dga_ref[...] += dg_a
        dgb_ref[...] += dg_b

    row = pl.BlockSpec((tm, W), lambda i: (i, 0))
    vec = pl.BlockSpec((1, W), lambda i: (0, 0))
    return pl.pallas_call(
        body, name=name, grid=(T // tm,),
        in_specs=[pl.BlockSpec((tm, D), lambda i: (i, 0)), pl.BlockSpec((M, D), lambda i: (0, 0)),
                  row, row, vec, vec],
        out_specs=[row, row, vec, vec],
        out_shape=[jax.ShapeDtypeStruct((T, W), F32), jax.ShapeDtypeStruct((T, W), F32),
                   jax.ShapeDtypeStruct((1, W), F32), jax.ShapeDtypeStruct((1, W), F32)],
        compiler_params=_params("arbitrary"),
    )(dx, w_out, ya, yb, ga, gb)


def final_loss(x, gamma, target, name):
    T, D = x.shape
    tr = min(256, T)

    def body(x_ref, g_ref, t_ref, dx_ref, dg_ref, loss_ref):
        xv = x_ref[...]
        gam = g_ref[...]
        r = lax.rsqrt(jnp.mean(xv * xv, axis=-1, keepdims=True) + EPS)
        err = (xv * r) * gam - t_ref[...]
        part = 0.5 * jnp.sum(jnp.mean(err * err, axis=-1, keepdims=True), axis=0, keepdims=True)
        dx, dgamma = _rms_bwd(xv, gam, err / D)
        dx_ref[...] = dx

        @pl.when(pl.program_id(0) == 0)
        def _():
            dg_ref[...] = jnp.zeros_like(dg_ref)
            loss_ref[...] = jnp.zeros_like(loss_ref)

        dg_ref[...] += dgamma
        loss_ref[...] += jnp.broadcast_to(part, loss_ref.shape)

    row = pl.BlockSpec((tr, D), lambda i: (i, 0))
    vec = pl.BlockSpec((1, D), lambda i: (0, 0))
    return pl.pallas_call(
        body, name=name, grid=(T // tr,),
        in_specs=[row, vec, row],
        out_specs=[row, vec, pl.BlockSpec((1, LANES), lambda i: (0, 0))],
        out_shape=[jax.ShapeDtypeStruct((T, D), F32), jax.ShapeDtypeStruct((1, D), F32),
                   jax.ShapeDtypeStruct((1, LANES), F32)],
        compiler_params=_params("arbitrary"),
    )(x, gamma, target)


def _lane_select(src_half, dst_halves):
    r = lax.broadcasted_iota(jnp.int32, (LANES, LANES), 0)
    c = lax.broadcasted_iota(jnp.int32, (LANES, LANES), 1)
    hit = ((r >> 6) == src_half) & ((r & (HEAD_DIM - 1)) == (c & (HEAD_DIM - 1)))
    ok = functools.reduce(jnp.logical_or, [(c >> 6) == d for d in dst_halves])
    return jnp.where(hit & ok, 1.0, 0.0).astype(BF16)


def _swa_slope(kvp, local_head):
    lo = 2.0 ** (-8.0 * (local_head + 1) / N_SWA_HEADS)
    hi = 2.0 ** (-8.0 * (8 + local_head + 1) / N_SWA_HEADS)
    return jnp.where(kvp == 0, jnp.float32(lo), jnp.float32(hi))


GQA = N_SWA_HEADS // N_SWA_KV


def _per_head_rows(values):
    return jnp.concatenate([jnp.zeros((SWA_BLOCK, 1), F32) + v for v in values], axis=0)


def _swa_probs(qblk, kwin, n, start, slope, sink):
    z = _dot_nt(qblk, kwin) * ATT_SCALE
    t = n * SWA_BLOCK + (lax.broadcasted_iota(jnp.int32, z.shape, 0) & (SWA_BLOCK - 1))
    s = start + lax.broadcasted_iota(jnp.int32, z.shape, 1)
    dist = t - s
    valid = (dist >= 0) & (dist < WINDOW)
    z = jnp.where(valid, z - slope * dist.astype(F32), NEG_BIG)
    m = jnp.maximum(jnp.max(z, axis=-1, keepdims=True), sink)
    e = jnp.exp(z - m)
    den = jnp.sum(e, axis=-1, keepdims=True) + jnp.exp(sink - m)
    return e / den, m, den


def swa_fwd(proj, sinks, b_loc, seq, name, riders=()):
    T = proj.shape[0]
    nblk = seq // SWA_BLOCK

    def body(sink_ref, q_ref, k_ref, v_ref, o_ref, qs, vv):
        kvp = pl.program_id(1)
        lane = lax.broadcasted_iota(jnp.int32, (SWA_BLOCK, LANES), 1)
        for kvh in range(2):
            vv[...] = _dot(v_ref[...], _lane_select(kvh, (0, 1))).astype(BF16)
            for g in range(GQA):
                cols = slice(LANES * (2 * kvh + g // 2), LANES * (2 * kvh + g // 2 + 1))
                qs[g] = _dot(q_ref[:, cols], _lane_select(g % 2, (kvh,))).astype(BF16)
            slope = _per_head_rows([_swa_slope(kvp, GQA * kvh + g) for g in range(GQA)])
            sink = _per_head_rows([sink_ref[0, 8 * kvp + GQA * kvh + g] for g in range(GQA)])

            def blk(n, carry):
                start = pl.multiple_of(jnp.maximum(n - 1, 0) * SWA_BLOCK, SWA_BLOCK)
                rows = pl.ds(pl.multiple_of(n * SWA_BLOCK, SWA_BLOCK), SWA_BLOCK)
                win = pl.ds(start, 2 * SWA_BLOCK)
                qblk = jnp.concatenate([qs[g, rows, :] for g in range(GQA)], axis=0)
                p, _, _ = _swa_probs(qblk, k_ref[win, :], n, start, slope, sink)
                out = _dot(p.astype(BF16), vv[win, :])
                for pair in range(GQA // 2):
                    cols = slice(LANES * (2 * kvh + pair), LANES * (2 * kvh + pair + 1))
                    first = out[2 * pair * SWA_BLOCK:(2 * pair + 1) * SWA_BLOCK]
                    second = out[(2 * pair + 1) * SWA_BLOCK:(2 * pair + 2) * SWA_BLOCK]
                    o_ref[rows, cols] = jnp.where(lane < HEAD_DIM, first, second)
                return carry

            lax.fori_loop(0, nblk, blk, 0)

    return _call(
        body, name=name, grid=(b_loc, 2),
        in_specs=[pl.BlockSpec(memory_space=pltpu.SMEM),
                  pl.BlockSpec((seq, 4 * LANES), lambda b, p: (b, p)),
                  pl.BlockSpec((seq, LANES), lambda b, p: (b, 8 + p)),
                  pl.BlockSpec((seq, LANES), lambda b, p: (b, 10 + p))],
        out_specs=[pl.BlockSpec((seq, 4 * LANES), lambda b, p: (b, p))],
        out_shape=[jax.ShapeDtypeStruct((T, N_SWA_HEADS * HEAD_DIM), F32)],
        scratch_shapes=[pltpu.VMEM((GQA, seq, LANES), BF16), pltpu.VMEM((seq, LANES), BF16)],
        sem=("parallel", "parallel"), args=(sinks, proj, proj, proj), riders=riders)


def swa_bwd(proj, sinks, dya, b_loc, seq, name, riders=()):
    T = proj.shape[0]
    nblk = seq // SWA_BLOCK

    def body(sink_ref, q_ref, k_ref, v_ref, do_ref, out_ref, ds_ref, stage, qs, dos, kk, dk_acc, dv_acc, sems):
        compute(sink_ref, q_ref, k_ref, v_ref, do_ref, stage, ds_ref, qs, dos, kk, dk_acc, dv_acc)
        kvp = pl.program_id(0)
        rows = pl.ds(pl.multiple_of(pl.program_id(1) * seq, seq), seq)
        tiles = [4 * kvp + j for j in range(4)] + [8 + kvp, 10 + kvp]
        copies = []
        for j, t in enumerate(tiles):
            cols = pl.ds(pl.multiple_of(t * LANES, LANES), LANES)
            copies.append(pltpu.make_async_copy(stage.at[j], out_ref.at[rows, cols], sems.at[j]))
        for cp in copies:
            cp.start()
        for cp in copies:
            cp.wait()

    def compute(sink_ref, q_ref, k_ref, v_ref, do_ref, stage, ds_ref, qs, dos, kk, dk_acc, dv_acc):
        kvp = pl.program_id(0)
        b = pl.program_id(1)
        lane = lax.broadcasted_iota(jnp.int32, (SWA_BLOCK, LANES), 1)

        @pl.when(b == 0)
        def _():
            ds_ref[...] = jnp.zeros_like(ds_ref)

        dk_acc[...] = jnp.zeros_like(dk_acc)
        dv_acc[...] = jnp.zeros_like(dv_acc)
        for kvh in range(2):
            kk[...] = _dot(k_ref[...], _lane_select(kvh, (0, 1))).astype(BF16)
            for g in range(GQA):
                cols = slice(LANES * (2 * kvh + g // 2), LANES * (2 * kvh + g // 2 + 1))
                move = _lane_select(g % 2, (kvh,))
                qs[g] = _dot(q_ref[:, cols], move).astype(BF16)
                dos[g] = _dot(do_ref[:, cols].astype(BF16), move).astype(BF16)
            slope = _per_head_rows([_swa_slope(kvp, GQA * kvh + g) for g in range(GQA)])
            sink = _per_head_rows([sink_ref[0, 8 * kvp + GQA * kvh + g] for g in range(GQA)])

            def blk(n, carry):
                start = pl.multiple_of(jnp.maximum(n - 1, 0) * SWA_BLOCK, SWA_BLOCK)
                rows = pl.ds(pl.multiple_of(n * SWA_BLOCK, SWA_BLOCK), SWA_BLOCK)
                win = pl.ds(start, 2 * SWA_BLOCK)
                qblk = jnp.concatenate([qs[g, rows, :] for g in range(GQA)], axis=0)
                doblk = jnp.concatenate([dos[g, rows, :] for g in range(GQA)], axis=0)
                p, m, den = _swa_probs(qblk, k_ref[win, :], n, start, slope, sink)
                dp = _dot_nt(doblk, v_ref[win, :])
                delta = jnp.sum(p * dp, axis=-1, keepdims=True)
                leak = (jnp.exp(sink - m) / den) * delta
                for g in range(GQA):
                    head = GQA * kvh + g
                    ds_ref[head:head + 1, :] -= jnp.broadcast_to(
                        jnp.sum(leak[g * SWA_BLOCK:(g + 1) * SWA_BLOCK], axis=0, keepdims=True), (1, LANES))
                dsc = ((p * (dp - delta)) * ATT_SCALE).astype(BF16)
                dv_acc[win, :] += _dot_tn(p.astype(BF16), doblk)
                dk_acc[win, :] += _dot_tn(dsc, qblk)
                dq = _dot(dsc, kk[win, :])
                for pair in range(GQA // 2):
                    first = dq[2 * pair * SWA_BLOCK:(2 * pair + 1) * SWA_BLOCK]
                    second = dq[(2 * pair + 1) * SWA_BLOCK:(2 * pair + 2) * SWA_BLOCK]
                    stage[2 * kvh + pair, rows, :] = jnp.where(lane < HEAD_DIM, first, second).astype(BF16)
                return carry

            lax.fori_loop(0, nblk, blk, 0)
        stage[4] = dk_acc[...].astype(BF16)
        stage[5] = dv_acc[...].astype(BF16)

    return _call(
        body, name=name, grid=(2, b_loc),
        in_specs=[pl.BlockSpec(memory_space=pltpu.SMEM),
                  pl.BlockSpec((seq, 4 * LANES), lambda p, b: (b, p)),
                  pl.BlockSpec((seq, LANES), lambda p, b: (b, 8 + p)),
                  pl.BlockSpec((seq, LANES), lambda p, b: (b, 10 + p)),
                  pl.BlockSpec((seq, 4 * LANES), lambda p, b: (b, p))],
        out_specs=[pl.BlockSpec(memory_space=pl.ANY),
                   pl.BlockSpec((8, LANES), lambda p, b: (p, 0))],
        out_shape=[jax.ShapeDtypeStruct((T, proj.shape[1]), BF16),
                   jax.ShapeDtypeStruct((N_SWA_HEADS, LANES), F32)],
        scratch_shapes=[pltpu.VMEM((6, seq, LANES), BF16),
                        pltpu.VMEM((GQA, seq, LANES), BF16), pltpu.VMEM((GQA, seq, LANES), BF16),
                        pltpu.VMEM((seq, LANES), BF16),
                        pltpu.VMEM((seq, LANES), F32), pltpu.VMEM((seq, LANES), F32),
                        pltpu.SemaphoreType.DMA((6,))],
        sem=("arbitrary", "arbitrary"), args=(sinks, proj, proj, proj, dya), riders=riders)


def _softplus(z):
    return jnp.maximum(z, 0.0) + jnp.log(1.0 + jnp.exp(-jnp.abs(z)))


def _tri(n, rel):
    r = lax.broadcasted_iota(jnp.int32, (n, n), 0)
    c = lax.broadcasted_iota(jnp.int32, (n, n), 1)
    return jnp.where(rel(r, c), 1.0, 0.0).astype(BF16)


def sb_fwd(proj, b_loc, seq, name, riders=()):
    T = proj.shape[0]
    bq = min(SB_BLOCK, seq)
    nq = seq // bq
    npair = N_SB_HEADS // 2

    def body(q_ref, k_ref, v_ref, o_ref, lt_ref):
        i = pl.program_id(2)
        lane = lax.broadcasted_iota(jnp.int32, (bq, LANES), 1)
        after = _tri(bq, lambda r, c: r > c)
        row = lax.broadcasted_iota(jnp.int32, (bq, bq), 0)
        col = lax.broadcasted_iota(jnp.int32, (bq, bq), 1)
        qv = q_ref[...] * jnp.asarray(ATT_SCALE, BF16)
        blank = jnp.zeros((bq, LANES), BF16)
        q2 = jnp.concatenate([jnp.where((lane >> 6) == h, qv, blank) for h in range(2)], axis=0)


        def tile(j, state, mask):
            rows = pl.ds(pl.multiple_of(j * bq, bq), bq)
            kj, vj = k_ref[rows, :], v_ref[rows, :]
            zz = _dot_nt(q2, kj)
            log_beta, totals, pieces = [], [], []
            for h in range(2):
                z = zz[h * bq:(h + 1) * bq]
                sp = _softplus(z)
                lg = -sp if mask is None else jnp.where(mask, -sp, 0.0)
                hi = lg.astype(BF16)
                pieces += [hi, (lg - hi.astype(F32)).astype(BF16)]
                log_beta.append(z - sp if mask is None else jnp.where(mask, z - sp, NEG_BIG))
                totals.append(jnp.sum(lg, axis=-1, keepdims=True))
            cc = _dot(jnp.concatenate(pieces, axis=0), after)
            probs = []
            for h in range(2):
                inside = cc[2 * h * bq:(2 * h + 1) * bq] + cc[(2 * h + 1) * bq:(2 * h + 2) * bq]
                probs.append(jnp.exp(log_beta[h] + (inside + state[2 * h])).astype(BF16))
            pv = _dot(jnp.concatenate(probs, axis=0), vj)
            return (state[0] + totals[0], state[1] + pv[:bq], state[2] + totals[1], state[3] + pv[bq:])

        zero = (jnp.zeros((bq, 1), F32), jnp.zeros((bq, LANES), F32))
        state = tile(i, zero + zero, col < row)
        state = lax.fori_loop(0, i, lambda jj, st: tile(i - 1 - jj, st, None), state)
        o_ref[...] = jnp.where(lane < HEAD_DIM, state[1], state[3])
        lt_ref[...] = jnp.where(lane < HEAD_DIM, state[0], state[2])

    return _call(
        body, name=name, grid=(b_loc, npair, nq),
        in_specs=[pl.BlockSpec((bq, LANES), lambda b, p, i: (b * nq + i, 12 + p)),
                  pl.BlockSpec((seq, LANES), lambda b, p, i: (b, 20 + p)),
                  pl.BlockSpec((seq, LANES), lambda b, p, i: (b, 28 + p))],
        out_specs=[pl.BlockSpec((bq, LANES), lambda b, p, i: (b * nq + i, p)),
                   pl.BlockSpec((None, bq, LANES), lambda b, p, i: (b * npair + p, i, 0))],
        out_shape=[jax.ShapeDtypeStruct((T, N_SB_HEADS * HEAD_DIM), F32),
                   jax.ShapeDtypeStruct((b_loc * npair, seq, LANES), F32)],
        sem=("parallel", "parallel", "parallel"), args=(proj, proj, proj), riders=riders)


def sb_bwd(proj, dyb, ltot, dproj, b_loc, seq, name, riders=()):
    T = proj.shape[0]
    bq = min(SB_BLOCK, seq)
    nq = seq // bq
    npair = N_SB_HEADS // 2

    def body(q_ref, k_ref, v_ref, do_ref, lt_ref, _, out_ref, stage, dk_acc, dv_acc, sems):
        compute(q_ref, k_ref, v_ref, do_ref, lt_ref, stage, dk_acc, dv_acc)
        rows = pl.ds(pl.multiple_of(pl.program_id(0) * seq, seq), seq)
        copies = []
        for j in range(3):
            tile = pl.multiple_of((12 + pl.program_id(1) + npair * j) * LANES, LANES)
            copies.append(pltpu.make_async_copy(stage.at[j], out_ref.at[rows, pl.ds(tile, LANES)], sems.at[j]))
        for cp in copies:
            cp.start()
        for cp in copies:
            cp.wait()

    def compute(q_ref, k_ref, v_ref, do_ref, lt_ref, stage, dk_acc, dv_acc):
        lane = lax.broadcasted_iota(jnp.int32, (bq, LANES), 1)
        upto = _tri(bq, lambda r, c: r <= c)
        before = _tri(bq, lambda r, c: r < c)
        row = lax.broadcasted_iota(jnp.int32, (bq, bq), 0)
        col = lax.broadcasted_iota(jnp.int32, (bq, bq), 1)
        dk_acc[...] = jnp.zeros_like(dk_acc)
        dv_acc[...] = jnp.zeros_like(dv_acc)
        scale = jnp.asarray(ATT_SCALE, BF16)
        heads = [(lane >> 6) == h for h in range(2)]
        blank = jnp.zeros((bq, LANES), BF16)

        def qblock(i, carry):
            qrows = pl.ds(pl.multiple_of(i * bq, bq), bq)
            qv = q_ref[qrows, :] * scale
            dov = do_ref[qrows, :].astype(BF16)
            ltv = lt_ref[qrows, :]
            q2 = jnp.concatenate([jnp.where(m, qv, blank) for m in heads], axis=0)
            do2 = jnp.concatenate([jnp.where(m, dov, blank) for m in heads], axis=0)
            totals = [jnp.sum(jnp.where(lane == h * HEAD_DIM, ltv, 0.0), axis=-1, keepdims=True) for h in range(2)]

            def split2(xs):
                out = []
                for x in xs:
                    hi = x.astype(BF16)
                    out += [hi, (x - hi.astype(F32)).astype(BF16)]
                return jnp.concatenate(out, axis=0)

            def join2(cc, h):
                return cc[2 * h * bq:(2 * h + 1) * bq] + cc[(2 * h + 1) * bq:(2 * h + 2) * bq]

            def tile(j, state, mask):
                krows = pl.ds(pl.multiple_of(j * bq, bq), bq)
                kj, vj = k_ref[krows, :], v_ref[krows, :]
                kv = kj * scale
                k2 = jnp.concatenate([jnp.where(m, kv, blank) for m in heads], axis=0)
                zz = _dot_nt(q2, kj)
                dd = _dot_nt(do2, vj)
                lgs, log_betas = [], []
                for h in range(2):
                    z = zz[h * bq:(h + 1) * bq]
                    sp = _softplus(z)
                    lgs.append(-sp if mask is None else jnp.where(mask, -sp, 0.0))
                    log_betas.append(z - sp)
                cl = _dot(split2(lgs), upto)
                probs, gs = [], []
                for h in range(2):
                    a = jnp.exp(log_betas[h] + (totals[h] - (join2(cl, h) + state[2 * h])))
                    if mask is not None:
                        a = jnp.where(mask, a, 0.0)
                    probs.append(a.astype(BF16))
                    gs.append(a * dd[h * bq:(h + 1) * bq])
                cg = _dot(split2(gs), before)
                dzs = []
                for h in range(2):
                    beta = jnp.exp(log_betas[h])
                    dz = gs[h] * (1.0 - beta) - beta * (join2(cg, h) + state[2 * h + 1])
                    if mask is not None:
                        dz = jnp.where(mask, dz, 0.0)
                    dzs.append(dz.astype(BF16))
                dv_acc[krows, :] += _dot_tn(jnp.concatenate(probs, axis=0), do2)
                dk_acc[krows, :] += _dot_tn(jnp.concatenate(dzs, axis=0), q2)
                dq = state[4] + _dot(jnp.concatenate(dzs, axis=1), k2)
                new = []
                for h in range(2):
                    new += [state[2 * h] + jnp.sum(lgs[h], axis=-1, keepdims=True),
                            state[2 * h + 1] + jnp.sum(gs[h], axis=-1, keepdims=True)]
                return (*new, dq)

            zero = jnp.zeros((bq, 1), F32)
            state = lax.fori_loop(0, i, lambda j, st: tile(j, st, None),
                                  (zero, zero, zero, zero, jnp.zeros((bq, LANES), F32)))
            state = tile(i, state, col < row)
            stage[0, qrows, :] = state[4].astype(BF16)
            return carry

        lax.fori_loop(0, nq, qblock, 0)
        stage[1] = dk_acc[...].astype(BF16)
        stage[2] = dv_acc[...].astype(BF16)

    return _call(
        body, name=name, grid=(b_loc, npair),
        in_specs=[pl.BlockSpec((seq, LANES), lambda b, p: (b, 12 + p)),
                  pl.BlockSpec((seq, LANES), lambda b, p: (b, 20 + p)),
                  pl.BlockSpec((seq, LANES), lambda b, p: (b, 28 + p)),
                  pl.BlockSpec((seq, LANES), lambda b, p: (b, p)),
                  pl.BlockSpec((None, seq, LANES), lambda b, p: (b * npair + p, 0, 0)),
                  pl.BlockSpec(memory_space=pl.ANY)],
        out_specs=[pl.BlockSpec(memory_space=pl.ANY)],
        out_shape=[jax.ShapeDtypeStruct(dproj.shape, BF16)],
        scratch_shapes=[pltpu.VMEM((3, seq, LANES), BF16)] + [pltpu.VMEM((seq, LANES), F32)] * 2
        + [pltpu.SemaphoreType.DMA((3,))],
        sem=("arbitrary", "arbitrary"), args=(proj, proj, proj, dyb, ltot, dproj), riders=riders,
        aliases={5: 0})


def allreduce_small(v, name):
    R = v.shape[0]
    ndev = 8

    def body(v_ref, o_ref, land, send_sems, recv_sems):
        x, y, c = _place()
        me = 4 * x + 2 * y + c
        land[me] = v_ref[...]
        started = []
        for f in range(1, ndev):
            fx, fy, fc = (f >> 2) & 1, (f >> 1) & 1, f & 1
            peer = ((1 - x) if fx else x, (1 - y) if fy else y, (1 - c) if fc else c)
            cp = pltpu.make_async_remote_copy(
                src_ref=v_ref, dst_ref=land.at[me], send_sem=send_sems.at[f - 1], recv_sem=recv_sems.at[f - 1],
                device_id=peer, device_id_type=MESH)
            cp.start()
            started.append(cp)
        for cp in started:
            cp.wait()
        total = land[0]
        for d in range(1, ndev):
            total = total + land[d]
        o_ref[...] = total

    return pl.pallas_call(
        body, name=name,
        in_specs=[VMEM_WHOLE], out_specs=VMEM_WHOLE,
        out_shape=jax.ShapeDtypeStruct(v.shape, F32),
        scratch_shapes=[pltpu.VMEM((ndev, R, LANES), F32),
                        pltpu.SemaphoreType.DMA((ndev - 1,)), pltpu.SemaphoreType.DMA((ndev - 1,))],
    )(v)


def sum_cores(grads, theirs, name):
    _, _, r, c = grads.shape
    tr, tc = _tiles(r, c, 1 << 18)
    core = lax.axis_index("c").astype(jnp.int32).reshape(1)

    def body(core_ref, g_ref, t_ref, o_ref):
        o_ref[...] = (g_ref[...].astype(F32) + t_ref[...].astype(F32)).astype(BF16)

    blk = pl.BlockSpec((N_CHIPS, tr, tc), lambda i, j, core_ref: (0, i, j))
    return pl.pallas_call(
        body, name=name,
        grid_spec=pltpu.PrefetchScalarGridSpec(
            num_scalar_prefetch=1, grid=(r // tr, c // tc),
            in_specs=[pl.BlockSpec((None, N_CHIPS, tr, tc), lambda i, j, core_ref: (core_ref[0], 0, i, j)), blk],
            out_specs=blk),
        out_shape=jax.ShapeDtypeStruct((N_CHIPS, r, c), BF16),
        compiler_params=_params("parallel", "parallel"),
    )(core, grads, theirs)


def sum_chips(halves, landed, name):
    _, r, c = landed.shape
    tr, tc = _tiles(r, c, 1 << 18)
    chip = (2 * lax.axis_index("x") + lax.axis_index("y")).astype(jnp.int32).reshape(1)

    def body(chip_ref, own_ref, p_ref, o_ref):
        total = own_ref[...].astype(F32)
        for j in range(N_CHIPS - 1):
            total = total + p_ref[j].astype(F32)
        o_ref[...] = total

    return pl.pallas_call(
        body, name=name,
        grid_spec=pltpu.PrefetchScalarGridSpec(
            num_scalar_prefetch=1, grid=(r // tr, c // tc),
            in_specs=[pl.BlockSpec((None, tr, tc), lambda i, j, chip_ref: (chip_ref[0], i, j)),
                      pl.BlockSpec((N_CHIPS - 1, tr, tc), lambda i, j, chip_ref: (0, i, j))],
            out_specs=pl.BlockSpec((tr, tc), lambda i, j, chip_ref: (i, j))),
        out_shape=jax.ShapeDtypeStruct((r, c), F32),
        compiler_params=_params("parallel", "parallel"),
    )(chip, halves, landed)


def _adamw_math(w, g, m, v):
    m = ADAM_B1 * m + (1.0 - ADAM_B1) * g
    v = ADAM_B2 * v + (1.0 - ADAM_B2) * (g * g)
    m_hat = m / (1.0 - ADAM_B1 ** ADAM_STEP)
    v_hat = v / (1.0 - ADAM_B2 ** ADAM_STEP)
    delta = -ADAM_LR * (m_hat / (jnp.sqrt(v_hat) + ADAM_EPS) + ADAM_WD * w)
    return delta, m, v


def adamw(w, m, v, g, name):
    r, c = w.shape
    tr, tc = _tiles(r, c, 1 << 18, multiple=8)

    def body(w_ref, m_ref, v_ref, gin_ref, g_ref, d_ref, nm_ref, nv_ref):
        g = gin_ref[...]
        delta, nm, nv = _adamw_math(w_ref[...], g, m_ref[...], v_ref[...])
        g_ref[...] = g
        d_ref[...] = delta
        nm_ref[...] = nm
        nv_ref[...] = nv

    blk = pl.BlockSpec((tr, tc), lambda i, j: (i, j))
    shape = jax.ShapeDtypeStruct((r, c), F32)
    return pl.pallas_call(
        body, name=name, grid=(r // tr, c // tc),
        in_specs=[blk] * 4, out_specs=[blk] * 4, out_shape=[shape] * 4,
        compiler_params=_params("parallel", "parallel"),
    )(w, m, v, g)


def adamw_halves(w, m, v, mine, theirs, name, col_halves=False):
    r, c = w.shape
    hr, hc = (r, c // 2) if col_halves else (r // 2, c)
    tr, tc = _tiles(hr, hc, 1 << 18, multiple=8)
    g_cols = mine.shape[-1]
    padded = g_cols != hc
    if padded and tc != hc:
        raise ValueError("a gradient with padded columns is read in full-width row blocks")
    per_half = hc // tc if col_halves else hr // tr
    core = lax.axis_index("c").astype(jnp.int32).reshape(1)

    def body(core_ref, w_ref, m_ref, v_ref, mine_ref, theirs_ref, g_ref, d_ref, nm_ref, nv_ref):
        half = pl.program_id(1 if col_halves else 0) // per_half
        cols = slice(0, hc) if padded else slice(None)
        g = jnp.where(half == core_ref[0], mine_ref[:, cols], theirs_ref[:, cols])
        delta, nm, nv = _adamw_math(w_ref[...], g, m_ref[...], v_ref[...])
        g_ref[...] = g
        d_ref[...] = delta
        nm_ref[...] = nm
        nv_ref[...] = nv

    blk = pl.BlockSpec((tr, tc), lambda i, j, core_ref: (i, j))

    def g_spec(wanted):
        def index(i, j, core_ref):
            use = ((j if col_halves else i) // per_half == core_ref[0]) == wanted
            if col_halves:
                return i, jnp.where(use, j % per_half, 0)
            return jnp.where(use, i % per_half, 0), j
        return pl.BlockSpec((tr, g_cols if padded else tc), index)

    shape = jax.ShapeDtypeStruct((r, c), F32)
    return pl.pallas_call(
        body, name=name,
        grid_spec=pltpu.PrefetchScalarGridSpec(
            num_scalar_prefetch=1, grid=(r // tr, c // tc),
            in_specs=[blk, blk, blk, g_spec(True), g_spec(False)], out_specs=[blk] * 4),
        out_shape=[shape] * 4,
        compiler_params=_params("parallel", "parallel"),
    )(core, w, m, v, mine, theirs)


MATRICES = ("ffn1_w_gate", "ffn1_w_up", "ffn1_w_down", "w_in", "w_out", "ffn2_w_gate", "ffn2_w_up", "ffn2_w_down")
VECTORS = ("ffn1_norm", "mix_norm", "swa_sinks", "swa_out_norm", "sb_out_norm", "ffn2_norm", "final_norm")
WEIGHTS = ("ffn1_norm", "ffn1_w_gate", "ffn1_w_up", "ffn1_w_down", "mix_norm", "w_in", "swa_sinks", "swa_out_norm",
           "sb_out_norm", "w_out", "ffn2_norm", "ffn2_w_gate", "ffn2_w_up", "ffn2_w_down", "final_norm")


def _pack_rows(vec):
    flat = vec.reshape(-1).astype(F32)
    rows = -(-flat.shape[0] // LANES)
    rows8 = -(-rows // 8) * 8
    flat = jnp.pad(flat, (0, rows8 * LANES - flat.shape[0]))
    return flat.reshape(rows8, LANES)


class GradPipe:
    def __init__(self, name, grads):
        self.name, self.step, self.value = name, 0, grads

    def rider(self):
        make = (half_swap_rider, scatter_rider, final_swap_rider)[self.step]
        return make([self.value])

    def landed(self, outs):
        if self.step == 0:
            self.value = sum_cores(self.value, outs[0], "sum_cores_" + self.name)
        elif self.step == 1:
            self.value = sum_chips(self.value, outs[0], "sum_chips_" + self.name)
        else:
            self.value = (self.value, outs[0])
        self.step += 1


def _hosted(fn, *args, advance=(), **kw):
    outs, routs = fn(*args, riders=[p.rider() for p in advance], **kw)
    for p, ro in zip(advance, routs):
        p.landed(ro)
    return outs


def kernel(x, ffn1_norm, ffn1_w_gate, ffn1_w_up, ffn1_w_down, mix_norm, w_in, swa_sinks, swa_out_norm, sb_out_norm, w_out, ffn2_norm, ffn2_w_gate, ffn2_w_up, ffn2_w_down, final_norm, loss_target, m_ffn1_norm, m_ffn1_w_gate, m_ffn1_w_up, m_ffn1_w_down, m_mix_norm, m_w_in, m_swa_sinks, m_swa_out_norm, m_sb_out_norm, m_w_out, m_ffn2_norm, m_ffn2_w_gate, m_ffn2_w_up, m_ffn2_w_down, m_final_norm, v_ffn1_norm, v_ffn1_w_gate, v_ffn1_w_up, v_ffn1_w_down, v_mix_norm, v_w_in, v_swa_sinks, v_swa_out_norm, v_sb_out_norm, v_w_out, v_ffn2_norm, v_ffn2_w_gate, v_ffn2_w_up, v_ffn2_w_down, v_final_norm):
    given = dict(locals())
    b_loc, seq, D = x.shape
    T = b_loc * seq
    x0 = x.reshape(T, D)
    target = loss_target.reshape(T, D)
    final_g = final_norm.reshape(1, D)

    gathered = {}

    def gather(*names):
        shards = []
        for nm in names:
            w = given[nm][0].astype(BF16)
            hidden_axis = 0 if nm.endswith("w_down") else 1 if nm.startswith("ffn") else None
            if hidden_axis is not None:
                pad = -w.shape[hidden_axis] % LANES
                w = jnp.pad(w, [(0, pad) if a == hidden_axis else (0, 0) for a in range(2)])
            shards.append(w.reshape(2, w.shape[0] // 2, w.shape[1]))
        return names, gather_rider(shards)

    def hosting(fn, *args, fetch):
        names, rider = fetch
        outs, landed = fn(*args, riders=[rider])
        for nm, o in zip(names, landed[0]):
            gathered[nm] = o.reshape(N_CHIPS, 2 * o.shape[2], o.shape[3])
        return outs

    (h1,) = hosting(rmsnorm_cast, x0, ffn1_norm, "ffn1_norm_fwd", fetch=gather("ffn1_w_gate"))
    (g1,) = hosting(ffn_gate, h1, gathered["ffn1_w_gate"], "ffn1_gate", fetch=gather("ffn1_w_up"))
    u1, a1 = hosting(ffn_up_act, h1, gathered["ffn1_w_up"], g1, "ffn1_up_act", fetch=gather("ffn1_w_down"))
    (x1,) = hosting(ffn_down, a1, gathered["ffn1_w_down"], x0, "ffn1_down", fetch=gather("w_in"))
    h2 = rmsnorm_cast(x1, mix_norm, "mix_norm_fwd")
    (proj,) = hosting(in_proj, h2, gathered["w_in"], "in_proj", fetch=gather("w_out"))
    w_out_all = gathered["w_out"].reshape(-1, D)
    (ya,) = hosting(swa_fwd, proj, swa_sinks, b_loc, seq, "swa_fwd", fetch=gather("ffn2_w_gate"))
    yb, ltot = hosting(sb_fwd, proj, b_loc, seq, "sb_fwd", fetch=gather("ffn2_w_up", "ffn2_w_down"))
    y = out_norm(ya, yb, swa_out_norm, sb_out_norm, "out_norm_fwd")
    x2 = out_proj(y, w_out_all, x1, "out_proj")
    h3 = rmsnorm_cast(x2, ffn2_norm, "ffn2_norm_fwd")
    (g2, u2, a2), _ = ffn_up(h3, gathered["ffn2_w_gate"], gathered["ffn2_w_up"], "ffn2_up")
    (x3,), _ = ffn_down(a2, gathered["ffn2_w_down"], x2, "ffn2_down")
    dx3, d_final, loss_part = final_loss(x3, final_g, target, "final_loss")

    Fs = gathered["ffn1_w_gate"].shape[2]
    Ps = gathered["w_in"].shape[2]
    Os = w_out_all.shape[0] // N_CHIPS
    pipes = {}

    def down_grad(tag, act, dx, advance):
        (grads,) = _hosted(wgrad, act, dx, tag + "_dwd", groups=N_CHIPS, m=Fs, n=D // 2,
                           a_block=lambda tk: (None, tk, Fs), a_map=lambda s, j, k: (s, k, 0),
                           b_block=lambda tk: (tk, D // 2), b_map=lambda s, j, k: (k, j), b_scale=0.5,
                           advance=advance)
        return grads

    def up_grad(name, h, d_act, advance):
        (grads,) = _hosted(wgrad, h, d_act, name, groups=N_CHIPS, m=D // 2, n=Fs,
                           a_block=lambda tk: (tk, D // 2), a_map=lambda s, j, k: (k, j),
                           b_block=lambda tk: (None, tk, Fs), b_map=lambda s, j, k: (s, k, 0), advance=advance)
        return grads

    (dg2, du2), _ = ffn_bwd_act(dx3, gathered["ffn2_w_down"], g2, u2, "ffn2_bwd_act")
    pipes["ffn2_w_down"] = p_d2 = GradPipe("ffn2_w_down", down_grad("ffn2", a2, dx3, []))
    pipes["ffn2_w_gate"] = p_g2 = GradPipe("ffn2_w_gate", up_grad("ffn2_dwg", h3, dg2, [p_d2]))
    pipes["ffn2_w_up"] = p_u2 = GradPipe("ffn2_w_up", up_grad("ffn2_dwu", h3, du2, [p_d2, p_g2]))
    (dh3,) = _hosted(ffn_bwd_dh, dg2, du2, gathered["ffn2_w_gate"], gathered["ffn2_w_up"], "ffn2_bwd_dh",
                     advance=[p_d2, p_g2, p_u2])
    (dx2, d_ffn2_norm), _ = rmsnorm_bwd_add(x2, ffn2_norm, dh3, dx3, "ffn2_norm_bwd")

    (d_w_out,) = _hosted(wgrad, y, dx2, "dw_out", groups=N_CHIPS, m=Os // 2, n=D,
                         a_block=lambda tk: (tk, Os // 2), a_map=lambda s, j, k: (k, 2 * s + j),
                         b_block=lambda tk: (tk, D), b_map=lambda s, j, k: (k, 0), advance=[p_g2])
    pipes["w_out"] = p_out = GradPipe("w_out", d_w_out)
    dya, dyb, d_swa_norm, d_sb_norm = out_proj_bwd(dx2, w_out_all, ya, yb, swa_out_norm, sb_out_norm, "out_proj_bwd")
    dproj, d_sinks = _hosted(swa_bwd, proj, swa_sinks, dya, b_loc, seq, "swa_bwd", advance=[p_u2, p_out])
    (dproj,) = _hosted(sb_bwd, proj, dyb, ltot, dproj, b_loc, seq, "sb_bwd", advance=[p_u2, p_out])
    (d_w_in,) = _hosted(wgrad, h2, dproj, "dw_in", groups=N_CHIPS, m=D // 2, n=Ps,
                        a_block=lambda tk: (tk, D // 2), a_map=lambda s, j, k: (k, j),
                        b_block=lambda tk: (tk, Ps), b_map=lambda s, j, k: (k, s), advance=[p_out])
    pipes["w_in"] = p_in = GradPipe("w_in", d_w_in)
    (dh2,) = _hosted(in_proj_bwd, dproj, gathered["w_in"], "in_proj_bwd", advance=[p_in])
    (dx1, d_mix_norm), _ = rmsnorm_bwd_add(x1, mix_norm, dh2, dx2, "mix_norm_bwd")

    pipes["ffn1_w_down"] = p_d1 = GradPipe("ffn1_w_down", down_grad("ffn1", a1, dx1, [p_in]))
    dg1, du1 = _hosted(ffn_bwd_act, dx1, gathered["ffn1_w_down"], g1, u1, "ffn1_bwd_act", advance=[p_in, p_d1])
    pipes["ffn1_w_gate"] = p_g1 = GradPipe("ffn1_w_gate", up_grad("ffn1_dwg", h1, dg1, [p_d1]))
    pipes["ffn1_w_up"] = p_u1 = GradPipe("ffn1_w_up", up_grad("ffn1_dwu", h1, du1, [p_d1, p_g1]))
    (dh1,) = _hosted(ffn_bwd_dh, dg1, du1, gathered["ffn1_w_gate"], gathered["ffn1_w_up"], "ffn1_bwd_dh_a",
                     part=(0, 2), advance=[p_g1, p_u1])
    (dh1,) = _hosted(ffn_bwd_dh, dg1, du1, gathered["ffn1_w_gate"], gathered["ffn1_w_up"], "ffn1_bwd_dh_b",
                     part=(1, 2), prev=dh1, advance=[p_g1, p_u1])
    (dx0, d_ffn1_norm), _ = rmsnorm_bwd_add(x0, ffn1_norm, dh1, dx1, "ffn1_norm_bwd")
    (last,) = exchange_now([p_u1.rider()], "swap_ffn1_w_up")
    p_u1.landed(last)

    vec_grads = dict(ffn1_norm=d_ffn1_norm, mix_norm=d_mix_norm, swa_sinks=d_sinks[:, 0], swa_out_norm=d_swa_norm,
                     sb_out_norm=d_sb_norm, ffn2_norm=d_ffn2_norm, final_norm=d_final)
    packed = [_pack_rows(vec_grads[nm]) for nm in VECTORS] + [_pack_rows(loss_part[0, :1])]
    offsets = [0]
    for p in packed:
        offsets.append(offsets[-1] + p.shape[0])
    reduced = allreduce_small(jnp.concatenate(packed, axis=0), "reduce_vectors")
    loss = reduced[offsets[len(VECTORS)], 0]

    out = {}
    for nm in MATRICES:
        shape = given[nm].shape
        mine, theirs = pipes[nm].value
        res = adamw_halves(given[nm][0], given["m_" + nm][0], given["v_" + nm][0], mine, theirs, "adamw_" + nm,
                           col_halves=nm.endswith("w_down"))
        out[nm] = [r.reshape(shape) for r in res]
    w_rows = jnp.concatenate([_pack_rows(given[nm]) for nm in VECTORS], axis=0)
    m_rows = jnp.concatenate([_pack_rows(given["m_" + nm]) for nm in VECTORS], axis=0)
    v_rows = jnp.concatenate([_pack_rows(given["v_" + nm]) for nm in VECTORS], axis=0)
    g_rows = reduced[:offsets[len(VECTORS)]]
    small = adamw(w_rows, m_rows, v_rows, g_rows, "adamw_vectors")
    for i, nm in enumerate(VECTORS):
        shape = given[nm].shape
        size = math.prod(shape)
        out[nm] = [r[offsets[i]:offsets[i + 1]].reshape(-1)[:size].reshape(shape) for r in small]

    grad_x = dx0.reshape(b_loc, seq, D)
    return (loss, grad_x, *[out[nm][0] for nm in WEIGHTS], *[out[nm][1] for nm in WEIGHTS],
            *[out[nm][2] for nm in WEIGHTS], *[out[nm][3] for nm in WEIGHTS])
```
